```python
import math
import numpy as np
import jax
import jax.numpy as jnp
from jax import lax

D_MODEL = 1024
BATCH = 2
SEQ = 8192
DEPTH = 4

D_MIX = D_MODEL
SSD_WIDTH = D_MIX // 2
SSD_HEAD_DIM = 64
SSD_HEADS = SSD_WIDTH // SSD_HEAD_DIM
SSD_GROUPS = 2
SSD_STATE = 64
SSD_CONV = 4
SSD_CONV_DIM = SSD_WIDTH + 2 * SSD_GROUPS * SSD_STATE
SSD_CHUNK = 128
DT_MIN = 0.001
DT_MAX = 0.1
HGRN_WIDTH = D_MIX // 4
HGRN_HEAD_DIM = 64
HGRN_HEADS = HGRN_WIDTH // HGRN_HEAD_DIM
HGRN_CHUNK = 16
GATE_FLOOR = 1e-30
RET_WIDTH = D_MIX // 4
RET_HEAD_DIM = 64
RET_HEADS = RET_WIDTH // RET_HEAD_DIM
RET_CHUNK = 128
ROPE_BASE = 10000.0
IN_WIDTHS = (SSD_WIDTH, SSD_CONV_DIM, SSD_HEADS,
             HGRN_WIDTH, HGRN_WIDTH, HGRN_WIDTH, HGRN_WIDTH,
             RET_WIDTH, RET_WIDTH, RET_WIDTH, RET_WIDTH)
D_IN = SSD_WIDTH + SSD_CONV_DIM + SSD_HEADS + 4 * HGRN_WIDTH + 4 * RET_WIDTH
N_GROUPS = 4
EXPERTS_PER_GROUP = 8
N_EXPERTS = N_GROUPS * EXPERTS_PER_GROUP
TOP_K = 2
D_EXPERT = D_MODEL // 4
ADA_SCALE = 0.5
EPS = 1e-6

kernel_name = 'hybrid_ssd_hgrn2_retention_hmoe_adaln'


def rms(x):
    xf = x.astype(jnp.float32)
    return xf * lax.rsqrt(jnp.mean(xf * xf, axis=-1, keepdims=True) + EPS)


def rmsnorm(x, g):
    return rms(x) * g


def causal_depthwise_conv(x, w, bias):
    width, ch = w.shape
    out = lax.conv_general_dilated(
        x, w[:, None, :].astype(x.dtype), window_strides=(1,), padding=[(width - 1, 0)],
        dimension_numbers=('NWC', 'WIO', 'NWC'), feature_group_count=ch)
    return out + bias


def rotary(x, pos):
    half = x.shape[-1] // 2
    inv_freq = ROPE_BASE ** (-jnp.arange(half, dtype=jnp.float32) / half)
    ang = pos[:, None] * inv_freq[None, :]
    cos = jnp.cos(ang)[None, :, None, :]
    sin = jnp.sin(ang)[None, :, None, :]
    x1 = x[..., :half].astype(jnp.float32)
    x2 = x[..., half:].astype(jnp.float32)
    return jnp.concatenate([x1 * cos - x2 * sin, x1 * sin + x2 * cos], axis=-1)


def chunked_linear_recurrence(q, k, v, log_a, chunk):
    b, l, h, dk = q.shape
    dv = v.shape[-1]
    nc = l // chunk
    q, k, v, log_a = (t.astype(jnp.float32).reshape(b, nc, chunk, h, t.shape[-1])
                      for t in (q, k, v, log_a))
    cum = jnp.cumsum(log_a, axis=2)
    causal = jnp.tril(jnp.ones((chunk, chunk), dtype=bool))
    if log_a.shape[-1] == 1:
        cs = jnp.moveaxis(cum[..., 0], 2, 3)
        diff = cs[..., :, None] - cs[..., None, :]
        decay = jnp.where(causal, jnp.exp(jnp.where(causal, diff, 0.0)), 0.0)
        scores = jnp.einsum('bcthk,bcshk->bchts', q, k) * decay
    else:
        diff = cum[:, :, :, None] - cum[:, :, None]
        m = causal[:, :, None, None]
        decay = jnp.where(m, jnp.exp(jnp.where(m, diff, 0.0)), 0.0)
        scores = jnp.einsum('bcthk,bcshk,bctshk->bchts', q, k, decay)
    o_intra = jnp.einsum('bchts,bcshv->bcthv', scores, v)
    k_end = k * jnp.exp(cum[:, :, -1:] - cum)
    chunk_states = jnp.einsum('bcshk,bcshv->bchkv', k_end, v)
    chunk_decay = jnp.exp(cum[:, :, -1])

    def carry_state(s, inp):
        a, s_c = inp
        return a[..., None] * s + s_c, s

    _, s_prev = lax.scan(carry_state, jnp.zeros((b, h, dk, dv), jnp.float32),
                         (jnp.moveaxis(chunk_decay, 1, 0), jnp.moveaxis(chunk_states, 1, 0)))
    s_prev = jnp.moveaxis(s_prev, 0, 1)
    o_inter = jnp.einsum('bcthk,bchkv->bcthv', q * jnp.exp(cum), s_prev)
    return (o_intra + o_inter).reshape(b, l, h, dv)


def token_mixer(h, w_in, conv_w, conv_b, dt_bias, a_log, d_skip, ssd_norm_g,
                lower_bound, hgrn_norm_g, w_out):
    b, l, _ = h.shape
    proj = h @ w_in
    offsets = np.cumsum(IN_WIDTHS)[:-1].tolist()
    z, xbc, dt, hq, hf, hi, hg, rq, rk, rv, rg = jnp.split(proj, offsets, axis=-1)

    def heads(t, n):
        return t.reshape(b, l, n, -1)

    xbc = jax.nn.silu(causal_depthwise_conv(xbc, conv_w, conv_b))
    xs, bm, cm = jnp.split(xbc, [SSD_WIDTH, SSD_WIDTH + SSD_GROUPS * SSD_STATE], axis=-1)
    rep = SSD_HEADS // SSD_GROUPS
    xs = heads(xs, SSD_HEADS)
    bm = jnp.repeat(heads(bm, SSD_GROUPS), rep, axis=2)
    cm = jnp.repeat(heads(cm, SSD_GROUPS), rep, axis=2)
    dt = jax.nn.softplus(dt.astype(jnp.float32) + dt_bias)
    log_a = (dt * -jnp.exp(a_log.astype(jnp.float32)))[..., None]
    y = chunked_linear_recurrence(cm, bm, xs * dt[..., None], log_a, SSD_CHUNK)
    y = y + d_skip[:, None] * xs
    y = y * jax.nn.silu(heads(z, SSD_HEADS).astype(jnp.float32))
    y_ssd = rmsnorm(y.reshape(b, l, SSD_GROUPS, -1),
                    ssd_norm_g.reshape(SSD_GROUPS, -1)).reshape(b, l, SSD_WIDTH)

    raw = hf.astype(jnp.float32)
    forget = lower_bound + (1.0 - lower_bound) * jax.nn.sigmoid(raw)
    log_f = jnp.log(jnp.maximum(forget, GATE_FLOOR))
    k_in = 1.0 - forget
    o = chunked_linear_recurrence(heads(hq, HGRN_HEADS), heads(k_in, HGRN_HEADS),
                                  heads(hi, HGRN_HEADS), heads(log_f, HGRN_HEADS), HGRN_CHUNK)
    o = rmsnorm(o, hgrn_norm_g.reshape(HGRN_HEADS, HGRN_HEAD_DIM))
    o_hgrn = (o * jax.nn.sigmoid(heads(hg, HGRN_HEADS).astype(jnp.float32))).reshape(b, l, HGRN_WIDTH)

    pos = jnp.arange(l, dtype=jnp.float32)
    q_r = rotary(heads(rq, RET_HEADS), pos)
    k_r = rotary(heads(rk, RET_HEADS), pos) * RET_HEAD_DIM ** -0.5
    log_gamma = jnp.log1p(-jnp.exp2(-5.0 - jnp.arange(RET_HEADS, dtype=jnp.float32)))
    log_g = jnp.broadcast_to(log_gamma[:, None], (b, l, RET_HEADS, 1))
    o = chunked_linear_recurrence(q_r, k_r, heads(rv, RET_HEADS), log_g, RET_CHUNK)
    o_ret = (rms(o) * jax.nn.silu(heads(rg, RET_HEADS).astype(jnp.float32))).reshape(b, l, RET_WIDTH)

    mixed = jnp.concatenate([y_ssd, o_hgrn, o_ret], axis=-1).astype(h.dtype)
    return mixed @ w_out


def hierarchical_moe(h, w_grp, b_grp, w_exp, b_exp, w_gate, w_up, w_down):
    b, l, d = h.shape
    t = h.reshape(b * l, d)
    grp_logits = (t @ w_grp + b_grp).astype(jnp.float32)
    grp_prob = jax.nn.softmax(grp_logits, axis=-1)
    g_idx = jnp.argmax(grp_logits, axis=-1)
    p_grp = jnp.max(grp_prob, axis=-1, keepdims=True)
    exp_logits = (t @ w_exp + b_exp).astype(jnp.float32).reshape(-1, N_GROUPS, EXPERTS_PER_GROUP)
    in_grp = jnp.einsum('tg,tge->te', jax.nn.one_hot(g_idx, N_GROUPS, dtype=jnp.float32), exp_logits)
    top_logit, top_idx = lax.top_k(in_grp, TOP_K)
    weights = jax.nn.softmax(top_logit, axis=-1) * p_grp
    expert_id = g_idx[:, None] * EXPERTS_PER_GROUP + top_idx
    combine = jnp.einsum('tk,tke->te', weights,
                         jax.nn.one_hot(expert_id, N_EXPERTS, dtype=jnp.float32)).astype(t.dtype)
    y = jnp.zeros_like(t)
    for gi in range(N_GROUPS):
        sl = slice(gi * EXPERTS_PER_GROUP, (gi + 1) * EXPERTS_PER_GROUP)
        hid = jax.nn.silu(jnp.einsum('td,edf->tef', t, w_gate[sl])) * jnp.einsum('td,edf->tef', t, w_up[sl])
        y = y + jnp.einsum('tef,efd->td', hid * combine[:, sl, None], w_down[sl])
    return y.reshape(b, l, d)


def setup_inputs(seed: int = 0) -> dict:
    key = jax.random.key(seed)
    ks = jax.random.split(key, 24)
    f32 = jnp.float32

    def nrm(k, shape, scale):
        return scale * jax.random.normal(k, shape, f32)

    dt0 = jnp.exp(jax.random.uniform(ks[9], (DEPTH, SSD_HEADS), f32, math.log(DT_MIN), math.log(DT_MAX)))
    return {
        'x': nrm(ks[0], (BATCH, SEQ, D_MODEL), 1.0),
        'c': nrm(ks[1], (BATCH, D_MODEL), 1.0),
        'norm_mix_g': 1.0 + nrm(ks[2], (DEPTH, D_MODEL), 0.02),
        'norm_ffn_g': 1.0 + nrm(ks[3], (DEPTH, D_MODEL), 0.02),
        'final_norm_g': 1.0 + nrm(ks[4], (D_MODEL,), 0.02),
        'w_ada': nrm(ks[5], (DEPTH, D_MODEL, 6 * D_MODEL), ADA_SCALE * D_MODEL ** -0.5),
        'b_ada': nrm(ks[6], (DEPTH, 6 * D_MODEL), 0.02),
        'w_in': nrm(ks[7], (DEPTH, D_MODEL, D_IN), D_MODEL ** -0.5),
        'conv_w': nrm(ks[8], (DEPTH, SSD_CONV, SSD_CONV_DIM), SSD_CONV ** -0.5),
        'conv_b': nrm(ks[10], (DEPTH, SSD_CONV_DIM), 0.02),
        'ssd_dt_bias': dt0 + jnp.log(-jnp.expm1(-dt0)),
        'ssd_a_log': jnp.log(jax.random.uniform(ks[11], (DEPTH, SSD_HEADS), f32, 1.0, 16.0)),
        'ssd_d': 1.0 + nrm(ks[12], (DEPTH, SSD_HEADS), 0.1),
        'ssd_norm_g': 1.0 + nrm(ks[13], (DEPTH, SSD_WIDTH), 0.02),
        'hgrn_lower_bounds': nrm(ks[14], (DEPTH, HGRN_WIDTH), 0.1),
        'hgrn_norm_g': 1.0 + nrm(ks[15], (DEPTH, HGRN_WIDTH), 0.02),
        'w_out': nrm(ks[16], (DEPTH, D_MIX, D_MODEL), D_MIX ** -0.5),
        'w_grp': nrm(ks[17], (DEPTH, D_MODEL, N_GROUPS), D_MODEL ** -0.5),
        'b_grp': nrm(ks[18], (DEPTH, N_GROUPS), 0.01),
        'w_exp': nrm(ks[19], (DEPTH, D_MODEL, N_EXPERTS), D_MODEL ** -0.5),
        'b_exp': nrm(ks[20], (DEPTH, N_EXPERTS), 0.01),
        'w_gate': nrm(ks[21], (DEPTH, N_EXPERTS, D_MODEL, D_EXPERT), D_MODEL ** -0.5),
        'w_up': nrm(ks[22], (DEPTH, N_EXPERTS, D_MODEL, D_EXPERT), D_MODEL ** -0.5),
        'w_down': nrm(ks[23], (DEPTH, N_EXPERTS, D_EXPERT, D_MODEL), D_EXPERT ** -0.5),
    }


def reference(x, c, norm_mix_g, norm_ffn_g, final_norm_g, w_ada, b_ada, w_in, conv_w, conv_b,
              ssd_dt_bias, ssd_a_log, ssd_d, ssd_norm_g, hgrn_lower_bounds, hgrn_norm_g, w_out,
              w_grp, b_grp, w_exp, b_exp, w_gate, w_up, w_down):
    lb_soft = jax.nn.softmax(hgrn_lower_bounds.astype(jnp.float32), axis=0)
    lower_bound = jnp.cumsum(lb_soft, axis=0) - lb_soft[0]
    c_act = jax.nn.silu(c)
    for layer in range(DEPTH):
        mod = c_act @ w_ada[layer] + b_ada[layer]
        sh_mix, sc_mix, g_mix, sh_ffn, sc_ffn, g_ffn = jnp.split(mod[:, None, :], 6, axis=-1)
        h = (rmsnorm(x, norm_mix_g[layer]) * (1.0 + sc_mix) + sh_mix).astype(x.dtype)
        mixed = token_mixer(h, w_in[layer], conv_w[layer], conv_b[layer], ssd_dt_bias[layer],
                            ssd_a_log[layer], ssd_d[layer], ssd_norm_g[layer], lower_bound[layer],
                            hgrn_norm_g[layer], w_out[layer])
        x = x + (g_mix * mixed).astype(x.dtype)
        h = (rmsnorm(x, norm_ffn_g[layer]) * (1.0 + sc_ffn) + sh_ffn).astype(x.dtype)
        moe = hierarchical_moe(h, w_grp[layer], b_grp[layer], w_exp[layer], b_exp[layer],
                               w_gate[layer], w_up[layer], w_down[layer])
        x = x + (g_ffn * moe).astype(x.dtype)
    return rmsnorm(x, final_norm_g).astype(x.dtype)
```

```python
import functools
import math

import jax
import jax.numpy as jnp
import numpy as np
from jax import lax
from jax.experimental import pallas as pl
from jax.experimental.pallas import tpu as pltpu

F32 = jnp.float32
BF16 = jnp.bfloat16
HIGHEST = lax.Precision.HIGHEST

EPS = 1e-6
GATE_FLOOR = 1e-30
ROPE_BASE = 10000.0

HEAD_DIM = 64
SSD_HEADS = 8
SSD_GROUPS = 2
SSD_CONV = 4
HGRN_HEADS = 4
RET_HEADS = 4
N_GROUPS = 4
EXPERTS_PER_GROUP = 8
N_EXPERTS = N_GROUPS * EXPERTS_PER_GROUP
CHUNK = 128
LANES = 128
ROUTER_LANES = 128

SH_MIX, SC_MIX, G_MIX, SH_FFN, SC_FFN, G_FFN = range(6)
MOD_ROWS = 8

VMEM_LIMIT = 56 * 1024 * 1024


def _cparams(sem):
    return pltpu.CompilerParams(dimension_semantics=sem, vmem_limit_bytes=VMEM_LIMIT)


def _silu(x):
    return x * jax.nn.sigmoid(x)


def _rms(x):
    return x * lax.rsqrt(jnp.mean(x * x, axis=-1, keepdims=True) + EPS)


def _split3(x):
    hi = x.astype(BF16)
    r = x - hi.astype(F32)
    mid = r.astype(BF16)
    lo = (r - mid.astype(F32)).astype(BF16)
    return hi, mid, lo


def _dot(a, b):
    return jnp.dot(a, b, preferred_element_type=F32)


def _dot_nt(a, b):
    return lax.dot_general(a, b, (((1,), (1,)), ((), ())), preferred_element_type=F32)


def _dot_tn(a, b):
    return lax.dot_general(a, b, (((0,), (0,)), ((), ())), preferred_element_type=F32)


def _ada_kernel(c_ref, w_ref, b_ref, o_ref):
    c = _silu(c_ref[...])
    o_ref[...] = jnp.dot(c, w_ref[...], precision=HIGHEST, preferred_element_type=F32) + b_ref[...]


def _ada(c8, w_ada, b_ada):
    depth, d, d6 = w_ada.shape
    nb = d6 // d
    return pl.pallas_call(
        _ada_kernel,
        grid=(depth, nb),
        in_specs=[
            pl.BlockSpec((MOD_ROWS, d), lambda l, n: (0, 0)),
            pl.BlockSpec((None, d, d), lambda l, n: (l, 0, n)),
            pl.BlockSpec((None, 1, d), lambda l, n: (l, 0, n)),
        ],
        out_specs=pl.BlockSpec((None, MOD_ROWS, d), lambda l, n: (l, 0, n)),
        out_shape=jax.ShapeDtypeStruct((depth, MOD_ROWS, d6), F32),
        compiler_params=_cparams(("arbitrary", "arbitrary")),
        name="ada_mod",
    )(c8, w_ada, b_ada.reshape(depth, 1, d6))


def _inproj_kernel(x_ref, mod_ref, g_ref, w_ref, o_ref):
    x = x_ref[...]
    h = _rms(x) * g_ref[...] * (1.0 + mod_ref[SC_MIX:SC_MIX + 1, :]) + mod_ref[SH_MIX:SH_MIX + 1, :]
    o_ref[...] = _dot(h.astype(BF16), w_ref[...])


def _inproj(x2, mod_l, g, w, seq, tm=256):
    t, d = x2.shape
    n = w.shape[1]
    per_b = seq // tm
    return pl.pallas_call(
        _inproj_kernel,
        grid=(t // tm,),
        in_specs=[
            pl.BlockSpec((tm, d), lambda i: (i, 0)),
            pl.BlockSpec((None, MOD_ROWS, d), lambda i: (i // per_b, 0, 0)),
            pl.BlockSpec((1, d), lambda i: (0, 0)),
            pl.BlockSpec((d, n), lambda i: (0, 0)),
        ],
        out_specs=pl.BlockSpec((tm, n), lambda i: (i, 0)),
        out_shape=jax.ShapeDtypeStruct((t, n), F32),
        compiler_params=_cparams(("arbitrary",)),
        name="norm_inproj",
    )(x2, mod_l, g, w)


OFF_Z, OFF_XBC, OFF_HQ, OFF_HF, OFF_HI, OFF_HG = 0, 512, 1280, 1536, 1792, 2048
OFF_RQ, OFF_RK, OFF_RV, OFF_RG, OFF_DT = 2304, 2560, 2816, 3072, 3328
N_PROJ = 3456
SSD_WIDTH = SSD_HEADS * HEAD_DIM
XBC_WIDTH = SSD_WIDTH + 2 * SSD_GROUPS * HEAD_DIM
HGRN_WIDTH = HGRN_HEADS * HEAD_DIM
RET_WIDTH = RET_HEADS * HEAD_DIM
TAIL = 8
HGRN_DIRECT = 8
HGRN_LEVELS = (8, 16, 32, 64)


def _pair_blockdiag(v_pair, lo_lane):
    return jnp.concatenate([jnp.where(lo_lane, v_pair, 0.0), jnp.where(lo_lane, 0.0, v_pair)], axis=0)


def _head_mean_square(o, bd):
    sq = (o * o).astype(BF16)
    cols = [_dot(sq[:, p * LANES:(p + 1) * LANES], bd) for p in range(o.shape[1] // LANES)]
    return jnp.concatenate(cols, axis=1) if len(cols) > 1 else cols[0]


def _mixer_kernel(layer, proj_ref, convw_ref, convb_ref, dtb_ref, alog_ref, dexp_ref, ssdg_ref,
                  lbraw_ref, hgg_ref, cos_ref, sin_ref, rdec_ref, rdq_ref, rdk_ref, rtot_ref,
                  out_ref, xbuf, s_ssd, s_hg, s_ret):
    C = CHUNK
    j = pl.program_id(1)

    @pl.when(j == 0)
    def _():
        xbuf[0:TAIL, :] = jnp.zeros((TAIL, XBC_WIDTH), F32)
        s_ssd[...] = jnp.zeros_like(s_ssd)
        s_hg[...] = jnp.zeros_like(s_hg)
        s_ret[...] = jnp.zeros_like(s_ret)

    row = lax.broadcasted_iota(jnp.int32, (C, LANES), 0)
    lane = lax.broadcasted_iota(jnp.int32, (C, LANES), 1)
    lo_lane = lane < HEAD_DIM
    causal = row >= lane
    bd_mask = jnp.right_shift(row, 6) == jnp.right_shift(lane, 6)
    bd_mean = jnp.where(bd_mask, 1.0 / HEAD_DIM, 0.0).astype(BF16)
    tril = jnp.where(causal, 1.0, 0.0).astype(BF16)

    def cumsum_rows(x):
        hi, mid, lo = _split3(x)
        return _dot(tril, hi) + _dot(tril, mid) + _dot(tril, lo)

    xbuf[TAIL:TAIL + C, :] = proj_ref[:, OFF_XBC:OFF_XBC + XBC_WIDTH]
    conv = convb_ref[...]
    for jj in range(SSD_CONV):
        off = TAIL - (SSD_CONV - 1) + jj
        conv = conv + convw_ref[jj:jj + 1, :] * xbuf[off:off + C, :]
    xbuf[0:TAIL, :] = xbuf[C:C + TAIL, :]
    xc = _silu(conv)
    xs = xc[:, 0:SSD_WIDTH]
    bm = xc[:, SSD_WIDTH:SSD_WIDTH + LANES]
    cm = xc[:, SSD_WIDTH + LANES:SSD_WIDTH + 2 * LANES]

    dt8 = jax.nn.softplus(proj_ref[:, OFF_DT:OFF_DT + LANES] + dtb_ref[...])
    la8 = dt8 * (-jnp.exp(alog_ref[...]))
    cs8 = cumsum_rows(la8)
    cs8t = cs8.T

    bm_b = bm.astype(BF16)
    cm_b = cm.astype(BF16)
    scores_g = [_dot_nt(jnp.where(lo_lane, cm, 0.0).astype(BF16), bm_b),
                _dot_nt(jnp.where(lo_lane, 0.0, cm).astype(BF16), bm_b)]

    o_intra, e1_cols, e2_cols, dt_cols = [], [], [], []
    for p in range(SSD_HEADS // 2):
        a, b = 2 * p, 2 * p + 1
        g = a // (SSD_HEADS // SSD_GROUPS)
        col_a = jnp.broadcast_to(cs8[:, a:a + 1], (C, LANES))
        col_b = jnp.broadcast_to(cs8[:, b:b + 1], (C, LANES))
        cs_pair = jnp.where(lo_lane, col_a, col_b)
        dt_pair = jnp.where(lo_lane, jnp.broadcast_to(dt8[:, a:a + 1], (C, LANES)),
                            jnp.broadcast_to(dt8[:, b:b + 1], (C, LANES)))
        last = cs_pair[C - 1:C, :]
        e1_cols.append(jnp.exp(cs_pair))
        e2_cols.append(jnp.exp(last - cs_pair))
        dt_cols.append(dt_pair)
        dec_a = jnp.where(causal, jnp.exp(jnp.minimum(col_a - cs8t[a:a + 1, :], 0.0)), 0.0)
        dec_b = jnp.where(causal, jnp.exp(jnp.minimum(col_b - cs8t[b:b + 1, :], 0.0)), 0.0)
        pa = (scores_g[g] * dec_a).astype(BF16)
        pb = (scores_g[g] * dec_b).astype(BF16)
        v_pair = xs[:, p * LANES:(p + 1) * LANES] * dt_pair
        o_intra.append(_dot(jnp.concatenate([pa, pb], axis=1),
                            _pair_blockdiag(v_pair, lo_lane).astype(BF16)))
    o_intra = jnp.concatenate(o_intra, axis=1)
    e1 = jnp.concatenate(e1_cols, axis=1)
    e2 = jnp.concatenate(e2_cols, axis=1)
    dtx = jnp.concatenate(dt_cols, axis=1)
    v_all = xs * dtx
    s_prev = s_ssd[...]
    o_inter = e1 * _dot(cm_b, s_prev.astype(BF16))
    y = o_intra + o_inter + dexp_ref[...] * xs
    y = y * _silu(proj_ref[:, OFF_Z:OFF_Z + SSD_WIDTH])
    gw = SSD_WIDTH // SSD_GROUPS
    y_norm = []
    for g in range(SSD_GROUPS):
        yg = y[:, g * gw:(g + 1) * gw]
        y_norm.append(yg * lax.rsqrt(jnp.mean(yg * yg, axis=-1, keepdims=True) + EPS))
    out_ref[:, 0:SSD_WIDTH] = (jnp.concatenate(y_norm, axis=1) * ssdg_ref[...]).astype(out_ref.dtype)
    row_s = lax.broadcasted_iota(jnp.int32, (LANES, SSD_WIDTH), 0)
    lane_s = lax.broadcasted_iota(jnp.int32, (LANES, SSD_WIDTH), 1)
    grp_mask = jnp.right_shift(row_s, 6) == jnp.right_shift(lane_s, 8)
    upd = _dot_tn(bm_b, (v_all * e2).astype(BF16))
    s_ssd[...] = e1[C - 1:C, :] * s_prev + jnp.where(grp_mask, upd, 0.0)

    W2 = HGRN_WIDTH
    row2 = lax.broadcasted_iota(jnp.int32, (C, W2), 0)
    lb_raw = lbraw_ref[...]
    lb_e = jnp.exp(lb_raw - jnp.max(lb_raw, axis=0, keepdims=True))
    lb_soft = lb_e / jnp.sum(lb_e, axis=0, keepdims=True)
    lb = jnp.sum(lb_soft[0:layer + 1, :], axis=0, keepdims=True) - lb_soft[0:1, :]
    forget = lb + (1.0 - lb) * jax.nn.sigmoid(proj_ref[:, OFF_HF:OFF_HF + W2])
    fc = jnp.maximum(forget, GATE_FLOOR)
    kk = 1.0 - forget
    hq = proj_ref[:, OFF_HQ:OFF_HQ + W2]
    hv = proj_ref[:, OFF_HI:OFF_HI + W2]
    cum = cumsum_rows(jnp.log(fc))
    npair = W2 // LANES

    def pair_scores_to_out(qm, km, mask):
        cols = []
        km_b = km.astype(BF16)
        for p in range(npair):
            qp = qm[:, p * LANES:(p + 1) * LANES]
            kp = km_b[:, p * LANES:(p + 1) * LANES]
            sa = _dot_nt(jnp.where(lo_lane, qp, 0.0).astype(BF16), kp)
            sb = _dot_nt(jnp.where(lo_lane, 0.0, qp).astype(BF16), kp)
            pa = jnp.where(mask, sa, 0.0).astype(BF16)
            pb = jnp.where(mask, sb, 0.0).astype(BF16)
            vbd = _pair_blockdiag(hv[:, p * LANES:(p + 1) * LANES], lo_lane).astype(BF16)
            cols.append(_dot(jnp.concatenate([pa, pb], axis=1), vbd))
        return jnp.concatenate(cols, axis=1)

    sub = jnp.bitwise_and(row2, HGRN_DIRECT - 1)
    bd2 = jnp.where(jnp.right_shift(lax.broadcasted_iota(jnp.int32, (W2, W2), 0), 6)
                    == jnp.right_shift(lax.broadcasted_iota(jnp.int32, (W2, W2), 1), 6), 1.0, 0.0).astype(BF16)
    o_h = _dot((hq * kk).astype(BF16), bd2) * hv
    dprod = jnp.ones((C, W2), F32)
    for dlt in range(1, HGRN_DIRECT):
        f_shift = fc if dlt == 1 else pltpu.roll(fc, dlt - 1, 0)
        dprod = jnp.where(sub >= dlt, dprod * f_shift, 0.0)
        pterm = hq * dprod * pltpu.roll(kk, dlt, 0)
        o_h = o_h + _dot(pterm.astype(BF16), bd2) * pltpu.roll(hv, dlt, 0)
    for m in HGRN_LEVELS:
        nb = C // (2 * m)
        cum3 = cum.reshape(nb, 2 * m, W2)
        ref = jnp.broadcast_to(cum3[:, m - 1:m, :], (nb, 2 * m, W2)).reshape(C, W2)
        right = jnp.bitwise_and(row2, m) != 0
        e = jnp.exp(jnp.where(right, cum - ref, ref - cum))
        qm = jnp.where(right, hq * e, 0.0)
        km = jnp.where(right, 0.0, kk * e)
        sh = int(math.log2(2 * m))
        same_block = jnp.right_shift(row, sh) == jnp.right_shift(lane, sh)
        o_h = o_h + pair_scores_to_out(qm, km, same_block)
    q_in = (hq * jnp.exp(cum)).astype(BF16)
    last2 = cum[C - 1:C, :]
    k_end = (kk * jnp.exp(last2 - cum)).astype(BF16)
    tot2 = jnp.exp(last2)
    hv_b = hv.astype(BF16)
    inter = []
    for p in range(npair):
        sl = slice(p * LANES, (p + 1) * LANES)
        st = s_hg[p]
        inter.append(_dot_nt(q_in[:, sl], st.astype(BF16)))
        upd = _dot_tn(hv_b[:, sl], k_end[:, sl])
        s_hg[p] = st * tot2[:, sl] + jnp.where(bd_mask, upd, 0.0)
    o_h = o_h + jnp.concatenate(inter, axis=1)
    ms = _head_mean_square(o_h, bd_mean)
    o_h = o_h * lax.rsqrt(ms + EPS) * hgg_ref[...]
    o_h = o_h * jax.nn.sigmoid(proj_ref[:, OFF_HG:OFF_HG + W2])
    out_ref[:, SSD_WIDTH:SSD_WIDTH + W2] = o_h.astype(out_ref.dtype)

    W3 = RET_WIDTH
    lane3 = lax.broadcasted_iota(jnp.int32, (C, W3), 1)
    first_half = jnp.bitwise_and(lane3, HEAD_DIM // 2) == 0
    cosv = cos_ref[...]
    sinv = sin_ref[...]

    def rotary(xr):
        swapped = jnp.where(first_half, pltpu.roll(xr, W3 - HEAD_DIM // 2, 1),
                            pltpu.roll(xr, HEAD_DIM // 2, 1))
        return xr * cosv + swapped * sinv

    qr = rotary(proj_ref[:, OFF_RQ:OFF_RQ + W3])
    kr = rotary(proj_ref[:, OFF_RK:OFF_RK + W3]) * (HEAD_DIM ** -0.5)
    rv = proj_ref[:, OFF_RV:OFF_RV + W3]
    kr_b = kr.astype(BF16)
    q_in = (qr * rdq_ref[...]).astype(BF16)
    k_end = (kr * rdk_ref[...]).astype(BF16)
    rv_b = rv.astype(BF16)
    o_cols = []
    for p in range(W3 // LANES):
        sl = slice(p * LANES, (p + 1) * LANES)
        qp = qr[:, sl]
        sa = _dot_nt(jnp.where(lo_lane, qp, 0.0).astype(BF16), kr_b[:, sl]) * rdec_ref[2 * p]
        sb = _dot_nt(jnp.where(lo_lane, 0.0, qp).astype(BF16), kr_b[:, sl]) * rdec_ref[2 * p + 1]
        vbd = _pair_blockdiag(rv[:, sl], lo_lane).astype(BF16)
        o_p = _dot(jnp.concatenate([sa.astype(BF16), sb.astype(BF16)], axis=1), vbd)
        st = s_ret[p]
        o_p = o_p + _dot(q_in[:, sl], st.astype(BF16))
        upd = _dot_tn(k_end[:, sl], rv_b[:, sl])
        s_ret[p] = st * rtot_ref[:, sl] + jnp.where(bd_mask, upd, 0.0)
        o_cols.append(o_p)
    o_r = jnp.concatenate(o_cols, axis=1)
    ms = _head_mean_square(o_r, bd_mean)
    o_r = o_r * lax.rsqrt(ms + EPS) * _silu(proj_ref[:, OFF_RG:OFF_RG + W3])
    out_ref[:, SSD_WIDTH + W2:SSD_WIDTH + W2 + W3] = o_r.astype(out_ref.dtype)


def _mixer(proj, layer, batch, seq, conv_w, conv_b, dtb, alog, dexp, ssdg, lbraw, hgg,
           cos_t, sin_t, rdec, rdq, rdk, rtot):
    t = proj.shape[0]
    d_mix = SSD_WIDTH + HGRN_WIDTH + RET_WIDTH
    nj = seq // CHUNK
    const2 = lambda b, j: (0, 0)
    return pl.pallas_call(
        functools.partial(_mixer_kernel, layer),
        grid=(batch, nj),
        in_specs=[
            pl.BlockSpec((CHUNK, N_PROJ), lambda b, j: (b * nj + j, 0)),
            pl.BlockSpec(conv_w.shape, const2),
            pl.BlockSpec(conv_b.shape, const2),
            pl.BlockSpec(dtb.shape, const2),
            pl.BlockSpec(alog.shape, const2),
            pl.BlockSpec(dexp.shape, const2),
            pl.BlockSpec(ssdg.shape, const2),
            pl.BlockSpec(lbraw.shape, const2),
            pl.BlockSpec(hgg.shape, const2),
            pl.BlockSpec((CHUNK, RET_WIDTH), lambda b, j: (j, 0)),
            pl.BlockSpec((CHUNK, RET_WIDTH), lambda b, j: (j, 0)),
            pl.BlockSpec(rdec.shape, lambda b, j: (0, 0, 0)),
            pl.BlockSpec(rdq.shape, const2),
            pl.BlockSpec(rdk.shape, const2),
            pl.BlockSpec(rtot.shape, const2),
        ],
        out_specs=pl.BlockSpec((CHUNK, d_mix), lambda b, j: (b * nj + j, 0)),
        out_shape=jax.ShapeDtypeStruct((t, d_mix), BF16),
        scratch_shapes=[
            pltpu.VMEM((TAIL + CHUNK, XBC_WIDTH), F32),
            pltpu.VMEM((LANES, SSD_WIDTH), F32),
            pltpu.VMEM((HGRN_WIDTH // LANES, LANES, LANES), F32),
            pltpu.VMEM((RET_WIDTH // LANES, LANES, LANES), F32),
        ],
        compiler_params=_cparams(("arbitrary", "arbitrary")),
        name="token_mixers",
    )(proj, conv_w, conv_b, dtb, alog, dexp, ssdg, lbraw, hgg, cos_t, sin_t, rdec, rdq, rdk, rtot)


def _outproj_kernel(x_ref, mixed_ref, mod_ref, g_ref, w_ref, wr_ref, br_ref,
                    xo_ref, h_ref, comb_ref):
    x = x_ref[...] + mod_ref[G_MIX:G_MIX + 1, :] * _dot(mixed_ref[...], w_ref[...])
    xo_ref[...] = x
    h = _rms(x) * g_ref[...] * (1.0 + mod_ref[SC_FFN:SC_FFN + 1, :]) + mod_ref[SH_FFN:SH_FFN + 1, :]
    h_ref[...] = h.astype(h_ref.dtype)
    logits = jnp.dot(h, wr_ref[...], precision=HIGHEST, preferred_element_type=F32) + br_ref[...]
    lane = lax.broadcasted_iota(jnp.int32, logits.shape, 1).astype(F32)
    neg = -jnp.inf
    big = float(ROUTER_LANES)
    is_grp = lane < N_GROUPS
    gl = jnp.where(is_grp, logits, neg)
    gmax = jnp.max(gl, axis=-1, keepdims=True)
    g_idx = jnp.min(jnp.where(gl == gmax, lane, big), axis=-1, keepdims=True)
    p_grp = 1.0 / jnp.sum(jnp.where(is_grp, jnp.exp(gl - gmax), 0.0), axis=-1, keepdims=True)
    first = N_GROUPS + g_idx * EXPERTS_PER_GROUP
    valid = (lane >= first) & (lane < first + EXPERTS_PER_GROUP)
    el = jnp.where(valid, logits, neg)
    m1 = jnp.max(el, axis=-1, keepdims=True)
    i1 = jnp.min(jnp.where(el == m1, lane, big), axis=-1, keepdims=True)
    el2 = jnp.where(lane == i1, neg, el)
    m2 = jnp.max(el2, axis=-1, keepdims=True)
    i2 = jnp.min(jnp.where(el2 == m2, lane, big), axis=-1, keepdims=True)
    e = jnp.exp(m2 - m1)
    w1 = p_grp / (1.0 + e)
    w2 = p_grp * e / (1.0 + e)
    comb_ref[...] = jnp.where(lane == i1, w1, 0.0) + jnp.where(lane == i2, w2, 0.0)


def _outproj(x2, mixed, mod_l, g, w, wr, br, seq, tm=512):
    t, d = x2.shape
    per_b = seq // tm
    return pl.pallas_call(
        _outproj_kernel,
        grid=(t // tm,),
        in_specs=[
            pl.BlockSpec((tm, d), lambda i: (i, 0)),
            pl.BlockSpec((tm, mixed.shape[1]), lambda i: (i, 0)),
            pl.BlockSpec((None, MOD_ROWS, d), lambda i: (i // per_b, 0, 0)),
            pl.BlockSpec((1, d), lambda i: (0, 0)),
            pl.BlockSpec(w.shape, lambda i: (0, 0)),
            pl.BlockSpec(wr.shape, lambda i: (0, 0)),
            pl.BlockSpec(br.shape, lambda i: (0, 0)),
        ],
        out_specs=[
            pl.BlockSpec((tm, d), lambda i: (i, 0)),
            pl.BlockSpec((tm, d), lambda i: (i, 0)),
            pl.BlockSpec((tm, ROUTER_LANES), lambda i: (i, 0)),
        ],
        out_shape=[
            jax.ShapeDtypeStruct((t, d), F32),
            jax.ShapeDtypeStruct((t, d), BF16),
            jax.ShapeDtypeStruct((t, ROUTER_LANES), F32),
        ],
        compiler_params=_cparams(("arbitrary",)),
        name="outproj_router",
    )(x2, mixed, mod_l, g, w, wr, br)


def _moe_dense_kernel(final, x_ref, h_ref, comb_ref, mod_ref, wg_ref, wu_ref, wd_ref, fg_ref,
                      o_ref, acc_ref):
    e = pl.program_id(1)

    @pl.when(e == 0)
    def _():
        acc_ref[...] = jnp.zeros_like(acc_ref)

    h = h_ref[...]
    lane = lax.broadcasted_iota(jnp.int32, comb_ref.shape, 1)
    wcol = jnp.sum(jnp.where(lane == e + N_GROUPS, comb_ref[...], 0.0), axis=-1, keepdims=True)
    gate = _dot(h, wg_ref[...].astype(BF16))
    up = _dot(h, wu_ref[...].astype(BF16))
    hid = _silu(gate) * up * wcol
    acc_ref[...] += _dot(hid.astype(BF16), wd_ref[...].astype(BF16))

    @pl.when(e == pl.num_programs(1) - 1)
    def _():
        x = x_ref[...] + mod_ref[G_FFN:G_FFN + 1, :] * acc_ref[...]
        if final:
            x = _rms(x) * fg_ref[...]
        o_ref[...] = x


def _moe_dense(x2, h, comb, mod_l, wg, wu, wd, fg, seq, final, tm=1024):
    t, d = x2.shape
    ne, _, f = wg.shape
    per_b = seq // tm
    return pl.pallas_call(
        functools.partial(_moe_dense_kernel, final),
        grid=(t // tm, ne),
        in_specs=[
            pl.BlockSpec((tm, d), lambda i, e: (i, 0)),
            pl.BlockSpec((tm, d), lambda i, e: (i, 0)),
            pl.BlockSpec((tm, ROUTER_LANES), lambda i, e: (i, 0)),
            pl.BlockSpec((None, MOD_ROWS, d), lambda i, e: (i // per_b, 0, 0)),
            pl.BlockSpec((None, d, f), lambda i, e: (e, 0, 0)),
            pl.BlockSpec((None, d, f), lambda i, e: (e, 0, 0)),
            pl.BlockSpec((None, f, d), lambda i, e: (e, 0, 0)),
            pl.BlockSpec((1, d), lambda i, e: (0, 0)),
        ],
        out_specs=pl.BlockSpec((tm, d), lambda i, e: (i, 0)),
        out_shape=jax.ShapeDtypeStruct((t, d), F32),
        scratch_shapes=[pltpu.VMEM((tm, d), F32)],
        compiler_params=_cparams(("arbitrary", "arbitrary")),
        name="moe_dense",
    )(x2, h, comb, mod_l, wg, wu, wd, fg)


def _retention_tables(seq):
    half = HEAD_DIM // 2
    inv_freq = ROPE_BASE ** (-jnp.arange(half, dtype=F32) / half)
    ang = jnp.arange(seq, dtype=F32)[:, None] * inv_freq[None, :]
    cos, sin = jnp.cos(ang), jnp.sin(ang)
    cos_t = jnp.tile(jnp.concatenate([cos, cos], axis=-1), (1, RET_HEADS))
    sin_t = jnp.tile(jnp.concatenate([-sin, sin], axis=-1), (1, RET_HEADS))
    log_gamma = jnp.log1p(-jnp.exp2(-5.0 - jnp.arange(RET_HEADS, dtype=F32)))
    tt = jnp.arange(CHUNK, dtype=F32)
    diff = tt[:, None] - tt[None, :]
    rdec = jnp.where(diff >= 0, jnp.exp(jnp.maximum(diff, 0.0)[None] * log_gamma[:, None, None]), 0.0)
    lg_lane = jnp.repeat(log_gamma, HEAD_DIM)[None, :]
    rdq = jnp.exp((tt[:, None] + 1.0) * lg_lane)
    rdk = jnp.exp((CHUNK - 1.0 - tt[:, None]) * lg_lane)
    rtot = jnp.exp(CHUNK * lg_lane)
    return cos_t, sin_t, rdec, rdq, rdk, rtot


def _pad_lanes(v, width=LANES):
    return jnp.pad(v, ((0, 0), (0, width - v.shape[-1])))


def kernel(x, c, norm_mix_g, norm_ffn_g, final_norm_g, w_ada, b_ada, w_in, conv_w, conv_b, ssd_dt_bias, ssd_a_log, ssd_d, ssd_norm_g, hgrn_lower_bounds, hgrn_norm_g, w_out, w_grp, b_grp, w_exp, b_exp, w_gate, w_up, w_down):
    batch, seq, d = x.shape
    depth = w_in.shape[0]
    t = batch * seq

    c8 = jnp.pad(c, ((0, MOD_ROWS - batch), (0, 0)))
    mod = _ada(c8, w_ada, b_ada)
    mod = mod[:, :batch].reshape(depth, batch, 6, d)
    mod = jnp.pad(mod, ((0, 0), (0, 0), (0, MOD_ROWS - 6), (0, 0)))

    dt0 = SSD_WIDTH + XBC_WIDTH
    w_in_p = jnp.concatenate(
        [w_in[:, :, :dt0], w_in[:, :, dt0 + SSD_HEADS:], w_in[:, :, dt0:dt0 + SSD_HEADS],
         jnp.zeros((depth, d, N_PROJ - w_in.shape[2]), w_in.dtype)], axis=-1).astype(BF16)
    w_out_b = w_out.astype(BF16)
    w_router = _pad_lanes(jnp.concatenate([w_grp, w_exp], axis=-1).reshape(depth * d, -1)).reshape(depth, d, ROUTER_LANES)
    b_router = _pad_lanes(jnp.concatenate([b_grp, b_exp], axis=-1))[:, None, :]
    dtb = _pad_lanes(ssd_dt_bias)[:, None, :]
    alog = _pad_lanes(ssd_a_log)[:, None, :]
    dexp = jnp.repeat(ssd_d, HEAD_DIM, axis=-1)[:, None, :]
    tables = _retention_tables(seq)

    x2 = x.reshape(t, d)
    for layer in range(depth):
        proj = _inproj(x2, mod[layer], norm_mix_g[layer][None, :], w_in_p[layer], seq)
        mixed = _mixer(proj, layer, batch, seq, conv_w[layer], conv_b[layer][None, :], dtb[layer],
                       alog[layer], dexp[layer], ssd_norm_g[layer][None, :], hgrn_lower_bounds,
                       hgrn_norm_g[layer][None, :], *tables)
        x2, h, comb = _outproj(x2, mixed, mod[layer], norm_ffn_g[layer][None, :], w_out_b[layer],
                               w_router[layer], b_router[layer], seq)
        x2 = _moe_dense(x2, h, comb, mod[layer], w_gate[layer], w_up[layer], w_down[layer],
                        final_norm_g[None, :], seq, final=(layer == depth - 1))
    return x2.reshape(batch, seq, d)
```

```python
import functools
import math

import jax
import jax.numpy as jnp
import numpy as np
from jax import lax
from jax.experimental import pallas as pl
from jax.experimental.pallas import tpu as pltpu

F32 = jnp.float32
BF16 = jnp.bfloat16
HIGHEST = lax.Precision.HIGHEST

EPS = 1e-6
GATE_FLOOR = 1e-30
ROPE_BASE = 10000.0

HEAD_DIM = 64
SSD_HEADS = 8
SSD_GROUPS = 2
SSD_CONV = 4
HGRN_HEADS = 4
RET_HEADS = 4
N_GROUPS = 4
EXPERTS_PER_GROUP = 8
N_EXPERTS = N_GROUPS * EXPERTS_PER_GROUP
CHUNK = 128
LANES = 128
ROUTER_LANES = 128

SH_MIX, SC_MIX, G_MIX, SH_FFN, SC_FFN, G_FFN = range(6)
MOD_ROWS = 8

VMEM_LIMIT = 56 * 1024 * 1024


def _cparams(sem):
    return pltpu.CompilerParams(dimension_semantics=sem, vmem_limit_bytes=VMEM_LIMIT)


def _silu(x):
    return x * jax.nn.sigmoid(x)


def _rms(x):
    return x * lax.rsqrt(jnp.mean(x * x, axis=-1, keepdims=True) + EPS)


def _split3(x):
    hi = x.astype(BF16)
    r = x - hi.astype(F32)
    mid = r.astype(BF16)
    lo = (r - mid.astype(F32)).astype(BF16)
    return hi, mid, lo


def _dot(a, b):
    return jnp.dot(a, b, preferred_element_type=F32)


def _dot_nt(a, b):
    return lax.dot_general(a, b, (((1,), (1,)), ((), ())), preferred_element_type=F32)


def _dot_tn(a, b):
    return lax.dot_general(a, b, (((0,), (0,)), ((), ())), preferred_element_type=F32)


def _ada_kernel(c_ref, w_ref, b_ref, o_ref):
    c = _silu(c_ref[...])
    o_ref[...] = jnp.dot(c, w_ref[...], precision=HIGHEST, preferred_element_type=F32) + b_ref[...]


def _ada(c8, w_ada, b_ada):
    depth, d, d6 = w_ada.shape
    nb = d6 // d
    return pl.pallas_call(
        _ada_kernel,
        grid=(depth, nb),
        in_specs=[
            pl.BlockSpec((MOD_ROWS, d), lambda l, n: (0, 0)),
            pl.BlockSpec((None, d, d), lambda l, n: (l, 0, n)),
            pl.BlockSpec((None, 1, d), lambda l, n: (l, 0, n)),
        ],
        out_specs=pl.BlockSpec((None, MOD_ROWS, d), lambda l, n: (l, 0, n)),
        out_shape=jax.ShapeDtypeStruct((depth, MOD_ROWS, d6), F32),
        compiler_params=_cparams(("arbitrary", "arbitrary")),
        name="ada_mod",
    )(c8, w_ada, b_ada.reshape(depth, 1, d6))


def _inproj_kernel(x_ref, mod_ref, g_ref, w_ref, o_ref):
    x = x_ref[...]
    h = _rms(x) * g_ref[...] * (1.0 + mod_ref[SC_MIX:SC_MIX + 1, :]) + mod_ref[SH_MIX:SH_MIX + 1, :]
    o_ref[...] = _dot(h.astype(BF16), w_ref[...])


def _inproj_res_kernel(x_ref, y_ref, modp_ref, mod_ref, g_ref, w_ref, xo_ref, o_ref):
    x = x_ref[...] + modp_ref[G_FFN:G_FFN + 1, :] * y_ref[...]
    xo_ref[...] = x
    h = _rms(x) * g_ref[...] * (1.0 + mod_ref[SC_MIX:SC_MIX + 1, :]) + mod_ref[SH_MIX:SH_MIX + 1, :]
    o_ref[...] = _dot(h.astype(BF16), w_ref[...])


def _inproj(x2, y, mod_prev, mod_l, g, w, seq, tm=256):
    t, d = x2.shape
    n = w.shape[1]
    per_b = seq // tm
    row_spec = pl.BlockSpec((tm, d), lambda i: (i, 0))
    mod_spec = pl.BlockSpec((None, MOD_ROWS, d), lambda i: (i // per_b, 0, 0))
    tail_specs = [mod_spec, pl.BlockSpec((1, d), lambda i: (0, 0)), pl.BlockSpec((d, n), lambda i: (0, 0))]
    proj_spec = pl.BlockSpec((tm, n), lambda i: (i, 0))
    proj_shape = jax.ShapeDtypeStruct((t, n), F32)
    if y is None:
        proj = pl.pallas_call(
            _inproj_kernel, grid=(t // tm,), in_specs=[row_spec] + tail_specs, out_specs=proj_spec,
            out_shape=proj_shape, compiler_params=_cparams(("arbitrary",)), name="norm_inproj",
        )(x2, mod_l, g, w)
        return x2, proj
    return pl.pallas_call(
        _inproj_res_kernel, grid=(t // tm,), in_specs=[row_spec, row_spec, mod_spec] + tail_specs,
        out_specs=[row_spec, proj_spec], out_shape=[jax.ShapeDtypeStruct((t, d), F32), proj_shape],
        compiler_params=_cparams(("arbitrary",)), name="res_norm_inproj",
    )(x2, y, mod_prev, mod_l, g, w)


OFF_Z, OFF_XBC, OFF_HQ, OFF_HF, OFF_HI, OFF_HG = 0, 512, 1280, 1536, 1792, 2048
OFF_RQ, OFF_RK, OFF_RV, OFF_RG, OFF_DT = 2304, 2560, 2816, 3072, 3328
N_PROJ = 3456
SSD_WIDTH = SSD_HEADS * HEAD_DIM
XBC_WIDTH = SSD_WIDTH + 2 * SSD_GROUPS * HEAD_DIM
HGRN_WIDTH = HGRN_HEADS * HEAD_DIM
RET_WIDTH = RET_HEADS * HEAD_DIM
TAIL = 8
HGRN_DIRECT = 8
HGRN_LEVELS = (8, 16, 32, 64)


def _pair_blockdiag(v_pair, lo_lane):
    return jnp.concatenate([jnp.where(lo_lane, v_pair, 0.0), jnp.where(lo_lane, 0.0, v_pair)], axis=0)


def _head_mean_square(o, bd):
    sq = (o * o).astype(BF16)
    cols = [_dot(sq[:, p * LANES:(p + 1) * LANES], bd) for p in range(o.shape[1] // LANES)]
    return jnp.concatenate(cols, axis=1) if len(cols) > 1 else cols[0]


def _mixer_kernel(layer, proj_ref, convw_ref, convb_ref, dtb_ref, alog_ref, dexp_ref, ssdg_ref,
                  lbraw_ref, hgg_ref, cos_ref, sin_ref, rdec_ref, rdq_ref, rdk_ref, rtot_ref,
                  out_ref, xbuf, s_ssd, s_hg, s_ret):
    C = CHUNK
    j = pl.program_id(1)

    @pl.when(j == 0)
    def _():
        xbuf[0:TAIL, :] = jnp.zeros((TAIL, XBC_WIDTH), F32)
        s_ssd[...] = jnp.zeros_like(s_ssd)
        s_hg[...] = jnp.zeros_like(s_hg)
        s_ret[...] = jnp.zeros_like(s_ret)

    row = lax.broadcasted_iota(jnp.int32, (C, LANES), 0)
    lane = lax.broadcasted_iota(jnp.int32, (C, LANES), 1)
    lo_lane = lane < HEAD_DIM
    causal = row >= lane
    bd_mask = jnp.right_shift(row, 6) == jnp.right_shift(lane, 6)
    bd_mean = jnp.where(bd_mask, 1.0 / HEAD_DIM, 0.0).astype(BF16)
    tril = jnp.where(causal, 1.0, 0.0).astype(BF16)

    def cumsum_rows(x):
        hi, mid, lo = _split3(x)
        return _dot(tril, hi) + _dot(tril, mid) + _dot(tril, lo)

    xbuf[TAIL:TAIL + C, :] = proj_ref[:, OFF_XBC:OFF_XBC + XBC_WIDTH]
    conv = convb_ref[...]
    for jj in range(SSD_CONV):
        off = TAIL - (SSD_CONV - 1) + jj
        conv = conv + convw_ref[jj:jj + 1, :] * xbuf[off:off + C, :]
    xbuf[0:TAIL, :] = xbuf[C:C + TAIL, :]
    xc = _silu(conv)
    xs = xc[:, 0:SSD_WIDTH]
    bm = xc[:, SSD_WIDTH:SSD_WIDTH + LANES]
    cm = xc[:, SSD_WIDTH + LANES:SSD_WIDTH + 2 * LANES]

    dt8 = jax.nn.softplus(proj_ref[:, OFF_DT:OFF_DT + LANES] + dtb_ref[...])
    la8 = dt8 * (-jnp.exp(alog_ref[...]))
    cs8 = cumsum_rows(la8)
    cs8t = cs8.T

    bm_b = bm.astype(BF16)
    cm_b = cm.astype(BF16)
    scores_g = [_dot_nt(jnp.where(lo_lane, cm, 0.0).astype(BF16), bm_b),
                _dot_nt(jnp.where(lo_lane, 0.0, cm).astype(BF16), bm_b)]

    o_intra, e1_cols, e2_cols, dt_cols = [], [], [], []
    for p in range(SSD_HEADS // 2):
        a, b = 2 * p, 2 * p + 1
        g = a // (SSD_HEADS // SSD_GROUPS)
        col_a = jnp.broadcast_to(cs8[:, a:a + 1], (C, LANES))
        col_b = jnp.broadcast_to(cs8[:, b:b + 1], (C, LANES))
        cs_pair = jnp.where(lo_lane, col_a, col_b)
        dt_pair = jnp.where(lo_lane, jnp.broadcast_to(dt8[:, a:a + 1], (C, LANES)),
                            jnp.broadcast_to(dt8[:, b:b + 1], (C, LANES)))
        last = cs_pair[C - 1:C, :]
        e1_cols.append(jnp.exp(cs_pair))
        e2_cols.append(jnp.exp(last - cs_pair))
        dt_cols.append(dt_pair)
        dec_a = jnp.where(causal, jnp.exp(jnp.minimum(col_a - cs8t[a:a + 1, :], 0.0)), 0.0)
        dec_b = jnp.where(causal, jnp.exp(jnp.minimum(col_b - cs8t[b:b + 1, :], 0.0)), 0.0)
        pa = (scores_g[g] * dec_a).astype(BF16)
        pb = (scores_g[g] * dec_b).astype(BF16)
        v_pair = xs[:, p * LANES:(p + 1) * LANES] * dt_pair
        o_intra.append(_dot(jnp.concatenate([pa, pb], axis=1),
                            _pair_blockdiag(v_pair, lo_lane).astype(BF16)))
    o_intra = jnp.concatenate(o_intra, axis=1)
    e1 = jnp.concatenate(e1_cols, axis=1)
    e2 = jnp.concatenate(e2_cols, axis=1)
    dtx = jnp.concatenate(dt_cols, axis=1)
    v_all = xs * dtx
    s_prev = s_ssd[...]
    o_inter = e1 * _dot(cm_b, s_prev.astype(BF16))
    y = o_intra + o_inter + dexp_ref[...] * xs
    y = y * _silu(proj_ref[:, OFF_Z:OFF_Z + SSD_WIDTH])
    gw = SSD_WIDTH // SSD_GROUPS
    y_norm = []
    for g in range(SSD_GROUPS):
        yg = y[:, g * gw:(g + 1) * gw]
        y_norm.append(yg * lax.rsqrt(jnp.mean(yg * yg, axis=-1, keepdims=True) + EPS))
    out_ref[:, 0:SSD_WIDTH] = (jnp.concatenate(y_norm, axis=1) * ssdg_ref[...]).astype(out_ref.dtype)
    row_s = lax.broadcasted_iota(jnp.int32, (LANES, SSD_WIDTH), 0)
    lane_s = lax.broadcasted_iota(jnp.int32, (LANES, SSD_WIDTH), 1)
    grp_mask = jnp.right_shift(row_s, 6) == jnp.right_shift(lane_s, 8)
    upd = _dot_tn(bm_b, (v_all * e2).astype(BF16))
    s_ssd[...] = e1[C - 1:C, :] * s_prev + jnp.where(grp_mask, upd, 0.0)

    W2 = HGRN_WIDTH
    row2 = lax.broadcasted_iota(jnp.int32, (C, W2), 0)
    lb_raw = lbraw_ref[...]
    lb_e = jnp.exp(lb_raw - jnp.max(lb_raw, axis=0, keepdims=True))
    lb_soft = lb_e / jnp.sum(lb_e, axis=0, keepdims=True)
    lb = jnp.sum(lb_soft[0:layer + 1, :], axis=0, keepdims=True) - lb_soft[0:1, :]
    forget = lb + (1.0 - lb) * jax.nn.sigmoid(proj_ref[:, OFF_HF:OFF_HF + W2])
    fc = jnp.maximum(forget, GATE_FLOOR)
    kk = 1.0 - forget
    hq = proj_ref[:, OFF_HQ:OFF_HQ + W2]
    hv = proj_ref[:, OFF_HI:OFF_HI + W2]
    cum = cumsum_rows(jnp.log(fc))
    npair = W2 // LANES

    def pair_scores_to_out(qm, km, mask):
        cols = []
        km_b = km.astype(BF16)
        for p in range(npair):
            qp = qm[:, p * LANES:(p + 1) * LANES]
            kp = km_b[:, p * LANES:(p + 1) * LANES]
            sa = _dot_nt(jnp.where(lo_lane, qp, 0.0).astype(BF16), kp)
            sb = _dot_nt(jnp.where(lo_lane, 0.0, qp).astype(BF16), kp)
            pa = jnp.where(mask, sa, 0.0).astype(BF16)
            pb = jnp.where(mask, sb, 0.0).astype(BF16)
            vbd = _pair_blockdiag(hv[:, p * LANES:(p + 1) * LANES], lo_lane).astype(BF16)
            cols.append(_dot(jnp.concatenate([pa, pb], axis=1), vbd))
        return jnp.concatenate(cols, axis=1)

    sub = jnp.bitwise_and(row2, HGRN_DIRECT - 1)
    bd2 = jnp.where(jnp.right_shift(lax.broadcasted_iota(jnp.int32, (W2, W2), 0), 6)
                    == jnp.right_shift(lax.broadcasted_iota(jnp.int32, (W2, W2), 1), 6), 1.0, 0.0).astype(BF16)
    o_h = _dot((hq * kk).astype(BF16), bd2) * hv
    dprod = jnp.ones((C, W2), F32)
    for dlt in range(1, HGRN_DIRECT):
        f_shift = fc if dlt == 1 else pltpu.roll(fc, dlt - 1, 0)
        dprod = jnp.where(sub >= dlt, dprod * f_shift, 0.0)
        pterm = hq * dprod * pltpu.roll(kk, dlt, 0)
        o_h = o_h + _dot(pterm.astype(BF16), bd2) * pltpu.roll(hv, dlt, 0)
    for m in HGRN_LEVELS:
        nb = C // (2 * m)
        cum3 = cum.reshape(nb, 2 * m, W2)
        ref = jnp.broadcast_to(cum3[:, m - 1:m, :], (nb, 2 * m, W2)).reshape(C, W2)
        right = jnp.bitwise_and(row2, m) != 0
        e = jnp.exp(jnp.where(right, cum - ref, ref - cum))
        qm = jnp.where(right, hq * e, 0.0)
        km = jnp.where(right, 0.0, kk * e)
        sh = int(math.log2(2 * m))
        same_block = jnp.right_shift(row, sh) == jnp.right_shift(lane, sh)
        o_h = o_h + pair_scores_to_out(qm, km, same_block)
    q_in = (hq * jnp.exp(cum)).astype(BF16)
    last2 = cum[C - 1:C, :]
    k_end = (kk * jnp.exp(last2 - cum)).astype(BF16)
    tot2 = jnp.exp(last2)
    hv_b = hv.astype(BF16)
    inter = []
    for p in range(npair):
        sl = slice(p * LANES, (p + 1) * LANES)
        st = s_hg[p]
        inter.append(_dot_nt(q_in[:, sl], st.astype(BF16)))
        upd = _dot_tn(hv_b[:, sl], k_end[:, sl])
        s_hg[p] = st * tot2[:, sl] + jnp.where(bd_mask, upd, 0.0)
    o_h = o_h + jnp.concatenate(inter, axis=1)
    ms = _head_mean_square(o_h, bd_mean)
    o_h = o_h * lax.rsqrt(ms + EPS) * hgg_ref[...]
    o_h = o_h * jax.nn.sigmoid(proj_ref[:, OFF_HG:OFF_HG + W2])
    out_ref[:, SSD_WIDTH:SSD_WIDTH + W2] = o_h.astype(out_ref.dtype)

    W3 = RET_WIDTH
    lane3 = lax.broadcasted_iota(jnp.int32, (C, W3), 1)
    first_half = jnp.bitwise_and(lane3, HEAD_DIM // 2) == 0
    cosv = cos_ref[...]
    sinv = sin_ref[...]

    def rotary(xr):
        swapped = jnp.where(first_half, pltpu.roll(xr, W3 - HEAD_DIM // 2, 1),
                            pltpu.roll(xr, HEAD_DIM // 2, 1))
        return xr * cosv + swapped * sinv

    qr = rotary(proj_ref[:, OFF_RQ:OFF_RQ + W3])
    kr = rotary(proj_ref[:, OFF_RK:OFF_RK + W3]) * (HEAD_DIM ** -0.5)
    rv = proj_ref[:, OFF_RV:OFF_RV + W3]
    kr_b = kr.astype(BF16)
    q_in = (qr * rdq_ref[...]).astype(BF16)
    k_end = (kr * rdk_ref[...]).astype(BF16)
    rv_b = rv.astype(BF16)
    o_cols = []
    for p in range(W3 // LANES):
        sl = slice(p * LANES, (p + 1) * LANES)
        qp = qr[:, sl]
        sa = _dot_nt(jnp.where(lo_lane, qp, 0.0).astype(BF16), kr_b[:, sl]) * rdec_ref[2 * p]
        sb = _dot_nt(jnp.where(lo_lane, 0.0, qp).astype(BF16), kr_b[:, sl]) * rdec_ref[2 * p + 1]
        vbd = _pair_blockdiag(rv[:, sl], lo_lane).astype(BF16)
        o_p = _dot(jnp.concatenate([sa.astype(BF16), sb.astype(BF16)], axis=1), vbd)
        st = s_ret[p]
        o_p = o_p + _dot(q_in[:, sl], st.astype(BF16))
        upd = _dot_tn(k_end[:, sl], rv_b[:, sl])
        s_ret[p] = st * rtot_ref[:, sl] + jnp.where(bd_mask, upd, 0.0)
        o_cols.append(o_p)
    o_r = jnp.concatenate(o_cols, axis=1)
    ms = _head_mean_square(o_r, bd_mean)
    o_r = o_r * lax.rsqrt(ms + EPS) * _silu(proj_ref[:, OFF_RG:OFF_RG + W3])
    out_ref[:, SSD_WIDTH + W2:SSD_WIDTH + W2 + W3] = o_r.astype(out_ref.dtype)


def _mixer(proj, layer, batch, seq, conv_w, conv_b, dtb, alog, dexp, ssdg, lbraw, hgg,
           cos_t, sin_t, rdec, rdq, rdk, rtot):
    t = proj.shape[0]
    d_mix = SSD_WIDTH + HGRN_WIDTH + RET_WIDTH
    nj = seq // CHUNK
    const2 = lambda b, j: (0, 0)
    return pl.pallas_call(
        functools.partial(_mixer_kernel, layer),
        grid=(batch, nj),
        in_specs=[
            pl.BlockSpec((CHUNK, N_PROJ), lambda b, j: (b * nj + j, 0)),
            pl.BlockSpec(conv_w.shape, const2),
            pl.BlockSpec(conv_b.shape, const2),
            pl.BlockSpec(dtb.shape, const2),
            pl.BlockSpec(alog.shape, const2),
            pl.BlockSpec(dexp.shape, const2),
            pl.BlockSpec(ssdg.shape, const2),
            pl.BlockSpec(lbraw.shape, const2),
            pl.BlockSpec(hgg.shape, const2),
            pl.BlockSpec((CHUNK, RET_WIDTH), lambda b, j: (j, 0)),
            pl.BlockSpec((CHUNK, RET_WIDTH), lambda b, j: (j, 0)),
            pl.BlockSpec(rdec.shape, lambda b, j: (0, 0, 0)),
            pl.BlockSpec(rdq.shape, const2),
            pl.BlockSpec(rdk.shape, const2),
            pl.BlockSpec(rtot.shape, const2),
        ],
        out_specs=pl.BlockSpec((CHUNK, d_mix), lambda b, j: (b * nj + j, 0)),
        out_shape=jax.ShapeDtypeStruct((t, d_mix), BF16),
        scratch_shapes=[
            pltpu.VMEM((TAIL + CHUNK, XBC_WIDTH), F32),
            pltpu.VMEM((LANES, SSD_WIDTH), F32),
            pltpu.VMEM((HGRN_WIDTH // LANES, LANES, LANES), F32),
            pltpu.VMEM((RET_WIDTH // LANES, LANES, LANES), F32),
        ],
        compiler_params=_cparams(("arbitrary", "arbitrary")),
        name="token_mixers",
    )(proj, conv_w, conv_b, dtb, alog, dexp, ssdg, lbraw, hgg, cos_t, sin_t, rdec, rdq, rdk, rtot)


def _pack_bf16_pairs(h):
    k = h.shape[1] // 2
    bits = lax.bitcast_convert_type(h.astype(BF16).astype(F32), jnp.uint32)
    return bits[:, :k] | (bits[:, k:] >> 16)


def _unpack_bf16_pairs(u):
    hi = lax.bitcast_convert_type(u & jnp.uint32(0xFFFF0000), F32).astype(BF16)
    lo = lax.bitcast_convert_type(u << 16, F32).astype(BF16)
    return hi, lo


R_E1, R_E2, R_W1, R_W2 = range(4)


def _outproj_kernel(x_ref, mixed_ref, mod_ref, g_ref, w_ref, wr_ref, br_ref,
                    xo_ref, hp_ref, route_ref):
    x = x_ref[...] + mod_ref[G_MIX:G_MIX + 1, :] * _dot(mixed_ref[...], w_ref[...])
    xo_ref[...] = x
    h = _rms(x) * g_ref[...] * (1.0 + mod_ref[SC_FFN:SC_FFN + 1, :]) + mod_ref[SH_FFN:SH_FFN + 1, :]
    hp_ref[...] = _pack_bf16_pairs(h)
    logits = jnp.dot(h, wr_ref[...], precision=HIGHEST, preferred_element_type=F32) + br_ref[...]
    lane = lax.broadcasted_iota(jnp.int32, logits.shape, 1).astype(F32)
    neg = -jnp.inf
    big = float(ROUTER_LANES)
    is_grp = lane < N_GROUPS
    gl = jnp.where(is_grp, logits, neg)
    gmax = jnp.max(gl, axis=-1, keepdims=True)
    g_idx = jnp.min(jnp.where(gl == gmax, lane, big), axis=-1, keepdims=True)
    p_grp = 1.0 / jnp.sum(jnp.where(is_grp, jnp.exp(gl - gmax), 0.0), axis=-1, keepdims=True)
    first = N_GROUPS + g_idx * EXPERTS_PER_GROUP
    valid = (lane >= first) & (lane < first + EXPERTS_PER_GROUP)
    el = jnp.where(valid, logits, neg)
    m1 = jnp.max(el, axis=-1, keepdims=True)
    i1 = jnp.min(jnp.where(el == m1, lane, big), axis=-1, keepdims=True)
    el2 = jnp.where(lane == i1, neg, el)
    m2 = jnp.max(el2, axis=-1, keepdims=True)
    i2 = jnp.min(jnp.where(el2 == m2, lane, big), axis=-1, keepdims=True)
    e = jnp.exp(m2 - m1)
    w1 = p_grp / (1.0 + e)
    w2 = p_grp * e / (1.0 + e)
    route_ref[...] = jnp.where(lane == R_E1, i1 - N_GROUPS,
                               jnp.where(lane == R_E2, i2 - N_GROUPS,
                                         jnp.where(lane == R_W1, w1, jnp.where(lane == R_W2, w2, 0.0))))


def _outproj(x2, mixed, mod_l, g, w, wr, br, seq, tm=512):
    t, d = x2.shape
    per_b = seq // tm
    return pl.pallas_call(
        _outproj_kernel,
        grid=(t // tm,),
        in_specs=[
            pl.BlockSpec((tm, d), lambda i: (i, 0)),
            pl.BlockSpec((tm, mixed.shape[1]), lambda i: (i, 0)),
            pl.BlockSpec((None, MOD_ROWS, d), lambda i: (i // per_b, 0, 0)),
            pl.BlockSpec((1, d), lambda i: (0, 0)),
            pl.BlockSpec(w.shape, lambda i: (0, 0)),
            pl.BlockSpec(wr.shape, lambda i: (0, 0)),
            pl.BlockSpec(br.shape, lambda i: (0, 0)),
        ],
        out_specs=[
            pl.BlockSpec((tm, d), lambda i: (i, 0)),
            pl.BlockSpec((tm, d // 2), lambda i: (i, 0)),
            pl.BlockSpec((tm, ROUTER_LANES), lambda i: (i, 0)),
        ],
        out_shape=[
            jax.ShapeDtypeStruct((t, d), F32),
            jax.ShapeDtypeStruct((t, d // 2), jnp.uint32),
            jax.ShapeDtypeStruct((t, ROUTER_LANES), F32),
        ],
        compiler_params=_cparams(("arbitrary",)),
        name="outproj_router",
    )(x2, mixed, mod_l, g, w, wr, br)


MOE_CHUNK = 2048
MOE_TILE = 128
SEG_ALIGN = 8
SLOTS_PAD = 2 * MOE_CHUNK + N_EXPERTS * SEG_ALIGN
RANK_BLOCK = 256
M_OFF, M_CNT = 0, 1


def _rank_kernel(route_ref, pos_ref, meta_ref):
    tb = route_ref.shape[0]
    lane = lax.broadcasted_iota(jnp.int32, (RANK_BLOCK, ROUTER_LANES), 1).astype(F32)
    r_i = lax.broadcasted_iota(jnp.int32, (RANK_BLOCK, RANK_BLOCK), 0)
    c_i = lax.broadcasted_iota(jnp.int32, (RANK_BLOCK, RANK_BLOCK), 1)
    strict_tril = jnp.where(r_i > c_i, 1.0, 0.0).astype(BF16)
    carry = jnp.zeros((1, ROUTER_LANES), F32)
    ranks = []
    for b in range(tb // RANK_BLOCK):
        blk = route_ref[b * RANK_BLOCK:(b + 1) * RANK_BLOCK, :]
        onehot = jnp.where((lane == blk[:, R_E1:R_E1 + 1]) | (lane == blk[:, R_E2:R_E2 + 1]), 1.0, 0.0)
        ranks.append(_dot(strict_tril, onehot.astype(BF16)) + carry)
        carry = carry + jnp.sum(onehot, axis=0, keepdims=True)
    cnt = carry
    seg = jnp.floor((cnt + (SEG_ALIGN - 1)) * (1.0 / SEG_ALIGN)) * SEG_ALIGN
    u_r = lax.broadcasted_iota(jnp.int32, (ROUTER_LANES, ROUTER_LANES), 0)
    u_c = lax.broadcasted_iota(jnp.int32, (ROUTER_LANES, ROUTER_LANES), 1)
    strict_triu = jnp.where(u_r < u_c, 1.0, 0.0)
    off = jnp.dot(jnp.broadcast_to(seg, (8, ROUTER_LANES)), strict_triu, precision=HIGHEST,
                  preferred_element_type=F32)[0:1, :]
    for b in range(tb // RANK_BLOCK):
        blk = route_ref[b * RANK_BLOCK:(b + 1) * RANK_BLOCK, :]
        dest = off + ranks[b]
        p1 = jnp.sum(jnp.where(lane == blk[:, R_E1:R_E1 + 1], dest, 0.0), axis=-1, keepdims=True)
        p2 = jnp.sum(jnp.where(lane == blk[:, R_E2:R_E2 + 1], dest, 0.0), axis=-1, keepdims=True)
        pos_ref[b * RANK_BLOCK:(b + 1) * RANK_BLOCK, :] = jnp.where(lane == 0, p1, p2).astype(jnp.int32)
    row8 = lax.broadcasted_iota(jnp.int32, (8, ROUTER_LANES), 0)
    meta_ref[...] = jnp.where(row8 == M_OFF, off, jnp.where(row8 == M_CNT, cnt, 0.0)).astype(jnp.int32)


def _rank(route):
    t = route.shape[0]
    nc = t // MOE_CHUNK
    return pl.pallas_call(
        _rank_kernel,
        grid=(nc,),
        in_specs=[pl.BlockSpec((MOE_CHUNK, ROUTER_LANES), lambda c: (c, 0))],
        out_specs=[pl.BlockSpec((MOE_CHUNK, ROUTER_LANES), lambda c: (c, 0)),
                   pl.BlockSpec((None, 8, ROUTER_LANES), lambda c: (c, 0, 0))],
        out_shape=[jax.ShapeDtypeStruct((t, ROUTER_LANES), jnp.int32),
                   jax.ShapeDtypeStruct((nc, 8, ROUTER_LANES), jnp.int32)],
        compiler_params=_cparams(("arbitrary",)),
        name="slot_rank",
    )(route)


def _moe_kernel(pos1_ref, pos2_ref, w1_ref, w2_ref, off_ref, cnt_ref,
                hp_ref, wg_ref, wu_ref, wd_ref, y_ref, inv_ref, xt_ref, ys_ref):
    c = pl.program_id(0)
    e = pl.program_id(1)
    tb = hp_ref.shape[0]
    kh = hp_ref.shape[1]
    tok0 = c * tb

    @pl.when(e == 0)
    def _():
        def clear(i, carry):
            inv_ref[i] = 0
            return carry
        lax.fori_loop(0, inv_ref.shape[0], clear, 0)

        def place(i, carry):
            inv_ref[pos1_ref[tok0 + i]] = i
            inv_ref[pos2_ref[tok0 + i]] = i
            return carry
        lax.fori_loop(0, tb, place, 0)

    n = cnt_ref[c * N_EXPERTS + e]
    base = off_ref[c * N_EXPERTS + e]

    @pl.when(n > 0)
    def _():
        wg = wg_ref[...].astype(BF16)
        wu = wu_ref[...].astype(BF16)
        wd = wd_ref[...].astype(BF16)

        def tile(i, carry):
            start = pl.multiple_of(base + i * MOE_TILE, SEG_ALIGN)

            def gather(r, carry2):
                xt_ref[pl.ds(r, 1), :] = hp_ref[pl.ds(inv_ref[start + r], 1), :]
                return carry2
            lax.fori_loop(0, MOE_TILE, gather, 0, unroll=8)
            hi, lo = _unpack_bf16_pairs(xt_ref[...])
            gate = _dot(hi, wg[:kh]) + _dot(lo, wg[kh:])
            up = _dot(hi, wu[:kh]) + _dot(lo, wu[kh:])
            hid = (_silu(gate) * up).astype(BF16)
            ys_ref[pl.ds(start, MOE_TILE), :] = _dot(hid, wd)
            return carry
        lax.fori_loop(0, (n + MOE_TILE - 1) // MOE_TILE, tile, 0)

    @pl.when(e == pl.num_programs(1) - 1)
    def _():
        def combine(i, carry):
            r1 = ys_ref[pl.ds(pos1_ref[tok0 + i], 1), :]
            r2 = ys_ref[pl.ds(pos2_ref[tok0 + i], 1), :]
            y_ref[pl.ds(i, 1), :] = w1_ref[tok0 + i] * r1 + w2_ref[tok0 + i] * r2
            return carry
        lax.fori_loop(0, tb, combine, 0, unroll=4)


def _moe(hp, pos1, pos2, w1, w2, off, cnt, w_gate, w_up, w_down, layer):
    t, kh = hp.shape
    _, ne, d, f = w_gate.shape
    nc = t // MOE_CHUNK
    grid_spec = pltpu.PrefetchScalarGridSpec(
        num_scalar_prefetch=6,
        grid=(nc, ne),
        in_specs=[
            pl.BlockSpec((MOE_CHUNK, kh), lambda c, e, *_: (c, 0)),
            pl.BlockSpec((None, None, d, f), lambda c, e, *_: (layer, e, 0, 0)),
            pl.BlockSpec((None, None, d, f), lambda c, e, *_: (layer, e, 0, 0)),
            pl.BlockSpec((None, None, f, d), lambda c, e, *_: (layer, e, 0, 0)),
        ],
        out_specs=pl.BlockSpec((MOE_CHUNK, d), lambda c, e, *_: (c, 0)),
        scratch_shapes=[
            pltpu.SMEM((SLOTS_PAD + MOE_TILE,), jnp.int32),
            pltpu.VMEM((MOE_TILE, kh), jnp.uint32),
            pltpu.VMEM((SLOTS_PAD + MOE_TILE, d), F32),
        ],
    )
    return pl.pallas_call(
        _moe_kernel,
        grid_spec=grid_spec,
        out_shape=jax.ShapeDtypeStruct((t, d), F32),
        compiler_params=_cparams(("arbitrary", "arbitrary")),
        name="moe_top2",
    )(pos1, pos2, w1, w2, off, cnt, hp, w_gate, w_up, w_down)


def _final_kernel(x_ref, y_ref, mod_ref, g_ref, o_ref):
    x = x_ref[...] + mod_ref[G_FFN:G_FFN + 1, :] * y_ref[...]
    o_ref[...] = _rms(x) * g_ref[...]


def _final(x2, y, mod_l, g, seq, tm=512):
    t, d = x2.shape
    per_b = seq // tm
    return pl.pallas_call(
        _final_kernel,
        grid=(t // tm,),
        in_specs=[
            pl.BlockSpec((tm, d), lambda i: (i, 0)),
            pl.BlockSpec((tm, d), lambda i: (i, 0)),
            pl.BlockSpec((None, MOD_ROWS, d), lambda i: (i // per_b, 0, 0)),
            pl.BlockSpec((1, d), lambda i: (0, 0)),
        ],
        out_specs=pl.BlockSpec((tm, d), lambda i: (i, 0)),
        out_shape=jax.ShapeDtypeStruct((t, d), F32),
        compiler_params=_cparams(("arbitrary",)),
        name="final_norm",
    )(x2, y, mod_l, g)


def _retention_tables(seq):
    half = HEAD_DIM // 2
    inv_freq = ROPE_BASE ** (-jnp.arange(half, dtype=F32) / half)
    ang = jnp.arange(seq, dtype=F32)[:, None] * inv_freq[None, :]
    cos, sin = jnp.cos(ang), jnp.sin(ang)
    cos_t = jnp.tile(jnp.concatenate([cos, cos], axis=-1), (1, RET_HEADS))
    sin_t = jnp.tile(jnp.concatenate([-sin, sin], axis=-1), (1, RET_HEADS))
    log_gamma = jnp.log1p(-jnp.exp2(-5.0 - jnp.arange(RET_HEADS, dtype=F32)))
    tt = jnp.arange(CHUNK, dtype=F32)
    diff = tt[:, None] - tt[None, :]
    rdec = jnp.where(diff >= 0, jnp.exp(jnp.maximum(diff, 0.0)[None] * log_gamma[:, None, None]), 0.0)
    lg_lane = jnp.repeat(log_gamma, HEAD_DIM)[None, :]
    rdq = jnp.exp((tt[:, None] + 1.0) * lg_lane)
    rdk = jnp.exp((CHUNK - 1.0 - tt[:, None]) * lg_lane)
    rtot = jnp.exp(CHUNK * lg_lane)
    return cos_t, sin_t, rdec, rdq, rdk, rtot


def _pad_lanes(v, width=LANES):
    return jnp.pad(v, ((0, 0), (0, width - v.shape[-1])))


def kernel(x, c, norm_mix_g, norm_ffn_g, final_norm_g, w_ada, b_ada, w_in, conv_w, conv_b, ssd_dt_bias, ssd_a_log, ssd_d, ssd_norm_g, hgrn_lower_bounds, hgrn_norm_g, w_out, w_grp, b_grp, w_exp, b_exp, w_gate, w_up, w_down):
    batch, seq, d = x.shape
    depth = w_in.shape[0]
    t = batch * seq

    c8 = jnp.pad(c, ((0, MOD_ROWS - batch), (0, 0)))
    mod = _ada(c8, w_ada, b_ada)
    mod = mod[:, :batch].reshape(depth, batch, 6, d)
    mod = jnp.pad(mod, ((0, 0), (0, 0), (0, MOD_ROWS - 6), (0, 0)))

    dt0 = SSD_WIDTH + XBC_WIDTH
    w_in_p = jnp.concatenate(
        [w_in[:, :, :dt0], w_in[:, :, dt0 + SSD_HEADS:], w_in[:, :, dt0:dt0 + SSD_HEADS],
         jnp.zeros((depth, d, N_PROJ - w_in.shape[2]), w_in.dtype)], axis=-1).astype(BF16)
    w_out_b = w_out.astype(BF16)
    w_router = _pad_lanes(jnp.concatenate([w_grp, w_exp], axis=-1).reshape(depth * d, -1)).reshape(depth, d, ROUTER_LANES)
    b_router = _pad_lanes(jnp.concatenate([b_grp, b_exp], axis=-1))[:, None, :]
    dtb = _pad_lanes(ssd_dt_bias)[:, None, :]
    alog = _pad_lanes(ssd_a_log)[:, None, :]
    dexp = jnp.repeat(ssd_d, HEAD_DIM, axis=-1)[:, None, :]
    tables = _retention_tables(seq)

    x2 = x.reshape(t, d)
    y = None
    for layer in range(depth):
        x2, proj = _inproj(x2, y, mod[layer - 1] if layer else None, mod[layer],
                           norm_mix_g[layer][None, :], w_in_p[layer], seq)
        mixed = _mixer(proj, layer, batch, seq, conv_w[layer], conv_b[layer][None, :], dtb[layer],
                       alog[layer], dexp[layer], ssd_norm_g[layer][None, :], hgrn_lower_bounds,
                       hgrn_norm_g[layer][None, :], *tables)
        x2, hp, route = _outproj(x2, mixed, mod[layer], norm_ffn_g[layer][None, :], w_out_b[layer],
                                 w_router[layer], b_router[layer], seq)
        pos, meta = _rank(route)
        y = _moe(hp, pos[:, 0], pos[:, 1], route[:, R_W1], route[:, R_W2],
                 meta[:, M_OFF, :N_EXPERTS].reshape(-1), meta[:, M_CNT, :N_EXPERTS].reshape(-1),
                 w_gate, w_up, w_down, layer)
    out = _final(x2, y, mod[depth - 1], final_norm_g[None, :], seq)
    return out.reshape(batch, seq, d)
```

```python
import functools
import math

import jax
import jax.numpy as jnp
import numpy as np
from jax import lax
from jax.experimental import pallas as pl
from jax.experimental.pallas import tpu as pltpu

F32 = jnp.float32
BF16 = jnp.bfloat16
HIGHEST = lax.Precision.HIGHEST

EPS = 1e-6
GATE_FLOOR = 1e-30
ROPE_BASE = 10000.0

HEAD_DIM = 64
SSD_HEADS = 8
SSD_GROUPS = 2
SSD_CONV = 4
HGRN_HEADS = 4
RET_HEADS = 4
N_GROUPS = 4
EXPERTS_PER_GROUP = 8
N_EXPERTS = N_GROUPS * EXPERTS_PER_GROUP
CHUNK = 128
LANES = 128
ROUTER_LANES = 128

SH_MIX, SC_MIX, G_MIX, SH_FFN, SC_FFN, G_FFN = range(6)
MOD_ROWS = 8

VMEM_LIMIT = 56 * 1024 * 1024


def _cparams(sem):
    return pltpu.CompilerParams(dimension_semantics=sem, vmem_limit_bytes=VMEM_LIMIT)


def _silu(x):
    return x * jax.nn.sigmoid(x)


def _rms(x):
    return x * lax.rsqrt(jnp.mean(x * x, axis=-1, keepdims=True) + EPS)


def _split3(x):
    hi = x.astype(BF16)
    r = x - hi.astype(F32)
    mid = r.astype(BF16)
    lo = (r - mid.astype(F32)).astype(BF16)
    return hi, mid, lo


def _dot(a, b):
    return jnp.dot(a, b, preferred_element_type=F32)


def _dot_nt(a, b):
    return lax.dot_general(a, b, (((1,), (1,)), ((), ())), preferred_element_type=F32)


def _dot_tn(a, b):
    return lax.dot_general(a, b, (((0,), (0,)), ((), ())), preferred_element_type=F32)


def _ada_kernel(c_ref, w_ref, b_ref, o_ref):
    c = _silu(c_ref[...])
    o_ref[...] = jnp.dot(c, w_ref[...], precision=HIGHEST, preferred_element_type=F32) + b_ref[...]


def _ada(c8, w_ada, b_ada):
    depth, d, d6 = w_ada.shape
    nb = d6 // d
    return pl.pallas_call(
        _ada_kernel,
        grid=(depth, nb),
        in_specs=[
            pl.BlockSpec((MOD_ROWS, d), lambda l, n: (0, 0)),
            pl.BlockSpec((None, d, d), lambda l, n: (l, 0, n)),
            pl.BlockSpec((None, 1, d), lambda l, n: (l, 0, n)),
        ],
        out_specs=pl.BlockSpec((None, MOD_ROWS, d), lambda l, n: (l, 0, n)),
        out_shape=jax.ShapeDtypeStruct((depth, MOD_ROWS, d6), F32),
        compiler_params=_cparams(("arbitrary", "arbitrary")),
        name="ada_mod",
    )(c8, w_ada, b_ada.reshape(depth, 1, d6))


def _inproj_kernel(x_ref, mod_ref, g_ref, w_ref, o_ref):
    x = x_ref[...]
    h = _rms(x) * g_ref[...] * (1.0 + mod_ref[SC_MIX:SC_MIX + 1, :]) + mod_ref[SH_MIX:SH_MIX + 1, :]
    o_ref[...] = _dot(h.astype(BF16), w_ref[...])


def _inproj_res_kernel(x_ref, y_ref, modp_ref, mod_ref, g_ref, w_ref, xo_ref, o_ref):
    x = x_ref[...] + modp_ref[G_FFN:G_FFN + 1, :] * _load_row_tiled(y_ref)
    xo_ref[...] = x
    h = _rms(x) * g_ref[...] * (1.0 + mod_ref[SC_MIX:SC_MIX + 1, :]) + mod_ref[SH_MIX:SH_MIX + 1, :]
    o_ref[...] = _dot(h.astype(BF16), w_ref[...])


def _inproj(x2, y, mod_prev, mod_l, g, w, seq, tm=256):
    t, d = x2.shape
    n = w.shape[1]
    per_b = seq // tm
    row_spec = pl.BlockSpec((tm, d), lambda i: (i, 0))
    mod_spec = pl.BlockSpec((None, MOD_ROWS, d), lambda i: (i // per_b, 0, 0))
    tail_specs = [mod_spec, pl.BlockSpec((1, d), lambda i: (0, 0)), pl.BlockSpec((d, n), lambda i: (0, 0))]
    proj_spec = pl.BlockSpec((tm, n), lambda i: (i, 0))
    proj_shape = jax.ShapeDtypeStruct((t, n), F32)
    if y is None:
        proj = pl.pallas_call(
            _inproj_kernel, grid=(t // tm,), in_specs=[row_spec] + tail_specs, out_specs=proj_spec,
            out_shape=proj_shape, compiler_params=_cparams(("arbitrary",)), name="norm_inproj",
        )(x2, mod_l, g, w)
        return x2, proj
    return pl.pallas_call(
        _inproj_res_kernel, grid=(t // tm,),
        in_specs=[row_spec, pl.BlockSpec((tm * ROW_TILES, LANES), lambda i: (i, 0)), mod_spec] + tail_specs,
        out_specs=[row_spec, proj_spec], out_shape=[jax.ShapeDtypeStruct((t, d), F32), proj_shape],
        compiler_params=_cparams(("arbitrary",)), name="res_norm_inproj",
    )(x2, y, mod_prev, mod_l, g, w)


OFF_Z, OFF_XBC, OFF_HQ, OFF_HF, OFF_HI, OFF_HG = 0, 512, 1280, 1536, 1792, 2048
OFF_RQ, OFF_RK, OFF_RV, OFF_RG, OFF_DT = 2304, 2560, 2816, 3072, 3328
N_PROJ = 3456
SSD_WIDTH = SSD_HEADS * HEAD_DIM
XBC_WIDTH = SSD_WIDTH + 2 * SSD_GROUPS * HEAD_DIM
HGRN_WIDTH = HGRN_HEADS * HEAD_DIM
RET_WIDTH = RET_HEADS * HEAD_DIM
TAIL = 8
HGRN_DIRECT = 8
HGRN_LEVELS = (8, 16, 32, 64)


def _pair_blockdiag(v_pair, lo_lane):
    return jnp.concatenate([jnp.where(lo_lane, v_pair, 0.0), jnp.where(lo_lane, 0.0, v_pair)], axis=0)


def _head_mean_square(o, bd):
    sq = (o * o).astype(BF16)
    cols = [_dot(sq[:, p * LANES:(p + 1) * LANES], bd) for p in range(o.shape[1] // LANES)]
    return jnp.concatenate(cols, axis=1) if len(cols) > 1 else cols[0]


def _mixer_kernel(layer, proj_ref, convw_ref, convb_ref, dtb_ref, alog_ref, dexp_ref, ssdg_ref,
                  lbraw_ref, hgg_ref, cos_ref, sin_ref, rdec_ref, rdq_ref, rdk_ref, rtot_ref,
                  out_ref, xbuf, s_ssd, s_hg, s_ret):
    C = CHUNK
    j = pl.program_id(1)

    @pl.when(j == 0)
    def _():
        xbuf[0:TAIL, :] = jnp.zeros((TAIL, XBC_WIDTH), F32)
        s_ssd[...] = jnp.zeros_like(s_ssd)
        s_hg[...] = jnp.zeros_like(s_hg)
        s_ret[...] = jnp.zeros_like(s_ret)

    row = lax.broadcasted_iota(jnp.int32, (C, LANES), 0)
    lane = lax.broadcasted_iota(jnp.int32, (C, LANES), 1)
    lo_lane = lane < HEAD_DIM
    causal = row >= lane
    bd_mask = jnp.right_shift(row, 6) == jnp.right_shift(lane, 6)
    bd_mean = jnp.where(bd_mask, 1.0 / HEAD_DIM, 0.0).astype(BF16)
    tril = jnp.where(causal, 1.0, 0.0).astype(BF16)

    def cumsum_rows(x):
        hi, mid, lo = _split3(x)
        return _dot(tril, hi) + _dot(tril, mid) + _dot(tril, lo)

    xbuf[TAIL:TAIL + C, :] = proj_ref[:, OFF_XBC:OFF_XBC + XBC_WIDTH]
    conv = convb_ref[...]
    for jj in range(SSD_CONV):
        off = TAIL - (SSD_CONV - 1) + jj
        conv = conv + convw_ref[jj:jj + 1, :] * xbuf[off:off + C, :]
    xbuf[0:TAIL, :] = xbuf[C:C + TAIL, :]
    xc = _silu(conv)
    xs = xc[:, 0:SSD_WIDTH]
    bm = xc[:, SSD_WIDTH:SSD_WIDTH + LANES]
    cm = xc[:, SSD_WIDTH + LANES:SSD_WIDTH + 2 * LANES]

    dt8 = jax.nn.softplus(proj_ref[:, OFF_DT:OFF_DT + LANES] + dtb_ref[...])
    la8 = dt8 * (-jnp.exp(alog_ref[...]))
    cs8 = cumsum_rows(la8)
    cs8t = cs8.T

    bm_b = bm.astype(BF16)
    cm_b = cm.astype(BF16)
    scores_g = [_dot_nt(jnp.where(lo_lane, cm, 0.0).astype(BF16), bm_b),
                _dot_nt(jnp.where(lo_lane, 0.0, cm).astype(BF16), bm_b)]

    o_intra, e1_cols, e2_cols, dt_cols = [], [], [], []
    for p in range(SSD_HEADS // 2):
        a, b = 2 * p, 2 * p + 1
        g = a // (SSD_HEADS // SSD_GROUPS)
        col_a = jnp.broadcast_to(cs8[:, a:a + 1], (C, LANES))
        col_b = jnp.broadcast_to(cs8[:, b:b + 1], (C, LANES))
        cs_pair = jnp.where(lo_lane, col_a, col_b)
        dt_pair = jnp.where(lo_lane, jnp.broadcast_to(dt8[:, a:a + 1], (C, LANES)),
                            jnp.broadcast_to(dt8[:, b:b + 1], (C, LANES)))
        last = cs_pair[C - 1:C, :]
        e1_cols.append(jnp.exp(cs_pair))
        e2_cols.append(jnp.exp(last - cs_pair))
        dt_cols.append(dt_pair)
        dec_a = jnp.where(causal, jnp.exp(jnp.minimum(col_a - cs8t[a:a + 1, :], 0.0)), 0.0)
        dec_b = jnp.where(causal, jnp.exp(jnp.minimum(col_b - cs8t[b:b + 1, :], 0.0)), 0.0)
        pa = (scores_g[g] * dec_a).astype(BF16)
        pb = (scores_g[g] * dec_b).astype(BF16)
        v_pair = xs[:, p * LANES:(p + 1) * LANES] * dt_pair
        o_intra.append(_dot(jnp.concatenate([pa, pb], axis=1),
                            _pair_blockdiag(v_pair, lo_lane).astype(BF16)))
    o_intra = jnp.concatenate(o_intra, axis=1)
    e1 = jnp.concatenate(e1_cols, axis=1)
    e2 = jnp.concatenate(e2_cols, axis=1)
    dtx = jnp.concatenate(dt_cols, axis=1)
    v_all = xs * dtx
    s_prev = s_ssd[...]
    o_inter = e1 * _dot(cm_b, s_prev.astype(BF16))
    y = o_intra + o_inter + dexp_ref[...] * xs
    y = y * _silu(proj_ref[:, OFF_Z:OFF_Z + SSD_WIDTH])
    gw = SSD_WIDTH // SSD_GROUPS
    y_norm = []
    for g in range(SSD_GROUPS):
        yg = y[:, g * gw:(g + 1) * gw]
        y_norm.append(yg * lax.rsqrt(jnp.mean(yg * yg, axis=-1, keepdims=True) + EPS))
    out_ref[:, 0:SSD_WIDTH] = (jnp.concatenate(y_norm, axis=1) * ssdg_ref[...]).astype(out_ref.dtype)
    row_s = lax.broadcasted_iota(jnp.int32, (LANES, SSD_WIDTH), 0)
    lane_s = lax.broadcasted_iota(jnp.int32, (LANES, SSD_WIDTH), 1)
    grp_mask = jnp.right_shift(row_s, 6) == jnp.right_shift(lane_s, 8)
    upd = _dot_tn(bm_b, (v_all * e2).astype(BF16))
    s_ssd[...] = e1[C - 1:C, :] * s_prev + jnp.where(grp_mask, upd, 0.0)

    W2 = HGRN_WIDTH
    row2 = lax.broadcasted_iota(jnp.int32, (C, W2), 0)
    lb_raw = lbraw_ref[...]
    lb_e = jnp.exp(lb_raw - jnp.max(lb_raw, axis=0, keepdims=True))
    lb_soft = lb_e / jnp.sum(lb_e, axis=0, keepdims=True)
    lb = jnp.sum(lb_soft[0:layer + 1, :], axis=0, keepdims=True) - lb_soft[0:1, :]
    forget = lb + (1.0 - lb) * jax.nn.sigmoid(proj_ref[:, OFF_HF:OFF_HF + W2])
    fc = jnp.maximum(forget, GATE_FLOOR)
    kk = 1.0 - forget
    hq = proj_ref[:, OFF_HQ:OFF_HQ + W2]
    hv = proj_ref[:, OFF_HI:OFF_HI + W2]
    cum = cumsum_rows(jnp.log(fc))
    npair = W2 // LANES

    def pair_scores_to_out(qm, km, mask):
        cols = []
        km_b = km.astype(BF16)
        for p in range(npair):
            qp = qm[:, p * LANES:(p + 1) * LANES]
            kp = km_b[:, p * LANES:(p + 1) * LANES]
            sa = _dot_nt(jnp.where(lo_lane, qp, 0.0).astype(BF16), kp)
            sb = _dot_nt(jnp.where(lo_lane, 0.0, qp).astype(BF16), kp)
            pa = jnp.where(mask, sa, 0.0).astype(BF16)
            pb = jnp.where(mask, sb, 0.0).astype(BF16)
            vbd = _pair_blockdiag(hv[:, p * LANES:(p + 1) * LANES], lo_lane).astype(BF16)
            cols.append(_dot(jnp.concatenate([pa, pb], axis=1), vbd))
        return jnp.concatenate(cols, axis=1)

    sub = jnp.bitwise_and(row2, HGRN_DIRECT - 1)
    bd2 = jnp.where(jnp.right_shift(lax.broadcasted_iota(jnp.int32, (W2, W2), 0), 6)
                    == jnp.right_shift(lax.broadcasted_iota(jnp.int32, (W2, W2), 1), 6), 1.0, 0.0).astype(BF16)
    o_h = _dot((hq * kk).astype(BF16), bd2) * hv
    dprod = jnp.ones((C, W2), F32)
    for dlt in range(1, HGRN_DIRECT):
        f_shift = fc if dlt == 1 else pltpu.roll(fc, dlt - 1, 0)
        dprod = jnp.where(sub >= dlt, dprod * f_shift, 0.0)
        pterm = hq * dprod * pltpu.roll(kk, dlt, 0)
        o_h = o_h + _dot(pterm.astype(BF16), bd2) * pltpu.roll(hv, dlt, 0)
    for m in HGRN_LEVELS:
        nb = C // (2 * m)
        cum3 = cum.reshape(nb, 2 * m, W2)
        ref = jnp.broadcast_to(cum3[:, m - 1:m, :], (nb, 2 * m, W2)).reshape(C, W2)
        right = jnp.bitwise_and(row2, m) != 0
        e = jnp.exp(jnp.where(right, cum - ref, ref - cum))
        qm = jnp.where(right, hq * e, 0.0)
        km = jnp.where(right, 0.0, kk * e)
        sh = int(math.log2(2 * m))
        same_block = jnp.right_shift(row, sh) == jnp.right_shift(lane, sh)
        o_h = o_h + pair_scores_to_out(qm, km, same_block)
    q_in = (hq * jnp.exp(cum)).astype(BF16)
    last2 = cum[C - 1:C, :]
    k_end = (kk * jnp.exp(last2 - cum)).astype(BF16)
    tot2 = jnp.exp(last2)
    hv_b = hv.astype(BF16)
    inter = []
    for p in range(npair):
        sl = slice(p * LANES, (p + 1) * LANES)
        st = s_hg[p]
        inter.append(_dot_nt(q_in[:, sl], st.astype(BF16)))
        upd = _dot_tn(hv_b[:, sl], k_end[:, sl])
        s_hg[p] = st * tot2[:, sl] + jnp.where(bd_mask, upd, 0.0)
    o_h = o_h + jnp.concatenate(inter, axis=1)
    ms = _head_mean_square(o_h, bd_mean)
    o_h = o_h * lax.rsqrt(ms + EPS) * hgg_ref[...]
    o_h = o_h * jax.nn.sigmoid(proj_ref[:, OFF_HG:OFF_HG + W2])
    out_ref[:, SSD_WIDTH:SSD_WIDTH + W2] = o_h.astype(out_ref.dtype)

    W3 = RET_WIDTH
    lane3 = lax.broadcasted_iota(jnp.int32, (C, W3), 1)
    first_half = jnp.bitwise_and(lane3, HEAD_DIM // 2) == 0
    cosv = cos_ref[...]
    sinv = sin_ref[...]

    def rotary(xr):
        swapped = jnp.where(first_half, pltpu.roll(xr, W3 - HEAD_DIM // 2, 1),
                            pltpu.roll(xr, HEAD_DIM // 2, 1))
        return xr * cosv + swapped * sinv

    qr = rotary(proj_ref[:, OFF_RQ:OFF_RQ + W3])
    kr = rotary(proj_ref[:, OFF_RK:OFF_RK + W3]) * (HEAD_DIM ** -0.5)
    rv = proj_ref[:, OFF_RV:OFF_RV + W3]
    kr_b = kr.astype(BF16)
    q_in = (qr * rdq_ref[...]).astype(BF16)
    k_end = (kr * rdk_ref[...]).astype(BF16)
    rv_b = rv.astype(BF16)
    o_cols = []
    for p in range(W3 // LANES):
        sl = slice(p * LANES, (p + 1) * LANES)
        qp = qr[:, sl]
        sa = _dot_nt(jnp.where(lo_lane, qp, 0.0).astype(BF16), kr_b[:, sl]) * rdec_ref[2 * p]
        sb = _dot_nt(jnp.where(lo_lane, 0.0, qp).astype(BF16), kr_b[:, sl]) * rdec_ref[2 * p + 1]
        vbd = _pair_blockdiag(rv[:, sl], lo_lane).astype(BF16)
        o_p = _dot(jnp.concatenate([sa.astype(BF16), sb.astype(BF16)], axis=1), vbd)
        st = s_ret[p]
        o_p = o_p + _dot(q_in[:, sl], st.astype(BF16))
        upd = _dot_tn(k_end[:, sl], rv_b[:, sl])
        s_ret[p] = st * rtot_ref[:, sl] + jnp.where(bd_mask, upd, 0.0)
        o_cols.append(o_p)
    o_r = jnp.concatenate(o_cols, axis=1)
    ms = _head_mean_square(o_r, bd_mean)
    o_r = o_r * lax.rsqrt(ms + EPS) * _silu(proj_ref[:, OFF_RG:OFF_RG + W3])
    out_ref[:, SSD_WIDTH + W2:SSD_WIDTH + W2 + W3] = o_r.astype(out_ref.dtype)


def _mixer(proj, layer, batch, seq, conv_w, conv_b, dtb, alog, dexp, ssdg, lbraw, hgg,
           cos_t, sin_t, rdec, rdq, rdk, rtot):
    t = proj.shape[0]
    d_mix = SSD_WIDTH + HGRN_WIDTH + RET_WIDTH
    nj = seq // CHUNK
    const2 = lambda b, j: (0, 0)
    return pl.pallas_call(
        functools.partial(_mixer_kernel, layer),
        grid=(batch, nj),
        in_specs=[
            pl.BlockSpec((CHUNK, N_PROJ), lambda b, j: (b * nj + j, 0)),
            pl.BlockSpec(conv_w.shape, const2),
            pl.BlockSpec(conv_b.shape, const2),
            pl.BlockSpec(dtb.shape, const2),
            pl.BlockSpec(alog.shape, const2),
            pl.BlockSpec(dexp.shape, const2),
            pl.BlockSpec(ssdg.shape, const2),
            pl.BlockSpec(lbraw.shape, const2),
            pl.BlockSpec(hgg.shape, const2),
            pl.BlockSpec((CHUNK, RET_WIDTH), lambda b, j: (j, 0)),
            pl.BlockSpec((CHUNK, RET_WIDTH), lambda b, j: (j, 0)),
            pl.BlockSpec(rdec.shape, lambda b, j: (0, 0, 0)),
            pl.BlockSpec(rdq.shape, const2),
            pl.BlockSpec(rdk.shape, const2),
            pl.BlockSpec(rtot.shape, const2),
        ],
        out_specs=pl.BlockSpec((CHUNK, d_mix), lambda b, j: (b * nj + j, 0)),
        out_shape=jax.ShapeDtypeStruct((t, d_mix), BF16),
        scratch_shapes=[
            pltpu.VMEM((TAIL + CHUNK, XBC_WIDTH), F32),
            pltpu.VMEM((LANES, SSD_WIDTH), F32),
            pltpu.VMEM((HGRN_WIDTH // LANES, LANES, LANES), F32),
            pltpu.VMEM((RET_WIDTH // LANES, LANES, LANES), F32),
        ],
        compiler_params=_cparams(("arbitrary", "arbitrary")),
        name="token_mixers",
    )(proj, conv_w, conv_b, dtb, alog, dexp, ssdg, lbraw, hgg, cos_t, sin_t, rdec, rdq, rdk, rtot)


ROW_TILES = 8


def _store_row_tiled(ref, val):
    m = val.shape[0]
    for k in range(ROW_TILES):
        ref[pl.ds(k, m, stride=ROW_TILES), :] = val[:, k * LANES:(k + 1) * LANES]


def _load_row_tiled(ref):
    m = ref.shape[0] // ROW_TILES
    return jnp.concatenate([ref[pl.ds(k, m, stride=ROW_TILES), :] for k in range(ROW_TILES)], axis=1)


R_E1, R_E2, R_W1, R_W2 = range(4)


def _outproj_kernel(x_ref, mixed_ref, mod_ref, g_ref, w_ref, wr_ref, br_ref,
                    xo_ref, hp_ref, route_ref):
    x = x_ref[...] + mod_ref[G_MIX:G_MIX + 1, :] * _dot(mixed_ref[...], w_ref[...])
    xo_ref[...] = x
    h = _rms(x) * g_ref[...] * (1.0 + mod_ref[SC_FFN:SC_FFN + 1, :]) + mod_ref[SH_FFN:SH_FFN + 1, :]
    _store_row_tiled(hp_ref, h)
    logits = jnp.dot(h, wr_ref[...], precision=HIGHEST, preferred_element_type=F32) + br_ref[...]
    lane = lax.broadcasted_iota(jnp.int32, logits.shape, 1).astype(F32)
    neg = -jnp.inf
    big = float(ROUTER_LANES)
    is_grp = lane < N_GROUPS
    gl = jnp.where(is_grp, logits, neg)
    gmax = jnp.max(gl, axis=-1, keepdims=True)
    g_idx = jnp.min(jnp.where(gl == gmax, lane, big), axis=-1, keepdims=True)
    p_grp = 1.0 / jnp.sum(jnp.where(is_grp, jnp.exp(gl - gmax), 0.0), axis=-1, keepdims=True)
    first = N_GROUPS + g_idx * EXPERTS_PER_GROUP
    valid = (lane >= first) & (lane < first + EXPERTS_PER_GROUP)
    el = jnp.where(valid, logits, neg)
    m1 = jnp.max(el, axis=-1, keepdims=True)
    i1 = jnp.min(jnp.where(el == m1, lane, big), axis=-1, keepdims=True)
    el2 = jnp.where(lane == i1, neg, el)
    m2 = jnp.max(el2, axis=-1, keepdims=True)
    i2 = jnp.min(jnp.where(el2 == m2, lane, big), axis=-1, keepdims=True)
    e = jnp.exp(m2 - m1)
    w1 = p_grp / (1.0 + e)
    w2 = p_grp * e / (1.0 + e)
    route_ref[...] = jnp.where(lane == R_E1, i1 - N_GROUPS,
                               jnp.where(lane == R_E2, i2 - N_GROUPS,
                                         jnp.where(lane == R_W1, w1, jnp.where(lane == R_W2, w2, 0.0))))


def _outproj(x2, mixed, mod_l, g, w, wr, br, seq, tm=512):
    t, d = x2.shape
    per_b = seq // tm
    return pl.pallas_call(
        _outproj_kernel,
        grid=(t // tm,),
        in_specs=[
            pl.BlockSpec((tm, d), lambda i: (i, 0)),
            pl.BlockSpec((tm, mixed.shape[1]), lambda i: (i, 0)),
            pl.BlockSpec((None, MOD_ROWS, d), lambda i: (i // per_b, 0, 0)),
            pl.BlockSpec((1, d), lambda i: (0, 0)),
            pl.BlockSpec(w.shape, lambda i: (0, 0)),
            pl.BlockSpec(wr.shape, lambda i: (0, 0)),
            pl.BlockSpec(br.shape, lambda i: (0, 0)),
        ],
        out_specs=[
            pl.BlockSpec((tm, d), lambda i: (i, 0)),
            pl.BlockSpec((tm * ROW_TILES, LANES), lambda i: (i, 0)),
            pl.BlockSpec((tm, ROUTER_LANES), lambda i: (i, 0)),
        ],
        out_shape=[
            jax.ShapeDtypeStruct((t, d), F32),
            jax.ShapeDtypeStruct((t * ROW_TILES, LANES), F32),
            jax.ShapeDtypeStruct((t, ROUTER_LANES), F32),
        ],
        compiler_params=_cparams(("arbitrary",)),
        name="outproj_router",
    )(x2, mixed, mod_l, g, w, wr, br)


MOE_CHUNK = 2048
MOE_TILE = 128
SEG_ALIGN = 8
COMBINE_STEPS = 4
SLOTS_PAD = 2 * MOE_CHUNK + N_EXPERTS * SEG_ALIGN
RANK_BLOCK = 256
M_OFF, M_CNT = 0, 1


def _rank_kernel(route_ref, pos_ref, meta_ref):
    tb = route_ref.shape[0]
    lane = lax.broadcasted_iota(jnp.int32, (RANK_BLOCK, ROUTER_LANES), 1).astype(F32)
    r_i = lax.broadcasted_iota(jnp.int32, (RANK_BLOCK, RANK_BLOCK), 0)
    c_i = lax.broadcasted_iota(jnp.int32, (RANK_BLOCK, RANK_BLOCK), 1)
    strict_tril = jnp.where(r_i > c_i, 1.0, 0.0).astype(BF16)
    carry = jnp.zeros((1, ROUTER_LANES), F32)
    ranks = []
    for b in range(tb // RANK_BLOCK):
        blk = route_ref[b * RANK_BLOCK:(b + 1) * RANK_BLOCK, :]
        onehot = jnp.where((lane == blk[:, R_E1:R_E1 + 1]) | (lane == blk[:, R_E2:R_E2 + 1]), 1.0, 0.0)
        ranks.append(_dot(strict_tril, onehot.astype(BF16)) + carry)
        carry = carry + jnp.sum(onehot, axis=0, keepdims=True)
    cnt = carry
    seg = jnp.floor((cnt + (SEG_ALIGN - 1)) * (1.0 / SEG_ALIGN)) * SEG_ALIGN
    u_r = lax.broadcasted_iota(jnp.int32, (ROUTER_LANES, ROUTER_LANES), 0)
    u_c = lax.broadcasted_iota(jnp.int32, (ROUTER_LANES, ROUTER_LANES), 1)
    strict_triu = jnp.where(u_r < u_c, 1.0, 0.0)
    off = jnp.dot(jnp.broadcast_to(seg, (8, ROUTER_LANES)), strict_triu, precision=HIGHEST,
                  preferred_element_type=F32)[0:1, :]
    for b in range(tb // RANK_BLOCK):
        blk = route_ref[b * RANK_BLOCK:(b + 1) * RANK_BLOCK, :]
        dest = off + ranks[b]
        p1 = jnp.sum(jnp.where(lane == blk[:, R_E1:R_E1 + 1], dest, 0.0), axis=-1, keepdims=True)
        p2 = jnp.sum(jnp.where(lane == blk[:, R_E2:R_E2 + 1], dest, 0.0), axis=-1, keepdims=True)
        pos_ref[b * RANK_BLOCK:(b + 1) * RANK_BLOCK, :] = jnp.where(lane == 0, p1, p2).astype(jnp.int32)
    row8 = lax.broadcasted_iota(jnp.int32, (8, ROUTER_LANES), 0)
    meta_ref[...] = jnp.where(row8 == M_OFF, off, jnp.where(row8 == M_CNT, cnt, 0.0)).astype(jnp.int32)


def _rank(route):
    t = route.shape[0]
    nc = t // MOE_CHUNK
    return pl.pallas_call(
        _rank_kernel,
        grid=(nc,),
        in_specs=[pl.BlockSpec((MOE_CHUNK, ROUTER_LANES), lambda c: (c, 0))],
        out_specs=[pl.BlockSpec((MOE_CHUNK, ROUTER_LANES), lambda c: (c, 0)),
                   pl.BlockSpec((None, 8, ROUTER_LANES), lambda c: (c, 0, 0))],
        out_shape=[jax.ShapeDtypeStruct((t, ROUTER_LANES), jnp.int32),
                   jax.ShapeDtypeStruct((nc, 8, ROUTER_LANES), jnp.int32)],
        compiler_params=_cparams(("arbitrary",)),
        name="slot_rank",
    )(route)


def _moe_kernel(pos1_ref, pos2_ref, w1_ref, w2_ref, off_ref, cnt_ref,
                hp_ref, wg_ref, wu_ref, wd_ref, y_ref, inv_ref, xt_ref, ys_ref):
    c = pl.program_id(0)
    e = pl.program_id(1)
    rt = ROW_TILES
    tb = hp_ref.shape[0] // rt
    tok0 = c * tb
    seg0 = c * N_EXPERTS

    @pl.when(e == 0)
    def _():
        def pads(ee, carry):
            o = off_ref[seg0 + ee]
            n_e = cnt_ref[seg0 + ee]

            def zero(i, carry2):
                inv_ref[o + i] = 0
                return carry2
            lax.fori_loop(n_e, (n_e + SEG_ALIGN - 1) // SEG_ALIGN * SEG_ALIGN, zero, 0)
            return carry
        lax.fori_loop(0, N_EXPERTS, pads, 0)
        n_last = cnt_ref[seg0 + N_EXPERTS - 1]
        total = off_ref[seg0 + N_EXPERTS - 1] + (n_last + SEG_ALIGN - 1) // SEG_ALIGN * SEG_ALIGN

        def zero_tail(i, carry):
            inv_ref[total + i] = 0
            return carry
        lax.fori_loop(0, MOE_TILE, zero_tail, 0)

        def place(i, carry):
            inv_ref[pos1_ref[tok0 + i]] = i
            inv_ref[pos2_ref[tok0 + i]] = i
            return carry
        lax.fori_loop(0, tb, place, 0, unroll=8)

    is_expert = e < N_EXPERTS
    e_c = jnp.minimum(e, N_EXPERTS - 1)
    n = jnp.where(is_expert, cnt_ref[seg0 + e_c], 0)
    base = off_ref[seg0 + e_c]

    @pl.when(n > 0)
    def _():
        wg = wg_ref[...].astype(BF16)
        wu = wu_ref[...].astype(BF16)
        wd = wd_ref[...].astype(BF16)

        def tile(i, carry):
            start = pl.multiple_of(base + i * MOE_TILE, SEG_ALIGN)

            def gather(r, carry2):
                src = pl.multiple_of(inv_ref[start + r] * rt, rt)
                xt_ref[pl.ds(pl.multiple_of(r * rt, rt), rt), :] = hp_ref[pl.ds(src, rt), :]
                return carry2
            lax.fori_loop(0, MOE_TILE, gather, 0, unroll=8)
            xrow = jnp.concatenate([xt_ref[pl.ds(k, MOE_TILE, stride=rt), :] for k in range(rt)], axis=1)
            xb = xrow.astype(BF16)
            hid = (_silu(_dot(xb, wg)) * _dot(xb, wu)).astype(BF16)
            out = _dot(hid, wd)
            for k in range(rt):
                ys_ref[pl.ds(start * rt + k, MOE_TILE, stride=rt), :] = out[:, k * LANES:(k + 1) * LANES]
            return carry
        lax.fori_loop(0, (n + MOE_TILE - 1) // MOE_TILE, tile, 0)

    @pl.when(e >= N_EXPERTS)
    def _():
        tq = y_ref.shape[0] // rt
        t0 = tok0 + (e - N_EXPERTS) * tq

        def combine(i, carry):
            r1 = ys_ref[pl.ds(pl.multiple_of(pos1_ref[t0 + i] * rt, rt), rt), :]
            r2 = ys_ref[pl.ds(pl.multiple_of(pos2_ref[t0 + i] * rt, rt), rt), :]
            y_ref[pl.ds(pl.multiple_of(i * rt, rt), rt), :] = w1_ref[t0 + i] * r1 + w2_ref[t0 + i] * r2
            return carry
        lax.fori_loop(0, tq, combine, 0, unroll=8)


def _moe(hp, pos1, pos2, w1, w2, off, cnt, w_gate, w_up, w_down, layer):
    rt = ROW_TILES
    t = hp.shape[0] // rt
    _, ne, d, f = w_gate.shape
    nc = t // MOE_CHUNK
    tq = MOE_CHUNK // COMBINE_STEPS

    def w_idx(c, e, *_):
        return (layer, jnp.minimum(e, ne - 1), 0, 0)

    grid_spec = pltpu.PrefetchScalarGridSpec(
        num_scalar_prefetch=6,
        grid=(nc, ne + COMBINE_STEPS),
        in_specs=[
            pl.BlockSpec((MOE_CHUNK * rt, LANES), lambda c, e, *_: (c, 0)),
            pl.BlockSpec((None, None, d, f), w_idx),
            pl.BlockSpec((None, None, d, f), w_idx),
            pl.BlockSpec((None, None, f, d), w_idx),
        ],
        out_specs=pl.BlockSpec((tq * rt, LANES),
                               lambda c, e, *_: (c * COMBINE_STEPS + jnp.maximum(e - ne, 0), 0)),
        scratch_shapes=[
            pltpu.SMEM((SLOTS_PAD + MOE_TILE,), jnp.int32),
            pltpu.VMEM((MOE_TILE * rt, LANES), F32),
            pltpu.VMEM(((SLOTS_PAD + MOE_TILE) * rt, LANES), F32),
        ],
    )
    return pl.pallas_call(
        _moe_kernel,
        grid_spec=grid_spec,
        out_shape=jax.ShapeDtypeStruct((t * rt, LANES), F32),
        compiler_params=_cparams(("arbitrary", "arbitrary")),
        name="moe_top2",
    )(pos1, pos2, w1, w2, off, cnt, hp, w_gate, w_up, w_down)


def _final_kernel(x_ref, y_ref, mod_ref, g_ref, o_ref):
    x = x_ref[...] + mod_ref[G_FFN:G_FFN + 1, :] * _load_row_tiled(y_ref)
    o_ref[...] = _rms(x) * g_ref[...]


def _final(x2, y, mod_l, g, seq, tm=512):
    t, d = x2.shape
    per_b = seq // tm
    return pl.pallas_call(
        _final_kernel,
        grid=(t // tm,),
        in_specs=[
            pl.BlockSpec((tm, d), lambda i: (i, 0)),
            pl.BlockSpec((tm * ROW_TILES, LANES), lambda i: (i, 0)),
            pl.BlockSpec((None, MOD_ROWS, d), lambda i: (i // per_b, 0, 0)),
            pl.BlockSpec((1, d), lambda i: (0, 0)),
        ],
        out_specs=pl.BlockSpec((tm, d), lambda i: (i, 0)),
        out_shape=jax.ShapeDtypeStruct((t, d), F32),
        compiler_params=_cparams(("arbitrary",)),
        name="final_norm",
    )(x2, y, mod_l, g)


def _retention_tables(seq):
    half = HEAD_DIM // 2
    inv_freq = ROPE_BASE ** (-jnp.arange(half, dtype=F32) / half)
    ang = jnp.arange(seq, dtype=F32)[:, None] * inv_freq[None, :]
    cos, sin = jnp.cos(ang), jnp.sin(ang)
    cos_t = jnp.tile(jnp.concatenate([cos, cos], axis=-1), (1, RET_HEADS))
    sin_t = jnp.tile(jnp.concatenate([-sin, sin], axis=-1), (1, RET_HEADS))
    log_gamma = jnp.log1p(-jnp.exp2(-5.0 - jnp.arange(RET_HEADS, dtype=F32)))
    tt = jnp.arange(CHUNK, dtype=F32)
    diff = tt[:, None] - tt[None, :]
    rdec = jnp.where(diff >= 0, jnp.exp(jnp.maximum(diff, 0.0)[None] * log_gamma[:, None, None]), 0.0)
    lg_lane = jnp.repeat(log_gamma, HEAD_DIM)[None, :]
    rdq = jnp.exp((tt[:, None] + 1.0) * lg_lane)
    rdk = jnp.exp((CHUNK - 1.0 - tt[:, None]) * lg_lane)
    rtot = jnp.exp(CHUNK * lg_lane)
    return cos_t, sin_t, rdec, rdq, rdk, rtot


def _pad_lanes(v, width=LANES):
    return jnp.pad(v, ((0, 0), (0, width - v.shape[-1])))


def kernel(x, c, norm_mix_g, norm_ffn_g, final_norm_g, w_ada, b_ada, w_in, conv_w, conv_b, ssd_dt_bias, ssd_a_log, ssd_d, ssd_norm_g, hgrn_lower_bounds, hgrn_norm_g, w_out, w_grp, b_grp, w_exp, b_exp, w_gate, w_up, w_down):
    batch, seq, d = x.shape
    depth = w_in.shape[0]
    t = batch * seq

    c8 = jnp.pad(c, ((0, MOD_ROWS - batch), (0, 0)))
    mod = _ada(c8, w_ada, b_ada)
    mod = mod[:, :batch].reshape(depth, batch, 6, d)
    mod = jnp.pad(mod, ((0, 0), (0, 0), (0, MOD_ROWS - 6), (0, 0)))

    dt0 = SSD_WIDTH + XBC_WIDTH
    w_in_p = jnp.concatenate(
        [w_in[:, :, :dt0], w_in[:, :, dt0 + SSD_HEADS:], w_in[:, :, dt0:dt0 + SSD_HEADS],
         jnp.zeros((depth, d, N_PROJ - w_in.shape[2]), w_in.dtype)], axis=-1).astype(BF16)
    w_out_b = w_out.astype(BF16)
    w_router = _pad_lanes(jnp.concatenate([w_grp, w_exp], axis=-1).reshape(depth * d, -1)).reshape(depth, d, ROUTER_LANES)
    b_router = _pad_lanes(jnp.concatenate([b_grp, b_exp], axis=-1))[:, None, :]
    dtb = _pad_lanes(ssd_dt_bias)[:, None, :]
    alog = _pad_lanes(ssd_a_log)[:, None, :]
    dexp = jnp.repeat(ssd_d, HEAD_DIM, axis=-1)[:, None, :]
    tables = _retention_tables(seq)

    x2 = x.reshape(t, d)
    y = None
    for layer in range(depth):
        x2, proj = _inproj(x2, y, mod[layer - 1] if layer else None, mod[layer],
                           norm_mix_g[layer][None, :], w_in_p[layer], seq)
        mixed = _mixer(proj, layer, batch, seq, conv_w[layer], conv_b[layer][None, :], dtb[layer],
                       alog[layer], dexp[layer], ssd_norm_g[layer][None, :], hgrn_lower_bounds,
                       hgrn_norm_g[layer][None, :], *tables)
        x2, hp, route = _outproj(x2, mixed, mod[layer], norm_ffn_g[layer][None, :], w_out_b[layer],
                                 w_router[layer], b_router[layer], seq)
        pos, meta = _rank(route)
        y = _moe(hp, pos[:, 0], pos[:, 1], route[:, R_W1], route[:, R_W2],
                 meta[:, M_OFF, :N_EXPERTS].reshape(-1), meta[:, M_CNT, :N_EXPERTS].reshape(-1),
                 w_gate, w_up, w_down, layer)
    out = _final(x2, y, mod[depth - 1], final_norm_g[None, :], seq)
    return out.reshape(batch, seq, d)
```

```python
import functools
import math

import jax
import jax.numpy as jnp
import numpy as np
from jax import lax
from jax.experimental import pallas as pl
from jax.experimental.pallas import tpu as pltpu

F32 = jnp.float32
BF16 = jnp.bfloat16
HIGHEST = lax.Precision.HIGHEST

EPS = 1e-6
GATE_FLOOR = 1e-30
ROPE_BASE = 10000.0

HEAD_DIM = 64
SSD_HEADS = 8
SSD_GROUPS = 2
SSD_CONV = 4
HGRN_HEADS = 4
RET_HEADS = 4
N_GROUPS = 4
EXPERTS_PER_GROUP = 8
N_EXPERTS = N_GROUPS * EXPERTS_PER_GROUP
CHUNK = 128
LANES = 128
ROUTER_LANES = 128

SH_MIX, SC_MIX, G_MIX, SH_FFN, SC_FFN, G_FFN = range(6)
MOD_ROWS = 8

VMEM_LIMIT = 56 * 1024 * 1024


def _cparams(sem):
    return pltpu.CompilerParams(dimension_semantics=sem, vmem_limit_bytes=VMEM_LIMIT)


def _silu(x):
    return x * jax.nn.sigmoid(x)


def _rms(x):
    return x * lax.rsqrt(jnp.mean(x * x, axis=-1, keepdims=True) + EPS)


def _split3(x):
    hi = x.astype(BF16)
    r = x - hi.astype(F32)
    mid = r.astype(BF16)
    lo = (r - mid.astype(F32)).astype(BF16)
    return hi, mid, lo


def _dot(a, b):
    return jnp.dot(a, b, preferred_element_type=F32)


def _dot_nt(a, b):
    return lax.dot_general(a, b, (((1,), (1,)), ((), ())), preferred_element_type=F32)


def _dot_tn(a, b):
    return lax.dot_general(a, b, (((0,), (0,)), ((), ())), preferred_element_type=F32)


def _ada_kernel(c_ref, w_ref, b_ref, o_ref):
    c = _silu(c_ref[...])
    o_ref[...] = jnp.dot(c, w_ref[...], precision=HIGHEST, preferred_element_type=F32) + b_ref[...]


def _ada(c8, w_ada, b_ada):
    depth, d, d6 = w_ada.shape
    nb = d6 // d
    return pl.pallas_call(
        _ada_kernel,
        grid=(depth, nb),
        in_specs=[
            pl.BlockSpec((MOD_ROWS, d), lambda l, n: (0, 0)),
            pl.BlockSpec((None, d, d), lambda l, n: (l, 0, n)),
            pl.BlockSpec((None, 1, d), lambda l, n: (l, 0, n)),
        ],
        out_specs=pl.BlockSpec((None, MOD_ROWS, d), lambda l, n: (l, 0, n)),
        out_shape=jax.ShapeDtypeStruct((depth, MOD_ROWS, d6), F32),
        compiler_params=_cparams(("arbitrary", "arbitrary")),
        name="ada_mod",
    )(c8, w_ada, b_ada.reshape(depth, 1, d6))


OFF_Z, OFF_XBC, OFF_HQ, OFF_HF, OFF_HI, OFF_HG = 0, 512, 1280, 1536, 1792, 2048
OFF_RQ, OFF_RK, OFF_RV, OFF_RG, OFF_DT = 2304, 2560, 2816, 3072, 3328
N_PROJ = 3456
SSD_WIDTH = SSD_HEADS * HEAD_DIM
XBC_WIDTH = SSD_WIDTH + 2 * SSD_GROUPS * HEAD_DIM
HGRN_WIDTH = HGRN_HEADS * HEAD_DIM
RET_WIDTH = RET_HEADS * HEAD_DIM
TAIL = 8
HGRN_DIRECT = 8
HGRN_LEVELS = (8, 16, 32, 64)


def _pair_blockdiag(v_pair, lo_lane):
    return jnp.concatenate([jnp.where(lo_lane, v_pair, 0.0), jnp.where(lo_lane, 0.0, v_pair)], axis=0)


def _head_mean_square(o, bd):
    sq = (o * o).astype(BF16)
    cols = [_dot(sq[:, p * LANES:(p + 1) * LANES], bd) for p in range(o.shape[1] // LANES)]
    return jnp.concatenate(cols, axis=1) if len(cols) > 1 else cols[0]


SIDE_POINTS = 9
N_MIXER_PARAMS = 14


def _mixer_kernel(layer, has_res, *refs):
    if has_res:
        x_ref, y_ref, modp_ref, mod_ref, g_ref, w_ref = refs[:6]
        rest = refs[6:]
    else:
        x_ref, mod_ref, g_ref, w_ref = refs[:4]
        rest = refs[4:]
    params, outs = rest[:N_MIXER_PARAMS], rest[N_MIXER_PARAMS:]
    if has_res:
        xo_ref, out_ref, proj_next, proj_cur, xbuf, s_ssd, s_hg, s_ret = outs
    else:
        out_ref, proj_next, proj_cur, xbuf, s_ssd, s_hg, s_ret = outs
    j = pl.program_id(0)
    nb = x_ref.shape[0]

    def reset_carried():
        xbuf[:, 0:TAIL, :] = jnp.zeros((nb, TAIL, XBC_WIDTH), F32)
        s_ssd[...] = jnp.zeros_like(s_ssd)
        s_hg[...] = jnp.zeros_like(s_hg)
        s_ret[...] = jnp.zeros_like(s_ret)

    @pl.when(j == 0)
    def _():
        reset_carried()
        proj_cur[...] = jnp.zeros_like(proj_cur)

    hs = []
    for b in range(nb):
        x = x_ref[b]
        if has_res:
            x = x + modp_ref[b, G_FFN:G_FFN + 1, :] * _load_row_tiled(y_ref.at[b])
            xo_ref[b] = x
        hs.append(_rms(x) * g_ref[...] * (1.0 + mod_ref[b, SC_MIX:SC_MIX + 1, :]) + mod_ref[b, SH_MIX:SH_MIX + 1, :])
    proj = _dot(jnp.concatenate(hs, axis=0).astype(BF16), w_ref[...])
    for b in range(nb):
        proj_next[b] = proj[b * CHUNK:(b + 1) * CHUNK]

    stages = [_mixer_body(layer, proj_cur.at[b], *params, out_ref.at[b], xbuf.at[b], s_ssd.at[b],
                          s_hg.at[b], s_ret.at[b]) for b in range(nb)]
    while stages:
        for gen in list(stages):
            if next(gen, StopIteration) is StopIteration:
                stages.remove(gen)
    proj_cur[...] = proj_next[...]

    @pl.when(j == 0)
    def _():
        reset_carried()


def _mixer_body(layer, proj_ref, convw_ref, convb_ref, dtb_ref, alog_ref, dexp_ref, ssdg_ref,
                lbraw_ref, hgg_ref, cos_ref, sin_ref, rdec_ref, rdq_ref, rdk_ref, rtot_ref,
                out_ref, xbuf, s_ssd, s_hg, s_ret):
    C = CHUNK

    row = lax.broadcasted_iota(jnp.int32, (C, LANES), 0)
    lane = lax.broadcasted_iota(jnp.int32, (C, LANES), 1)
    lo_lane = lane < HEAD_DIM
    causal = row >= lane
    bd_mask = jnp.right_shift(row, 6) == jnp.right_shift(lane, 6)
    bd_mean = jnp.where(bd_mask, 1.0 / HEAD_DIM, 0.0).astype(BF16)
    tril = jnp.where(causal, 1.0, 0.0).astype(BF16)

    def cumsum_rows(x):
        hi, mid, lo = _split3(x)
        return _dot(tril, hi) + _dot(tril, mid) + _dot(tril, lo)

    xbuf[TAIL:TAIL + C, :] = proj_ref[:, OFF_XBC:OFF_XBC + XBC_WIDTH]
    conv = convb_ref[...]
    for jj in range(SSD_CONV):
        off = TAIL - (SSD_CONV - 1) + jj
        conv = conv + convw_ref[jj:jj + 1, :] * xbuf[off:off + C, :]
    xbuf[0:TAIL, :] = xbuf[C:C + TAIL, :]
    xc = _silu(conv)
    xs = xc[:, 0:SSD_WIDTH]
    bm = xc[:, SSD_WIDTH:SSD_WIDTH + LANES]
    cm = xc[:, SSD_WIDTH + LANES:SSD_WIDTH + 2 * LANES]

    dt8 = jax.nn.softplus(proj_ref[:, OFF_DT:OFF_DT + LANES] + dtb_ref[...])
    la8 = dt8 * (-jnp.exp(alog_ref[...]))
    cs8 = cumsum_rows(la8)
    cs8t = cs8.T

    bm_b = bm.astype(BF16)
    cm_b = cm.astype(BF16)
    scores_g = [_dot_nt(jnp.where(lo_lane, cm, 0.0).astype(BF16), bm_b),
                _dot_nt(jnp.where(lo_lane, 0.0, cm).astype(BF16), bm_b)]

    o_intra, e1_cols, e2_cols, dt_cols = [], [], [], []
    for p in range(SSD_HEADS // 2):
        a, b = 2 * p, 2 * p + 1
        g = a // (SSD_HEADS // SSD_GROUPS)
        col_a = jnp.broadcast_to(cs8[:, a:a + 1], (C, LANES))
        col_b = jnp.broadcast_to(cs8[:, b:b + 1], (C, LANES))
        cs_pair = jnp.where(lo_lane, col_a, col_b)
        dt_pair = jnp.where(lo_lane, jnp.broadcast_to(dt8[:, a:a + 1], (C, LANES)),
                            jnp.broadcast_to(dt8[:, b:b + 1], (C, LANES)))
        last = cs_pair[C - 1:C, :]
        e1_cols.append(jnp.exp(cs_pair))
        e2_cols.append(jnp.exp(last - cs_pair))
        dt_cols.append(dt_pair)
        dec_a = jnp.where(causal, jnp.exp(jnp.minimum(col_a - cs8t[a:a + 1, :], 0.0)), 0.0)
        dec_b = jnp.where(causal, jnp.exp(jnp.minimum(col_b - cs8t[b:b + 1, :], 0.0)), 0.0)
        pa = (scores_g[g] * dec_a).astype(BF16)
        pb = (scores_g[g] * dec_b).astype(BF16)
        v_pair = xs[:, p * LANES:(p + 1) * LANES] * dt_pair
        o_intra.append(_dot(jnp.concatenate([pa, pb], axis=1),
                            _pair_blockdiag(v_pair, lo_lane).astype(BF16)))
    o_intra = jnp.concatenate(o_intra, axis=1)
    e1 = jnp.concatenate(e1_cols, axis=1)
    e2 = jnp.concatenate(e2_cols, axis=1)
    dtx = jnp.concatenate(dt_cols, axis=1)
    v_all = xs * dtx
    s_prev = s_ssd[...]
    o_inter = e1 * _dot(cm_b, s_prev.astype(BF16))
    y = o_intra + o_inter + dexp_ref[...] * xs
    y = y * _silu(proj_ref[:, OFF_Z:OFF_Z + SSD_WIDTH])
    gw = SSD_WIDTH // SSD_GROUPS
    grp_mean = jnp.full((gw, gw), 1.0 / gw, BF16)
    ysq = (y * y).astype(BF16)
    ms = jnp.concatenate([_dot(ysq[:, g * gw:(g + 1) * gw], grp_mean) for g in range(SSD_GROUPS)], axis=1)
    out_ref[:, 0:SSD_WIDTH] = (y * lax.rsqrt(ms + EPS) * ssdg_ref[...]).astype(out_ref.dtype)
    row_s = lax.broadcasted_iota(jnp.int32, (LANES, SSD_WIDTH), 0)
    lane_s = lax.broadcasted_iota(jnp.int32, (LANES, SSD_WIDTH), 1)
    grp_mask = jnp.right_shift(row_s, 6) == jnp.right_shift(lane_s, 8)
    upd = _dot_tn(bm_b, (v_all * e2).astype(BF16))
    s_ssd[...] = e1[C - 1:C, :] * s_prev + jnp.where(grp_mask, upd, 0.0)
    yield

    W2 = HGRN_WIDTH
    row2 = lax.broadcasted_iota(jnp.int32, (C, W2), 0)
    lb_raw = lbraw_ref[...]
    lb_e = jnp.exp(lb_raw - jnp.max(lb_raw, axis=0, keepdims=True))
    lb_soft = lb_e / jnp.sum(lb_e, axis=0, keepdims=True)
    lb = jnp.sum(lb_soft[0:layer + 1, :], axis=0, keepdims=True) - lb_soft[0:1, :]
    forget = lb + (1.0 - lb) * jax.nn.sigmoid(proj_ref[:, OFF_HF:OFF_HF + W2])
    fc = jnp.maximum(forget, GATE_FLOOR)
    kk = 1.0 - forget
    hq = proj_ref[:, OFF_HQ:OFF_HQ + W2]
    hv = proj_ref[:, OFF_HI:OFF_HI + W2]
    cum = cumsum_rows(jnp.log(fc))
    npair = W2 // LANES

    def pair_scores_to_out(qm, km, mask):
        cols = []
        km_b = km.astype(BF16)
        for p in range(npair):
            qp = qm[:, p * LANES:(p + 1) * LANES]
            kp = km_b[:, p * LANES:(p + 1) * LANES]
            sa = _dot_nt(jnp.where(lo_lane, qp, 0.0).astype(BF16), kp)
            sb = _dot_nt(jnp.where(lo_lane, 0.0, qp).astype(BF16), kp)
            pa = jnp.where(mask, sa, 0.0).astype(BF16)
            pb = jnp.where(mask, sb, 0.0).astype(BF16)
            vbd = _pair_blockdiag(hv[:, p * LANES:(p + 1) * LANES], lo_lane).astype(BF16)
            cols.append(_dot(jnp.concatenate([pa, pb], axis=1), vbd))
        return jnp.concatenate(cols, axis=1)

    sub = jnp.bitwise_and(row2, HGRN_DIRECT - 1)
    bd2 = jnp.where(jnp.right_shift(lax.broadcasted_iota(jnp.int32, (W2, W2), 0), 6)
                    == jnp.right_shift(lax.broadcasted_iota(jnp.int32, (W2, W2), 1), 6), 1.0, 0.0).astype(BF16)
    o_h = _dot((hq * kk).astype(BF16), bd2) * hv

    def prev_row(a):
        return pltpu.roll(a.reshape(C // HGRN_DIRECT, HGRN_DIRECT, W2), 1, 1).reshape(C, W2)

    not_first = sub != 0
    kdec = kk
    vprev = hv
    for dlt in range(1, HGRN_DIRECT):
        kdec = jnp.where(not_first, fc * prev_row(kdec), 0.0)
        vprev = prev_row(vprev)
        o_h = o_h + _dot((hq * kdec).astype(BF16), bd2) * vprev
    yield
    for m in HGRN_LEVELS:
        nb = C // (2 * m)
        cum3 = cum.reshape(nb, 2 * m, W2)
        ref = jnp.broadcast_to(cum3[:, m - 1:m, :], (nb, 2 * m, W2)).reshape(C, W2)
        right = jnp.bitwise_and(row2, m) != 0
        e = jnp.exp(jnp.where(right, cum - ref, ref - cum))
        qm = jnp.where(right, hq * e, 0.0)
        km = jnp.where(right, 0.0, kk * e)
        sh = int(math.log2(2 * m))
        same_block = jnp.right_shift(row, sh) == jnp.right_shift(lane, sh)
        o_h = o_h + pair_scores_to_out(qm, km, same_block)
        yield
    q_in = (hq * jnp.exp(cum)).astype(BF16)
    last2 = cum[C - 1:C, :]
    k_end = (kk * jnp.exp(last2 - cum)).astype(BF16)
    tot2 = jnp.exp(last2)
    hv_b = hv.astype(BF16)
    inter = []
    for p in range(npair):
        sl = slice(p * LANES, (p + 1) * LANES)
        st = s_hg[p]
        inter.append(_dot_nt(q_in[:, sl], st.astype(BF16)))
        upd = _dot_tn(hv_b[:, sl], k_end[:, sl])
        s_hg[p] = st * tot2[:, sl] + jnp.where(bd_mask, upd, 0.0)
    o_h = o_h + jnp.concatenate(inter, axis=1)
    ms = _head_mean_square(o_h, bd_mean)
    o_h = o_h * lax.rsqrt(ms + EPS) * hgg_ref[...]
    o_h = o_h * jax.nn.sigmoid(proj_ref[:, OFF_HG:OFF_HG + W2])
    out_ref[:, SSD_WIDTH:SSD_WIDTH + W2] = o_h.astype(out_ref.dtype)
    yield

    W3 = RET_WIDTH
    lane3 = lax.broadcasted_iota(jnp.int32, (C, W3), 1)
    first_half = jnp.bitwise_and(lane3, HEAD_DIM // 2) == 0
    cosv = cos_ref[...]
    sinv = sin_ref[...]

    def rotary(xr):
        swapped = jnp.where(first_half, pltpu.roll(xr, W3 - HEAD_DIM // 2, 1),
                            pltpu.roll(xr, HEAD_DIM // 2, 1))
        return xr * cosv + swapped * sinv

    qr = rotary(proj_ref[:, OFF_RQ:OFF_RQ + W3])
    kr = rotary(proj_ref[:, OFF_RK:OFF_RK + W3]) * (HEAD_DIM ** -0.5)
    rv = proj_ref[:, OFF_RV:OFF_RV + W3]
    kr_b = kr.astype(BF16)
    q_in = (qr * rdq_ref[...]).astype(BF16)
    k_end = (kr * rdk_ref[...]).astype(BF16)
    rv_b = rv.astype(BF16)
    o_cols = []
    for p in range(W3 // LANES):
        sl = slice(p * LANES, (p + 1) * LANES)
        qp = qr[:, sl]
        sa = _dot_nt(jnp.where(lo_lane, qp, 0.0).astype(BF16), kr_b[:, sl]) * rdec_ref[2 * p]
        sb = _dot_nt(jnp.where(lo_lane, 0.0, qp).astype(BF16), kr_b[:, sl]) * rdec_ref[2 * p + 1]
        vbd = _pair_blockdiag(rv[:, sl], lo_lane).astype(BF16)
        o_p = _dot(jnp.concatenate([sa.astype(BF16), sb.astype(BF16)], axis=1), vbd)
        st = s_ret[p]
        o_p = o_p + _dot(q_in[:, sl], st.astype(BF16))
        upd = _dot_tn(k_end[:, sl], rv_b[:, sl])
        s_ret[p] = st * rtot_ref[:, sl] + jnp.where(bd_mask, upd, 0.0)
        o_cols.append(o_p)
        yield
    o_r = jnp.concatenate(o_cols, axis=1)
    ms = _head_mean_square(o_r, bd_mean)
    o_r = o_r * lax.rsqrt(ms + EPS) * _silu(proj_ref[:, OFF_RG:OFF_RG + W3])
    out_ref[:, SSD_WIDTH + W2:SSD_WIDTH + W2 + W3] = o_r.astype(out_ref.dtype)


def _mixer(x2, y, mod_prev, mod_l, g, w, layer, batch, seq, conv_w, conv_b, dtb, alog, dexp, ssdg, lbraw, hgg,
           cos_t, sin_t, rdec, rdq, rdk, rtot):
    t, d = x2.shape
    d_mix = SSD_WIDTH + HGRN_WIDTH + RET_WIDTH
    nj = seq // CHUNK
    const2 = lambda j: (0, 0)
    has_res = y is not None
    proj_idx = lambda j: (0, jnp.minimum(j, nj - 1), 0)
    mix_idx = lambda j: (0, jnp.maximum(j - 1, 0), 0)
    x_spec = pl.BlockSpec((batch, CHUNK, d), proj_idx)
    mod_spec = pl.BlockSpec((batch, MOD_ROWS, d), lambda j: (0, 0, 0))
    head_specs = [x_spec]
    head_args = [x2.reshape(batch, seq, d)]
    if has_res:
        head_specs += [pl.BlockSpec((batch, CHUNK * ROW_TILES, LANES), proj_idx), mod_spec]
        head_args += [y.reshape(batch, seq * ROW_TILES, LANES), mod_prev]
    head_specs += [mod_spec, pl.BlockSpec((1, d), const2), pl.BlockSpec(w.shape, const2)]
    head_args += [mod_l, g, w]
    table_spec = pl.BlockSpec((CHUNK, RET_WIDTH), lambda j: (jnp.maximum(j - 1, 0), 0))
    mixed_spec = pl.BlockSpec((batch, CHUNK, d_mix), mix_idx)
    mixed_shape = jax.ShapeDtypeStruct((batch, seq, d_mix), BF16)
    res = pl.pallas_call(
        functools.partial(_mixer_kernel, layer, has_res),
        grid=(nj + 1,),
        in_specs=head_specs + [
            pl.BlockSpec(conv_w.shape, const2),
            pl.BlockSpec(conv_b.shape, const2),
            pl.BlockSpec(dtb.shape, const2),
            pl.BlockSpec(alog.shape, const2),
            pl.BlockSpec(dexp.shape, const2),
            pl.BlockSpec(ssdg.shape, const2),
            pl.BlockSpec(lbraw.shape, const2),
            pl.BlockSpec(hgg.shape, const2),
            table_spec,
            table_spec,
            pl.BlockSpec(rdec.shape, lambda j: (0, 0, 0)),
            pl.BlockSpec(rdq.shape, const2),
            pl.BlockSpec(rdk.shape, const2),
            pl.BlockSpec(rtot.shape, const2),
        ],
        out_specs=[x_spec, mixed_spec] if has_res else mixed_spec,
        out_shape=[jax.ShapeDtypeStruct((batch, seq, d), F32), mixed_shape] if has_res else mixed_shape,
        scratch_shapes=[
            pltpu.VMEM((batch, CHUNK, N_PROJ), F32),
            pltpu.VMEM((batch, CHUNK, N_PROJ), F32),
            pltpu.VMEM((batch, TAIL + CHUNK, XBC_WIDTH), F32),
            pltpu.VMEM((batch, LANES, SSD_WIDTH), F32),
            pltpu.VMEM((batch, HGRN_WIDTH // LANES, LANES, LANES), F32),
            pltpu.VMEM((batch, RET_WIDTH // LANES, LANES, LANES), F32),
        ],
        compiler_params=_cparams(("arbitrary",)),
        name="norm_inproj_mixers",
    )(*head_args, conv_w, conv_b, dtb, alog, dexp, ssdg, lbraw, hgg, cos_t, sin_t, rdec, rdq, rdk, rtot)
    if has_res:
        return res[0].reshape(t, d), res[1].reshape(t, d_mix)
    return x2, res.reshape(t, d_mix)


ROW_TILES = 8


def _store_row_tiled(ref, val):
    m = val.shape[0]
    for k in range(ROW_TILES):
        ref[pl.ds(k, m, stride=ROW_TILES), :] = val[:, k * LANES:(k + 1) * LANES]


def _load_row_tiled(ref):
    m = ref.shape[0] // ROW_TILES
    return jnp.concatenate([ref[pl.ds(k, m, stride=ROW_TILES), :] for k in range(ROW_TILES)], axis=1)


R_E1, R_E2, R_W1, R_W2 = range(4)


def _outproj_kernel(x_ref, mixed_ref, mod_ref, g_ref, w_ref, wr_ref, br_ref,
                    xo_ref, hp_ref, route_ref):
    x = x_ref[...] + mod_ref[G_MIX:G_MIX + 1, :] * _dot(mixed_ref[...], w_ref[...])
    xo_ref[...] = x
    h = _rms(x) * g_ref[...] * (1.0 + mod_ref[SC_FFN:SC_FFN + 1, :]) + mod_ref[SH_FFN:SH_FFN + 1, :]
    _store_row_tiled(hp_ref, h)
    h_hi = h.astype(BF16)
    h_lo = (h - h_hi.astype(F32)).astype(BF16)
    logits = (_dot(h_hi, wr_ref[0]) + _dot(h_lo, wr_ref[0]) + _dot(h_hi, wr_ref[1])) + br_ref[...]
    lane = lax.broadcasted_iota(jnp.int32, logits.shape, 1).astype(F32)
    neg = -jnp.inf
    big = float(ROUTER_LANES)
    is_grp = lane < N_GROUPS
    gl = jnp.where(is_grp, logits, neg)
    gmax = jnp.max(gl, axis=-1, keepdims=True)
    g_idx = jnp.min(jnp.where(gl == gmax, lane, big), axis=-1, keepdims=True)
    p_grp = 1.0 / jnp.sum(jnp.where(is_grp, jnp.exp(gl - gmax), 0.0), axis=-1, keepdims=True)
    first = N_GROUPS + g_idx * EXPERTS_PER_GROUP
    valid = (lane >= first) & (lane < first + EXPERTS_PER_GROUP)
    el = jnp.where(valid, logits, neg)
    m1 = jnp.max(el, axis=-1, keepdims=True)
    i1 = jnp.min(jnp.where(el == m1, lane, big), axis=-1, keepdims=True)
    el2 = jnp.where(lane == i1, neg, el)
    m2 = jnp.max(el2, axis=-1, keepdims=True)
    i2 = jnp.min(jnp.where(el2 == m2, lane, big), axis=-1, keepdims=True)
    e = jnp.exp(m2 - m1)
    w1 = p_grp / (1.0 + e)
    w2 = p_grp * e / (1.0 + e)
    route_ref[...] = jnp.where(lane == R_E1, i1 - N_GROUPS,
                               jnp.where(lane == R_E2, i2 - N_GROUPS,
                                         jnp.where(lane == R_W1, w1, jnp.where(lane == R_W2, w2, 0.0))))


def _outproj(x2, mixed, mod_l, g, w, wr, br, seq, tm=512):
    t, d = x2.shape
    per_b = seq // tm
    return pl.pallas_call(
        _outproj_kernel,
        grid=(t // tm,),
        in_specs=[
            pl.BlockSpec((tm, d), lambda i: (i, 0)),
            pl.BlockSpec((tm, mixed.shape[1]), lambda i: (i, 0)),
            pl.BlockSpec((None, MOD_ROWS, d), lambda i: (i // per_b, 0, 0)),
            pl.BlockSpec((1, d), lambda i: (0, 0)),
            pl.BlockSpec(w.shape, lambda i: (0, 0)),
            pl.BlockSpec(wr.shape, lambda i: (0, 0, 0)),
            pl.BlockSpec(br.shape, lambda i: (0, 0)),
        ],
        out_specs=[
            pl.BlockSpec((tm, d), lambda i: (i, 0)),
            pl.BlockSpec((tm * ROW_TILES, LANES), lambda i: (i, 0)),
            pl.BlockSpec((tm, ROUTER_LANES), lambda i: (i, 0)),
        ],
        out_shape=[
            jax.ShapeDtypeStruct((t, d), F32),
            jax.ShapeDtypeStruct((t * ROW_TILES, LANES), F32),
            jax.ShapeDtypeStruct((t, ROUTER_LANES), F32),
        ],
        compiler_params=_cparams(("arbitrary",)),
        name="outproj_router",
    )(x2, mixed, mod_l, g, w, wr, br)


MOE_CHUNK = 2048
MOE_TILE = 128
SEG_ALIGN = 8
COMBINE_STEPS = 4
SLOTS_PAD = 2 * MOE_CHUNK + N_EXPERTS * SEG_ALIGN
RANK_BLOCK = 256
M_OFF, M_CNT = 0, 1


def _rank_kernel(route_ref, pos_ref, meta_ref):
    tb = route_ref.shape[0]
    lane = lax.broadcasted_iota(jnp.int32, (RANK_BLOCK, ROUTER_LANES), 1).astype(F32)
    r_i = lax.broadcasted_iota(jnp.int32, (RANK_BLOCK, RANK_BLOCK), 0)
    c_i = lax.broadcasted_iota(jnp.int32, (RANK_BLOCK, RANK_BLOCK), 1)
    strict_tril = jnp.where(r_i > c_i, 1.0, 0.0).astype(BF16)
    carry = jnp.zeros((1, ROUTER_LANES), F32)
    ranks = []
    for b in range(tb // RANK_BLOCK):
        blk = route_ref[b * RANK_BLOCK:(b + 1) * RANK_BLOCK, :]
        onehot = jnp.where((lane == blk[:, R_E1:R_E1 + 1]) | (lane == blk[:, R_E2:R_E2 + 1]), 1.0, 0.0)
        ranks.append(_dot(strict_tril, onehot.astype(BF16)) + carry)
        carry = carry + jnp.sum(onehot, axis=0, keepdims=True)
    cnt = carry
    seg = jnp.floor((cnt + (SEG_ALIGN - 1)) * (1.0 / SEG_ALIGN)) * SEG_ALIGN
    u_r = lax.broadcasted_iota(jnp.int32, (ROUTER_LANES, ROUTER_LANES), 0)
    u_c = lax.broadcasted_iota(jnp.int32, (ROUTER_LANES, ROUTER_LANES), 1)
    strict_triu = jnp.where(u_r < u_c, 1.0, 0.0)
    off = jnp.dot(jnp.broadcast_to(seg, (8, ROUTER_LANES)), strict_triu, precision=HIGHEST,
                  preferred_element_type=F32)[0:1, :]
    for b in range(tb // RANK_BLOCK):
        blk = route_ref[b * RANK_BLOCK:(b + 1) * RANK_BLOCK, :]
        dest = off + ranks[b]
        p1 = jnp.sum(jnp.where(lane == blk[:, R_E1:R_E1 + 1], dest, 0.0), axis=-1, keepdims=True)
        p2 = jnp.sum(jnp.where(lane == blk[:, R_E2:R_E2 + 1], dest, 0.0), axis=-1, keepdims=True)
        pos_ref[b * RANK_BLOCK:(b + 1) * RANK_BLOCK, :] = jnp.where(lane == 0, p1, p2).astype(jnp.int32)
    row8 = lax.broadcasted_iota(jnp.int32, (8, ROUTER_LANES), 0)
    meta_ref[...] = jnp.where(row8 == M_OFF, off, jnp.where(row8 == M_CNT, cnt, 0.0)).astype(jnp.int32)


def _rank(route):
    t = route.shape[0]
    nc = t // MOE_CHUNK
    return pl.pallas_call(
        _rank_kernel,
        grid=(nc,),
        in_specs=[pl.BlockSpec((MOE_CHUNK, ROUTER_LANES), lambda c: (c, 0))],
        out_specs=[pl.BlockSpec((MOE_CHUNK, ROUTER_LANES), lambda c: (c, 0)),
                   pl.BlockSpec((None, 8, ROUTER_LANES), lambda c: (c, 0, 0))],
        out_shape=[jax.ShapeDtypeStruct((t, ROUTER_LANES), jnp.int32),
                   jax.ShapeDtypeStruct((nc, 8, ROUTER_LANES), jnp.int32)],
        compiler_params=_cparams(("arbitrary",)),
        name="slot_rank",
    )(route)


def _moe_kernel(pos1_ref, pos2_ref, w1_ref, w2_ref, off_ref, cnt_ref,
                hp_ref, wg_ref, wu_ref, wd_ref, y_ref, inv_ref, xt_ref, ys_ref):
    c = pl.program_id(0)
    e = pl.program_id(1)
    rt = ROW_TILES
    tb = hp_ref.shape[0] // rt
    tok0 = c * tb
    seg0 = c * N_EXPERTS

    @pl.when(e == 0)
    def _():
        def pads(ee, carry):
            o = off_ref[seg0 + ee]
            n_e = cnt_ref[seg0 + ee]

            def zero(i, carry2):
                inv_ref[o + i] = 0
                return carry2
            lax.fori_loop(n_e, (n_e + SEG_ALIGN - 1) // SEG_ALIGN * SEG_ALIGN, zero, 0)
            return carry
        lax.fori_loop(0, N_EXPERTS, pads, 0)
        n_last = cnt_ref[seg0 + N_EXPERTS - 1]
        total = off_ref[seg0 + N_EXPERTS - 1] + (n_last + SEG_ALIGN - 1) // SEG_ALIGN * SEG_ALIGN

        def zero_tail(i, carry):
            inv_ref[total + i] = 0
            return carry
        lax.fori_loop(0, MOE_TILE, zero_tail, 0)

        def place(i, carry):
            inv_ref[pos1_ref[tok0 + i]] = i
            inv_ref[pos2_ref[tok0 + i]] = i
            return carry
        lax.fori_loop(0, tb, place, 0, unroll=8)

    is_expert = e < N_EXPERTS
    e_c = jnp.minimum(e, N_EXPERTS - 1)
    n = jnp.where(is_expert, cnt_ref[seg0 + e_c], 0)
    base = off_ref[seg0 + e_c]

    @pl.when(n > 0)
    def _():
        wg = wg_ref[...].astype(BF16)
        wu = wu_ref[...].astype(BF16)
        wd = wd_ref[...].astype(BF16)

        def tile(i, carry):
            start = pl.multiple_of(base + i * MOE_TILE, SEG_ALIGN)

            def gather(r, carry2):
                src = pl.multiple_of(inv_ref[start + r] * rt, rt)
                xt_ref[pl.ds(pl.multiple_of(r * rt, rt), rt), :] = hp_ref[pl.ds(src, rt), :]
                return carry2
            lax.fori_loop(0, MOE_TILE, gather, 0, unroll=8)
            xrow = jnp.concatenate([xt_ref[pl.ds(k, MOE_TILE, stride=rt), :] for k in range(rt)], axis=1)
            xb = xrow.astype(BF16)
            hid = (_silu(_dot(xb, wg)) * _dot(xb, wu)).astype(BF16)
            out = _dot(hid, wd)
            for k in range(rt):
                ys_ref[pl.ds(start * rt + k, MOE_TILE, stride=rt), :] = out[:, k * LANES:(k + 1) * LANES]
            return carry
        lax.fori_loop(0, (n + MOE_TILE - 1) // MOE_TILE, tile, 0)

    @pl.when(e >= N_EXPERTS)
    def _():
        tq = y_ref.shape[0] // rt
        t0 = tok0 + (e - N_EXPERTS) * tq

        def combine(i, carry):
            r1 = ys_ref[pl.ds(pl.multiple_of(pos1_ref[t0 + i] * rt, rt), rt), :]
            r2 = ys_ref[pl.ds(pl.multiple_of(pos2_ref[t0 + i] * rt, rt), rt), :]
            y_ref[pl.ds(pl.multiple_of(i * rt, rt), rt), :] = w1_ref[t0 + i] * r1 + w2_ref[t0 + i] * r2
            return carry
        lax.fori_loop(0, tq, combine, 0, unroll=8)


def _moe(hp, pos1, pos2, w1, w2, off, cnt, w_gate, w_up, w_down, layer):
    rt = ROW_TILES
    t = hp.shape[0] // rt
    _, ne, d, f = w_gate.shape
    nc = t // MOE_CHUNK
    tq = MOE_CHUNK // COMBINE_STEPS

    def w_idx(c, e, *_):
        return (layer, jnp.minimum(e, ne - 1), 0, 0)

    grid_spec = pltpu.PrefetchScalarGridSpec(
        num_scalar_prefetch=6,
        grid=(nc, ne + COMBINE_STEPS),
        in_specs=[
            pl.BlockSpec((MOE_CHUNK * rt, LANES), lambda c, e, *_: (c, 0)),
            pl.BlockSpec((None, None, d, f), w_idx),
            pl.BlockSpec((None, None, d, f), w_idx),
            pl.BlockSpec((None, None, f, d), w_idx),
        ],
        out_specs=pl.BlockSpec((tq * rt, LANES),
                               lambda c, e, *_: (c * COMBINE_STEPS + jnp.maximum(e - ne, 0), 0)),
        scratch_shapes=[
            pltpu.SMEM((SLOTS_PAD + MOE_TILE,), jnp.int32),
            pltpu.VMEM((MOE_TILE * rt, LANES), F32),
            pltpu.VMEM(((SLOTS_PAD + MOE_TILE) * rt, LANES), F32),
        ],
    )
    return pl.pallas_call(
        _moe_kernel,
        grid_spec=grid_spec,
        out_shape=jax.ShapeDtypeStruct((t * rt, LANES), F32),
        compiler_params=_cparams(("arbitrary", "arbitrary")),
        name="moe_top2",
    )(pos1, pos2, w1, w2, off, cnt, hp, w_gate, w_up, w_down)


def _final_kernel(x_ref, y_ref, mod_ref, g_ref, o_ref):
    x = x_ref[...] + mod_ref[G_FFN:G_FFN + 1, :] * _load_row_tiled(y_ref)
    o_ref[...] = _rms(x) * g_ref[...]


def _final(x2, y, mod_l, g, seq, tm=512):
    t, d = x2.shape
    per_b = seq // tm
    return pl.pallas_call(
        _final_kernel,
        grid=(t // tm,),
        in_specs=[
            pl.BlockSpec((tm, d), lambda i: (i, 0)),
            pl.BlockSpec((tm * ROW_TILES, LANES), lambda i: (i, 0)),
            pl.BlockSpec((None, MOD_ROWS, d), lambda i: (i // per_b, 0, 0)),
            pl.BlockSpec((1, d), lambda i: (0, 0)),
        ],
        out_specs=pl.BlockSpec((tm, d), lambda i: (i, 0)),
        out_shape=jax.ShapeDtypeStruct((t, d), F32),
        compiler_params=_cparams(("arbitrary",)),
        name="final_norm",
    )(x2, y, mod_l, g)


def _retention_tables(seq):
    half = HEAD_DIM // 2
    inv_freq = ROPE_BASE ** (-jnp.arange(half, dtype=F32) / half)
    ang = jnp.arange(seq, dtype=F32)[:, None] * inv_freq[None, :]
    cos, sin = jnp.cos(ang), jnp.sin(ang)
    cos_t = jnp.tile(jnp.concatenate([cos, cos], axis=-1), (1, RET_HEADS))
    sin_t = jnp.tile(jnp.concatenate([-sin, sin], axis=-1), (1, RET_HEADS))
    log_gamma = jnp.log1p(-jnp.exp2(-5.0 - jnp.arange(RET_HEADS, dtype=F32)))
    tt = jnp.arange(CHUNK, dtype=F32)
    diff = tt[:, None] - tt[None, :]
    rdec = jnp.where(diff >= 0, jnp.exp(jnp.maximum(diff, 0.0)[None] * log_gamma[:, None, None]), 0.0)
    lg_lane = jnp.repeat(log_gamma, HEAD_DIM)[None, :]
    rdq = jnp.exp((tt[:, None] + 1.0) * lg_lane)
    rdk = jnp.exp((CHUNK - 1.0 - tt[:, None]) * lg_lane)
    rtot = jnp.exp(CHUNK * lg_lane)
    return cos_t, sin_t, rdec, rdq, rdk, rtot


def _pad_lanes(v, width=LANES):
    return jnp.pad(v, ((0, 0), (0, width - v.shape[-1])))


def kernel(x, c, norm_mix_g, norm_ffn_g, final_norm_g, w_ada, b_ada, w_in, conv_w, conv_b, ssd_dt_bias, ssd_a_log, ssd_d, ssd_norm_g, hgrn_lower_bounds, hgrn_norm_g, w_out, w_grp, b_grp, w_exp, b_exp, w_gate, w_up, w_down):
    batch, seq, d = x.shape
    depth = w_in.shape[0]
    t = batch * seq

    c8 = jnp.pad(c, ((0, MOD_ROWS - batch), (0, 0)))
    mod = _ada(c8, w_ada, b_ada)
    mod = mod[:, :batch].reshape(depth, batch, 6, d)
    mod = jnp.pad(mod, ((0, 0), (0, 0), (0, MOD_ROWS - 6), (0, 0)))

    dt0 = SSD_WIDTH + XBC_WIDTH
    w_in_p = jnp.concatenate(
        [w_in[:, :, :dt0], w_in[:, :, dt0 + SSD_HEADS:], w_in[:, :, dt0:dt0 + SSD_HEADS],
         jnp.zeros((depth, d, N_PROJ - w_in.shape[2]), w_in.dtype)], axis=-1).astype(BF16)
    w_out_b = w_out.astype(BF16)
    w_router = _pad_lanes(jnp.concatenate([w_grp, w_exp], axis=-1).reshape(depth * d, -1)).reshape(depth, d, ROUTER_LANES)
    w_router_hi = w_router.astype(BF16)
    w_router = jnp.stack([w_router_hi, (w_router - w_router_hi.astype(F32)).astype(BF16)], axis=1)
    b_router = _pad_lanes(jnp.concatenate([b_grp, b_exp], axis=-1))[:, None, :]
    dtb = _pad_lanes(ssd_dt_bias)[:, None, :]
    alog = _pad_lanes(ssd_a_log)[:, None, :]
    dexp = jnp.repeat(ssd_d, HEAD_DIM, axis=-1)[:, None, :]
    tables = _retention_tables(seq)

    x2 = x.reshape(t, d)
    y = None
    for layer in range(depth):
        x2, mixed = _mixer(x2, y, mod[layer - 1] if layer else None, mod[layer],
                           norm_mix_g[layer][None, :], w_in_p[layer], layer, batch, seq,
                           conv_w[layer], conv_b[layer][None, :], dtb[layer],
                           alog[layer], dexp[layer], ssd_norm_g[layer][None, :], hgrn_lower_bounds,
                           hgrn_norm_g[layer][None, :], *tables)
        x2, hp, route = _outproj(x2, mixed, mod[layer], norm_ffn_g[layer][None, :], w_out_b[layer],
                                 w_router[layer], b_router[layer], seq)
        pos, meta = _rank(route)
        y = _moe(hp, pos[:, 0], pos[:, 1], route[:, R_W1], route[:, R_W2],
                 meta[:, M_OFF, :N_EXPERTS].reshape(-1), meta[:, M_CNT, :N_EXPERTS].reshape(-1),
                 w_gate, w_up, w_down, layer)
    out = _final(x2, y, mod[depth - 1], final_norm_g[None, :], seq)
    return out.reshape(batch, seq, d)
```

```python
import functools
import math

import jax
import jax.numpy as jnp
import numpy as np
from jax import lax
from jax.experimental import pallas as pl
from jax.experimental.pallas import tpu as pltpu

F32 = jnp.float32
BF16 = jnp.bfloat16
HIGHEST = lax.Precision.HIGHEST

EPS = 1e-6
GATE_FLOOR = 1e-30
ROPE_BASE = 10000.0

HEAD_DIM = 64
SSD_HEADS = 8
SSD_GROUPS = 2
SSD_CONV = 4
HGRN_HEADS = 4
RET_HEADS = 4
N_GROUPS = 4
EXPERTS_PER_GROUP = 8
N_EXPERTS = N_GROUPS * EXPERTS_PER_GROUP
CHUNK = 128
LANES = 128
ROUTER_LANES = 128

SH_MIX, SC_MIX, G_MIX, SH_FFN, SC_FFN, G_FFN = range(6)
MOD_ROWS = 8

VMEM_LIMIT = 56 * 1024 * 1024


def _cparams(sem):
    return pltpu.CompilerParams(dimension_semantics=sem, vmem_limit_bytes=VMEM_LIMIT)


def _silu(x):
    return x * jax.nn.sigmoid(x)


def _rms(x):
    return x * lax.rsqrt(jnp.mean(x * x, axis=-1, keepdims=True) + EPS)


def _split3(x):
    hi = x.astype(BF16)
    r = x - hi.astype(F32)
    mid = r.astype(BF16)
    lo = (r - mid.astype(F32)).astype(BF16)
    return hi, mid, lo


def _dot(a, b):
    return jnp.dot(a, b, preferred_element_type=F32)


def _dot_nt(a, b):
    return lax.dot_general(a, b, (((1,), (1,)), ((), ())), preferred_element_type=F32)


def _dot_tn(a, b):
    return lax.dot_general(a, b, (((0,), (0,)), ((), ())), preferred_element_type=F32)


def _ada_kernel(c_ref, w_ref, b_ref, o_ref):
    c = _silu(c_ref[...])
    o_ref[...] = jnp.dot(c, w_ref[...], precision=HIGHEST, preferred_element_type=F32) + b_ref[...]


def _ada(c8, w_ada, b_ada):
    depth, d, d6 = w_ada.shape
    nb = d6 // d
    return pl.pallas_call(
        _ada_kernel,
        grid=(depth, nb),
        in_specs=[
            pl.BlockSpec((MOD_ROWS, d), lambda l, n: (0, 0)),
            pl.BlockSpec((None, d, d), lambda l, n: (l, 0, n)),
            pl.BlockSpec((None, 1, d), lambda l, n: (l, 0, n)),
        ],
        out_specs=pl.BlockSpec((None, MOD_ROWS, d), lambda l, n: (l, 0, n)),
        out_shape=jax.ShapeDtypeStruct((depth, MOD_ROWS, d6), F32),
        compiler_params=_cparams(("arbitrary", "arbitrary")),
        name="ada_mod",
    )(c8, w_ada, b_ada.reshape(depth, 1, d6))


OFF_Z, OFF_XBC, OFF_HQ, OFF_HF, OFF_HI, OFF_HG = 0, 512, 1280, 1536, 1792, 2048
OFF_RQ, OFF_RK, OFF_RV, OFF_RG, OFF_DT = 2304, 2560, 2816, 3072, 3328
N_PROJ = 3456
SSD_WIDTH = SSD_HEADS * HEAD_DIM
XBC_WIDTH = SSD_WIDTH + 2 * SSD_GROUPS * HEAD_DIM
HGRN_WIDTH = HGRN_HEADS * HEAD_DIM
RET_WIDTH = RET_HEADS * HEAD_DIM
TAIL = 8
HGRN_DIRECT = 8
HGRN_LEVELS = (8, 16, 32, 64)


def _pair_blockdiag(v_pair, lo_lane):
    return jnp.concatenate([jnp.where(lo_lane, v_pair, 0.0), jnp.where(lo_lane, 0.0, v_pair)], axis=0)


def _head_mean_square(o, bd):
    c, n = o.shape[0], o.shape[1] // LANES
    sq = (o * o).astype(BF16)
    ms = _dot(jnp.concatenate([sq[:, p * LANES:(p + 1) * LANES] for p in range(n)], axis=0), bd)
    return jnp.concatenate([ms[p * c:(p + 1) * c] for p in range(n)], axis=1)


def _pair_scores(q_pair, k_pair_b, lo_lane):
    c = q_pair.shape[0]
    q2 = jnp.concatenate([jnp.where(lo_lane, q_pair, 0.0), jnp.where(lo_lane, 0.0, q_pair)], axis=0)
    s = _dot_nt(q2.astype(BF16), k_pair_b)
    return s[:c], s[c:]


SIDE_POINTS = 9
N_MIXER_PARAMS = 14


def _mixer_kernel(layer, has_res, *refs):
    if has_res:
        x_ref, y_ref, modp_ref, mod_ref, g_ref, w_ref = refs[:6]
        rest = refs[6:]
    else:
        x_ref, mod_ref, g_ref, w_ref = refs[:4]
        rest = refs[4:]
    params, outs = rest[:N_MIXER_PARAMS], rest[N_MIXER_PARAMS:]
    if has_res:
        xo_ref, out_ref, proj_next, proj_cur, xbuf, s_ssd, s_hg, s_ret = outs
    else:
        out_ref, proj_next, proj_cur, xbuf, s_ssd, s_hg, s_ret = outs
    j = pl.program_id(0)
    nb = x_ref.shape[0]

    def reset_carried():
        xbuf[:, 0:TAIL, :] = jnp.zeros((nb, TAIL, XBC_WIDTH), F32)
        s_ssd[...] = jnp.zeros_like(s_ssd)
        s_hg[...] = jnp.zeros_like(s_hg)
        s_ret[...] = jnp.zeros_like(s_ret)

    @pl.when(j == 0)
    def _():
        reset_carried()
        proj_cur[...] = jnp.zeros_like(proj_cur)

    hs = []
    for b in range(nb):
        x = x_ref[b]
        if has_res:
            x = x + modp_ref[b, G_FFN:G_FFN + 1, :] * _load_row_tiled(y_ref.at[b])
            xo_ref[b] = x
        hs.append(_rms(x) * g_ref[...] * (1.0 + mod_ref[b, SC_MIX:SC_MIX + 1, :]) + mod_ref[b, SH_MIX:SH_MIX + 1, :])
    proj = _dot(jnp.concatenate(hs, axis=0).astype(BF16), w_ref[...])
    for b in range(nb):
        proj_next[b] = proj[b * CHUNK:(b + 1) * CHUNK]

    stages = [_mixer_body(layer, proj_cur.at[b], *params, out_ref.at[b], xbuf.at[b], s_ssd.at[b],
                          s_hg.at[b], s_ret.at[b]) for b in range(nb)]
    while stages:
        for gen in list(stages):
            if next(gen, StopIteration) is StopIteration:
                stages.remove(gen)
    proj_cur[...] = proj_next[...]

    @pl.when(j == 0)
    def _():
        reset_carried()


def _mixer_body(layer, proj_ref, convw_ref, convb_ref, dtb_ref, alog_ref, dexp_ref, ssdg_ref,
                lbraw_ref, hgg_ref, cos_ref, sin_ref, rdec_ref, rdq_ref, rdk_ref, rtot_ref,
                out_ref, xbuf, s_ssd, s_hg, s_ret):
    C = CHUNK

    row = lax.broadcasted_iota(jnp.int32, (C, LANES), 0)
    lane = lax.broadcasted_iota(jnp.int32, (C, LANES), 1)
    lo_lane = lane < HEAD_DIM
    causal = row >= lane
    bd_mask = jnp.right_shift(row, 6) == jnp.right_shift(lane, 6)
    bd_mean = jnp.where(bd_mask, 1.0 / HEAD_DIM, 0.0).astype(BF16)
    tril = jnp.where(causal, 1.0, 0.0).astype(BF16)

    def cumsum_rows(x):
        hi, mid, lo = _split3(x)
        return _dot(tril, hi) + _dot(tril, mid) + _dot(tril, lo)

    xbuf[TAIL:TAIL + C, :] = proj_ref[:, OFF_XBC:OFF_XBC + XBC_WIDTH]
    conv = convb_ref[...]
    for jj in range(SSD_CONV):
        off = TAIL - (SSD_CONV - 1) + jj
        conv = conv + convw_ref[jj:jj + 1, :] * xbuf[off:off + C, :]
    xbuf[0:TAIL, :] = xbuf[C:C + TAIL, :]
    xc = _silu(conv)
    xs = xc[:, 0:SSD_WIDTH]
    bm = xc[:, SSD_WIDTH:SSD_WIDTH + LANES]
    cm = xc[:, SSD_WIDTH + LANES:SSD_WIDTH + 2 * LANES]

    dt8 = jax.nn.softplus(proj_ref[:, OFF_DT:OFF_DT + LANES] + dtb_ref[...])
    la8 = dt8 * (-jnp.exp(alog_ref[...]))
    cs8 = cumsum_rows(la8)
    cs8t = cs8.T

    bm_b = bm.astype(BF16)
    cm_b = cm.astype(BF16)
    scores_g = _pair_scores(cm, bm_b, lo_lane)

    o_intra, e1_cols, e2_cols, dt_cols = [], [], [], []
    for p in range(SSD_HEADS // 2):
        a, b = 2 * p, 2 * p + 1
        g = a // (SSD_HEADS // SSD_GROUPS)
        col_a = jnp.broadcast_to(cs8[:, a:a + 1], (C, LANES))
        col_b = jnp.broadcast_to(cs8[:, b:b + 1], (C, LANES))
        cs_pair = jnp.where(lo_lane, col_a, col_b)
        dt_pair = jnp.where(lo_lane, jnp.broadcast_to(dt8[:, a:a + 1], (C, LANES)),
                            jnp.broadcast_to(dt8[:, b:b + 1], (C, LANES)))
        last = cs_pair[C - 1:C, :]
        e1_cols.append(jnp.exp(cs_pair))
        e2_cols.append(jnp.exp(last - cs_pair))
        dt_cols.append(dt_pair)
        dec_a = jnp.where(causal, jnp.exp(jnp.minimum(col_a - cs8t[a:a + 1, :], 0.0)), 0.0)
        dec_b = jnp.where(causal, jnp.exp(jnp.minimum(col_b - cs8t[b:b + 1, :], 0.0)), 0.0)
        pa = (scores_g[g] * dec_a).astype(BF16)
        pb = (scores_g[g] * dec_b).astype(BF16)
        v_pair = xs[:, p * LANES:(p + 1) * LANES] * dt_pair
        o_intra.append(_dot(jnp.concatenate([pa, pb], axis=1),
                            _pair_blockdiag(v_pair, lo_lane).astype(BF16)))
    o_intra = jnp.concatenate(o_intra, axis=1)
    e1 = jnp.concatenate(e1_cols, axis=1)
    e2 = jnp.concatenate(e2_cols, axis=1)
    dtx = jnp.concatenate(dt_cols, axis=1)
    v_all = xs * dtx
    s_prev = s_ssd[...]
    o_inter = e1 * _dot(cm_b, s_prev.astype(BF16))
    y = o_intra + o_inter + dexp_ref[...] * xs
    y = y * _silu(proj_ref[:, OFF_Z:OFF_Z + SSD_WIDTH])
    gw = SSD_WIDTH // SSD_GROUPS
    grp_mean = jnp.full((gw, gw), 1.0 / gw, BF16)
    ysq = (y * y).astype(BF16)
    ms = jnp.concatenate([_dot(ysq[:, g * gw:(g + 1) * gw], grp_mean) for g in range(SSD_GROUPS)], axis=1)
    out_ref[:, 0:SSD_WIDTH] = (y * lax.rsqrt(ms + EPS) * ssdg_ref[...]).astype(out_ref.dtype)
    row_s = lax.broadcasted_iota(jnp.int32, (LANES, SSD_WIDTH), 0)
    lane_s = lax.broadcasted_iota(jnp.int32, (LANES, SSD_WIDTH), 1)
    grp_mask = jnp.right_shift(row_s, 6) == jnp.right_shift(lane_s, 8)
    upd = _dot_tn(bm_b, (v_all * e2).astype(BF16))
    s_ssd[...] = e1[C - 1:C, :] * s_prev + jnp.where(grp_mask, upd, 0.0)
    yield

    W2 = HGRN_WIDTH
    row2 = lax.broadcasted_iota(jnp.int32, (C, W2), 0)
    lb_raw = lbraw_ref[...]
    lb_e = jnp.exp(lb_raw - jnp.max(lb_raw, axis=0, keepdims=True))
    lb_soft = lb_e / jnp.sum(lb_e, axis=0, keepdims=True)
    lb = jnp.sum(lb_soft[0:layer + 1, :], axis=0, keepdims=True) - lb_soft[0:1, :]
    forget = lb + (1.0 - lb) * jax.nn.sigmoid(proj_ref[:, OFF_HF:OFF_HF + W2])
    fc = jnp.maximum(forget, GATE_FLOOR)
    kk = 1.0 - forget
    hq = proj_ref[:, OFF_HQ:OFF_HQ + W2]
    hv = proj_ref[:, OFF_HI:OFF_HI + W2]
    cum = cumsum_rows(jnp.log(fc))
    npair = W2 // LANES

    sub = jnp.bitwise_and(row2, HGRN_DIRECT - 1)
    bd2 = jnp.where(jnp.right_shift(lax.broadcasted_iota(jnp.int32, (W2, W2), 0), 6)
                    == jnp.right_shift(lax.broadcasted_iota(jnp.int32, (W2, W2), 1), 6), 1.0, 0.0).astype(BF16)
    def prev_row(a):
        return pltpu.roll(a.reshape(C // HGRN_DIRECT, HGRN_DIRECT, W2), 1, 1).reshape(C, W2)

    not_first = sub != 0
    kdec = kk
    vprev = hv
    terms, vals = [(hq * kk).astype(BF16)], [hv]
    for dlt in range(1, HGRN_DIRECT):
        kdec = jnp.where(not_first, fc * prev_row(kdec), 0.0)
        vprev = prev_row(vprev)
        terms.append((hq * kdec).astype(BF16))
        vals.append(vprev)
    sc_all = _dot(jnp.concatenate(terms, axis=0), bd2)
    o_h = sc_all[0:C] * vals[0]
    for dlt in range(1, HGRN_DIRECT):
        o_h = o_h + sc_all[dlt * C:(dlt + 1) * C] * vals[dlt]
    yield
    p_sum = [[None, None] for _ in range(npair)]
    for m in HGRN_LEVELS:
        nb = C // (2 * m)
        cum3 = cum.reshape(nb, 2 * m, W2)
        ref = jnp.broadcast_to(cum3[:, m - 1:m, :], (nb, 2 * m, W2)).reshape(C, W2)
        right = jnp.bitwise_and(row2, m) != 0
        e = jnp.exp(jnp.where(right, cum - ref, ref - cum))
        qm = jnp.where(right, hq * e, 0.0)
        km = jnp.where(right, 0.0, kk * e).astype(BF16)
        sh = int(math.log2(2 * m))
        same_block = jnp.right_shift(row, sh) == jnp.right_shift(lane, sh)
        for p in range(npair):
            sl = slice(p * LANES, (p + 1) * LANES)
            for hh, s_h in enumerate(_pair_scores(qm[:, sl], km[:, sl], lo_lane)):
                sc = jnp.where(same_block, s_h, 0.0)
                p_sum[p][hh] = sc if p_sum[p][hh] is None else p_sum[p][hh] + sc
        yield
    cols = []
    for p in range(npair):
        vbd = _pair_blockdiag(hv[:, p * LANES:(p + 1) * LANES], lo_lane).astype(BF16)
        cols.append(_dot(jnp.concatenate([p_sum[p][0].astype(BF16), p_sum[p][1].astype(BF16)], axis=1), vbd))
    o_h = o_h + jnp.concatenate(cols, axis=1)
    q_in = (hq * jnp.exp(cum)).astype(BF16)
    last2 = cum[C - 1:C, :]
    k_end = (kk * jnp.exp(last2 - cum)).astype(BF16)
    tot2 = jnp.exp(last2)
    hv_b = hv.astype(BF16)
    inter = []
    for p in range(npair):
        sl = slice(p * LANES, (p + 1) * LANES)
        st = s_hg[p]
        inter.append(_dot_nt(q_in[:, sl], st.astype(BF16)))
        upd = _dot_tn(hv_b[:, sl], k_end[:, sl])
        s_hg[p] = st * tot2[:, sl] + jnp.where(bd_mask, upd, 0.0)
    o_h = o_h + jnp.concatenate(inter, axis=1)
    ms = _head_mean_square(o_h, bd_mean)
    o_h = o_h * lax.rsqrt(ms + EPS) * hgg_ref[...]
    o_h = o_h * jax.nn.sigmoid(proj_ref[:, OFF_HG:OFF_HG + W2])
    out_ref[:, SSD_WIDTH:SSD_WIDTH + W2] = o_h.astype(out_ref.dtype)
    yield

    W3 = RET_WIDTH
    lane3 = lax.broadcasted_iota(jnp.int32, (C, W3), 1)
    first_half = jnp.bitwise_and(lane3, HEAD_DIM // 2) == 0
    cosv = cos_ref[...]
    sinv = sin_ref[...]

    def rotary(xr):
        swapped = jnp.where(first_half, pltpu.roll(xr, W3 - HEAD_DIM // 2, 1),
                            pltpu.roll(xr, HEAD_DIM // 2, 1))
        return xr * cosv + swapped * sinv

    qr = rotary(proj_ref[:, OFF_RQ:OFF_RQ + W3])
    kr = rotary(proj_ref[:, OFF_RK:OFF_RK + W3]) * (HEAD_DIM ** -0.5)
    rv = proj_ref[:, OFF_RV:OFF_RV + W3]
    kr_b = kr.astype(BF16)
    q_in = (qr * rdq_ref[...]).astype(BF16)
    k_end = (kr * rdk_ref[...]).astype(BF16)
    rv_b = rv.astype(BF16)
    o_cols = []
    for p in range(W3 // LANES):
        sl = slice(p * LANES, (p + 1) * LANES)
        sa, sb = _pair_scores(qr[:, sl], kr_b[:, sl], lo_lane)
        sa = sa * rdec_ref[2 * p]
        sb = sb * rdec_ref[2 * p + 1]
        vbd = _pair_blockdiag(rv[:, sl], lo_lane).astype(BF16)
        o_p = _dot(jnp.concatenate([sa.astype(BF16), sb.astype(BF16)], axis=1), vbd)
        st = s_ret[p]
        o_p = o_p + _dot(q_in[:, sl], st.astype(BF16))
        upd = _dot_tn(k_end[:, sl], rv_b[:, sl])
        s_ret[p] = st * rtot_ref[:, sl] + jnp.where(bd_mask, upd, 0.0)
        o_cols.append(o_p)
        yield
    o_r = jnp.concatenate(o_cols, axis=1)
    ms = _head_mean_square(o_r, bd_mean)
    o_r = o_r * lax.rsqrt(ms + EPS) * _silu(proj_ref[:, OFF_RG:OFF_RG + W3])
    out_ref[:, SSD_WIDTH + W2:SSD_WIDTH + W2 + W3] = o_r.astype(out_ref.dtype)


def _mixer(x2, y, mod_prev, mod_l, g, w, layer, batch, seq, conv_w, conv_b, dtb, alog, dexp, ssdg, lbraw, hgg,
           cos_t, sin_t, rdec, rdq, rdk, rtot):
    t, d = x2.shape
    d_mix = SSD_WIDTH + HGRN_WIDTH + RET_WIDTH
    nj = seq // CHUNK
    const2 = lambda j: (0, 0)
    has_res = y is not None
    proj_idx = lambda j: (0, jnp.minimum(j, nj - 1), 0)
    mix_idx = lambda j: (0, jnp.maximum(j - 1, 0), 0)
    x_spec = pl.BlockSpec((batch, CHUNK, d), proj_idx)
    mod_spec = pl.BlockSpec((batch, MOD_ROWS, d), lambda j: (0, 0, 0))
    head_specs = [x_spec]
    head_args = [x2.reshape(batch, seq, d)]
    if has_res:
        head_specs += [pl.BlockSpec((batch, CHUNK * ROW_TILES, LANES), proj_idx), mod_spec]
        head_args += [y.reshape(batch, seq * ROW_TILES, LANES), mod_prev]
    head_specs += [mod_spec, pl.BlockSpec((1, d), const2), pl.BlockSpec(w.shape, const2)]
    head_args += [mod_l, g, w]
    table_spec = pl.BlockSpec((CHUNK, RET_WIDTH), lambda j: (jnp.maximum(j - 1, 0), 0))
    mixed_spec = pl.BlockSpec((batch, CHUNK, d_mix), mix_idx)
    mixed_shape = jax.ShapeDtypeStruct((batch, seq, d_mix), BF16)
    res = pl.pallas_call(
        functools.partial(_mixer_kernel, layer, has_res),
        grid=(nj + 1,),
        in_specs=head_specs + [
            pl.BlockSpec(conv_w.shape, const2),
            pl.BlockSpec(conv_b.shape, const2),
            pl.BlockSpec(dtb.shape, const2),
            pl.BlockSpec(alog.shape, const2),
            pl.BlockSpec(dexp.shape, const2),
            pl.BlockSpec(ssdg.shape, const2),
            pl.BlockSpec(lbraw.shape, const2),
            pl.BlockSpec(hgg.shape, const2),
            table_spec,
            table_spec,
            pl.BlockSpec(rdec.shape, lambda j: (0, 0, 0)),
            pl.BlockSpec(rdq.shape, const2),
            pl.BlockSpec(rdk.shape, const2),
            pl.BlockSpec(rtot.shape, const2),
        ],
        out_specs=[x_spec, mixed_spec] if has_res else mixed_spec,
        out_shape=[jax.ShapeDtypeStruct((batch, seq, d), F32), mixed_shape] if has_res else mixed_shape,
        scratch_shapes=[
            pltpu.VMEM((batch, CHUNK, N_PROJ), F32),
            pltpu.VMEM((batch, CHUNK, N_PROJ), F32),
            pltpu.VMEM((batch, TAIL + CHUNK, XBC_WIDTH), F32),
            pltpu.VMEM((batch, LANES, SSD_WIDTH), F32),
            pltpu.VMEM((batch, HGRN_WIDTH // LANES, LANES, LANES), F32),
            pltpu.VMEM((batch, RET_WIDTH // LANES, LANES, LANES), F32),
        ],
        compiler_params=_cparams(("arbitrary",)),
        name="norm_inproj_mixers",
    )(*head_args, conv_w, conv_b, dtb, alog, dexp, ssdg, lbraw, hgg, cos_t, sin_t, rdec, rdq, rdk, rtot)
    if has_res:
        return res[0].reshape(t, d), res[1].reshape(t, d_mix)
    return x2, res.reshape(t, d_mix)


ROW_TILES = 8


def _store_row_tiled(ref, val):
    m = val.shape[0]
    for k in range(ROW_TILES):
        ref[pl.ds(k, m, stride=ROW_TILES), :] = val[:, k * LANES:(k + 1) * LANES]


def _load_row_tiled(ref):
    m = ref.shape[0] // ROW_TILES
    return jnp.concatenate([ref[pl.ds(k, m, stride=ROW_TILES), :] for k in range(ROW_TILES)], axis=1)


R_E1, R_E2, R_W1, R_W2 = range(4)


def _outproj_kernel(x_ref, mixed_ref, mod_ref, g_ref, w_ref, wr_ref, br_ref,
                    xo_ref, hp_ref, route_ref):
    x = x_ref[...] + mod_ref[G_MIX:G_MIX + 1, :] * _dot(mixed_ref[...], w_ref[...])
    xo_ref[...] = x
    h = _rms(x) * g_ref[...] * (1.0 + mod_ref[SC_FFN:SC_FFN + 1, :]) + mod_ref[SH_FFN:SH_FFN + 1, :]
    _store_row_tiled(hp_ref, h)
    h_hi = h.astype(BF16)
    h_lo = (h - h_hi.astype(F32)).astype(BF16)
    logits = (_dot(h_hi, wr_ref[0]) + _dot(h_lo, wr_ref[0]) + _dot(h_hi, wr_ref[1])) + br_ref[...]
    lane = lax.broadcasted_iota(jnp.int32, logits.shape, 1).astype(F32)
    neg = -jnp.inf
    big = float(ROUTER_LANES)
    is_grp = lane < N_GROUPS
    gl = jnp.where(is_grp, logits, neg)
    gmax = jnp.max(gl, axis=-1, keepdims=True)
    g_idx = jnp.min(jnp.where(gl == gmax, lane, big), axis=-1, keepdims=True)
    p_grp = 1.0 / jnp.sum(jnp.where(is_grp, jnp.exp(gl - gmax), 0.0), axis=-1, keepdims=True)
    first = N_GROUPS + g_idx * EXPERTS_PER_GROUP
    valid = (lane >= first) & (lane < first + EXPERTS_PER_GROUP)
    el = jnp.where(valid, logits, neg)
    m1 = jnp.max(el, axis=-1, keepdims=True)
    i1 = jnp.min(jnp.where(el == m1, lane, big), axis=-1, keepdims=True)
    el2 = jnp.where(lane == i1, neg, el)
    m2 = jnp.max(el2, axis=-1, keepdims=True)
    i2 = jnp.min(jnp.where(el2 == m2, lane, big), axis=-1, keepdims=True)
    e = jnp.exp(m2 - m1)
    w1 = p_grp / (1.0 + e)
    w2 = p_grp * e / (1.0 + e)
    route_ref[...] = jnp.where(lane == R_E1, i1 - N_GROUPS,
                               jnp.where(lane == R_E2, i2 - N_GROUPS,
                                         jnp.where(lane == R_W1, w1, jnp.where(lane == R_W2, w2, 0.0))))


def _outproj(x2, mixed, mod_l, g, w, wr, br, seq, tm=512):
    t, d = x2.shape
    per_b = seq // tm
    return pl.pallas_call(
        _outproj_kernel,
        grid=(t // tm,),
        in_specs=[
            pl.BlockSpec((tm, d), lambda i: (i, 0)),
            pl.BlockSpec((tm, mixed.shape[1]), lambda i: (i, 0)),
            pl.BlockSpec((None, MOD_ROWS, d), lambda i: (i // per_b, 0, 0)),
            pl.BlockSpec((1, d), lambda i: (0, 0)),
            pl.BlockSpec(w.shape, lambda i: (0, 0)),
            pl.BlockSpec(wr.shape, lambda i: (0, 0, 0)),
            pl.BlockSpec(br.shape, lambda i: (0, 0)),
        ],
        out_specs=[
            pl.BlockSpec((tm, d), lambda i: (i, 0)),
            pl.BlockSpec((tm * ROW_TILES, LANES), lambda i: (i, 0)),
            pl.BlockSpec((tm, ROUTER_LANES), lambda i: (i, 0)),
        ],
        out_shape=[
            jax.ShapeDtypeStruct((t, d), F32),
            jax.ShapeDtypeStruct((t * ROW_TILES, LANES), F32),
            jax.ShapeDtypeStruct((t, ROUTER_LANES), F32),
        ],
        compiler_params=_cparams(("arbitrary",)),
        name="outproj_router",
    )(x2, mixed, mod_l, g, w, wr, br)


MOE_CHUNK = 2048
MOE_TILE = 128
SEG_ALIGN = 8
COMBINE_STEPS = 4
EXPERTS_PER_STEP = 2
SLOTS_PAD = 2 * MOE_CHUNK + N_EXPERTS * SEG_ALIGN
RANK_BLOCK = 256
M_OFF, M_CNT = 0, 1
S_POS1, S_POS2, S_W1, S_W2 = 0, 1, R_W1, R_W2


def _rank_kernel(route_ref, slots_ref, meta_ref):
    tb = route_ref.shape[0]
    lane = lax.broadcasted_iota(jnp.int32, (RANK_BLOCK, ROUTER_LANES), 1).astype(F32)
    r_i = lax.broadcasted_iota(jnp.int32, (RANK_BLOCK, RANK_BLOCK), 0)
    c_i = lax.broadcasted_iota(jnp.int32, (RANK_BLOCK, RANK_BLOCK), 1)
    strict_tril = jnp.where(r_i > c_i, 1.0, 0.0).astype(BF16)
    carry = jnp.zeros((1, ROUTER_LANES), F32)
    ranks = []
    for b in range(tb // RANK_BLOCK):
        blk = route_ref[b * RANK_BLOCK:(b + 1) * RANK_BLOCK, :]
        onehot = jnp.where((lane == blk[:, R_E1:R_E1 + 1]) | (lane == blk[:, R_E2:R_E2 + 1]), 1.0, 0.0)
        ranks.append(_dot(strict_tril, onehot.astype(BF16)) + carry)
        carry = carry + jnp.sum(onehot, axis=0, keepdims=True)
    cnt = carry
    seg = jnp.floor((cnt + (SEG_ALIGN - 1)) * (1.0 / SEG_ALIGN)) * SEG_ALIGN
    u_r = lax.broadcasted_iota(jnp.int32, (ROUTER_LANES, ROUTER_LANES), 0)
    u_c = lax.broadcasted_iota(jnp.int32, (ROUTER_LANES, ROUTER_LANES), 1)
    strict_triu = jnp.where(u_r < u_c, 1.0, 0.0)
    off = jnp.dot(jnp.broadcast_to(seg, (8, ROUTER_LANES)), strict_triu, precision=HIGHEST,
                  preferred_element_type=F32)[0:1, :]
    for b in range(tb // RANK_BLOCK):
        blk = route_ref[b * RANK_BLOCK:(b + 1) * RANK_BLOCK, :]
        dest = off + ranks[b]
        p1 = jnp.sum(jnp.where(lane == blk[:, R_E1:R_E1 + 1], dest, 0.0), axis=-1, keepdims=True)
        p2 = jnp.sum(jnp.where(lane == blk[:, R_E2:R_E2 + 1], dest, 0.0), axis=-1, keepdims=True)
        rec = jnp.where(lane == S_POS1, p1, jnp.where(lane == S_POS2, p2,
                        jnp.where((lane == S_W1) | (lane == S_W2), blk, 0.0)))
        for h in range(RANK_BLOCK // LANES):
            t0 = b * RANK_BLOCK + h * LANES
            slots_ref[:, t0:t0 + LANES] = rec[h * LANES:(h + 1) * LANES, :].T[0:8, :]
    row8 = lax.broadcasted_iota(jnp.int32, (8, ROUTER_LANES), 0)
    meta_ref[...] = jnp.where(row8 == M_OFF, off, jnp.where(row8 == M_CNT, cnt, 0.0)).astype(jnp.int32)


def _rank(route):
    t = route.shape[0]
    nc = t // MOE_CHUNK
    return pl.pallas_call(
        _rank_kernel,
        grid=(nc,),
        in_specs=[pl.BlockSpec((MOE_CHUNK, ROUTER_LANES), lambda c: (c, 0))],
        out_specs=[pl.BlockSpec((8, MOE_CHUNK), lambda c: (0, c)),
                   pl.BlockSpec((None, 8, ROUTER_LANES), lambda c: (c, 0, 0))],
        out_shape=[jax.ShapeDtypeStruct((8, t), F32),
                   jax.ShapeDtypeStruct((nc, 8, ROUTER_LANES), jnp.int32)],
        compiler_params=_cparams(("arbitrary",)),
        name="slot_rank",
    )(route)


def _moe_kernel(pos1_ref, pos2_ref, w1_ref, w2_ref, off_ref, cnt_ref,
                hp_ref, wg_ref, wu_ref, wd_ref, y_ref, inv_ref, xt_ref, ys_ref):
    c = pl.program_id(0)
    e = pl.program_id(1)
    rt = ROW_TILES
    tb = hp_ref.shape[0] // rt
    tok0 = c * tb
    seg0 = c * N_EXPERTS

    @pl.when((c == 0) & (e == 0))
    def _():
        xt_ref[...] = jnp.zeros_like(xt_ref)

    @pl.when(e == 0)
    def _():
        def pads(ee, carry):
            o = off_ref[seg0 + ee]
            n_e = cnt_ref[seg0 + ee]

            def zero(i, carry2):
                inv_ref[o + i] = 0
                return carry2
            lax.fori_loop(n_e, (n_e + SEG_ALIGN - 1) // SEG_ALIGN * SEG_ALIGN, zero, 0)
            return carry
        lax.fori_loop(0, N_EXPERTS, pads, 0)
        n_last = cnt_ref[seg0 + N_EXPERTS - 1]
        total = off_ref[seg0 + N_EXPERTS - 1] + (n_last + SEG_ALIGN - 1) // SEG_ALIGN * SEG_ALIGN

        def zero_tail(i, carry):
            inv_ref[total + i] = 0
            return carry
        lax.fori_loop(0, MOE_TILE, zero_tail, 0)

        def place(i, carry):
            inv_ref[pos1_ref[tok0 + i]] = i
            inv_ref[pos2_ref[tok0 + i]] = i
            return carry
        lax.fori_loop(0, tb, place, 0, unroll=8)

    expert_steps = N_EXPERTS // EXPERTS_PER_STEP
    is_expert = e < expert_steps
    for k in range(EXPERTS_PER_STEP):
        e_k = jnp.minimum(e, expert_steps - 1) * EXPERTS_PER_STEP + k
        n = jnp.where(is_expert, cnt_ref[seg0 + e_k], 0)
        base = off_ref[seg0 + e_k]

        @pl.when(n > 0)
        def _(k=k, n=n, base=base):
            wg = wg_ref[k].astype(BF16)
            wu = wu_ref[k].astype(BF16)
            wd = wd_ref[k].astype(BF16)

            def tile(i, carry):
                start = pl.multiple_of(base + i * MOE_TILE, SEG_ALIGN)

                def gather(r, carry2):
                    src = pl.multiple_of(inv_ref[start + r] * rt, rt)
                    xt_ref[pl.ds(pl.multiple_of(r * rt, rt), rt), :] = hp_ref[pl.ds(src, rt), :]
                    return carry2
                rows = jnp.minimum(MOE_TILE, (n - i * MOE_TILE + SEG_ALIGN - 1) // SEG_ALIGN * SEG_ALIGN)
                def gather_group(g, carry2):
                    for u in range(SEG_ALIGN):
                        gather(g * SEG_ALIGN + u, carry2)
                    return carry2
                lax.fori_loop(0, rows // SEG_ALIGN, gather_group, 0)
                xrow = jnp.concatenate([xt_ref[pl.ds(j, MOE_TILE, stride=rt), :] for j in range(rt)], axis=1)
                xb = xrow.astype(BF16)
                hid = (_silu(_dot(xb, wg)) * _dot(xb, wu)).astype(BF16)
                out = _dot(hid, wd)
                for j in range(rt):
                    ys_ref[pl.ds(start * rt + j, MOE_TILE, stride=rt), :] = out[:, j * LANES:(j + 1) * LANES]
                return carry
            lax.fori_loop(0, (n + MOE_TILE - 1) // MOE_TILE, tile, 0)

    @pl.when(e >= expert_steps)
    def _():
        tq = y_ref.shape[0] // rt
        t0 = tok0 + (e - expert_steps) * tq

        def combine(i, carry):
            r1 = ys_ref[pl.ds(pl.multiple_of(pos1_ref[t0 + i] * rt, rt), rt), :]
            r2 = ys_ref[pl.ds(pl.multiple_of(pos2_ref[t0 + i] * rt, rt), rt), :]
            y_ref[pl.ds(pl.multiple_of(i * rt, rt), rt), :] = w1_ref[t0 + i] * r1 + w2_ref[t0 + i] * r2
            return carry
        lax.fori_loop(0, tq, combine, 0, unroll=8)


def _moe(hp, pos1, pos2, w1, w2, off, cnt, w_gate, w_up, w_down, layer):
    rt = ROW_TILES
    t = hp.shape[0] // rt
    _, ne, d, f = w_gate.shape
    nc = t // MOE_CHUNK
    tq = MOE_CHUNK // COMBINE_STEPS

    es = ne // EXPERTS_PER_STEP

    def w_idx(c, e, *_):
        return (layer, jnp.minimum(e, es - 1), 0, 0)

    grid_spec = pltpu.PrefetchScalarGridSpec(
        num_scalar_prefetch=6,
        grid=(nc, es + COMBINE_STEPS),
        in_specs=[
            pl.BlockSpec((MOE_CHUNK * rt, LANES), lambda c, e, *_: (c, 0)),
            pl.BlockSpec((None, EXPERTS_PER_STEP, d, f), w_idx),
            pl.BlockSpec((None, EXPERTS_PER_STEP, d, f), w_idx),
            pl.BlockSpec((None, EXPERTS_PER_STEP, f, d), w_idx),
        ],
        out_specs=pl.BlockSpec((tq * rt, LANES),
                               lambda c, e, *_: (c * COMBINE_STEPS + jnp.maximum(e - es, 0), 0)),
        scratch_shapes=[
            pltpu.SMEM((SLOTS_PAD + MOE_TILE,), jnp.int32),
            pltpu.VMEM((MOE_TILE * rt, LANES), F32),
            pltpu.VMEM(((SLOTS_PAD + MOE_TILE) * rt, LANES), F32),
        ],
    )
    return pl.pallas_call(
        _moe_kernel,
        grid_spec=grid_spec,
        out_shape=jax.ShapeDtypeStruct((t * rt, LANES), F32),
        compiler_params=_cparams(("arbitrary", "arbitrary")),
        name="moe_top2",
    )(pos1, pos2, w1, w2, off, cnt, hp, w_gate, w_up, w_down)


def _final_kernel(x_ref, y_ref, mod_ref, g_ref, o_ref):
    x = x_ref[...] + mod_ref[G_FFN:G_FFN + 1, :] * _load_row_tiled(y_ref)
    o_ref[...] = _rms(x) * g_ref[...]


def _final(x2, y, mod_l, g, seq, tm=512):
    t, d = x2.shape
    per_b = seq // tm
    return pl.pallas_call(
        _final_kernel,
        grid=(t // tm,),
        in_specs=[
            pl.BlockSpec((tm, d), lambda i: (i, 0)),
            pl.BlockSpec((tm * ROW_TILES, LANES), lambda i: (i, 0)),
            pl.BlockSpec((None, MOD_ROWS, d), lambda i: (i // per_b, 0, 0)),
            pl.BlockSpec((1, d), lambda i: (0, 0)),
        ],
        out_specs=pl.BlockSpec((tm, d), lambda i: (i, 0)),
        out_shape=jax.ShapeDtypeStruct((t, d), F32),
        compiler_params=_cparams(("arbitrary",)),
        name="final_norm",
    )(x2, y, mod_l, g)


def _retention_tables(seq):
    half = HEAD_DIM // 2
    inv_freq = ROPE_BASE ** (-jnp.arange(half, dtype=F32) / half)
    ang = jnp.arange(seq, dtype=F32)[:, None] * inv_freq[None, :]
    cos, sin = jnp.cos(ang), jnp.sin(ang)
    cos_t = jnp.tile(jnp.concatenate([cos, cos], axis=-1), (1, RET_HEADS))
    sin_t = jnp.tile(jnp.concatenate([-sin, sin], axis=-1), (1, RET_HEADS))
    log_gamma = jnp.log1p(-jnp.exp2(-5.0 - jnp.arange(RET_HEADS, dtype=F32)))
    tt = jnp.arange(CHUNK, dtype=F32)
    diff = tt[:, None] - tt[None, :]
    rdec = jnp.where(diff >= 0, jnp.exp(jnp.maximum(diff, 0.0)[None] * log_gamma[:, None, None]), 0.0)
    lg_lane = jnp.repeat(log_gamma, HEAD_DIM)[None, :]
    rdq = jnp.exp((tt[:, None] + 1.0) * lg_lane)
    rdk = jnp.exp((CHUNK - 1.0 - tt[:, None]) * lg_lane)
    rtot = jnp.exp(CHUNK * lg_lane)
    return cos_t, sin_t, rdec, rdq, rdk, rtot


def _pad_lanes(v, width=LANES):
    return jnp.pad(v, ((0, 0), (0, width - v.shape[-1])))


def kernel(x, c, norm_mix_g, norm_ffn_g, final_norm_g, w_ada, b_ada, w_in, conv_w, conv_b, ssd_dt_bias, ssd_a_log, ssd_d, ssd_norm_g, hgrn_lower_bounds, hgrn_norm_g, w_out, w_grp, b_grp, w_exp, b_exp, w_gate, w_up, w_down):
    batch, seq, d = x.shape
    depth = w_in.shape[0]
    t = batch * seq

    c8 = jnp.pad(c, ((0, MOD_ROWS - batch), (0, 0)))
    mod = _ada(c8, w_ada, b_ada)
    mod = mod[:, :batch].reshape(depth, batch, 6, d)
    mod = jnp.pad(mod, ((0, 0), (0, 0), (0, MOD_ROWS - 6), (0, 0)))

    dt0 = SSD_WIDTH + XBC_WIDTH
    w_in_p = jnp.concatenate(
        [w_in[:, :, :dt0], w_in[:, :, dt0 + SSD_HEADS:], w_in[:, :, dt0:dt0 + SSD_HEADS],
         jnp.zeros((depth, d, N_PROJ - w_in.shape[2]), w_in.dtype)], axis=-1).astype(BF16)
    w_out_b = w_out.astype(BF16)
    w_router = _pad_lanes(jnp.concatenate([w_grp, w_exp], axis=-1).reshape(depth * d, -1)).reshape(depth, d, ROUTER_LANES)
    w_router_hi = w_router.astype(BF16)
    w_router = jnp.stack([w_router_hi, (w_router - w_router_hi.astype(F32)).astype(BF16)], axis=1)
    b_router = _pad_lanes(jnp.concatenate([b_grp, b_exp], axis=-1))[:, None, :]
    dtb = _pad_lanes(ssd_dt_bias)[:, None, :]
    alog = _pad_lanes(ssd_a_log)[:, None, :]
    dexp = jnp.repeat(ssd_d, HEAD_DIM, axis=-1)[:, None, :]
    tables = _retention_tables(seq)

    x2 = x.reshape(t, d)
    y = None
    for layer in range(depth):
        x2, mixed = _mixer(x2, y, mod[layer - 1] if layer else None, mod[layer],
                           norm_mix_g[layer][None, :], w_in_p[layer], layer, batch, seq,
                           conv_w[layer], conv_b[layer][None, :], dtb[layer],
                           alog[layer], dexp[layer], ssd_norm_g[layer][None, :], hgrn_lower_bounds,
                           hgrn_norm_g[layer][None, :], *tables)
        x2, hp, route = _outproj(x2, mixed, mod[layer], norm_ffn_g[layer][None, :], w_out_b[layer],
                                 w_router[layer], b_router[layer], seq)
        slots, meta = _rank(route)
        y = _moe(hp, slots[S_POS1].astype(jnp.int32), slots[S_POS2].astype(jnp.int32), slots[S_W1], slots[S_W2],
                 meta[:, M_OFF, :N_EXPERTS].reshape(-1), meta[:, M_CNT, :N_EXPERTS].reshape(-1),
                 w_gate, w_up, w_down, layer)
    out = _final(x2, y, mod[depth - 1], final_norm_g[None, :], seq)
    return out.reshape(batch, seq, d)
```

```python
import functools
import math

import jax
import jax.numpy as jnp
import numpy as np
from jax import lax
from jax.experimental import pallas as pl
from jax.experimental.pallas import tpu as pltpu

F32 = jnp.float32
BF16 = jnp.bfloat16
HIGHEST = lax.Precision.HIGHEST

EPS = 1e-6
GATE_FLOOR = 1e-30
ROPE_BASE = 10000.0

HEAD_DIM = 64
SSD_HEADS = 8
SSD_GROUPS = 2
SSD_CONV = 4
HGRN_HEADS = 4
RET_HEADS = 4
N_GROUPS = 4
EXPERTS_PER_GROUP = 8
N_EXPERTS = N_GROUPS * EXPERTS_PER_GROUP
CHUNK = 128
LANES = 128
ROUTER_LANES = 128

SH_MIX, SC_MIX, G_MIX, SH_FFN, SC_FFN, G_FFN = range(6)
MOD_ROWS = 8

VMEM_LIMIT = 56 * 1024 * 1024


def _cparams(sem):
    return pltpu.CompilerParams(dimension_semantics=sem, vmem_limit_bytes=VMEM_LIMIT)


def _silu(x):
    return x * jax.nn.sigmoid(x)


def _rms(x):
    return x * lax.rsqrt(jnp.mean(x * x, axis=-1, keepdims=True) + EPS)


def _split3(x):
    hi = x.astype(BF16)
    r = x - hi.astype(F32)
    mid = r.astype(BF16)
    lo = (r - mid.astype(F32)).astype(BF16)
    return hi, mid, lo


def _dot(a, b):
    return jnp.dot(a, b, preferred_element_type=F32)


def _dot_nt(a, b):
    return lax.dot_general(a, b, (((1,), (1,)), ((), ())), preferred_element_type=F32)


def _dot_tn(a, b):
    return lax.dot_general(a, b, (((0,), (0,)), ((), ())), preferred_element_type=F32)


def _ada_kernel(c_ref, w_ref, b_ref, o_ref):
    c = _silu(c_ref[...])
    w = w_ref[...]
    c_hi, w_hi = c.astype(BF16), w.astype(BF16)
    c_lo = (c - c_hi.astype(F32)).astype(BF16)
    w_lo = (w - w_hi.astype(F32)).astype(BF16)
    o_ref[...] = (_dot(c_hi, w_hi) + _dot(c_lo, w_hi) + _dot(c_hi, w_lo)) + b_ref[...]


def _ada(c8, w_ada, b_ada):
    depth, d, d6 = w_ada.shape
    nb = d6 // d
    return pl.pallas_call(
        _ada_kernel,
        grid=(depth, nb),
        in_specs=[
            pl.BlockSpec((MOD_ROWS, d), lambda l, n: (0, 0)),
            pl.BlockSpec((None, d, d), lambda l, n: (l, 0, n)),
            pl.BlockSpec((None, 1, d), lambda l, n: (l, 0, n)),
        ],
        out_specs=pl.BlockSpec((None, MOD_ROWS, d), lambda l, n: (l, 0, n)),
        out_shape=jax.ShapeDtypeStruct((depth, MOD_ROWS, d6), F32),
        compiler_params=_cparams(("arbitrary", "arbitrary")),
        name="ada_mod",
    )(c8, w_ada, b_ada.reshape(depth, 1, d6))


OFF_Z, OFF_XBC, OFF_HQ, OFF_HF, OFF_HI, OFF_HG = 0, 512, 1280, 1536, 1792, 2048
OFF_RQ, OFF_RK, OFF_RV, OFF_RG, OFF_DT = 2304, 2560, 2816, 3072, 3328
N_PROJ = 3456
SSD_WIDTH = SSD_HEADS * HEAD_DIM
XBC_WIDTH = SSD_WIDTH + 2 * SSD_GROUPS * HEAD_DIM
HGRN_WIDTH = HGRN_HEADS * HEAD_DIM
RET_WIDTH = RET_HEADS * HEAD_DIM
TAIL = 8
HGRN_DIRECT = 8
HGRN_LEVELS = (8, 16, 32, 64)


def _pair_blockdiag(v_pair, lo_lane):
    return jnp.concatenate([jnp.where(lo_lane, v_pair, 0.0), jnp.where(lo_lane, 0.0, v_pair)], axis=0)


def _head_mean_square(o, bd):
    c, n = o.shape[0], o.shape[1] // LANES
    sq = (o * o).astype(BF16)
    ms = _dot(jnp.concatenate([sq[:, p * LANES:(p + 1) * LANES] for p in range(n)], axis=0), bd)
    return jnp.concatenate([ms[p * c:(p + 1) * c] for p in range(n)], axis=1)


def _pair_scores(q_pair, k_pair_b, lo_lane):
    c = q_pair.shape[0]
    q2 = jnp.concatenate([jnp.where(lo_lane, q_pair, 0.0), jnp.where(lo_lane, 0.0, q_pair)], axis=0)
    s = _dot_nt(q2.astype(BF16), k_pair_b)
    return s[:c], s[c:]


SIDE_POINTS = 9
N_MIXER_PARAMS = 14


def _mixer_kernel(layer, has_res, *refs):
    if has_res:
        x_ref, y_ref, modp_ref, mod_ref, g_ref, w_ref = refs[:6]
        rest = refs[6:]
    else:
        x_ref, mod_ref, g_ref, w_ref = refs[:4]
        rest = refs[4:]
    params, outs = rest[:N_MIXER_PARAMS], rest[N_MIXER_PARAMS:]
    if has_res:
        xo_ref, out_ref, proj_next, proj_cur, xbuf, s_ssd, s_hg, s_ret = outs
    else:
        out_ref, proj_next, proj_cur, xbuf, s_ssd, s_hg, s_ret = outs
    j = pl.program_id(0)
    nb = x_ref.shape[0]

    def reset_carried():
        xbuf[:, 0:TAIL, :] = jnp.zeros((nb, TAIL, XBC_WIDTH), F32)
        s_ssd[...] = jnp.zeros_like(s_ssd)
        s_hg[...] = jnp.zeros_like(s_hg)
        s_ret[...] = jnp.zeros_like(s_ret)

    @pl.when(j == 0)
    def _():
        reset_carried()
        proj_cur[...] = jnp.zeros_like(proj_cur)

    hs = []
    for b in range(nb):
        x = x_ref[b]
        if has_res:
            x = x + modp_ref[b, G_FFN:G_FFN + 1, :] * _load_row_tiled(y_ref.at[b])
            xo_ref[b] = x
        hs.append(_rms(x) * g_ref[...] * (1.0 + mod_ref[b, SC_MIX:SC_MIX + 1, :]) + mod_ref[b, SH_MIX:SH_MIX + 1, :])
    proj = _dot(jnp.concatenate(hs, axis=0).astype(BF16), w_ref[...])
    for b in range(nb):
        proj_next[b] = proj[b * CHUNK:(b + 1) * CHUNK]

    stages = [_mixer_body(layer, proj_cur.at[b], *params, out_ref.at[b], xbuf.at[b], s_ssd.at[b],
                          s_hg.at[b], s_ret.at[b]) for b in range(nb)]
    while stages:
        for gen in list(stages):
            if next(gen, StopIteration) is StopIteration:
                stages.remove(gen)
    proj_cur[...] = proj_next[...]

    @pl.when(j == 0)
    def _():
        reset_carried()


def _mixer_body(layer, proj_ref, convw_ref, convb_ref, dtb_ref, alog_ref, dexp_ref, ssdg_ref,
                lbraw_ref, hgg_ref, cos_ref, sin_ref, rdec_ref, rdq_ref, rdk_ref, rtot_ref,
                out_ref, xbuf, s_ssd, s_hg, s_ret):
    C = CHUNK

    row = lax.broadcasted_iota(jnp.int32, (C, LANES), 0)
    lane = lax.broadcasted_iota(jnp.int32, (C, LANES), 1)
    lo_lane = lane < HEAD_DIM
    causal = row >= lane
    bd_mask = jnp.right_shift(row, 6) == jnp.right_shift(lane, 6)
    bd_mean = jnp.where(bd_mask, 1.0 / HEAD_DIM, 0.0).astype(BF16)
    tril = jnp.where(causal, 1.0, 0.0).astype(BF16)

    def cumsum_rows(x):
        hi, mid, lo = _split3(x)
        return _dot(tril, hi) + _dot(tril, mid) + _dot(tril, lo)

    xbuf[TAIL:TAIL + C, :] = proj_ref[:, OFF_XBC:OFF_XBC + XBC_WIDTH]
    conv = convb_ref[...]
    for jj in range(SSD_CONV):
        off = TAIL - (SSD_CONV - 1) + jj
        conv = conv + convw_ref[jj:jj + 1, :] * xbuf[off:off + C, :]
    xbuf[0:TAIL, :] = xbuf[C:C + TAIL, :]
    xc = _silu(conv)
    xs = xc[:, 0:SSD_WIDTH]
    bm = xc[:, SSD_WIDTH:SSD_WIDTH + LANES]
    cm = xc[:, SSD_WIDTH + LANES:SSD_WIDTH + 2 * LANES]

    dt8 = jax.nn.softplus(proj_ref[:, OFF_DT:OFF_DT + LANES] + dtb_ref[...])
    la8 = dt8 * (-jnp.exp(alog_ref[...]))
    cs8 = cumsum_rows(la8)
    cs8t = cs8.T

    bm_b = bm.astype(BF16)
    cm_b = cm.astype(BF16)
    scores_g = _pair_scores(cm, bm_b, lo_lane)

    o_intra, e1_cols, e2_cols, dt_cols = [], [], [], []
    for p in range(SSD_HEADS // 2):
        a, b = 2 * p, 2 * p + 1
        g = a // (SSD_HEADS // SSD_GROUPS)
        col_a = jnp.broadcast_to(cs8[:, a:a + 1], (C, LANES))
        col_b = jnp.broadcast_to(cs8[:, b:b + 1], (C, LANES))
        cs_pair = jnp.where(lo_lane, col_a, col_b)
        dt_pair = jnp.where(lo_lane, jnp.broadcast_to(dt8[:, a:a + 1], (C, LANES)),
                            jnp.broadcast_to(dt8[:, b:b + 1], (C, LANES)))
        last = cs_pair[C - 1:C, :]
        e1_cols.append(jnp.exp(cs_pair))
        e2_cols.append(jnp.exp(last - cs_pair))
        dt_cols.append(dt_pair)
        dec_a = jnp.where(causal, jnp.exp(jnp.minimum(col_a - cs8t[a:a + 1, :], 0.0)), 0.0)
        dec_b = jnp.where(causal, jnp.exp(jnp.minimum(col_b - cs8t[b:b + 1, :], 0.0)), 0.0)
        pa = (scores_g[g] * dec_a).astype(BF16)
        pb = (scores_g[g] * dec_b).astype(BF16)
        v_pair = xs[:, p * LANES:(p + 1) * LANES] * dt_pair
        o_intra.append(_dot(jnp.concatenate([pa, pb], axis=1),
                            _pair_blockdiag(v_pair, lo_lane).astype(BF16)))
    o_intra = jnp.concatenate(o_intra, axis=1)
    e1 = jnp.concatenate(e1_cols, axis=1)
    e2 = jnp.concatenate(e2_cols, axis=1)
    dtx = jnp.concatenate(dt_cols, axis=1)
    v_all = xs * dtx
    s_prev = s_ssd[...]
    o_inter = e1 * _dot(cm_b, s_prev.astype(BF16))
    y = o_intra + o_inter + dexp_ref[...] * xs
    y = y * _silu(proj_ref[:, OFF_Z:OFF_Z + SSD_WIDTH])
    gw = SSD_WIDTH // SSD_GROUPS
    grp_mean = jnp.full((gw, gw), 1.0 / gw, BF16)
    ysq = (y * y).astype(BF16)
    ms = jnp.concatenate([_dot(ysq[:, g * gw:(g + 1) * gw], grp_mean) for g in range(SSD_GROUPS)], axis=1)
    out_ref[:, 0:SSD_WIDTH] = (y * lax.rsqrt(ms + EPS) * ssdg_ref[...]).astype(out_ref.dtype)
    row_s = lax.broadcasted_iota(jnp.int32, (LANES, SSD_WIDTH), 0)
    lane_s = lax.broadcasted_iota(jnp.int32, (LANES, SSD_WIDTH), 1)
    grp_mask = jnp.right_shift(row_s, 6) == jnp.right_shift(lane_s, 8)
    upd = _dot_tn(bm_b, (v_all * e2).astype(BF16))
    s_ssd[...] = e1[C - 1:C, :] * s_prev + jnp.where(grp_mask, upd, 0.0)
    yield

    W2 = HGRN_WIDTH
    row2 = lax.broadcasted_iota(jnp.int32, (C, W2), 0)
    lb_raw = lbraw_ref[...]
    lb_e = jnp.exp(lb_raw - jnp.max(lb_raw, axis=0, keepdims=True))
    lb_soft = lb_e / jnp.sum(lb_e, axis=0, keepdims=True)
    lb = jnp.sum(lb_soft[0:layer + 1, :], axis=0, keepdims=True) - lb_soft[0:1, :]
    forget = lb + (1.0 - lb) * jax.nn.sigmoid(proj_ref[:, OFF_HF:OFF_HF + W2])
    fc = jnp.maximum(forget, GATE_FLOOR)
    kk = 1.0 - forget
    hq = proj_ref[:, OFF_HQ:OFF_HQ + W2]
    hv = proj_ref[:, OFF_HI:OFF_HI + W2]
    cum = cumsum_rows(jnp.log(fc))
    npair = W2 // LANES

    sub = jnp.bitwise_and(row2, HGRN_DIRECT - 1)
    bd2 = jnp.where(jnp.right_shift(lax.broadcasted_iota(jnp.int32, (W2, W2), 0), 6)
                    == jnp.right_shift(lax.broadcasted_iota(jnp.int32, (W2, W2), 1), 6), 1.0, 0.0).astype(BF16)
    def prev_row(a):
        return pltpu.roll(a.reshape(C // HGRN_DIRECT, HGRN_DIRECT, W2), 1, 1).reshape(C, W2)

    not_first = sub != 0
    kdec = kk
    vprev = hv
    terms, vals = [(hq * kk).astype(BF16)], [hv]
    for dlt in range(1, HGRN_DIRECT):
        kdec = jnp.where(not_first, fc * prev_row(kdec), 0.0)
        vprev = prev_row(vprev)
        terms.append((hq * kdec).astype(BF16))
        vals.append(vprev)
    sc_all = _dot(jnp.concatenate(terms, axis=0), bd2)
    o_h = sc_all[0:C] * vals[0]
    for dlt in range(1, HGRN_DIRECT):
        o_h = o_h + sc_all[dlt * C:(dlt + 1) * C] * vals[dlt]
    yield
    p_sum = [[None, None] for _ in range(npair)]
    for m in HGRN_LEVELS:
        nb = C // (2 * m)
        cum3 = cum.reshape(nb, 2 * m, W2)
        ref = jnp.broadcast_to(cum3[:, m - 1:m, :], (nb, 2 * m, W2)).reshape(C, W2)
        right = jnp.bitwise_and(row2, m) != 0
        e = jnp.exp(jnp.where(right, cum - ref, ref - cum))
        qm = jnp.where(right, hq * e, 0.0)
        km = jnp.where(right, 0.0, kk * e).astype(BF16)
        sh = int(math.log2(2 * m))
        same_block = jnp.right_shift(row, sh) == jnp.right_shift(lane, sh)
        for p in range(npair):
            sl = slice(p * LANES, (p + 1) * LANES)
            for hh, s_h in enumerate(_pair_scores(qm[:, sl], km[:, sl], lo_lane)):
                sc = jnp.where(same_block, s_h, 0.0)
                p_sum[p][hh] = sc if p_sum[p][hh] is None else p_sum[p][hh] + sc
        yield
    cols = []
    for p in range(npair):
        vbd = _pair_blockdiag(hv[:, p * LANES:(p + 1) * LANES], lo_lane).astype(BF16)
        cols.append(_dot(jnp.concatenate([p_sum[p][0].astype(BF16), p_sum[p][1].astype(BF16)], axis=1), vbd))
    o_h = o_h + jnp.concatenate(cols, axis=1)
    q_in = (hq * jnp.exp(cum)).astype(BF16)
    last2 = cum[C - 1:C, :]
    k_end = (kk * jnp.exp(last2 - cum)).astype(BF16)
    tot2 = jnp.exp(last2)
    hv_b = hv.astype(BF16)
    inter = []
    for p in range(npair):
        sl = slice(p * LANES, (p + 1) * LANES)
        st = s_hg[p]
        inter.append(_dot_nt(q_in[:, sl], st.astype(BF16)))
        upd = _dot_tn(hv_b[:, sl], k_end[:, sl])
        s_hg[p] = st * tot2[:, sl] + jnp.where(bd_mask, upd, 0.0)
    o_h = o_h + jnp.concatenate(inter, axis=1)
    ms = _head_mean_square(o_h, bd_mean)
    o_h = o_h * lax.rsqrt(ms + EPS) * hgg_ref[...]
    o_h = o_h * jax.nn.sigmoid(proj_ref[:, OFF_HG:OFF_HG + W2])
    out_ref[:, SSD_WIDTH:SSD_WIDTH + W2] = o_h.astype(out_ref.dtype)
    yield

    W3 = RET_WIDTH
    lane3 = lax.broadcasted_iota(jnp.int32, (C, W3), 1)
    first_half = jnp.bitwise_and(lane3, HEAD_DIM // 2) == 0
    cosv = cos_ref[...]
    sinv = sin_ref[...]

    def rotary(xr):
        swapped = jnp.where(first_half, pltpu.roll(xr, W3 - HEAD_DIM // 2, 1),
                            pltpu.roll(xr, HEAD_DIM // 2, 1))
        return xr * cosv + swapped * sinv

    qr = rotary(proj_ref[:, OFF_RQ:OFF_RQ + W3])
    kr = rotary(proj_ref[:, OFF_RK:OFF_RK + W3]) * (HEAD_DIM ** -0.5)
    rv = proj_ref[:, OFF_RV:OFF_RV + W3]
    kr_b = kr.astype(BF16)
    q_in = (qr * rdq_ref[...]).astype(BF16)
    k_end = (kr * rdk_ref[...]).astype(BF16)
    rv_b = rv.astype(BF16)
    o_cols = []
    for p in range(W3 // LANES):
        sl = slice(p * LANES, (p + 1) * LANES)
        sa, sb = _pair_scores(qr[:, sl], kr_b[:, sl], lo_lane)
        sa = sa * rdec_ref[2 * p]
        sb = sb * rdec_ref[2 * p + 1]
        vbd = _pair_blockdiag(rv[:, sl], lo_lane).astype(BF16)
        o_p = _dot(jnp.concatenate([sa.astype(BF16), sb.astype(BF16)], axis=1), vbd)
        st = s_ret[p]
        o_p = o_p + _dot(q_in[:, sl], st.astype(BF16))
        upd = _dot_tn(k_end[:, sl], rv_b[:, sl])
        s_ret[p] = st * rtot_ref[:, sl] + jnp.where(bd_mask, upd, 0.0)
        o_cols.append(o_p)
        yield
    o_r = jnp.concatenate(o_cols, axis=1)
    ms = _head_mean_square(o_r, bd_mean)
    o_r = o_r * lax.rsqrt(ms + EPS) * _silu(proj_ref[:, OFF_RG:OFF_RG + W3])
    out_ref[:, SSD_WIDTH + W2:SSD_WIDTH + W2 + W3] = o_r.astype(out_ref.dtype)


def _mixer(x2, y, mod_prev, mod_l, g, w, layer, batch, seq, conv_w, conv_b, dtb, alog, dexp, ssdg, lbraw, hgg,
           cos_t, sin_t, rdec, rdq, rdk, rtot):
    t, d = x2.shape
    d_mix = SSD_WIDTH + HGRN_WIDTH + RET_WIDTH
    nj = seq // CHUNK
    const2 = lambda j: (0, 0)
    has_res = y is not None
    proj_idx = lambda j: (0, jnp.minimum(j, nj - 1), 0)
    mix_idx = lambda j: (0, jnp.maximum(j - 1, 0), 0)
    x_spec = pl.BlockSpec((batch, CHUNK, d), proj_idx)
    mod_spec = pl.BlockSpec((batch, MOD_ROWS, d), lambda j: (0, 0, 0))
    head_specs = [x_spec]
    head_args = [x2.reshape(batch, seq, d)]
    if has_res:
        head_specs += [pl.BlockSpec((batch, CHUNK * ROW_TILES, LANES), proj_idx), mod_spec]
        head_args += [y.reshape(batch, seq * ROW_TILES, LANES), mod_prev]
    head_specs += [mod_spec, pl.BlockSpec((1, d), const2),
                   pl.BlockSpec((None,) + w.shape[1:], lambda j: (layer, 0, 0))]
    head_args += [mod_l, g, w]
    table_spec = pl.BlockSpec((CHUNK, RET_WIDTH), lambda j: (jnp.maximum(j - 1, 0), 0))
    mixed_spec = pl.BlockSpec((batch, CHUNK, d_mix), mix_idx)
    mixed_shape = jax.ShapeDtypeStruct((batch, seq, d_mix), BF16)
    res = pl.pallas_call(
        functools.partial(_mixer_kernel, layer, has_res),
        grid=(nj + 1,),
        in_specs=head_specs + [
            pl.BlockSpec(conv_w.shape, const2),
            pl.BlockSpec(conv_b.shape, const2),
            pl.BlockSpec(dtb.shape, const2),
            pl.BlockSpec(alog.shape, const2),
            pl.BlockSpec(dexp.shape, const2),
            pl.BlockSpec(ssdg.shape, const2),
            pl.BlockSpec(lbraw.shape, const2),
            pl.BlockSpec(hgg.shape, const2),
            table_spec,
            table_spec,
            pl.BlockSpec(rdec.shape, lambda j: (0, 0, 0)),
            pl.BlockSpec(rdq.shape, const2),
            pl.BlockSpec(rdk.shape, const2),
            pl.BlockSpec(rtot.shape, const2),
        ],
        out_specs=[x_spec, mixed_spec] if has_res else mixed_spec,
        out_shape=[jax.ShapeDtypeStruct((batch, seq, d), F32), mixed_shape] if has_res else mixed_shape,
        scratch_shapes=[
            pltpu.VMEM((batch, CHUNK, N_PROJ), F32),
            pltpu.VMEM((batch, CHUNK, N_PROJ), F32),
            pltpu.VMEM((batch, TAIL + CHUNK, XBC_WIDTH), F32),
            pltpu.VMEM((batch, LANES, SSD_WIDTH), F32),
            pltpu.VMEM((batch, HGRN_WIDTH // LANES, LANES, LANES), F32),
            pltpu.VMEM((batch, RET_WIDTH // LANES, LANES, LANES), F32),
        ],
        compiler_params=_cparams(("arbitrary",)),
        name="norm_inproj_mixers",
    )(*head_args, conv_w, conv_b, dtb, alog, dexp, ssdg, lbraw, hgg, cos_t, sin_t, rdec, rdq, rdk, rtot)
    if has_res:
        return res[0].reshape(t, d), res[1].reshape(t, d_mix)
    return x2, res.reshape(t, d_mix)


ROW_TILES = 8


def _store_row_tiled(ref, val):
    m = val.shape[0]
    for k in range(ROW_TILES):
        ref[pl.ds(k, m, stride=ROW_TILES), :] = val[:, k * LANES:(k + 1) * LANES]


def _load_row_tiled(ref):
    m = ref.shape[0] // ROW_TILES
    return jnp.concatenate([ref[pl.ds(k, m, stride=ROW_TILES), :] for k in range(ROW_TILES)], axis=1)


R_E1, R_E2, R_W1, R_W2 = range(4)


def _outproj_kernel(x_ref, mixed_ref, mod_ref, g_ref, w_ref, wr_ref, br_ref,
                    xo_ref, hp_ref, route_ref):
    x = x_ref[...] + mod_ref[G_MIX:G_MIX + 1, :] * _dot(mixed_ref[...], w_ref[...])
    xo_ref[...] = x
    h = _rms(x) * g_ref[...] * (1.0 + mod_ref[SC_FFN:SC_FFN + 1, :]) + mod_ref[SH_FFN:SH_FFN + 1, :]
    _store_row_tiled(hp_ref, h)
    h_hi = h.astype(BF16)
    h_lo = (h - h_hi.astype(F32)).astype(BF16)
    logits = (_dot(h_hi, wr_ref[0]) + _dot(h_lo, wr_ref[0]) + _dot(h_hi, wr_ref[1])) + br_ref[...]
    lane = lax.broadcasted_iota(jnp.int32, logits.shape, 1).astype(F32)
    neg = -jnp.inf
    big = float(ROUTER_LANES)
    is_grp = lane < N_GROUPS
    gl = jnp.where(is_grp, logits, neg)
    gmax = jnp.max(gl, axis=-1, keepdims=True)
    g_idx = jnp.min(jnp.where(gl == gmax, lane, big), axis=-1, keepdims=True)
    p_grp = 1.0 / jnp.sum(jnp.where(is_grp, jnp.exp(gl - gmax), 0.0), axis=-1, keepdims=True)
    first = N_GROUPS + g_idx * EXPERTS_PER_GROUP
    valid = (lane >= first) & (lane < first + EXPERTS_PER_GROUP)
    el = jnp.where(valid, logits, neg)
    m1 = jnp.max(el, axis=-1, keepdims=True)
    i1 = jnp.min(jnp.where(el == m1, lane, big), axis=-1, keepdims=True)
    el2 = jnp.where(lane == i1, neg, el)
    m2 = jnp.max(el2, axis=-1, keepdims=True)
    i2 = jnp.min(jnp.where(el2 == m2, lane, big), axis=-1, keepdims=True)
    e = jnp.exp(m2 - m1)
    w1 = p_grp / (1.0 + e)
    w2 = p_grp * e / (1.0 + e)
    route_ref[...] = jnp.where(lane == R_E1, i1 - N_GROUPS,
                               jnp.where(lane == R_E2, i2 - N_GROUPS,
                                         jnp.where(lane == R_W1, w1, jnp.where(lane == R_W2, w2, 0.0))))


def _outproj(x2, mixed, mod_l, g, w, wr, br, seq, layer, tm=512):
    t, d = x2.shape
    per_b = seq // tm
    return pl.pallas_call(
        _outproj_kernel,
        grid=(t // tm,),
        in_specs=[
            pl.BlockSpec((tm, d), lambda i: (i, 0)),
            pl.BlockSpec((tm, mixed.shape[1]), lambda i: (i, 0)),
            pl.BlockSpec((None, MOD_ROWS, d), lambda i: (i // per_b, 0, 0)),
            pl.BlockSpec((1, d), lambda i: (0, 0)),
            pl.BlockSpec((None,) + w.shape[1:], lambda i: (layer, 0, 0)),
            pl.BlockSpec((None,) + wr.shape[1:], lambda i: (layer, 0, 0, 0)),
            pl.BlockSpec(br.shape, lambda i: (0, 0)),
        ],
        out_specs=[
            pl.BlockSpec((tm, d), lambda i: (i, 0)),
            pl.BlockSpec((tm * ROW_TILES, LANES), lambda i: (i, 0)),
            pl.BlockSpec((tm, ROUTER_LANES), lambda i: (i, 0)),
        ],
        out_shape=[
            jax.ShapeDtypeStruct((t, d), F32),
            jax.ShapeDtypeStruct((t * ROW_TILES, LANES), F32),
            jax.ShapeDtypeStruct((t, ROUTER_LANES), F32),
        ],
        compiler_params=_cparams(("arbitrary",)),
        name="outproj_router",
    )(x2, mixed, mod_l, g, w, wr, br)


MOE_CHUNK = 2048
MOE_TILE = 160
SEG_ALIGN = 8
COMBINE_STEPS = 4
EXPERTS_PER_STEP = 2
SLOTS_PAD = 2 * MOE_CHUNK + N_EXPERTS * SEG_ALIGN
RANK_BLOCK = 256
M_OFF, M_CNT = 0, 1
S_POS1, S_POS2, S_W1, S_W2 = 0, 1, R_W1, R_W2


def _rank_kernel(route_ref, slots_ref, meta_ref):
    tb = route_ref.shape[0]
    lane = lax.broadcasted_iota(jnp.int32, (RANK_BLOCK, ROUTER_LANES), 1).astype(F32)
    r_i = lax.broadcasted_iota(jnp.int32, (RANK_BLOCK, RANK_BLOCK), 0)
    c_i = lax.broadcasted_iota(jnp.int32, (RANK_BLOCK, RANK_BLOCK), 1)
    strict_tril = jnp.where(r_i > c_i, 1.0, 0.0).astype(BF16)
    carry = jnp.zeros((1, ROUTER_LANES), F32)
    ranks = []
    for b in range(tb // RANK_BLOCK):
        blk = route_ref[b * RANK_BLOCK:(b + 1) * RANK_BLOCK, :]
        onehot = jnp.where((lane == blk[:, R_E1:R_E1 + 1]) | (lane == blk[:, R_E2:R_E2 + 1]), 1.0, 0.0)
        ranks.append(_dot(strict_tril, onehot.astype(BF16)) + carry)
        carry = carry + jnp.sum(onehot, axis=0, keepdims=True)
    cnt = carry
    seg = jnp.floor((cnt + (SEG_ALIGN - 1)) * (1.0 / SEG_ALIGN)) * SEG_ALIGN
    u_r = lax.broadcasted_iota(jnp.int32, (ROUTER_LANES, ROUTER_LANES), 0)
    u_c = lax.broadcasted_iota(jnp.int32, (ROUTER_LANES, ROUTER_LANES), 1)
    strict_triu = jnp.where(u_r < u_c, 1.0, 0.0).astype(BF16)
    seg_parts = _split3(jnp.broadcast_to(seg, (8, ROUTER_LANES)))
    off = (_dot(seg_parts[0], strict_triu) + _dot(seg_parts[1], strict_triu)
           + _dot(seg_parts[2], strict_triu))[0:1, :]
    for b in range(tb // RANK_BLOCK):
        blk = route_ref[b * RANK_BLOCK:(b + 1) * RANK_BLOCK, :]
        dest = off + ranks[b]
        p1 = jnp.sum(jnp.where(lane == blk[:, R_E1:R_E1 + 1], dest, 0.0), axis=-1, keepdims=True)
        p2 = jnp.sum(jnp.where(lane == blk[:, R_E2:R_E2 + 1], dest, 0.0), axis=-1, keepdims=True)
        rec = jnp.where(lane == S_POS1, p1, jnp.where(lane == S_POS2, p2,
                        jnp.where((lane == S_W1) | (lane == S_W2), blk, 0.0)))
        for h in range(RANK_BLOCK // LANES):
            t0 = b * RANK_BLOCK + h * LANES
            slots_ref[:, t0:t0 + LANES] = rec[h * LANES:(h + 1) * LANES, :].T[0:8, :]
    row8 = lax.broadcasted_iota(jnp.int32, (8, ROUTER_LANES), 0)
    meta_ref[...] = jnp.where(row8 == M_OFF, off, jnp.where(row8 == M_CNT, cnt, 0.0)).astype(jnp.int32)


def _rank(route):
    t = route.shape[0]
    nc = t // MOE_CHUNK
    return pl.pallas_call(
        _rank_kernel,
        grid=(nc,),
        in_specs=[pl.BlockSpec((MOE_CHUNK, ROUTER_LANES), lambda c: (c, 0))],
        out_specs=[pl.BlockSpec((8, MOE_CHUNK), lambda c: (0, c)),
                   pl.BlockSpec((None, 8, ROUTER_LANES), lambda c: (c, 0, 0))],
        out_shape=[jax.ShapeDtypeStruct((8, t), F32),
                   jax.ShapeDtypeStruct((nc, 8, ROUTER_LANES), jnp.int32)],
        compiler_params=_cparams(("arbitrary",)),
        name="slot_rank",
    )(route)


def _moe_kernel(pos1_ref, pos2_ref, w1_ref, w2_ref, off_ref, cnt_ref,
                hp_ref, wg_ref, wu_ref, wd_ref, y_ref, inv_ref, xt_ref, ys_ref):
    c = pl.program_id(0)
    e = pl.program_id(1)
    rt = ROW_TILES
    tb = hp_ref.shape[0] // rt
    tok0 = c * tb
    seg0 = c * N_EXPERTS

    @pl.when((c == 0) & (e == 0))
    def _():
        xt_ref[...] = jnp.zeros_like(xt_ref)

    @pl.when(e == 0)
    def _():
        def pads(ee, carry):
            o = off_ref[seg0 + ee]
            n_e = cnt_ref[seg0 + ee]

            def zero(i, carry2):
                inv_ref[o + i] = 0
                return carry2
            lax.fori_loop(n_e, (n_e + SEG_ALIGN - 1) // SEG_ALIGN * SEG_ALIGN, zero, 0)
            return carry
        lax.fori_loop(0, N_EXPERTS, pads, 0)
        n_last = cnt_ref[seg0 + N_EXPERTS - 1]
        total = off_ref[seg0 + N_EXPERTS - 1] + (n_last + SEG_ALIGN - 1) // SEG_ALIGN * SEG_ALIGN

        def zero_tail(i, carry):
            inv_ref[total + i] = 0
            return carry
        lax.fori_loop(0, MOE_TILE, zero_tail, 0)

        def place(i, carry):
            inv_ref[pos1_ref[tok0 + i]] = i
            inv_ref[pos2_ref[tok0 + i]] = i
            return carry
        lax.fori_loop(0, tb, place, 0, unroll=8)

    expert_steps = N_EXPERTS // EXPERTS_PER_STEP
    is_expert = e < expert_steps
    for k in range(EXPERTS_PER_STEP):
        e_k = jnp.minimum(e, expert_steps - 1) * EXPERTS_PER_STEP + k
        n = jnp.where(is_expert, cnt_ref[seg0 + e_k], 0)
        base = off_ref[seg0 + e_k]

        @pl.when(n > 0)
        def _(k=k, n=n, base=base):
            wg = wg_ref[k].astype(BF16)
            wu = wu_ref[k].astype(BF16)
            wd = wd_ref[k].astype(BF16)

            def tile(i, carry):
                start = pl.multiple_of(base + i * MOE_TILE, SEG_ALIGN)

                def gather(r, carry2):
                    src = pl.multiple_of(inv_ref[start + r] * rt, rt)
                    xt_ref[pl.ds(pl.multiple_of(r * rt, rt), rt), :] = hp_ref[pl.ds(src, rt), :]
                    return carry2
                rows = jnp.minimum(MOE_TILE, (n - i * MOE_TILE + SEG_ALIGN - 1) // SEG_ALIGN * SEG_ALIGN)
                def gather_group(g, carry2):
                    for u in range(SEG_ALIGN):
                        gather(g * SEG_ALIGN + u, carry2)
                    return carry2
                lax.fori_loop(0, rows // SEG_ALIGN, gather_group, 0)
                xrow = jnp.concatenate([xt_ref[pl.ds(j, MOE_TILE, stride=rt), :] for j in range(rt)], axis=1)
                xb = xrow.astype(BF16)
                hid = (_silu(_dot(xb, wg)) * _dot(xb, wu)).astype(BF16)
                out = _dot(hid, wd)
                for j in range(rt):
                    ys_ref[pl.ds(start * rt + j, MOE_TILE, stride=rt), :] = out[:, j * LANES:(j + 1) * LANES]
                return carry
            lax.fori_loop(0, (n + MOE_TILE - 1) // MOE_TILE, tile, 0)

    @pl.when(e >= expert_steps)
    def _():
        tq = y_ref.shape[0] // rt
        t0 = tok0 + (e - expert_steps) * tq

        def combine(i, carry):
            r1 = ys_ref[pl.ds(pl.multiple_of(pos1_ref[t0 + i] * rt, rt), rt), :]
            r2 = ys_ref[pl.ds(pl.multiple_of(pos2_ref[t0 + i] * rt, rt), rt), :]
            y_ref[pl.ds(pl.multiple_of(i * rt, rt), rt), :] = w1_ref[t0 + i] * r1 + w2_ref[t0 + i] * r2
            return carry
        lax.fori_loop(0, tq, combine, 0, unroll=8)


def _moe(hp, pos1, pos2, w1, w2, off, cnt, w_gate, w_up, w_down, layer):
    rt = ROW_TILES
    t = hp.shape[0] // rt
    _, ne, d, f = w_gate.shape
    nc = t // MOE_CHUNK
    tq = MOE_CHUNK // COMBINE_STEPS

    es = ne // EXPERTS_PER_STEP

    def w_idx(c, e, *_):
        return (layer, jnp.minimum(e, es - 1), 0, 0)

    grid_spec = pltpu.PrefetchScalarGridSpec(
        num_scalar_prefetch=6,
        grid=(nc, es + COMBINE_STEPS),
        in_specs=[
            pl.BlockSpec((MOE_CHUNK * rt, LANES), lambda c, e, *_: (c, 0)),
            pl.BlockSpec((None, EXPERTS_PER_STEP, d, f), w_idx),
            pl.BlockSpec((None, EXPERTS_PER_STEP, d, f), w_idx),
            pl.BlockSpec((None, EXPERTS_PER_STEP, f, d), w_idx),
        ],
        out_specs=pl.BlockSpec((tq * rt, LANES),
                               lambda c, e, *_: (c * COMBINE_STEPS + jnp.maximum(e - es, 0), 0)),
        scratch_shapes=[
            pltpu.SMEM((SLOTS_PAD + MOE_TILE,), jnp.int32),
            pltpu.VMEM((MOE_TILE * rt, LANES), F32),
            pltpu.VMEM(((SLOTS_PAD + MOE_TILE) * rt, LANES), F32),
        ],
    )
    return pl.pallas_call(
        _moe_kernel,
        grid_spec=grid_spec,
        out_shape=jax.ShapeDtypeStruct((t * rt, LANES), F32),
        compiler_params=_cparams(("arbitrary", "arbitrary")),
        name="moe_top2",
    )(pos1, pos2, w1, w2, off, cnt, hp, w_gate, w_up, w_down)


def _final_kernel(x_ref, y_ref, mod_ref, g_ref, o_ref):
    x = x_ref[...] + mod_ref[G_FFN:G_FFN + 1, :] * _load_row_tiled(y_ref)
    o_ref[...] = _rms(x) * g_ref[...]


def _final(x2, y, mod_l, g, seq, tm=512):
    t, d = x2.shape
    per_b = seq // tm
    return pl.pallas_call(
        _final_kernel,
        grid=(t // tm,),
        in_specs=[
            pl.BlockSpec((tm, d), lambda i: (i, 0)),
            pl.BlockSpec((tm * ROW_TILES, LANES), lambda i: (i, 0)),
            pl.BlockSpec((None, MOD_ROWS, d), lambda i: (i // per_b, 0, 0)),
            pl.BlockSpec((1, d), lambda i: (0, 0)),
        ],
        out_specs=pl.BlockSpec((tm, d), lambda i: (i, 0)),
        out_shape=jax.ShapeDtypeStruct((t, d), F32),
        compiler_params=_cparams(("arbitrary",)),
        name="final_norm",
    )(x2, y, mod_l, g)


def _retention_tables(seq):
    half = HEAD_DIM // 2
    inv_freq = ROPE_BASE ** (-jnp.arange(half, dtype=F32) / half)
    ang = jnp.arange(seq, dtype=F32)[:, None] * inv_freq[None, :]
    cos, sin = jnp.cos(ang), jnp.sin(ang)
    cos_t = jnp.tile(jnp.concatenate([cos, cos], axis=-1), (1, RET_HEADS))
    sin_t = jnp.tile(jnp.concatenate([-sin, sin], axis=-1), (1, RET_HEADS))
    log_gamma = jnp.log1p(-jnp.exp2(-5.0 - jnp.arange(RET_HEADS, dtype=F32)))
    tt = jnp.arange(CHUNK, dtype=F32)
    diff = tt[:, None] - tt[None, :]
    rdec = jnp.where(diff >= 0, jnp.exp(jnp.maximum(diff, 0.0)[None] * log_gamma[:, None, None]), 0.0)
    lg_lane = jnp.repeat(log_gamma, HEAD_DIM)[None, :]
    rdq = jnp.exp((tt[:, None] + 1.0) * lg_lane)
    rdk = jnp.exp((CHUNK - 1.0 - tt[:, None]) * lg_lane)
    rtot = jnp.exp(CHUNK * lg_lane)
    return cos_t, sin_t, rdec, rdq, rdk, rtot


def _pad_lanes(v, width=LANES):
    return jnp.pad(v, ((0, 0), (0, width - v.shape[-1])))


def kernel(x, c, norm_mix_g, norm_ffn_g, final_norm_g, w_ada, b_ada, w_in, conv_w, conv_b, ssd_dt_bias, ssd_a_log, ssd_d, ssd_norm_g, hgrn_lower_bounds, hgrn_norm_g, w_out, w_grp, b_grp, w_exp, b_exp, w_gate, w_up, w_down):
    batch, seq, d = x.shape
    depth = w_in.shape[0]
    t = batch * seq

    c8 = jnp.pad(c, ((0, MOD_ROWS - batch), (0, 0)))
    mod = _ada(c8, w_ada, b_ada)
    mod = mod[:, :batch].reshape(depth, batch, 6, d)
    mod = jnp.pad(mod, ((0, 0), (0, 0), (0, MOD_ROWS - 6), (0, 0)))

    dt0 = SSD_WIDTH + XBC_WIDTH
    w_in_p = jnp.concatenate(
        [w_in[:, :, :dt0], w_in[:, :, dt0 + SSD_HEADS:], w_in[:, :, dt0:dt0 + SSD_HEADS],
         jnp.zeros((depth, d, N_PROJ - w_in.shape[2]), w_in.dtype)], axis=-1).astype(BF16)
    w_out_b = w_out.astype(BF16)
    w_router = _pad_lanes(jnp.concatenate([w_grp, w_exp], axis=-1).reshape(depth * d, -1)).reshape(depth, d, ROUTER_LANES)
    w_router_hi = w_router.astype(BF16)
    w_router = jnp.stack([w_router_hi, (w_router - w_router_hi.astype(F32)).astype(BF16)], axis=1)
    b_router = _pad_lanes(jnp.concatenate([b_grp, b_exp], axis=-1))[:, None, :]
    dtb = _pad_lanes(ssd_dt_bias)[:, None, :]
    alog = _pad_lanes(ssd_a_log)[:, None, :]
    dexp = jnp.repeat(ssd_d, HEAD_DIM, axis=-1)[:, None, :]
    tables = _retention_tables(seq)

    x2 = x.reshape(t, d)
    y = None
    for layer in range(depth):
        x2, mixed = _mixer(x2, y, mod[layer - 1] if layer else None, mod[layer],
                           norm_mix_g[layer][None, :], w_in_p, layer, batch, seq,
                           conv_w[layer], conv_b[layer][None, :], dtb[layer],
                           alog[layer], dexp[layer], ssd_norm_g[layer][None, :], hgrn_lower_bounds,
                           hgrn_norm_g[layer][None, :], *tables)
        x2, hp, route = _outproj(x2, mixed, mod[layer], norm_ffn_g[layer][None, :], w_out_b,
                                 w_router, b_router[layer], seq, layer)
        slots, meta = _rank(route)
        y = _moe(hp, slots[S_POS1].astype(jnp.int32), slots[S_POS2].astype(jnp.int32), slots[S_W1], slots[S_W2],
                 meta[:, M_OFF, :N_EXPERTS].reshape(-1), meta[:, M_CNT, :N_EXPERTS].reshape(-1),
                 w_gate, w_up, w_down, layer)
    out = _final(x2, y, mod[depth - 1], final_norm_g[None, :], seq)
    return out.reshape(batch, seq, d)
```

```python
import functools
import math

import jax
import jax.numpy as jnp
import numpy as np
from jax import lax
from jax.experimental import pallas as pl
from jax.experimental.pallas import tpu as pltpu

F32 = jnp.float32
BF16 = jnp.bfloat16
HIGHEST = lax.Precision.HIGHEST

EPS = 1e-6
GATE_FLOOR = 1e-30
ROPE_BASE = 10000.0

HEAD_DIM = 64
SSD_HEADS = 8
SSD_GROUPS = 2
SSD_CONV = 4
HGRN_HEADS = 4
RET_HEADS = 4
N_GROUPS = 4
EXPERTS_PER_GROUP = 8
N_EXPERTS = N_GROUPS * EXPERTS_PER_GROUP
CHUNK = 128
LANES = 128
ROUTER_LANES = 128

SH_MIX, SC_MIX, G_MIX, SH_FFN, SC_FFN, G_FFN = range(6)
MOD_ROWS = 8

VMEM_LIMIT = 56 * 1024 * 1024


def _cparams(sem):
    return pltpu.CompilerParams(dimension_semantics=sem, vmem_limit_bytes=VMEM_LIMIT)


def _silu(x):
    return x * jax.nn.sigmoid(x)


def _rms(x):
    return x * lax.rsqrt(jnp.mean(x * x, axis=-1, keepdims=True) + EPS)


def _split3(x):
    hi = x.astype(BF16)
    r = x - hi.astype(F32)
    mid = r.astype(BF16)
    lo = (r - mid.astype(F32)).astype(BF16)
    return hi, mid, lo


def _dot(a, b):
    return jnp.dot(a, b, preferred_element_type=F32)


def _dot_nt(a, b):
    return lax.dot_general(a, b, (((1,), (1,)), ((), ())), preferred_element_type=F32)


def _dot_tn(a, b):
    return lax.dot_general(a, b, (((0,), (0,)), ((), ())), preferred_element_type=F32)


def _ada_kernel(c_ref, w_ref, b_ref, o_ref):
    c = _silu(c_ref[...])
    w = w_ref[...]
    c_hi, w_hi = c.astype(BF16), w.astype(BF16)
    c_lo = (c - c_hi.astype(F32)).astype(BF16)
    w_lo = (w - w_hi.astype(F32)).astype(BF16)
    o_ref[...] = (_dot(c_hi, w_hi) + _dot(c_lo, w_hi) + _dot(c_hi, w_lo)) + b_ref[...]


def _ada(c8, w_ada, b_ada):
    depth, d, d6 = w_ada.shape
    nb = d6 // d
    return pl.pallas_call(
        _ada_kernel,
        grid=(depth, nb),
        in_specs=[
            pl.BlockSpec((MOD_ROWS, d), lambda l, n: (0, 0)),
            pl.BlockSpec((None, d, d), lambda l, n: (l, 0, n)),
            pl.BlockSpec((None, 1, d), lambda l, n: (l, 0, n)),
        ],
        out_specs=pl.BlockSpec((None, MOD_ROWS, d), lambda l, n: (l, 0, n)),
        out_shape=jax.ShapeDtypeStruct((depth, MOD_ROWS, d6), F32),
        compiler_params=_cparams(("arbitrary", "arbitrary")),
        name="ada_mod",
    )(c8, w_ada, b_ada.reshape(depth, 1, d6))


OFF_Z, OFF_XBC, OFF_HQ, OFF_HF, OFF_HI, OFF_HG = 0, 512, 1280, 1536, 1792, 2048
OFF_RQ, OFF_RK, OFF_RV, OFF_RG, OFF_DT = 2304, 2560, 2816, 3072, 3328
N_PROJ = 3456
SSD_WIDTH = SSD_HEADS * HEAD_DIM
XBC_WIDTH = SSD_WIDTH + 2 * SSD_GROUPS * HEAD_DIM
HGRN_WIDTH = HGRN_HEADS * HEAD_DIM
RET_WIDTH = RET_HEADS * HEAD_DIM
TAIL = 8
HGRN_DIRECT = 8
HGRN_LEVELS = (8, 16, 32, 64)


def _pair_blockdiag(v_pair, lo_lane):
    return jnp.concatenate([jnp.where(lo_lane, v_pair, 0.0), jnp.where(lo_lane, 0.0, v_pair)], axis=0)


def _head_mean_square(o, bd):
    c, n = o.shape[0], o.shape[1] // LANES
    sq = (o * o).astype(BF16)
    ms = _dot(jnp.concatenate([sq[:, p * LANES:(p + 1) * LANES] for p in range(n)], axis=0), bd)
    return jnp.concatenate([ms[p * c:(p + 1) * c] for p in range(n)], axis=1)


def _pair_scores(q_pair, k_pair_b, lo_lane):
    c = q_pair.shape[0]
    q2 = jnp.concatenate([jnp.where(lo_lane, q_pair, 0.0), jnp.where(lo_lane, 0.0, q_pair)], axis=0)
    s = _dot_nt(q2.astype(BF16), k_pair_b)
    return s[:c], s[c:]


SIDE_POINTS = 9
N_MIXER_PARAMS = 14


def _mixer_kernel(layer, has_res, *refs):
    if has_res:
        x_ref, y_ref, modp_ref, mod_ref, g_ref, w_ref = refs[:6]
        rest = refs[6:]
    else:
        x_ref, mod_ref, g_ref, w_ref = refs[:4]
        rest = refs[4:]
    params, outs = rest[:N_MIXER_PARAMS], rest[N_MIXER_PARAMS:]
    if has_res:
        xo_ref, out_ref, proj_next, proj_cur, xbuf, s_ssd, s_hg, s_ret = outs
    else:
        out_ref, proj_next, proj_cur, xbuf, s_ssd, s_hg, s_ret = outs
    j = pl.program_id(0)
    nb = x_ref.shape[0]

    def reset_carried():
        xbuf[:, 0:TAIL, :] = jnp.zeros((nb, TAIL, XBC_WIDTH), F32)
        s_ssd[...] = jnp.zeros_like(s_ssd)
        s_hg[...] = jnp.zeros_like(s_hg)
        s_ret[...] = jnp.zeros_like(s_ret)

    @pl.when(j == 0)
    def _():
        reset_carried()
        proj_cur[...] = jnp.zeros_like(proj_cur)

    hs = []
    for b in range(nb):
        x = x_ref[b]
        if has_res:
            x = x + modp_ref[b, G_FFN:G_FFN + 1, :] * _load_row_tiled(y_ref.at[b])
            xo_ref[b] = x
        hs.append(_rms(x) * g_ref[...] * (1.0 + mod_ref[b, SC_MIX:SC_MIX + 1, :]) + mod_ref[b, SH_MIX:SH_MIX + 1, :])
    proj = _dot(jnp.concatenate(hs, axis=0).astype(BF16), w_ref[...])
    for b in range(nb):
        proj_next[b] = proj[b * CHUNK:(b + 1) * CHUNK]

    stages = [_mixer_body(layer, proj_cur.at[b], *params, out_ref.at[b], xbuf.at[b], s_ssd.at[b],
                          s_hg.at[b], s_ret.at[b]) for b in range(nb)]
    while stages:
        for gen in list(stages):
            if next(gen, StopIteration) is StopIteration:
                stages.remove(gen)
    proj_cur[...] = proj_next[...]

    @pl.when(j == 0)
    def _():
        reset_carried()


def _mixer_body(layer, proj_ref, convw_ref, convb_ref, dtb_ref, alog_ref, dexp_ref, ssdg_ref,
                lbraw_ref, hgg_ref, cos_ref, sin_ref, rdec_ref, rdq_ref, rdk_ref, rtot_ref,
                out_ref, xbuf, s_ssd, s_hg, s_ret):
    C = CHUNK

    row = lax.broadcasted_iota(jnp.int32, (C, LANES), 0)
    lane = lax.broadcasted_iota(jnp.int32, (C, LANES), 1)
    lo_lane = lane < HEAD_DIM
    causal = row >= lane
    bd_mask = jnp.right_shift(row, 6) == jnp.right_shift(lane, 6)
    bd_mean = jnp.where(bd_mask, 1.0 / HEAD_DIM, 0.0).astype(BF16)
    tril = jnp.where(causal, 1.0, 0.0).astype(BF16)

    def cumsum_rows(x):
        hi, mid, lo = _split3(x)
        return _dot(tril, hi) + _dot(tril, mid) + _dot(tril, lo)

    xbuf[TAIL:TAIL + C, :] = proj_ref[:, OFF_XBC:OFF_XBC + XBC_WIDTH]
    conv = convb_ref[...]
    for jj in range(SSD_CONV):
        off = TAIL - (SSD_CONV - 1) + jj
        conv = conv + convw_ref[jj:jj + 1, :] * xbuf[off:off + C, :]
    xbuf[0:TAIL, :] = xbuf[C:C + TAIL, :]
    xc = _silu(conv)
    xs = xc[:, 0:SSD_WIDTH]
    bm = xc[:, SSD_WIDTH:SSD_WIDTH + LANES]
    cm = xc[:, SSD_WIDTH + LANES:SSD_WIDTH + 2 * LANES]

    dt8 = jax.nn.softplus(proj_ref[:, OFF_DT:OFF_DT + LANES] + dtb_ref[...])
    la8 = dt8 * (-jnp.exp(alog_ref[...]))
    cs8 = cumsum_rows(la8)
    cs8t = cs8.T

    bm_b = bm.astype(BF16)
    cm_b = cm.astype(BF16)
    scores_g = _pair_scores(cm, bm_b, lo_lane)

    o_intra, e1_cols, e2_cols, dt_cols = [], [], [], []
    for p in range(SSD_HEADS // 2):
        a, b = 2 * p, 2 * p + 1
        g = a // (SSD_HEADS // SSD_GROUPS)
        col_a = jnp.broadcast_to(cs8[:, a:a + 1], (C, LANES))
        col_b = jnp.broadcast_to(cs8[:, b:b + 1], (C, LANES))
        cs_pair = jnp.where(lo_lane, col_a, col_b)
        dt_pair = jnp.where(lo_lane, jnp.broadcast_to(dt8[:, a:a + 1], (C, LANES)),
                            jnp.broadcast_to(dt8[:, b:b + 1], (C, LANES)))
        last = cs_pair[C - 1:C, :]
        e1_cols.append(jnp.exp(cs_pair))
        e2_cols.append(jnp.exp(last - cs_pair))
        dt_cols.append(dt_pair)
        dec_a = jnp.where(causal, jnp.exp(jnp.minimum(col_a - cs8t[a:a + 1, :], 0.0)), 0.0)
        dec_b = jnp.where(causal, jnp.exp(jnp.minimum(col_b - cs8t[b:b + 1, :], 0.0)), 0.0)
        pa = (scores_g[g] * dec_a).astype(BF16)
        pb = (scores_g[g] * dec_b).astype(BF16)
        v_pair = xs[:, p * LANES:(p + 1) * LANES] * dt_pair
        o_intra.append(_dot(jnp.concatenate([pa, pb], axis=1),
                            _pair_blockdiag(v_pair, lo_lane).astype(BF16)))
    o_intra = jnp.concatenate(o_intra, axis=1)
    e1 = jnp.concatenate(e1_cols, axis=1)
    e2 = jnp.concatenate(e2_cols, axis=1)
    dtx = jnp.concatenate(dt_cols, axis=1)
    v_all = xs * dtx
    s_prev = s_ssd[...]
    o_inter = e1 * _dot(cm_b, s_prev.astype(BF16))
    y = o_intra + o_inter + dexp_ref[...] * xs
    y = y * _silu(proj_ref[:, OFF_Z:OFF_Z + SSD_WIDTH])
    gw = SSD_WIDTH // SSD_GROUPS
    grp_mean = jnp.full((gw, gw), 1.0 / gw, BF16)
    ysq = (y * y).astype(BF16)
    ms = jnp.concatenate([_dot(ysq[:, g * gw:(g + 1) * gw], grp_mean) for g in range(SSD_GROUPS)], axis=1)
    out_ref[:, 0:SSD_WIDTH] = (y * lax.rsqrt(ms + EPS) * ssdg_ref[...]).astype(out_ref.dtype)
    row_s = lax.broadcasted_iota(jnp.int32, (LANES, SSD_WIDTH), 0)
    lane_s = lax.broadcasted_iota(jnp.int32, (LANES, SSD_WIDTH), 1)
    grp_mask = jnp.right_shift(row_s, 6) == jnp.right_shift(lane_s, 8)
    upd = _dot_tn(bm_b, (v_all * e2).astype(BF16))
    s_ssd[...] = e1[C - 1:C, :] * s_prev + jnp.where(grp_mask, upd, 0.0)
    yield

    W2 = HGRN_WIDTH
    row2 = lax.broadcasted_iota(jnp.int32, (C, W2), 0)
    lb_raw = lbraw_ref[...]
    lb_e = jnp.exp(lb_raw - jnp.max(lb_raw, axis=0, keepdims=True))
    lb_soft = lb_e / jnp.sum(lb_e, axis=0, keepdims=True)
    lb = jnp.sum(lb_soft[0:layer + 1, :], axis=0, keepdims=True) - lb_soft[0:1, :]
    forget = lb + (1.0 - lb) * jax.nn.sigmoid(proj_ref[:, OFF_HF:OFF_HF + W2])
    fc = jnp.maximum(forget, GATE_FLOOR)
    kk = 1.0 - forget
    hq = proj_ref[:, OFF_HQ:OFF_HQ + W2]
    hv = proj_ref[:, OFF_HI:OFF_HI + W2]
    cum = cumsum_rows(jnp.log(fc))
    npair = W2 // LANES

    sub = jnp.bitwise_and(row2, HGRN_DIRECT - 1)
    bd2 = jnp.where(jnp.right_shift(lax.broadcasted_iota(jnp.int32, (W2, W2), 0), 6)
                    == jnp.right_shift(lax.broadcasted_iota(jnp.int32, (W2, W2), 1), 6), 1.0, 0.0).astype(BF16)
    def prev_row(a):
        return pltpu.roll(a.reshape(C // HGRN_DIRECT, HGRN_DIRECT, W2), 1, 1).reshape(C, W2)

    not_first = sub != 0
    kdec = kk
    vprev = hv
    terms, vals = [(hq * kk).astype(BF16)], [hv]
    for dlt in range(1, HGRN_DIRECT):
        kdec = jnp.where(not_first, fc * prev_row(kdec), 0.0)
        vprev = prev_row(vprev)
        terms.append((hq * kdec).astype(BF16))
        vals.append(vprev)
    sc_all = _dot(jnp.concatenate(terms, axis=0), bd2)
    o_h = sc_all[0:C] * vals[0]
    for dlt in range(1, HGRN_DIRECT):
        o_h = o_h + sc_all[dlt * C:(dlt + 1) * C] * vals[dlt]
    yield
    p_sum = [[None, None] for _ in range(npair)]
    for m in HGRN_LEVELS:
        nb = C // (2 * m)
        cum3 = cum.reshape(nb, 2 * m, W2)
        ref = jnp.broadcast_to(cum3[:, m - 1:m, :], (nb, 2 * m, W2)).reshape(C, W2)
        right = jnp.bitwise_and(row2, m) != 0
        e = jnp.exp(jnp.where(right, cum - ref, ref - cum))
        qm = jnp.where(right, hq * e, 0.0)
        km = jnp.where(right, 0.0, kk * e).astype(BF16)
        sh = int(math.log2(2 * m))
        same_block = jnp.right_shift(row, sh) == jnp.right_shift(lane, sh)
        for p in range(npair):
            sl = slice(p * LANES, (p + 1) * LANES)
            for hh, s_h in enumerate(_pair_scores(qm[:, sl], km[:, sl], lo_lane)):
                sc = jnp.where(same_block, s_h, 0.0)
                p_sum[p][hh] = sc if p_sum[p][hh] is None else p_sum[p][hh] + sc
        yield
    cols = []
    for p in range(npair):
        vbd = _pair_blockdiag(hv[:, p * LANES:(p + 1) * LANES], lo_lane).astype(BF16)
        cols.append(_dot(jnp.concatenate([p_sum[p][0].astype(BF16), p_sum[p][1].astype(BF16)], axis=1), vbd))
    o_h = o_h + jnp.concatenate(cols, axis=1)
    q_in = (hq * jnp.exp(cum)).astype(BF16)
    last2 = cum[C - 1:C, :]
    k_end = (kk * jnp.exp(last2 - cum)).astype(BF16)
    tot2 = jnp.exp(last2)
    hv_b = hv.astype(BF16)
    inter = []
    for p in range(npair):
        sl = slice(p * LANES, (p + 1) * LANES)
        st = s_hg[p]
        inter.append(_dot_nt(q_in[:, sl], st.astype(BF16)))
        upd = _dot_tn(hv_b[:, sl], k_end[:, sl])
        s_hg[p] = st * tot2[:, sl] + jnp.where(bd_mask, upd, 0.0)
    o_h = o_h + jnp.concatenate(inter, axis=1)
    ms = _head_mean_square(o_h, bd_mean)
    o_h = o_h * lax.rsqrt(ms + EPS) * hgg_ref[...]
    o_h = o_h * jax.nn.sigmoid(proj_ref[:, OFF_HG:OFF_HG + W2])
    out_ref[:, SSD_WIDTH:SSD_WIDTH + W2] = o_h.astype(out_ref.dtype)
    yield

    W3 = RET_WIDTH
    lane3 = lax.broadcasted_iota(jnp.int32, (C, W3), 1)
    first_half = jnp.bitwise_and(lane3, HEAD_DIM // 2) == 0
    cosv = cos_ref[...]
    sinv = sin_ref[...]

    def rotary(xr):
        swapped = jnp.where(first_half, pltpu.roll(xr, W3 - HEAD_DIM // 2, 1),
                            pltpu.roll(xr, HEAD_DIM // 2, 1))
        return xr * cosv + swapped * sinv

    qr = rotary(proj_ref[:, OFF_RQ:OFF_RQ + W3])
    kr = rotary(proj_ref[:, OFF_RK:OFF_RK + W3]) * (HEAD_DIM ** -0.5)
    rv = proj_ref[:, OFF_RV:OFF_RV + W3]
    kr_b = kr.astype(BF16)
    q_in = (qr * rdq_ref[...]).astype(BF16)
    k_end = (kr * rdk_ref[...]).astype(BF16)
    rv_b = rv.astype(BF16)
    o_cols = []
    for p in range(W3 // LANES):
        sl = slice(p * LANES, (p + 1) * LANES)
        sa, sb = _pair_scores(qr[:, sl], kr_b[:, sl], lo_lane)
        sa = sa * rdec_ref[2 * p]
        sb = sb * rdec_ref[2 * p + 1]
        vbd = _pair_blockdiag(rv[:, sl], lo_lane).astype(BF16)
        o_p = _dot(jnp.concatenate([sa.astype(BF16), sb.astype(BF16)], axis=1), vbd)
        st = s_ret[p]
        o_p = o_p + _dot(q_in[:, sl], st.astype(BF16))
        upd = _dot_tn(k_end[:, sl], rv_b[:, sl])
        s_ret[p] = st * rtot_ref[:, sl] + jnp.where(bd_mask, upd, 0.0)
        o_cols.append(o_p)
        yield
    o_r = jnp.concatenate(o_cols, axis=1)
    ms = _head_mean_square(o_r, bd_mean)
    o_r = o_r * lax.rsqrt(ms + EPS) * _silu(proj_ref[:, OFF_RG:OFF_RG + W3])
    out_ref[:, SSD_WIDTH + W2:SSD_WIDTH + W2 + W3] = o_r.astype(out_ref.dtype)


def _mixer(x2, y, mod_prev, mod_l, g, w, layer, batch, seq, conv_w, conv_b, dtb, alog, dexp, ssdg, lbraw, hgg,
           cos_t, sin_t, rdec, rdq, rdk, rtot):
    t, d = x2.shape
    d_mix = SSD_WIDTH + HGRN_WIDTH + RET_WIDTH
    nj = seq // CHUNK
    const2 = lambda j: (0, 0)
    has_res = y is not None
    proj_idx = lambda j: (0, jnp.minimum(j, nj - 1), 0)
    mix_idx = lambda j: (0, jnp.maximum(j - 1, 0), 0)
    x_spec = pl.BlockSpec((batch, CHUNK, d), proj_idx)
    mod_spec = pl.BlockSpec((batch, MOD_ROWS, d), lambda j: (0, 0, 0))
    head_specs = [x_spec]
    head_args = [x2.reshape(batch, seq, d)]
    if has_res:
        head_specs += [pl.BlockSpec((batch, CHUNK * ROW_TILES, LANES), proj_idx), mod_spec]
        head_args += [y.reshape(batch, seq * ROW_TILES, LANES), mod_prev]
    head_specs += [mod_spec, pl.BlockSpec((1, d), const2),
                   pl.BlockSpec((None,) + w.shape[1:], lambda j: (layer, 0, 0))]
    head_args += [mod_l, g, w]
    table_spec = pl.BlockSpec((CHUNK, RET_WIDTH), lambda j: (jnp.maximum(j - 1, 0), 0))
    mixed_spec = pl.BlockSpec((batch, CHUNK, d_mix), mix_idx)
    mixed_shape = jax.ShapeDtypeStruct((batch, seq, d_mix), BF16)
    res = pl.pallas_call(
        functools.partial(_mixer_kernel, layer, has_res),
        grid=(nj + 1,),
        in_specs=head_specs + [
            pl.BlockSpec(conv_w.shape, const2),
            pl.BlockSpec(conv_b.shape, const2),
            pl.BlockSpec(dtb.shape, const2),
            pl.BlockSpec(alog.shape, const2),
            pl.BlockSpec(dexp.shape, const2),
            pl.BlockSpec(ssdg.shape, const2),
            pl.BlockSpec(lbraw.shape, const2),
            pl.BlockSpec(hgg.shape, const2),
            table_spec,
            table_spec,
            pl.BlockSpec(rdec.shape, lambda j: (0, 0, 0)),
            pl.BlockSpec(rdq.shape, const2),
            pl.BlockSpec(rdk.shape, const2),
            pl.BlockSpec(rtot.shape, const2),
        ],
        out_specs=[x_spec, mixed_spec] if has_res else mixed_spec,
        out_shape=[jax.ShapeDtypeStruct((batch, seq, d), F32), mixed_shape] if has_res else mixed_shape,
        scratch_shapes=[
            pltpu.VMEM((batch, CHUNK, N_PROJ), F32),
            pltpu.VMEM((batch, CHUNK, N_PROJ), F32),
            pltpu.VMEM((batch, TAIL + CHUNK, XBC_WIDTH), F32),
            pltpu.VMEM((batch, LANES, SSD_WIDTH), F32),
            pltpu.VMEM((batch, HGRN_WIDTH // LANES, LANES, LANES), F32),
            pltpu.VMEM((batch, RET_WIDTH // LANES, LANES, LANES), F32),
        ],
        compiler_params=_cparams(("arbitrary",)),
        name="norm_inproj_mixers",
    )(*head_args, conv_w, conv_b, dtb, alog, dexp, ssdg, lbraw, hgg, cos_t, sin_t, rdec, rdq, rdk, rtot)
    if has_res:
        return res[0].reshape(t, d), res[1].reshape(t, d_mix)
    return x2, res.reshape(t, d_mix)


ROW_TILES = 8


def _store_row_tiled(ref, val):
    m = val.shape[0]
    for k in range(ROW_TILES):
        ref[pl.ds(k, m, stride=ROW_TILES), :] = val[:, k * LANES:(k + 1) * LANES]


def _load_row_tiled(ref):
    m = ref.shape[0] // ROW_TILES
    return jnp.concatenate([ref[pl.ds(k, m, stride=ROW_TILES), :] for k in range(ROW_TILES)], axis=1)


R_E1, R_E2, R_W1, R_W2 = range(4)


def _outproj_kernel(x_ref, mixed_ref, mod_ref, g_ref, w_ref, wr_ref, br_ref,
                    xo_ref, hp_ref, route_ref):
    x = x_ref[...] + mod_ref[G_MIX:G_MIX + 1, :] * _dot(mixed_ref[...], w_ref[...])
    xo_ref[...] = x
    h = _rms(x) * g_ref[...] * (1.0 + mod_ref[SC_FFN:SC_FFN + 1, :]) + mod_ref[SH_FFN:SH_FFN + 1, :]
    _store_row_tiled(hp_ref, h)
    h_hi = h.astype(BF16)
    h_lo = (h - h_hi.astype(F32)).astype(BF16)
    logits = (_dot(h_hi, wr_ref[0]) + _dot(h_lo, wr_ref[0]) + _dot(h_hi, wr_ref[1])) + br_ref[...]
    lane = lax.broadcasted_iota(jnp.int32, logits.shape, 1).astype(F32)
    neg = -jnp.inf
    big = float(ROUTER_LANES)
    is_grp = lane < N_GROUPS
    gl = jnp.where(is_grp, logits, neg)
    gmax = jnp.max(gl, axis=-1, keepdims=True)
    g_idx = jnp.min(jnp.where(gl == gmax, lane, big), axis=-1, keepdims=True)
    p_grp = 1.0 / jnp.sum(jnp.where(is_grp, jnp.exp(gl - gmax), 0.0), axis=-1, keepdims=True)
    first = N_GROUPS + g_idx * EXPERTS_PER_GROUP
    valid = (lane >= first) & (lane < first + EXPERTS_PER_GROUP)
    el = jnp.where(valid, logits, neg)
    m1 = jnp.max(el, axis=-1, keepdims=True)
    i1 = jnp.min(jnp.where(el == m1, lane, big), axis=-1, keepdims=True)
    el2 = jnp.where(lane == i1, neg, el)
    m2 = jnp.max(el2, axis=-1, keepdims=True)
    i2 = jnp.min(jnp.where(el2 == m2, lane, big), axis=-1, keepdims=True)
    e = jnp.exp(m2 - m1)
    w1 = p_grp / (1.0 + e)
    w2 = p_grp * e / (1.0 + e)
    route_ref[...] = jnp.where(lane == R_E1, i1 - N_GROUPS,
                               jnp.where(lane == R_E2, i2 - N_GROUPS,
                                         jnp.where(lane == R_W1, w1, jnp.where(lane == R_W2, w2, 0.0))))


def _outproj(x2, mixed, mod_l, g, w, wr, br, seq, layer, tm=512):
    t, d = x2.shape
    per_b = seq // tm
    return pl.pallas_call(
        _outproj_kernel,
        grid=(t // tm,),
        in_specs=[
            pl.BlockSpec((tm, d), lambda i: (i, 0)),
            pl.BlockSpec((tm, mixed.shape[1]), lambda i: (i, 0)),
            pl.BlockSpec((None, MOD_ROWS, d), lambda i: (i // per_b, 0, 0)),
            pl.BlockSpec((1, d), lambda i: (0, 0)),
            pl.BlockSpec((None,) + w.shape[1:], lambda i: (layer, 0, 0)),
            pl.BlockSpec((None,) + wr.shape[1:], lambda i: (layer, 0, 0, 0)),
            pl.BlockSpec(br.shape, lambda i: (0, 0)),
        ],
        out_specs=[
            pl.BlockSpec((tm, d), lambda i: (i, 0)),
            pl.BlockSpec((tm * ROW_TILES, LANES), lambda i: (i, 0)),
            pl.BlockSpec((tm, ROUTER_LANES), lambda i: (i, 0)),
        ],
        out_shape=[
            jax.ShapeDtypeStruct((t, d), F32),
            jax.ShapeDtypeStruct((t * ROW_TILES, LANES), F32),
            jax.ShapeDtypeStruct((t, ROUTER_LANES), F32),
        ],
        compiler_params=_cparams(("arbitrary",)),
        name="outproj_router",
    )(x2, mixed, mod_l, g, w, wr, br)


MOE_CHUNK = 2048
MOE_TILE = 160
SEG_ALIGN = 8
COMBINE_STEPS = 4
EXPERTS_PER_STEP = 4
SLOTS_PAD = 2 * MOE_CHUNK + N_EXPERTS * SEG_ALIGN
RANK_BLOCK = 256
M_OFF, M_CNT = 0, 1
S_POS1, S_POS2, S_W1, S_W2 = 0, 1, R_W1, R_W2


def _rank_kernel(route_ref, slots_ref, meta_ref):
    tb = route_ref.shape[0]
    lane = lax.broadcasted_iota(jnp.int32, (RANK_BLOCK, ROUTER_LANES), 1).astype(F32)
    r_i = lax.broadcasted_iota(jnp.int32, (RANK_BLOCK, RANK_BLOCK), 0)
    c_i = lax.broadcasted_iota(jnp.int32, (RANK_BLOCK, RANK_BLOCK), 1)
    strict_tril = jnp.where(r_i > c_i, 1.0, 0.0).astype(BF16)
    carry = jnp.zeros((1, ROUTER_LANES), F32)
    ranks = []
    for b in range(tb // RANK_BLOCK):
        blk = route_ref[b * RANK_BLOCK:(b + 1) * RANK_BLOCK, :]
        onehot = jnp.where((lane == blk[:, R_E1:R_E1 + 1]) | (lane == blk[:, R_E2:R_E2 + 1]), 1.0, 0.0)
        ranks.append(_dot(strict_tril, onehot.astype(BF16)) + carry)
        carry = carry + jnp.sum(onehot, axis=0, keepdims=True)
    cnt = carry
    seg = jnp.floor((cnt + (SEG_ALIGN - 1)) * (1.0 / SEG_ALIGN)) * SEG_ALIGN
    u_r = lax.broadcasted_iota(jnp.int32, (ROUTER_LANES, ROUTER_LANES), 0)
    u_c = lax.broadcasted_iota(jnp.int32, (ROUTER_LANES, ROUTER_LANES), 1)
    strict_triu = jnp.where(u_r < u_c, 1.0, 0.0).astype(BF16)
    seg_parts = _split3(jnp.broadcast_to(seg, (8, ROUTER_LANES)))
    off = (_dot(seg_parts[0], strict_triu) + _dot(seg_parts[1], strict_triu)
           + _dot(seg_parts[2], strict_triu))[0:1, :]
    for b in range(tb // RANK_BLOCK):
        blk = route_ref[b * RANK_BLOCK:(b + 1) * RANK_BLOCK, :]
        dest = off + ranks[b]
        p1 = jnp.sum(jnp.where(lane == blk[:, R_E1:R_E1 + 1], dest, 0.0), axis=-1, keepdims=True)
        p2 = jnp.sum(jnp.where(lane == blk[:, R_E2:R_E2 + 1], dest, 0.0), axis=-1, keepdims=True)
        rec = jnp.where(lane == S_POS1, p1, jnp.where(lane == S_POS2, p2,
                        jnp.where((lane == S_W1) | (lane == S_W2), blk, 0.0)))
        for h in range(RANK_BLOCK // LANES):
            t0 = b * RANK_BLOCK + h * LANES
            slots_ref[:, t0:t0 + LANES] = rec[h * LANES:(h + 1) * LANES, :].T[0:8, :]
    row8 = lax.broadcasted_iota(jnp.int32, (8, ROUTER_LANES), 0)
    meta_ref[...] = jnp.where(row8 == M_OFF, off, jnp.where(row8 == M_CNT, cnt, 0.0)).astype(jnp.int32)


def _rank(route):
    t = route.shape[0]
    nc = t // MOE_CHUNK
    return pl.pallas_call(
        _rank_kernel,
        grid=(nc,),
        in_specs=[pl.BlockSpec((MOE_CHUNK, ROUTER_LANES), lambda c: (c, 0))],
        out_specs=[pl.BlockSpec((8, MOE_CHUNK), lambda c: (0, c)),
                   pl.BlockSpec((None, 8, ROUTER_LANES), lambda c: (c, 0, 0))],
        out_shape=[jax.ShapeDtypeStruct((8, t), F32),
                   jax.ShapeDtypeStruct((nc, 8, ROUTER_LANES), jnp.int32)],
        compiler_params=_cparams(("arbitrary",)),
        name="slot_rank",
    )(route)


def _moe_kernel(pos1_ref, pos2_ref, w1_ref, w2_ref, off_ref, cnt_ref,
                hp_ref, wg_ref, wu_ref, wd_ref, y_ref, inv_ref, xt_ref, ys_ref):
    c = pl.program_id(0)
    e = pl.program_id(1)
    rt = ROW_TILES
    tb = hp_ref.shape[0] // rt
    tok0 = c * tb
    seg0 = c * N_EXPERTS

    @pl.when((c == 0) & (e == 0))
    def _():
        xt_ref[...] = jnp.zeros_like(xt_ref)

    @pl.when(e == 0)
    def _():
        def pads(ee, carry):
            o = off_ref[seg0 + ee]
            n_e = cnt_ref[seg0 + ee]

            def zero(i, carry2):
                inv_ref[o + i] = 0
                return carry2
            lax.fori_loop(n_e, (n_e + SEG_ALIGN - 1) // SEG_ALIGN * SEG_ALIGN, zero, 0)
            return carry
        lax.fori_loop(0, N_EXPERTS, pads, 0)
        n_last = cnt_ref[seg0 + N_EXPERTS - 1]
        total = off_ref[seg0 + N_EXPERTS - 1] + (n_last + SEG_ALIGN - 1) // SEG_ALIGN * SEG_ALIGN

        def zero_tail(i, carry):
            inv_ref[total + i] = 0
            return carry
        lax.fori_loop(0, MOE_TILE, zero_tail, 0)

        def place(i, carry):
            inv_ref[pos1_ref[tok0 + i]] = i
            inv_ref[pos2_ref[tok0 + i]] = i
            return carry
        lax.fori_loop(0, tb, place, 0, unroll=8)

    expert_steps = N_EXPERTS // EXPERTS_PER_STEP
    is_expert = e < expert_steps
    for k in range(EXPERTS_PER_STEP):
        e_k = jnp.minimum(e, expert_steps - 1) * EXPERTS_PER_STEP + k
        n = jnp.where(is_expert, cnt_ref[seg0 + e_k], 0)
        base = off_ref[seg0 + e_k]

        @pl.when(n > 0)
        def _(k=k, n=n, base=base):
            wg, wu, wd = wg_ref.at[k], wu_ref.at[k], wd_ref.at[k]

            def tile(i, carry):
                start = pl.multiple_of(base + i * MOE_TILE, SEG_ALIGN)

                def gather(r, carry2):
                    src = pl.multiple_of(inv_ref[start + r] * rt, rt)
                    xt_ref[pl.ds(pl.multiple_of(r * rt, rt), rt), :] = hp_ref[pl.ds(src, rt), :]
                    return carry2
                rows = jnp.minimum(MOE_TILE, (n - i * MOE_TILE + SEG_ALIGN - 1) // SEG_ALIGN * SEG_ALIGN)
                def gather_group(g, carry2):
                    for u in range(SEG_ALIGN):
                        gather(g * SEG_ALIGN + u, carry2)
                    return carry2
                lax.fori_loop(0, rows // SEG_ALIGN, gather_group, 0)
                xrow = jnp.concatenate([xt_ref[pl.ds(j, MOE_TILE, stride=rt), :] for j in range(rt)], axis=1)
                xb = xrow.astype(BF16)
                hid = (_silu(_dot(xb, wg[...])) * _dot(xb, wu[...])).astype(BF16)
                out = _dot(hid, wd[...])
                for j in range(rt):
                    ys_ref[pl.ds(start * rt + j, MOE_TILE, stride=rt), :] = out[:, j * LANES:(j + 1) * LANES]
                return carry
            lax.fori_loop(0, (n + MOE_TILE - 1) // MOE_TILE, tile, 0)

    @pl.when(e >= expert_steps)
    def _():
        tq = y_ref.shape[0] // rt
        t0 = tok0 + (e - expert_steps) * tq

        def combine(i, carry):
            r1 = ys_ref[pl.ds(pl.multiple_of(pos1_ref[t0 + i] * rt, rt), rt), :]
            r2 = ys_ref[pl.ds(pl.multiple_of(pos2_ref[t0 + i] * rt, rt), rt), :]
            y_ref[pl.ds(pl.multiple_of(i * rt, rt), rt), :] = w1_ref[t0 + i] * r1 + w2_ref[t0 + i] * r2
            return carry
        lax.fori_loop(0, tq, combine, 0, unroll=8)


def _moe(hp, pos1, pos2, w1, w2, off, cnt, w_gate, w_up, w_down, layer):
    rt = ROW_TILES
    t = hp.shape[0] // rt
    _, ne, d, f = w_gate.shape
    nc = t // MOE_CHUNK
    tq = MOE_CHUNK // COMBINE_STEPS

    es = ne // EXPERTS_PER_STEP

    def w_idx(c, e, *_):
        return (layer, jnp.minimum(e, es - 1), 0, 0)

    grid_spec = pltpu.PrefetchScalarGridSpec(
        num_scalar_prefetch=6,
        grid=(nc, es + COMBINE_STEPS),
        in_specs=[
            pl.BlockSpec((MOE_CHUNK * rt, LANES), lambda c, e, *_: (c, 0)),
            pl.BlockSpec((None, EXPERTS_PER_STEP, d, f), w_idx),
            pl.BlockSpec((None, EXPERTS_PER_STEP, d, f), w_idx),
            pl.BlockSpec((None, EXPERTS_PER_STEP, f, d), w_idx),
        ],
        out_specs=pl.BlockSpec((tq * rt, LANES),
                               lambda c, e, *_: (c * COMBINE_STEPS + jnp.maximum(e - es, 0), 0)),
        scratch_shapes=[
            pltpu.SMEM((SLOTS_PAD + MOE_TILE,), jnp.int32),
            pltpu.VMEM((MOE_TILE * rt, LANES), F32),
            pltpu.VMEM(((SLOTS_PAD + MOE_TILE) * rt, LANES), F32),
        ],
    )
    return pl.pallas_call(
        _moe_kernel,
        grid_spec=grid_spec,
        out_shape=jax.ShapeDtypeStruct((t * rt, LANES), F32),
        compiler_params=_cparams(("arbitrary", "arbitrary")),
        name="moe_top2",
    )(pos1, pos2, w1, w2, off, cnt, hp, w_gate, w_up, w_down)


def _final_kernel(x_ref, y_ref, mod_ref, g_ref, o_ref):
    x = x_ref[...] + mod_ref[G_FFN:G_FFN + 1, :] * _load_row_tiled(y_ref)
    o_ref[...] = _rms(x) * g_ref[...]


def _final(x2, y, mod_l, g, seq, tm=512):
    t, d = x2.shape
    per_b = seq // tm
    return pl.pallas_call(
        _final_kernel,
        grid=(t // tm,),
        in_specs=[
            pl.BlockSpec((tm, d), lambda i: (i, 0)),
            pl.BlockSpec((tm * ROW_TILES, LANES), lambda i: (i, 0)),
            pl.BlockSpec((None, MOD_ROWS, d), lambda i: (i // per_b, 0, 0)),
            pl.BlockSpec((1, d), lambda i: (0, 0)),
        ],
        out_specs=pl.BlockSpec((tm, d), lambda i: (i, 0)),
        out_shape=jax.ShapeDtypeStruct((t, d), F32),
        compiler_params=_cparams(("arbitrary",)),
        name="final_norm",
    )(x2, y, mod_l, g)


def _retention_tables(seq):
    half = HEAD_DIM // 2
    inv_freq = ROPE_BASE ** (-jnp.arange(half, dtype=F32) / half)
    ang = jnp.arange(seq, dtype=F32)[:, None] * inv_freq[None, :]
    cos, sin = jnp.cos(ang), jnp.sin(ang)
    cos_t = jnp.tile(jnp.concatenate([cos, cos], axis=-1), (1, RET_HEADS))
    sin_t = jnp.tile(jnp.concatenate([-sin, sin], axis=-1), (1, RET_HEADS))
    log_gamma = jnp.log1p(-jnp.exp2(-5.0 - jnp.arange(RET_HEADS, dtype=F32)))
    tt = jnp.arange(CHUNK, dtype=F32)
    diff = tt[:, None] - tt[None, :]
    rdec = jnp.where(diff >= 0, jnp.exp(jnp.maximum(diff, 0.0)[None] * log_gamma[:, None, None]), 0.0)
    lg_lane = jnp.repeat(log_gamma, HEAD_DIM)[None, :]
    rdq = jnp.exp((tt[:, None] + 1.0) * lg_lane)
    rdk = jnp.exp((CHUNK - 1.0 - tt[:, None]) * lg_lane)
    rtot = jnp.exp(CHUNK * lg_lane)
    return cos_t, sin_t, rdec, rdq, rdk, rtot


def _pad_lanes(v, width=LANES):
    return jnp.pad(v, ((0, 0), (0, width - v.shape[-1])))


def kernel(x, c, norm_mix_g, norm_ffn_g, final_norm_g, w_ada, b_ada, w_in, conv_w, conv_b, ssd_dt_bias, ssd_a_log, ssd_d, ssd_norm_g, hgrn_lower_bounds, hgrn_norm_g, w_out, w_grp, b_grp, w_exp, b_exp, w_gate, w_up, w_down):
    batch, seq, d = x.shape
    depth = w_in.shape[0]
    t = batch * seq

    c8 = jnp.pad(c, ((0, MOD_ROWS - batch), (0, 0)))
    mod = _ada(c8, w_ada, b_ada)
    mod = mod[:, :batch].reshape(depth, batch, 6, d)
    mod = jnp.pad(mod, ((0, 0), (0, 0), (0, MOD_ROWS - 6), (0, 0)))

    dt0 = SSD_WIDTH + XBC_WIDTH
    w_in_p = jnp.concatenate(
        [w_in[:, :, :dt0], w_in[:, :, dt0 + SSD_HEADS:], w_in[:, :, dt0:dt0 + SSD_HEADS],
         jnp.zeros((depth, d, N_PROJ - w_in.shape[2]), w_in.dtype)], axis=-1).astype(BF16)
    w_out_b = w_out.astype(BF16)
    w_gate_b, w_up_b, w_down_b = w_gate.astype(BF16), w_up.astype(BF16), w_down.astype(BF16)
    w_router = _pad_lanes(jnp.concatenate([w_grp, w_exp], axis=-1).reshape(depth * d, -1)).reshape(depth, d, ROUTER_LANES)
    w_router_hi = w_router.astype(BF16)
    w_router = jnp.stack([w_router_hi, (w_router - w_router_hi.astype(F32)).astype(BF16)], axis=1)
    b_router = _pad_lanes(jnp.concatenate([b_grp, b_exp], axis=-1))[:, None, :]
    dtb = _pad_lanes(ssd_dt_bias)[:, None, :]
    alog = _pad_lanes(ssd_a_log)[:, None, :]
    dexp = jnp.repeat(ssd_d, HEAD_DIM, axis=-1)[:, None, :]
    tables = _retention_tables(seq)

    x2 = x.reshape(t, d)
    y = None
    for layer in range(depth):
        x2, mixed = _mixer(x2, y, mod[layer - 1] if layer else None, mod[layer],
                           norm_mix_g[layer][None, :], w_in_p, layer, batch, seq,
                           conv_w[layer], conv_b[layer][None, :], dtb[layer],
                           alog[layer], dexp[layer], ssd_norm_g[layer][None, :], hgrn_lower_bounds,
                           hgrn_norm_g[layer][None, :], *tables)
        x2, hp, route = _outproj(x2, mixed, mod[layer], norm_ffn_g[layer][None, :], w_out_b,
                                 w_router, b_router[layer], seq, layer)
        slots, meta = _rank(route)
        y = _moe(hp, slots[S_POS1].astype(jnp.int32), slots[S_POS2].astype(jnp.int32), slots[S_W1], slots[S_W2],
                 meta[:, M_OFF, :N_EXPERTS].reshape(-1), meta[:, M_CNT, :N_EXPERTS].reshape(-1),
                 w_gate_b, w_up_b, w_down_b, layer)
    out = _final(x2, y, mod[depth - 1], final_norm_g[None, :], seq)
    return out.reshape(batch, seq, d)
```

```python
import functools
import math

import jax
import jax.numpy as jnp
import numpy as np
from jax import lax
from jax.experimental import pallas as pl
from jax.experimental.pallas import tpu as pltpu

F32 = jnp.float32
BF16 = jnp.bfloat16
HIGHEST = lax.Precision.HIGHEST

EPS = 1e-6
GATE_FLOOR = 1e-30
ROPE_BASE = 10000.0

HEAD_DIM = 64
SSD_HEADS = 8
SSD_GROUPS = 2
SSD_CONV = 4
HGRN_HEADS = 4
RET_HEADS = 4
N_GROUPS = 4
EXPERTS_PER_GROUP = 8
N_EXPERTS = N_GROUPS * EXPERTS_PER_GROUP
CHUNK = 128
LANES = 128
ROUTER_LANES = 128

SH_MIX, SC_MIX, G_MIX, SH_FFN, SC_FFN, G_FFN = range(6)
MOD_ROWS = 8

VMEM_LIMIT = 56 * 1024 * 1024


def _cparams(sem):
    return pltpu.CompilerParams(dimension_semantics=sem, vmem_limit_bytes=VMEM_LIMIT)


def _silu(x):
    return x * jax.nn.sigmoid(x)


def _rms(x):
    return x * lax.rsqrt(jnp.mean(x * x, axis=-1, keepdims=True) + EPS)


def _split3(x):
    hi = x.astype(BF16)
    r = x - hi.astype(F32)
    mid = r.astype(BF16)
    lo = (r - mid.astype(F32)).astype(BF16)
    return hi, mid, lo


def _dot(a, b):
    return jnp.dot(a, b, preferred_element_type=F32)


def _dot_nt(a, b):
    return lax.dot_general(a, b, (((1,), (1,)), ((), ())), preferred_element_type=F32)


def _dot_tn(a, b):
    return lax.dot_general(a, b, (((0,), (0,)), ((), ())), preferred_element_type=F32)


def _ada_kernel(c_ref, w_ref, b_ref, o_ref):
    c = _silu(c_ref[...])
    w = w_ref[...]
    c_hi, w_hi = c.astype(BF16), w.astype(BF16)
    c_lo = (c - c_hi.astype(F32)).astype(BF16)
    w_lo = (w - w_hi.astype(F32)).astype(BF16)
    o_ref[...] = (_dot(c_hi, w_hi) + _dot(c_lo, w_hi) + _dot(c_hi, w_lo)) + b_ref[...]


def _ada(c8, w_ada, b_ada):
    depth, d, d6 = w_ada.shape
    nb = d6 // d
    return pl.pallas_call(
        _ada_kernel,
        grid=(depth, nb),
        in_specs=[
            pl.BlockSpec((MOD_ROWS, d), lambda l, n: (0, 0)),
            pl.BlockSpec((None, d, d), lambda l, n: (l, 0, n)),
            pl.BlockSpec((None, 1, d), lambda l, n: (l, 0, n)),
        ],
        out_specs=pl.BlockSpec((None, MOD_ROWS, d), lambda l, n: (l, 0, n)),
        out_shape=jax.ShapeDtypeStruct((depth, MOD_ROWS, d6), F32),
        compiler_params=_cparams(("arbitrary", "arbitrary")),
        name="ada_mod",
    )(c8, w_ada, b_ada.reshape(depth, 1, d6))


OFF_Z, OFF_XBC, OFF_HQ, OFF_HF, OFF_HI, OFF_HG = 0, 512, 1280, 1536, 1792, 2048
OFF_RQ, OFF_RK, OFF_RV, OFF_RG, OFF_DT = 2304, 2560, 2816, 3072, 3328
N_PROJ = 3456
SSD_WIDTH = SSD_HEADS * HEAD_DIM
XBC_WIDTH = SSD_WIDTH + 2 * SSD_GROUPS * HEAD_DIM
HGRN_WIDTH = HGRN_HEADS * HEAD_DIM
RET_WIDTH = RET_HEADS * HEAD_DIM
TAIL = 8
HGRN_DIRECT = 8
HGRN_LEVELS = (8, 16, 32, 64)


def _pair_blockdiag(v_pair, lo_lane):
    return jnp.concatenate([jnp.where(lo_lane, v_pair, 0.0), jnp.where(lo_lane, 0.0, v_pair)], axis=0)


def _head_mean_square(o, bd):
    c, n = o.shape[0], o.shape[1] // LANES
    sq = (o * o).astype(BF16)
    ms = _dot(jnp.concatenate([sq[:, p * LANES:(p + 1) * LANES] for p in range(n)], axis=0), bd)
    return jnp.concatenate([ms[p * c:(p + 1) * c] for p in range(n)], axis=1)


def _pair_scores(q_pair, k_pair_b, lo_lane):
    c = q_pair.shape[0]
    q2 = jnp.concatenate([jnp.where(lo_lane, q_pair, 0.0), jnp.where(lo_lane, 0.0, q_pair)], axis=0)
    s = _dot_nt(q2.astype(BF16), k_pair_b)
    return s[:c], s[c:]


N_MIXER_PARAMS = 14
N_EXPERT_ARRAYS = 3


def _mixer_kernel(layer, has_res, *refs):
    if has_res:
        x_ref, y_ref, modp_ref = refs[:3]
        refs = refs[3:]
    else:
        x_ref = refs[0]
        refs = refs[1:]
    mod_ref, g_ref, w_ref = refs[:3]
    expert_w = refs[3:3 + N_EXPERT_ARRAYS]
    rest = refs[3 + N_EXPERT_ARRAYS:]
    params, outs = rest[:N_MIXER_PARAMS], rest[N_MIXER_PARAMS:]
    if has_res:
        xo_ref = outs[0]
        outs = outs[1:]
    out_ref = outs[0]
    expert_w_bf16 = outs[1:1 + N_EXPERT_ARRAYS]
    proj_next, proj_cur, xbuf, s_ssd, s_hg, s_ret = outs[1 + N_EXPERT_ARRAYS:]
    j = pl.program_id(0)
    nb = x_ref.shape[0]

    @pl.when(j == 0)
    def _():
        xbuf[:, 0:TAIL, :] = jnp.zeros((nb, TAIL, XBC_WIDTH), F32)
        s_ssd[...] = jnp.zeros_like(s_ssd)
        s_hg[...] = jnp.zeros_like(s_hg)
        s_ret[...] = jnp.zeros_like(s_ret)

    for src, dst in zip(expert_w, expert_w_bf16):
        dst[...] = src[...].astype(BF16)

    hs = []
    for b in range(nb):
        x = x_ref[b]
        if has_res:
            x = x + modp_ref[b, G_FFN:G_FFN + 1, :] * _load_row_tiled(y_ref.at[b])
            xo_ref[b] = x
        hs.append(_rms(x) * g_ref[...] * (1.0 + mod_ref[b, SC_MIX:SC_MIX + 1, :]) + mod_ref[b, SH_MIX:SH_MIX + 1, :])
    proj = _dot(jnp.concatenate(hs, axis=0).astype(BF16), w_ref[...])
    for b in range(nb):
        proj_next[b] = proj[b * CHUNK:(b + 1) * CHUNK]

    @pl.when(j > 0)
    def _():
        stages = [_mixer_body(layer, proj_cur.at[b], *params, out_ref.at[b], xbuf.at[b], s_ssd.at[b],
                              s_hg.at[b], s_ret.at[b]) for b in range(nb)]
        while stages:
            for gen in list(stages):
                if next(gen, StopIteration) is StopIteration:
                    stages.remove(gen)

    proj_cur[...] = proj_next[...]


def _mixer_body(layer, proj_ref, convw_ref, convb_ref, dtb_ref, alog_ref, dexp_ref, ssdg_ref,
                lbraw_ref, hgg_ref, cos_ref, sin_ref, rdec_ref, rdq_ref, rdk_ref, rtot_ref,
                out_ref, xbuf, s_ssd, s_hg, s_ret):
    C = CHUNK

    row = lax.broadcasted_iota(jnp.int32, (C, LANES), 0)
    lane = lax.broadcasted_iota(jnp.int32, (C, LANES), 1)
    lo_lane = lane < HEAD_DIM
    causal = row >= lane
    bd_mask = jnp.right_shift(row, 6) == jnp.right_shift(lane, 6)
    bd_mean = jnp.where(bd_mask, 1.0 / HEAD_DIM, 0.0).astype(BF16)
    tril = jnp.where(causal, 1.0, 0.0).astype(BF16)

    def cumsum_rows(x):
        hi, mid, lo = _split3(x)
        return _dot(tril, hi) + _dot(tril, mid) + _dot(tril, lo)

    xbuf[TAIL:TAIL + C, :] = proj_ref[:, OFF_XBC:OFF_XBC + XBC_WIDTH]
    conv = convb_ref[...]
    for jj in range(SSD_CONV):
        off = TAIL - (SSD_CONV - 1) + jj
        conv = conv + convw_ref[jj:jj + 1, :] * xbuf[off:off + C, :]
    xbuf[0:TAIL, :] = xbuf[C:C + TAIL, :]
    xc = _silu(conv)
    xs = xc[:, 0:SSD_WIDTH]
    bm = xc[:, SSD_WIDTH:SSD_WIDTH + LANES]
    cm = xc[:, SSD_WIDTH + LANES:SSD_WIDTH + 2 * LANES]

    dt8 = jax.nn.softplus(proj_ref[:, OFF_DT:OFF_DT + LANES] + dtb_ref[...])
    la8 = dt8 * (-jnp.exp(alog_ref[...]))
    cs8 = cumsum_rows(la8)
    cs8t = cs8.T

    bm_b = bm.astype(BF16)
    cm_b = cm.astype(BF16)
    scores_g = _pair_scores(cm, bm_b, lo_lane)
    yield

    o_intra, e1_cols, e2_cols, dt_cols = [], [], [], []
    for p in range(SSD_HEADS // 2):
        a, b = 2 * p, 2 * p + 1
        g = a // (SSD_HEADS // SSD_GROUPS)
        col_a = jnp.broadcast_to(cs8[:, a:a + 1], (C, LANES))
        col_b = jnp.broadcast_to(cs8[:, b:b + 1], (C, LANES))
        cs_pair = jnp.where(lo_lane, col_a, col_b)
        dt_pair = jnp.where(lo_lane, jnp.broadcast_to(dt8[:, a:a + 1], (C, LANES)),
                            jnp.broadcast_to(dt8[:, b:b + 1], (C, LANES)))
        last = cs_pair[C - 1:C, :]
        e1_cols.append(jnp.exp(cs_pair))
        e2_cols.append(jnp.exp(last - cs_pair))
        dt_cols.append(dt_pair)
        dec_a = jnp.where(causal, jnp.exp(jnp.minimum(col_a - cs8t[a:a + 1, :], 0.0)), 0.0)
        dec_b = jnp.where(causal, jnp.exp(jnp.minimum(col_b - cs8t[b:b + 1, :], 0.0)), 0.0)
        pa = (scores_g[g] * dec_a).astype(BF16)
        pb = (scores_g[g] * dec_b).astype(BF16)
        v_pair = xs[:, p * LANES:(p + 1) * LANES] * dt_pair
        o_intra.append(_dot(jnp.concatenate([pa, pb], axis=1),
                            _pair_blockdiag(v_pair, lo_lane).astype(BF16)))
        yield
    o_intra = jnp.concatenate(o_intra, axis=1)
    e1 = jnp.concatenate(e1_cols, axis=1)
    e2 = jnp.concatenate(e2_cols, axis=1)
    dtx = jnp.concatenate(dt_cols, axis=1)
    v_all = xs * dtx
    s_prev = s_ssd[...]
    o_inter = e1 * _dot(cm_b, s_prev.astype(BF16))
    y = o_intra + o_inter + dexp_ref[...] * xs
    y = y * _silu(proj_ref[:, OFF_Z:OFF_Z + SSD_WIDTH])
    gw = SSD_WIDTH // SSD_GROUPS
    grp_mean = jnp.full((gw, gw), 1.0 / gw, BF16)
    ysq = (y * y).astype(BF16)
    ms = jnp.concatenate([_dot(ysq[:, g * gw:(g + 1) * gw], grp_mean) for g in range(SSD_GROUPS)], axis=1)
    out_ref[:, 0:SSD_WIDTH] = (y * lax.rsqrt(ms + EPS) * ssdg_ref[...]).astype(out_ref.dtype)
    row_s = lax.broadcasted_iota(jnp.int32, (LANES, SSD_WIDTH), 0)
    lane_s = lax.broadcasted_iota(jnp.int32, (LANES, SSD_WIDTH), 1)
    grp_mask = jnp.right_shift(row_s, 6) == jnp.right_shift(lane_s, 8)
    upd = _dot_tn(bm_b, (v_all * e2).astype(BF16))
    s_ssd[...] = e1[C - 1:C, :] * s_prev + jnp.where(grp_mask, upd, 0.0)
    yield

    W2 = HGRN_WIDTH
    row2 = lax.broadcasted_iota(jnp.int32, (C, W2), 0)
    lb_raw = lbraw_ref[...]
    lb_e = jnp.exp(lb_raw - jnp.max(lb_raw, axis=0, keepdims=True))
    lb_soft = lb_e / jnp.sum(lb_e, axis=0, keepdims=True)
    lb = jnp.sum(lb_soft[0:layer + 1, :], axis=0, keepdims=True) - lb_soft[0:1, :]
    forget = lb + (1.0 - lb) * jax.nn.sigmoid(proj_ref[:, OFF_HF:OFF_HF + W2])
    fc = jnp.maximum(forget, GATE_FLOOR)
    kk = 1.0 - forget
    hq = proj_ref[:, OFF_HQ:OFF_HQ + W2]
    hv = proj_ref[:, OFF_HI:OFF_HI + W2]
    cum = cumsum_rows(jnp.log(fc))
    npair = W2 // LANES

    sub = jnp.bitwise_and(row2, HGRN_DIRECT - 1)
    bd2 = jnp.where(jnp.right_shift(lax.broadcasted_iota(jnp.int32, (W2, W2), 0), 6)
                    == jnp.right_shift(lax.broadcasted_iota(jnp.int32, (W2, W2), 1), 6), 1.0, 0.0).astype(BF16)
    def prev_row(a):
        return pltpu.roll(a.reshape(C // HGRN_DIRECT, HGRN_DIRECT, W2), 1, 1).reshape(C, W2)

    not_first = sub != 0
    kdec = kk
    vprev = hv
    terms, vals = [(hq * kk).astype(BF16)], [hv]
    for dlt in range(1, HGRN_DIRECT):
        kdec = jnp.where(not_first, fc * prev_row(kdec), 0.0)
        vprev = prev_row(vprev)
        terms.append((hq * kdec).astype(BF16))
        vals.append(vprev)
    sc_all = _dot(jnp.concatenate(terms, axis=0), bd2)
    o_h = sc_all[0:C] * vals[0]
    for dlt in range(1, HGRN_DIRECT):
        o_h = o_h + sc_all[dlt * C:(dlt + 1) * C] * vals[dlt]
    yield
    p_sum = [[None, None] for _ in range(npair)]
    for m in HGRN_LEVELS:
        nb = C // (2 * m)
        cum3 = cum.reshape(nb, 2 * m, W2)
        ref = jnp.broadcast_to(cum3[:, m - 1:m, :], (nb, 2 * m, W2)).reshape(C, W2)
        right = jnp.bitwise_and(row2, m) != 0
        e = jnp.exp(jnp.where(right, cum - ref, ref - cum))
        qm = jnp.where(right, hq * e, 0.0)
        km = jnp.where(right, 0.0, kk * e).astype(BF16)
        sh = int(math.log2(2 * m))
        same_block = jnp.right_shift(row, sh) == jnp.right_shift(lane, sh)
        for p in range(npair):
            sl = slice(p * LANES, (p + 1) * LANES)
            for hh, s_h in enumerate(_pair_scores(qm[:, sl], km[:, sl], lo_lane)):
                sc = jnp.where(same_block, s_h, 0.0)
                p_sum[p][hh] = sc if p_sum[p][hh] is None else p_sum[p][hh] + sc
        yield
    cols = []
    for p in range(npair):
        vbd = _pair_blockdiag(hv[:, p * LANES:(p + 1) * LANES], lo_lane).astype(BF16)
        cols.append(_dot(jnp.concatenate([p_sum[p][0].astype(BF16), p_sum[p][1].astype(BF16)], axis=1), vbd))
    o_h = o_h + jnp.concatenate(cols, axis=1)
    q_in = (hq * jnp.exp(cum)).astype(BF16)
    last2 = cum[C - 1:C, :]
    k_end = (kk * jnp.exp(last2 - cum)).astype(BF16)
    tot2 = jnp.exp(last2)
    hv_b = hv.astype(BF16)
    inter = []
    for p in range(npair):
        sl = slice(p * LANES, (p + 1) * LANES)
        st = s_hg[p]
        inter.append(_dot_nt(q_in[:, sl], st.astype(BF16)))
        upd = _dot_tn(hv_b[:, sl], k_end[:, sl])
        s_hg[p] = st * tot2[:, sl] + jnp.where(bd_mask, upd, 0.0)
    o_h = o_h + jnp.concatenate(inter, axis=1)
    ms = _head_mean_square(o_h, bd_mean)
    o_h = o_h * lax.rsqrt(ms + EPS) * hgg_ref[...]
    o_h = o_h * jax.nn.sigmoid(proj_ref[:, OFF_HG:OFF_HG + W2])
    out_ref[:, SSD_WIDTH:SSD_WIDTH + W2] = o_h.astype(out_ref.dtype)
    yield

    W3 = RET_WIDTH
    lane3 = lax.broadcasted_iota(jnp.int32, (C, W3), 1)
    first_half = jnp.bitwise_and(lane3, HEAD_DIM // 2) == 0
    cosv = cos_ref[...]
    sinv = sin_ref[...]

    def rotary(xr):
        swapped = jnp.where(first_half, pltpu.roll(xr, W3 - HEAD_DIM // 2, 1),
                            pltpu.roll(xr, HEAD_DIM // 2, 1))
        return xr * cosv + swapped * sinv

    qr = rotary(proj_ref[:, OFF_RQ:OFF_RQ + W3])
    kr = rotary(proj_ref[:, OFF_RK:OFF_RK + W3]) * (HEAD_DIM ** -0.5)
    rv = proj_ref[:, OFF_RV:OFF_RV + W3]
    kr_b = kr.astype(BF16)
    q_in = (qr * rdq_ref[...]).astype(BF16)
    k_end = (kr * rdk_ref[...]).astype(BF16)
    rv_b = rv.astype(BF16)
    o_cols = []
    for p in range(W3 // LANES):
        sl = slice(p * LANES, (p + 1) * LANES)
        sa, sb = _pair_scores(qr[:, sl], kr_b[:, sl], lo_lane)
        sa = sa * rdec_ref[2 * p]
        sb = sb * rdec_ref[2 * p + 1]
        vbd = _pair_blockdiag(rv[:, sl], lo_lane).astype(BF16)
        o_p = _dot(jnp.concatenate([sa.astype(BF16), sb.astype(BF16)], axis=1), vbd)
        st = s_ret[p]
        o_p = o_p + _dot(q_in[:, sl], st.astype(BF16))
        upd = _dot_tn(k_end[:, sl], rv_b[:, sl])
        s_ret[p] = st * rtot_ref[:, sl] + jnp.where(bd_mask, upd, 0.0)
        o_cols.append(o_p)
        yield
    o_r = jnp.concatenate(o_cols, axis=1)
    ms = _head_mean_square(o_r, bd_mean)
    o_r = o_r * lax.rsqrt(ms + EPS) * _silu(proj_ref[:, OFF_RG:OFF_RG + W3])
    out_ref[:, SSD_WIDTH + W2:SSD_WIDTH + W2 + W3] = o_r.astype(out_ref.dtype)


def _mixer(x2, y, mod_prev, mod_l, g, w, expert_w, layer, batch, seq, conv_w, conv_b, dtb, alog, dexp, ssdg,
           lbraw, hgg, cos_t, sin_t, rdec, rdq, rdk, rtot):
    t, d = x2.shape
    d_mix = SSD_WIDTH + HGRN_WIDTH + RET_WIDTH
    nj = seq // CHUNK
    const2 = lambda j: (0, 0)
    has_res = y is not None
    proj_idx = lambda j: (0, jnp.minimum(j, nj - 1), 0)
    mix_idx = lambda j: (0, jnp.maximum(j - 1, 0), 0)
    x_spec = pl.BlockSpec((batch, CHUNK, d), proj_idx)
    mod_spec = pl.BlockSpec((batch, MOD_ROWS, d), lambda j: (0, 0, 0))
    head_specs = [x_spec]
    head_args = [x2.reshape(batch, seq, d)]
    if has_res:
        head_specs += [pl.BlockSpec((batch, CHUNK * ROW_TILES, LANES), proj_idx), mod_spec]
        head_args += [y.reshape(batch, seq * ROW_TILES, LANES), mod_prev]
    head_specs += [mod_spec, pl.BlockSpec((1, d), const2),
                   pl.BlockSpec((None,) + w.shape[1:], lambda j: (layer, 0, 0))]
    head_args += [mod_l, g, w]
    cast_specs, cast_shapes = [], []
    for ew in expert_w:
        depth_e, ne, ra, cb = ew.shape
        rows = ne * ra
        assert rows % nj == 0
        blk = rows // nj
        head_specs.append(pl.BlockSpec((None, blk, cb), lambda j: (layer, jnp.minimum(j, nj - 1), 0)))
        head_args.append(ew.reshape(depth_e, rows, cb))
        cast_specs.append(pl.BlockSpec((blk, cb), lambda j: (jnp.minimum(j, nj - 1), 0)))
        cast_shapes.append(jax.ShapeDtypeStruct((rows, cb), BF16))
    table_spec = pl.BlockSpec((CHUNK, RET_WIDTH), lambda j: (jnp.maximum(j - 1, 0), 0))
    mixed_spec = pl.BlockSpec((batch, CHUNK, d_mix), mix_idx)
    mixed_shape = jax.ShapeDtypeStruct((batch, seq, d_mix), BF16)
    res = pl.pallas_call(
        functools.partial(_mixer_kernel, layer, has_res),
        grid=(nj + 1,),
        in_specs=head_specs + [
            pl.BlockSpec(conv_w.shape, const2),
            pl.BlockSpec(conv_b.shape, const2),
            pl.BlockSpec(dtb.shape, const2),
            pl.BlockSpec(alog.shape, const2),
            pl.BlockSpec(dexp.shape, const2),
            pl.BlockSpec(ssdg.shape, const2),
            pl.BlockSpec(lbraw.shape, const2),
            pl.BlockSpec(hgg.shape, const2),
            table_spec,
            table_spec,
            pl.BlockSpec(rdec.shape, lambda j: (0, 0, 0)),
            pl.BlockSpec(rdq.shape, const2),
            pl.BlockSpec(rdk.shape, const2),
            pl.BlockSpec(rtot.shape, const2),
        ],
        out_specs=([x_spec] if has_res else []) + [mixed_spec] + cast_specs,
        out_shape=([jax.ShapeDtypeStruct((batch, seq, d), F32)] if has_res else []) + [mixed_shape] + cast_shapes,
        scratch_shapes=[
            pltpu.VMEM((batch, CHUNK, N_PROJ), F32),
            pltpu.VMEM((batch, CHUNK, N_PROJ), F32),
            pltpu.VMEM((batch, TAIL + CHUNK, XBC_WIDTH), F32),
            pltpu.VMEM((batch, LANES, SSD_WIDTH), F32),
            pltpu.VMEM((batch, HGRN_WIDTH // LANES, LANES, LANES), F32),
            pltpu.VMEM((batch, RET_WIDTH // LANES, LANES, LANES), F32),
        ],
        compiler_params=_cparams(("arbitrary",)),
        name="norm_inproj_mixers",
    )(*head_args, conv_w, conv_b, dtb, alog, dexp, ssdg, lbraw, hgg, cos_t, sin_t, rdec, rdq, rdk, rtot)
    if has_res:
        x2, res = res[0].reshape(t, d), res[1:]
    casts = [c.reshape(ew.shape[1:]) for c, ew in zip(res[1:], expert_w)]
    return x2, res[0].reshape(t, d_mix), casts


ROW_TILES = 8


def _store_row_tiled(ref, val):
    m = val.shape[0]
    for k in range(ROW_TILES):
        ref[pl.ds(k, m, stride=ROW_TILES), :] = val[:, k * LANES:(k + 1) * LANES]


def _load_row_tiled(ref):
    m = ref.shape[0] // ROW_TILES
    return jnp.concatenate([ref[pl.ds(k, m, stride=ROW_TILES), :] for k in range(ROW_TILES)], axis=1)


R_E1, R_E2, R_W1, R_W2 = range(4)


def _outproj_kernel(x_ref, mixed_ref, mod_ref, g_ref, w_ref, wr_ref, br_ref,
                    xo_ref, hp_ref, route_ref):
    x = x_ref[...] + mod_ref[G_MIX:G_MIX + 1, :] * _dot(mixed_ref[...], w_ref[...])
    xo_ref[...] = x
    h = _rms(x) * g_ref[...] * (1.0 + mod_ref[SC_FFN:SC_FFN + 1, :]) + mod_ref[SH_FFN:SH_FFN + 1, :]
    _store_row_tiled(hp_ref, h)
    h_hi = h.astype(BF16)
    h_lo = (h - h_hi.astype(F32)).astype(BF16)
    logits = (_dot(h_hi, wr_ref[0]) + _dot(h_lo, wr_ref[0]) + _dot(h_hi, wr_ref[1])) + br_ref[...]
    lane = lax.broadcasted_iota(jnp.int32, logits.shape, 1).astype(F32)
    neg = -jnp.inf
    big = float(ROUTER_LANES)
    is_grp = lane < N_GROUPS
    gl = jnp.where(is_grp, logits, neg)
    gmax = jnp.max(gl, axis=-1, keepdims=True)
    g_idx = jnp.min(jnp.where(gl == gmax, lane, big), axis=-1, keepdims=True)
    p_grp = 1.0 / jnp.sum(jnp.where(is_grp, jnp.exp(gl - gmax), 0.0), axis=-1, keepdims=True)
    first = N_GROUPS + g_idx * EXPERTS_PER_GROUP
    valid = (lane >= first) & (lane < first + EXPERTS_PER_GROUP)
    el = jnp.where(valid, logits, neg)
    m1 = jnp.max(el, axis=-1, keepdims=True)
    i1 = jnp.min(jnp.where(el == m1, lane, big), axis=-1, keepdims=True)
    el2 = jnp.where(lane == i1, neg, el)
    m2 = jnp.max(el2, axis=-1, keepdims=True)
    i2 = jnp.min(jnp.where(el2 == m2, lane, big), axis=-1, keepdims=True)
    e = jnp.exp(m2 - m1)
    w1 = p_grp / (1.0 + e)
    w2 = p_grp * e / (1.0 + e)
    route_ref[...] = jnp.where(lane == R_E1, i1 - N_GROUPS,
                               jnp.where(lane == R_E2, i2 - N_GROUPS,
                                         jnp.where(lane == R_W1, w1, jnp.where(lane == R_W2, w2, 0.0))))


def _outproj(x2, mixed, mod_l, g, w, wr, br, seq, layer, tm=512):
    t, d = x2.shape
    per_b = seq // tm
    return pl.pallas_call(
        _outproj_kernel,
        grid=(t // tm,),
        in_specs=[
            pl.BlockSpec((tm, d), lambda i: (i, 0)),
            pl.BlockSpec((tm, mixed.shape[1]), lambda i: (i, 0)),
            pl.BlockSpec((None, MOD_ROWS, d), lambda i: (i // per_b, 0, 0)),
            pl.BlockSpec((1, d), lambda i: (0, 0)),
            pl.BlockSpec((None,) + w.shape[1:], lambda i: (layer, 0, 0)),
            pl.BlockSpec((None,) + wr.shape[1:], lambda i: (layer, 0, 0, 0)),
            pl.BlockSpec(br.shape, lambda i: (0, 0)),
        ],
        out_specs=[
            pl.BlockSpec((tm, d), lambda i: (i, 0)),
            pl.BlockSpec((tm * ROW_TILES, LANES), lambda i: (i, 0)),
            pl.BlockSpec((tm, ROUTER_LANES), lambda i: (i, 0)),
        ],
        out_shape=[
            jax.ShapeDtypeStruct((t, d), F32),
            jax.ShapeDtypeStruct((t * ROW_TILES, LANES), F32),
            jax.ShapeDtypeStruct((t, ROUTER_LANES), F32),
        ],
        compiler_params=_cparams(("arbitrary",)),
        name="outproj_router",
    )(x2, mixed, mod_l, g, w, wr, br)


MOE_CHUNK = 2048
MOE_TILE = 160
SEG_ALIGN = 8
COMBINE_STEPS = 4
EXPERTS_PER_STEP = 4
SLOTS_PAD = 2 * MOE_CHUNK + N_EXPERTS * SEG_ALIGN
RANK_BLOCK = 256
M_OFF, M_CNT = 0, 1
S_POS1, S_POS2, S_W1, S_W2 = 0, 1, R_W1, R_W2


def _rank_kernel(route_ref, slots_ref, meta_ref):
    tb = route_ref.shape[0]
    lane = lax.broadcasted_iota(jnp.int32, (RANK_BLOCK, ROUTER_LANES), 1).astype(F32)
    r_i = lax.broadcasted_iota(jnp.int32, (RANK_BLOCK, RANK_BLOCK), 0)
    c_i = lax.broadcasted_iota(jnp.int32, (RANK_BLOCK, RANK_BLOCK), 1)
    strict_tril = jnp.where(r_i > c_i, 1.0, 0.0).astype(BF16)
    carry = jnp.zeros((1, ROUTER_LANES), F32)
    ranks = []
    for b in range(tb // RANK_BLOCK):
        blk = route_ref[b * RANK_BLOCK:(b + 1) * RANK_BLOCK, :]
        onehot = jnp.where((lane == blk[:, R_E1:R_E1 + 1]) | (lane == blk[:, R_E2:R_E2 + 1]), 1.0, 0.0)
        ranks.append(_dot(strict_tril, onehot.astype(BF16)) + carry)
        carry = carry + jnp.sum(onehot, axis=0, keepdims=True)
    cnt = carry
    seg = jnp.floor((cnt + (SEG_ALIGN - 1)) * (1.0 / SEG_ALIGN)) * SEG_ALIGN
    u_r = lax.broadcasted_iota(jnp.int32, (ROUTER_LANES, ROUTER_LANES), 0)
    u_c = lax.broadcasted_iota(jnp.int32, (ROUTER_LANES, ROUTER_LANES), 1)
    strict_triu = jnp.where(u_r < u_c, 1.0, 0.0).astype(BF16)
    seg_parts = _split3(jnp.broadcast_to(seg, (8, ROUTER_LANES)))
    off = (_dot(seg_parts[0], strict_triu) + _dot(seg_parts[1], strict_triu)
           + _dot(seg_parts[2], strict_triu))[0:1, :]
    for b in range(tb // RANK_BLOCK):
        blk = route_ref[b * RANK_BLOCK:(b + 1) * RANK_BLOCK, :]
        dest = off + ranks[b]
        p1 = jnp.sum(jnp.where(lane == blk[:, R_E1:R_E1 + 1], dest, 0.0), axis=-1, keepdims=True)
        p2 = jnp.sum(jnp.where(lane == blk[:, R_E2:R_E2 + 1], dest, 0.0), axis=-1, keepdims=True)
        rec = jnp.where(lane == S_POS1, p1, jnp.where(lane == S_POS2, p2,
                        jnp.where((lane == S_W1) | (lane == S_W2), blk, 0.0)))
        for h in range(RANK_BLOCK // LANES):
            t0 = b * RANK_BLOCK + h * LANES
            slots_ref[:, t0:t0 + LANES] = rec[h * LANES:(h + 1) * LANES, :].T[0:8, :]
    row8 = lax.broadcasted_iota(jnp.int32, (8, ROUTER_LANES), 0)
    meta_ref[...] = jnp.where(row8 == M_OFF, off, jnp.where(row8 == M_CNT, cnt, 0.0)).astype(jnp.int32)


def _rank(route):
    t = route.shape[0]
    nc = t // MOE_CHUNK
    return pl.pallas_call(
        _rank_kernel,
        grid=(nc,),
        in_specs=[pl.BlockSpec((MOE_CHUNK, ROUTER_LANES), lambda c: (c, 0))],
        out_specs=[pl.BlockSpec((8, MOE_CHUNK), lambda c: (0, c)),
                   pl.BlockSpec((None, 8, ROUTER_LANES), lambda c: (c, 0, 0))],
        out_shape=[jax.ShapeDtypeStruct((8, t), F32),
                   jax.ShapeDtypeStruct((nc, 8, ROUTER_LANES), jnp.int32)],
        compiler_params=_cparams(("arbitrary",)),
        name="slot_rank",
    )(route)


def _moe_kernel(pos1_ref, pos2_ref, w1_ref, w2_ref, off_ref, cnt_ref,
                hp_ref, wg_ref, wu_ref, wd_ref, y_ref, inv_ref, xt_ref, ys_ref):
    c = pl.program_id(0)
    e = pl.program_id(1)
    rt = ROW_TILES
    tb = hp_ref.shape[0] // rt
    tok0 = c * tb
    seg0 = c * N_EXPERTS

    @pl.when((c == 0) & (e == 0))
    def _():
        xt_ref[...] = jnp.zeros_like(xt_ref)

    @pl.when(e == 0)
    def _():
        def pads(ee, carry):
            o = off_ref[seg0 + ee]
            n_e = cnt_ref[seg0 + ee]

            def zero(i, carry2):
                inv_ref[o + i] = 0
                return carry2
            lax.fori_loop(n_e, (n_e + SEG_ALIGN - 1) // SEG_ALIGN * SEG_ALIGN, zero, 0)
            return carry
        lax.fori_loop(0, N_EXPERTS, pads, 0)
        n_last = cnt_ref[seg0 + N_EXPERTS - 1]
        total = off_ref[seg0 + N_EXPERTS - 1] + (n_last + SEG_ALIGN - 1) // SEG_ALIGN * SEG_ALIGN

        def zero_tail(i, carry):
            inv_ref[total + i] = 0
            return carry
        lax.fori_loop(0, MOE_TILE, zero_tail, 0)

        def place(i, carry):
            inv_ref[pos1_ref[tok0 + i]] = i
            inv_ref[pos2_ref[tok0 + i]] = i
            return carry
        lax.fori_loop(0, tb, place, 0, unroll=8)

    expert_steps = N_EXPERTS // EXPERTS_PER_STEP
    is_expert = e < expert_steps
    for k in range(EXPERTS_PER_STEP):
        e_k = jnp.minimum(e, expert_steps - 1) * EXPERTS_PER_STEP + k
        n = jnp.where(is_expert, cnt_ref[seg0 + e_k], 0)
        base = off_ref[seg0 + e_k]

        @pl.when(n > 0)
        def _(k=k, n=n, base=base):
            wg, wu, wd = wg_ref.at[k], wu_ref.at[k], wd_ref.at[k]

            def tile(i, carry):
                start = pl.multiple_of(base + i * MOE_TILE, SEG_ALIGN)

                def gather(r, carry2):
                    src = pl.multiple_of(inv_ref[start + r] * rt, rt)
                    xt_ref[pl.ds(pl.multiple_of(r * rt, rt), rt), :] = hp_ref[pl.ds(src, rt), :]
                    return carry2
                rows = jnp.minimum(MOE_TILE, (n - i * MOE_TILE + SEG_ALIGN - 1) // SEG_ALIGN * SEG_ALIGN)
                def gather_group(g, carry2):
                    for u in range(SEG_ALIGN):
                        gather(g * SEG_ALIGN + u, carry2)
                    return carry2
                lax.fori_loop(0, rows // SEG_ALIGN, gather_group, 0)
                xrow = jnp.concatenate([xt_ref[pl.ds(j, MOE_TILE, stride=rt), :] for j in range(rt)], axis=1)
                xb = xrow.astype(BF16)
                hid = (_silu(_dot(xb, wg[...])) * _dot(xb, wu[...])).astype(BF16)
                out = _dot(hid, wd[...])
                for j in range(rt):
                    ys_ref[pl.ds(start * rt + j, MOE_TILE, stride=rt), :] = out[:, j * LANES:(j + 1) * LANES]
                return carry
            lax.fori_loop(0, (n + MOE_TILE - 1) // MOE_TILE, tile, 0)

    @pl.when(e >= expert_steps)
    def _():
        tq = y_ref.shape[0] // rt
        t0 = tok0 + (e - expert_steps) * tq

        def combine(i, carry):
            r1 = ys_ref[pl.ds(pl.multiple_of(pos1_ref[t0 + i] * rt, rt), rt), :]
            r2 = ys_ref[pl.ds(pl.multiple_of(pos2_ref[t0 + i] * rt, rt), rt), :]
            y_ref[pl.ds(pl.multiple_of(i * rt, rt), rt), :] = w1_ref[t0 + i] * r1 + w2_ref[t0 + i] * r2
            return carry
        lax.fori_loop(0, tq, combine, 0, unroll=8)


def _moe(hp, pos1, pos2, w1, w2, off, cnt, w_gate, w_up, w_down):
    rt = ROW_TILES
    t = hp.shape[0] // rt
    ne, d, f = w_gate.shape
    nc = t // MOE_CHUNK
    tq = MOE_CHUNK // COMBINE_STEPS

    es = ne // EXPERTS_PER_STEP

    def w_idx(c, e, *_):
        return (jnp.minimum(e, es - 1), 0, 0)

    grid_spec = pltpu.PrefetchScalarGridSpec(
        num_scalar_prefetch=6,
        grid=(nc, es + COMBINE_STEPS),
        in_specs=[
            pl.BlockSpec((MOE_CHUNK * rt, LANES), lambda c, e, *_: (c, 0)),
            pl.BlockSpec((EXPERTS_PER_STEP, d, f), w_idx),
            pl.BlockSpec((EXPERTS_PER_STEP, d, f), w_idx),
            pl.BlockSpec((EXPERTS_PER_STEP, f, d), w_idx),
        ],
        out_specs=pl.BlockSpec((tq * rt, LANES),
                               lambda c, e, *_: (c * COMBINE_STEPS + jnp.maximum(e - es, 0), 0)),
        scratch_shapes=[
            pltpu.SMEM((SLOTS_PAD + MOE_TILE,), jnp.int32),
            pltpu.VMEM((MOE_TILE * rt, LANES), F32),
            pltpu.VMEM(((SLOTS_PAD + MOE_TILE) * rt, LANES), F32),
        ],
    )
    return pl.pallas_call(
        _moe_kernel,
        grid_spec=grid_spec,
        out_shape=jax.ShapeDtypeStruct((t * rt, LANES), F32),
        compiler_params=_cparams(("arbitrary", "arbitrary")),
        name="moe_top2",
    )(pos1, pos2, w1, w2, off, cnt, hp, w_gate, w_up, w_down)


def _final_kernel(x_ref, y_ref, mod_ref, g_ref, o_ref):
    x = x_ref[...] + mod_ref[G_FFN:G_FFN + 1, :] * _load_row_tiled(y_ref)
    o_ref[...] = _rms(x) * g_ref[...]


def _final(x2, y, mod_l, g, seq, tm=512):
    t, d = x2.shape
    per_b = seq // tm
    return pl.pallas_call(
        _final_kernel,
        grid=(t // tm,),
        in_specs=[
            pl.BlockSpec((tm, d), lambda i: (i, 0)),
            pl.BlockSpec((tm * ROW_TILES, LANES), lambda i: (i, 0)),
            pl.BlockSpec((None, MOD_ROWS, d), lambda i: (i // per_b, 0, 0)),
            pl.BlockSpec((1, d), lambda i: (0, 0)),
        ],
        out_specs=pl.BlockSpec((tm, d), lambda i: (i, 0)),
        out_shape=jax.ShapeDtypeStruct((t, d), F32),
        compiler_params=_cparams(("arbitrary",)),
        name="final_norm",
    )(x2, y, mod_l, g)


def _retention_tables(seq):
    half = HEAD_DIM // 2
    inv_freq = ROPE_BASE ** (-jnp.arange(half, dtype=F32) / half)
    ang = jnp.arange(seq, dtype=F32)[:, None] * inv_freq[None, :]
    cos, sin = jnp.cos(ang), jnp.sin(ang)
    cos_t = jnp.tile(jnp.concatenate([cos, cos], axis=-1), (1, RET_HEADS))
    sin_t = jnp.tile(jnp.concatenate([-sin, sin], axis=-1), (1, RET_HEADS))
    log_gamma = jnp.log1p(-jnp.exp2(-5.0 - jnp.arange(RET_HEADS, dtype=F32)))
    tt = jnp.arange(CHUNK, dtype=F32)
    diff = tt[:, None] - tt[None, :]
    rdec = jnp.where(diff >= 0, jnp.exp(jnp.maximum(diff, 0.0)[None] * log_gamma[:, None, None]), 0.0)
    lg_lane = jnp.repeat(log_gamma, HEAD_DIM)[None, :]
    rdq = jnp.exp((tt[:, None] + 1.0) * lg_lane)
    rdk = jnp.exp((CHUNK - 1.0 - tt[:, None]) * lg_lane)
    rtot = jnp.exp(CHUNK * lg_lane)
    return cos_t, sin_t, rdec, rdq, rdk, rtot


def _pad_lanes(v, width=LANES):
    return jnp.pad(v, ((0, 0), (0, width - v.shape[-1])))


def kernel(x, c, norm_mix_g, norm_ffn_g, final_norm_g, w_ada, b_ada, w_in, conv_w, conv_b, ssd_dt_bias, ssd_a_log, ssd_d, ssd_norm_g, hgrn_lower_bounds, hgrn_norm_g, w_out, w_grp, b_grp, w_exp, b_exp, w_gate, w_up, w_down):
    batch, seq, d = x.shape
    depth = w_in.shape[0]
    t = batch * seq

    c8 = jnp.pad(c, ((0, MOD_ROWS - batch), (0, 0)))
    mod = _ada(c8, w_ada, b_ada)
    mod = mod[:, :batch].reshape(depth, batch, 6, d)
    mod = jnp.pad(mod, ((0, 0), (0, 0), (0, MOD_ROWS - 6), (0, 0)))

    dt0 = SSD_WIDTH + XBC_WIDTH
    w_in_p = jnp.concatenate(
        [w_in[:, :, :dt0], w_in[:, :, dt0 + SSD_HEADS:], w_in[:, :, dt0:dt0 + SSD_HEADS],
         jnp.zeros((depth, d, N_PROJ - w_in.shape[2]), w_in.dtype)], axis=-1).astype(BF16)
    w_out_b = w_out.astype(BF16)
    w_router = _pad_lanes(jnp.concatenate([w_grp, w_exp], axis=-1).reshape(depth * d, -1)).reshape(depth, d, ROUTER_LANES)
    w_router_hi = w_router.astype(BF16)
    w_router = jnp.stack([w_router_hi, (w_router - w_router_hi.astype(F32)).astype(BF16)], axis=1)
    b_router = _pad_lanes(jnp.concatenate([b_grp, b_exp], axis=-1))[:, None, :]
    dtb = _pad_lanes(ssd_dt_bias)[:, None, :]
    alog = _pad_lanes(ssd_a_log)[:, None, :]
    dexp = jnp.repeat(ssd_d, HEAD_DIM, axis=-1)[:, None, :]
    tables = _retention_tables(seq)

    x2 = x.reshape(t, d)
    y = None
    for layer in range(depth):
        x2, mixed, expert_w = _mixer(x2, y, mod[layer - 1] if layer else None, mod[layer],
                           norm_mix_g[layer][None, :], w_in_p, (w_gate, w_up, w_down), layer, batch, seq,
                           conv_w[layer], conv_b[layer][None, :], dtb[layer],
                           alog[layer], dexp[layer], ssd_norm_g[layer][None, :], hgrn_lower_bounds,
                           hgrn_norm_g[layer][None, :], *tables)
        x2, hp, route = _outproj(x2, mixed, mod[layer], norm_ffn_g[layer][None, :], w_out_b,
                                 w_router, b_router[layer], seq, layer)
        slots, meta = _rank(route)
        y = _moe(hp, slots[S_POS1].astype(jnp.int32), slots[S_POS2].astype(jnp.int32), slots[S_W1], slots[S_W2],
                 meta[:, M_OFF, :N_EXPERTS].reshape(-1), meta[:, M_CNT, :N_EXPERTS].reshape(-1),
                 *expert_w)
    out = _final(x2, y, mod[depth - 1], final_norm_g[None, :], seq)
    return out.reshape(batch, seq, d)
```

```python
import functools
import math

import jax
import jax.numpy as jnp
from jax import lax
from jax.experimental import pallas as pl
from jax.experimental.pallas import tpu as pltpu

F32 = jnp.float32
BF16 = jnp.bfloat16

EPS = 1e-6
GATE_FLOOR = 1e-30
ROPE_BASE = 10000.0

HEAD_DIM = 64
SSD_HEADS = 8
SSD_GROUPS = 2
SSD_CONV = 4
HGRN_HEADS = 4
RET_HEADS = 4
N_GROUPS = 4
EXPERTS_PER_GROUP = 8
N_EXPERTS = N_GROUPS * EXPERTS_PER_GROUP
CHUNK = 128
LANES = 128
ROUTER_LANES = 128

SH_MIX, SC_MIX, G_MIX, SH_FFN, SC_FFN, G_FFN = range(6)
MOD_ROWS = 8

VMEM_LIMIT = 56 * 1024 * 1024


def _cparams(sem):
    return pltpu.CompilerParams(dimension_semantics=sem, vmem_limit_bytes=VMEM_LIMIT)


def _silu(x):
    return x * jax.nn.sigmoid(x)


def _rms(x):
    return x * lax.rsqrt(jnp.mean(x * x, axis=-1, keepdims=True) + EPS)


def _split3(x):
    hi = x.astype(BF16)
    r = x - hi.astype(F32)
    mid = r.astype(BF16)
    lo = (r - mid.astype(F32)).astype(BF16)
    return hi, mid, lo


def _dot(a, b):
    return jnp.dot(a, b, preferred_element_type=F32)


def _dot_nt(a, b):
    return lax.dot_general(a, b, (((1,), (1,)), ((), ())), preferred_element_type=F32)


def _dot_tn(a, b):
    return lax.dot_general(a, b, (((0,), (0,)), ((), ())), preferred_element_type=F32)


def _ada_kernel(c_ref, w_ref, b_ref, o_ref):
    c = _silu(c_ref[...])
    w = w_ref[...]
    c_hi, w_hi = c.astype(BF16), w.astype(BF16)
    c_lo = (c - c_hi.astype(F32)).astype(BF16)
    w_lo = (w - w_hi.astype(F32)).astype(BF16)
    o_ref[...] = (_dot(c_hi, w_hi) + _dot(c_lo, w_hi) + _dot(c_hi, w_lo)) + b_ref[...]


def _ada(c8, w_ada, b_ada):
    depth, d, d6 = w_ada.shape
    nb = d6 // d
    return pl.pallas_call(
        _ada_kernel,
        grid=(depth, nb),
        in_specs=[
            pl.BlockSpec((MOD_ROWS, d), lambda l, n: (0, 0)),
            pl.BlockSpec((None, d, d), lambda l, n: (l, 0, n)),
            pl.BlockSpec((None, 1, d), lambda l, n: (l, 0, n)),
        ],
        out_specs=pl.BlockSpec((None, MOD_ROWS, d), lambda l, n: (l, 0, n)),
        out_shape=jax.ShapeDtypeStruct((depth, MOD_ROWS, d6), F32),
        compiler_params=_cparams(("arbitrary", "arbitrary")),
        name="ada_mod",
    )(c8, w_ada, b_ada.reshape(depth, 1, d6))


OFF_Z, OFF_XBC, OFF_HQ, OFF_HF, OFF_HI, OFF_HG = 0, 512, 1280, 1536, 1792, 2048
OFF_RQ, OFF_RK, OFF_RV, OFF_RG, OFF_DT = 2304, 2560, 2816, 3072, 3328
N_PROJ = 3456
SSD_WIDTH = SSD_HEADS * HEAD_DIM
XBC_WIDTH = SSD_WIDTH + 2 * SSD_GROUPS * HEAD_DIM
HGRN_WIDTH = HGRN_HEADS * HEAD_DIM
RET_WIDTH = RET_HEADS * HEAD_DIM
TAIL = 8
HGRN_DIRECT = 8
HGRN_LEVELS = (8, 16, 32, 64)


def _pair_blockdiag(v_pair, lo_lane):
    return jnp.concatenate([jnp.where(lo_lane, v_pair, 0.0), jnp.where(lo_lane, 0.0, v_pair)], axis=0)


def _head_mean_square(o, bd):
    c, n = o.shape[0], o.shape[1] // LANES
    sq = (o * o).astype(BF16)
    ms = _dot(jnp.concatenate([sq[:, p * LANES:(p + 1) * LANES] for p in range(n)], axis=0), bd)
    return jnp.concatenate([ms[p * c:(p + 1) * c] for p in range(n)], axis=1)


def _pair_scores(q_pair, k_pair_b, lo_lane):
    c = q_pair.shape[0]
    q2 = jnp.concatenate([jnp.where(lo_lane, q_pair, 0.0), jnp.where(lo_lane, 0.0, q_pair)], axis=0)
    s = _dot_nt(q2.astype(BF16), k_pair_b)
    return s[:c], s[c:]


N_MIXER_PARAMS = 14
N_EXPERT_ARRAYS = 3


def _mixer_kernel(layer, has_res, *refs):
    if has_res:
        x_ref, y_ref, modp_ref = refs[:3]
        refs = refs[3:]
    else:
        x_ref = refs[0]
        refs = refs[1:]
    mod_ref, g_ref, w_ref = refs[:3]
    expert_w = refs[3:3 + N_EXPERT_ARRAYS]
    rest = refs[3 + N_EXPERT_ARRAYS:]
    params, outs = rest[:N_MIXER_PARAMS], rest[N_MIXER_PARAMS:]
    if has_res:
        xo_ref = outs[0]
        outs = outs[1:]
    out_ref = outs[0]
    expert_w_bf16 = outs[1:1 + N_EXPERT_ARRAYS]
    proj_next, proj_cur, xbuf, s_ssd, s_hg, s_ret = outs[1 + N_EXPERT_ARRAYS:]
    j = pl.program_id(0)
    nb = x_ref.shape[0]

    @pl.when(j == 0)
    def _():
        xbuf[:, 0:TAIL, :] = jnp.zeros((nb, TAIL, XBC_WIDTH), F32)
        s_ssd[...] = jnp.zeros_like(s_ssd)
        s_hg[...] = jnp.zeros_like(s_hg)
        s_ret[...] = jnp.zeros_like(s_ret)

    for src, dst in zip(expert_w, expert_w_bf16):
        dst[...] = src[...].astype(BF16)

    hs = []
    for b in range(nb):
        x = x_ref[b]
        if has_res:
            x = x + modp_ref[b, G_FFN:G_FFN + 1, :] * _load_row_tiled(y_ref.at[b])
            xo_ref[b] = x
        hs.append(_rms(x) * g_ref[...] * (1.0 + mod_ref[b, SC_MIX:SC_MIX + 1, :]) + mod_ref[b, SH_MIX:SH_MIX + 1, :])
    proj = _dot(jnp.concatenate(hs, axis=0).astype(BF16), w_ref[...])
    for b in range(nb):
        proj_next[b] = proj[b * CHUNK:(b + 1) * CHUNK]

    @pl.when(j > 0)
    def _():
        stages = [_mixer_body(layer, proj_cur.at[b], *params, out_ref.at[b], xbuf.at[b], s_ssd.at[b],
                              s_hg.at[b], s_ret.at[b]) for b in range(nb)]
        while stages:
            for gen in list(stages):
                if next(gen, StopIteration) is StopIteration:
                    stages.remove(gen)

    proj_cur[...] = proj_next[...]


def _mixer_body(layer, proj_ref, convw_ref, convb_ref, dtb_ref, alog_ref, dexp_ref, ssdg_ref,
                lbraw_ref, hgg_ref, cos_ref, sin_ref, rdec_ref, rdq_ref, rdk_ref, rtot_ref,
                out_ref, xbuf, s_ssd, s_hg, s_ret):
    C = CHUNK

    row = lax.broadcasted_iota(jnp.int32, (C, LANES), 0)
    lane = lax.broadcasted_iota(jnp.int32, (C, LANES), 1)
    lo_lane = lane < HEAD_DIM
    causal = row >= lane
    bd_mask = jnp.right_shift(row, 6) == jnp.right_shift(lane, 6)
    bd_mean = jnp.where(bd_mask, 1.0 / HEAD_DIM, 0.0).astype(BF16)
    tril = jnp.where(causal, 1.0, 0.0).astype(BF16)

    def cumsum_rows(x):
        hi, mid, lo = _split3(x)
        return _dot(tril, hi) + _dot(tril, mid) + _dot(tril, lo)

    xbuf[TAIL:TAIL + C, :] = proj_ref[:, OFF_XBC:OFF_XBC + XBC_WIDTH]
    conv = convb_ref[...]
    for jj in range(SSD_CONV):
        off = TAIL - (SSD_CONV - 1) + jj
        conv = conv + convw_ref[jj:jj + 1, :] * xbuf[off:off + C, :]
    xbuf[0:TAIL, :] = xbuf[C:C + TAIL, :]
    xc = _silu(conv)
    xs = xc[:, 0:SSD_WIDTH]
    bm = xc[:, SSD_WIDTH:SSD_WIDTH + LANES]
    cm = xc[:, SSD_WIDTH + LANES:SSD_WIDTH + 2 * LANES]

    dt8 = jax.nn.softplus(proj_ref[:, OFF_DT:OFF_DT + LANES] + dtb_ref[...])
    la8 = dt8 * (-jnp.exp(alog_ref[...]))
    cs8 = cumsum_rows(la8)
    cs8t = cs8.T

    bm_b = bm.astype(BF16)
    cm_b = cm.astype(BF16)
    scores_g = _pair_scores(cm, bm_b, lo_lane)
    yield

    o_intra, e1_cols, e2_cols, dt_cols = [], [], [], []
    for p in range(SSD_HEADS // 2):
        a, b = 2 * p, 2 * p + 1
        g = a // (SSD_HEADS // SSD_GROUPS)
        col_a = jnp.broadcast_to(cs8[:, a:a + 1], (C, LANES))
        col_b = jnp.broadcast_to(cs8[:, b:b + 1], (C, LANES))
        cs_pair = jnp.where(lo_lane, col_a, col_b)
        dt_pair = jnp.where(lo_lane, jnp.broadcast_to(dt8[:, a:a + 1], (C, LANES)),
                            jnp.broadcast_to(dt8[:, b:b + 1], (C, LANES)))
        last = cs_pair[C - 1:C, :]
        e1_cols.append(jnp.exp(cs_pair))
        e2_cols.append(jnp.exp(last - cs_pair))
        dt_cols.append(dt_pair)
        dec_a = jnp.where(causal, jnp.exp(jnp.minimum(col_a - cs8t[a:a + 1, :], 0.0)), 0.0)
        dec_b = jnp.where(causal, jnp.exp(jnp.minimum(col_b - cs8t[b:b + 1, :], 0.0)), 0.0)
        pa = (scores_g[g] * dec_a).astype(BF16)
        pb = (scores_g[g] * dec_b).astype(BF16)
        v_pair = xs[:, p * LANES:(p + 1) * LANES] * dt_pair
        o_intra.append(_dot(jnp.concatenate([pa, pb], axis=1),
                            _pair_blockdiag(v_pair, lo_lane).astype(BF16)))
        yield
    o_intra = jnp.concatenate(o_intra, axis=1)
    e1 = jnp.concatenate(e1_cols, axis=1)
    e2 = jnp.concatenate(e2_cols, axis=1)
    dtx = jnp.concatenate(dt_cols, axis=1)
    v_all = xs * dtx
    s_prev = s_ssd[...]
    o_inter = e1 * _dot(cm_b, s_prev.astype(BF16))
    y = o_intra + o_inter + dexp_ref[...] * xs
    y = y * _silu(proj_ref[:, OFF_Z:OFF_Z + SSD_WIDTH])
    gw = SSD_WIDTH // SSD_GROUPS
    grp_mean = jnp.full((gw, gw), 1.0 / gw, BF16)
    ysq = (y * y).astype(BF16)
    ms = jnp.concatenate([_dot(ysq[:, g * gw:(g + 1) * gw], grp_mean) for g in range(SSD_GROUPS)], axis=1)
    out_ref[:, 0:SSD_WIDTH] = (y * lax.rsqrt(ms + EPS) * ssdg_ref[...]).astype(out_ref.dtype)
    row_s = lax.broadcasted_iota(jnp.int32, (LANES, SSD_WIDTH), 0)
    lane_s = lax.broadcasted_iota(jnp.int32, (LANES, SSD_WIDTH), 1)
    grp_mask = jnp.right_shift(row_s, 6) == jnp.right_shift(lane_s, 8)
    upd = _dot_tn(bm_b, (v_all * e2).astype(BF16))
    s_ssd[...] = e1[C - 1:C, :] * s_prev + jnp.where(grp_mask, upd, 0.0)
    yield

    W2 = HGRN_WIDTH
    row2 = lax.broadcasted_iota(jnp.int32, (C, W2), 0)
    lb_raw = lbraw_ref[...]
    lb_e = jnp.exp(lb_raw - jnp.max(lb_raw, axis=0, keepdims=True))
    lb_soft = lb_e / jnp.sum(lb_e, axis=0, keepdims=True)
    lb = jnp.sum(lb_soft[0:layer + 1, :], axis=0, keepdims=True) - lb_soft[0:1, :]
    forget = lb + (1.0 - lb) * jax.nn.sigmoid(proj_ref[:, OFF_HF:OFF_HF + W2])
    fc = jnp.maximum(forget, GATE_FLOOR)
    kk = 1.0 - forget
    hq = proj_ref[:, OFF_HQ:OFF_HQ + W2]
    hv = proj_ref[:, OFF_HI:OFF_HI + W2]
    cum = cumsum_rows(jnp.log(fc))
    npair = W2 // LANES

    sub = jnp.bitwise_and(row2, HGRN_DIRECT - 1)
    bd2 = jnp.where(jnp.right_shift(lax.broadcasted_iota(jnp.int32, (W2, W2), 0), 6)
                    == jnp.right_shift(lax.broadcasted_iota(jnp.int32, (W2, W2), 1), 6), 1.0, 0.0).astype(BF16)
    def prev_row(a):
        return pltpu.roll(a.reshape(C // HGRN_DIRECT, HGRN_DIRECT, W2), 1, 1).reshape(C, W2)

    not_first = sub != 0
    kdec = kk
    vprev = hv
    terms, vals = [(hq * kk).astype(BF16)], [hv]
    for dlt in range(1, HGRN_DIRECT):
        kdec = jnp.where(not_first, fc * prev_row(kdec), 0.0)
        vprev = prev_row(vprev)
        terms.append((hq * kdec).astype(BF16))
        vals.append(vprev)
    sc_all = _dot(jnp.concatenate(terms, axis=0), bd2)
    o_h = sc_all[0:C] * vals[0]
    for dlt in range(1, HGRN_DIRECT):
        o_h = o_h + sc_all[dlt * C:(dlt + 1) * C] * vals[dlt]
    yield
    p_sum = [[None, None] for _ in range(npair)]
    for m in HGRN_LEVELS:
        nb = C // (2 * m)
        cum3 = cum.reshape(nb, 2 * m, W2)
        ref = jnp.broadcast_to(cum3[:, m - 1:m, :], (nb, 2 * m, W2)).reshape(C, W2)
        right = jnp.bitwise_and(row2, m) != 0
        e = jnp.exp(jnp.where(right, cum - ref, ref - cum))
        qm = jnp.where(right, hq * e, 0.0)
        km = jnp.where(right, 0.0, kk * e).astype(BF16)
        sh = int(math.log2(2 * m))
        same_block = jnp.right_shift(row, sh) == jnp.right_shift(lane, sh)
        for p in range(npair):
            sl = slice(p * LANES, (p + 1) * LANES)
            for hh, s_h in enumerate(_pair_scores(qm[:, sl], km[:, sl], lo_lane)):
                sc = jnp.where(same_block, s_h, 0.0)
                p_sum[p][hh] = sc if p_sum[p][hh] is None else p_sum[p][hh] + sc
        yield
    cols = []
    for p in range(npair):
        vbd = _pair_blockdiag(hv[:, p * LANES:(p + 1) * LANES], lo_lane).astype(BF16)
        cols.append(_dot(jnp.concatenate([p_sum[p][0].astype(BF16), p_sum[p][1].astype(BF16)], axis=1), vbd))
    o_h = o_h + jnp.concatenate(cols, axis=1)
    q_in = (hq * jnp.exp(cum)).astype(BF16)
    last2 = cum[C - 1:C, :]
    k_end = (kk * jnp.exp(last2 - cum)).astype(BF16)
    tot2 = jnp.exp(last2)
    hv_b = hv.astype(BF16)
    inter = []
    for p in range(npair):
        sl = slice(p * LANES, (p + 1) * LANES)
        st = s_hg[p]
        inter.append(_dot_nt(q_in[:, sl], st.astype(BF16)))
        upd = _dot_tn(hv_b[:, sl], k_end[:, sl])
        s_hg[p] = st * tot2[:, sl] + jnp.where(bd_mask, upd, 0.0)
    o_h = o_h + jnp.concatenate(inter, axis=1)
    ms = _head_mean_square(o_h, bd_mean)
    o_h = o_h * lax.rsqrt(ms + EPS) * hgg_ref[...]
    o_h = o_h * jax.nn.sigmoid(proj_ref[:, OFF_HG:OFF_HG + W2])
    out_ref[:, SSD_WIDTH:SSD_WIDTH + W2] = o_h.astype(out_ref.dtype)
    yield

    W3 = RET_WIDTH
    lane3 = lax.broadcasted_iota(jnp.int32, (C, W3), 1)
    first_half = jnp.bitwise_and(lane3, HEAD_DIM // 2) == 0
    cosv = cos_ref[...]
    sinv = sin_ref[...]

    def rotary(xr):
        swapped = jnp.where(first_half, pltpu.roll(xr, W3 - HEAD_DIM // 2, 1),
                            pltpu.roll(xr, HEAD_DIM // 2, 1))
        return xr * cosv + swapped * sinv

    qr = rotary(proj_ref[:, OFF_RQ:OFF_RQ + W3])
    kr = rotary(proj_ref[:, OFF_RK:OFF_RK + W3]) * (HEAD_DIM ** -0.5)
    rv = proj_ref[:, OFF_RV:OFF_RV + W3]
    kr_b = kr.astype(BF16)
    q_in = (qr * rdq_ref[...]).astype(BF16)
    k_end = (kr * rdk_ref[...]).astype(BF16)
    rv_b = rv.astype(BF16)
    o_cols = []
    for p in range(W3 // LANES):
        sl = slice(p * LANES, (p + 1) * LANES)
        sa, sb = _pair_scores(qr[:, sl], kr_b[:, sl], lo_lane)
        sa = sa * rdec_ref[2 * p]
        sb = sb * rdec_ref[2 * p + 1]
        vbd = _pair_blockdiag(rv[:, sl], lo_lane).astype(BF16)
        o_p = _dot(jnp.concatenate([sa.astype(BF16), sb.astype(BF16)], axis=1), vbd)
        st = s_ret[p]
        o_p = o_p + _dot(q_in[:, sl], st.astype(BF16))
        upd = _dot_tn(k_end[:, sl], rv_b[:, sl])
        s_ret[p] = st * rtot_ref[:, sl] + jnp.where(bd_mask, upd, 0.0)
        o_cols.append(o_p)
        yield
    o_r = jnp.concatenate(o_cols, axis=1)
    ms = _head_mean_square(o_r, bd_mean)
    o_r = o_r * lax.rsqrt(ms + EPS) * _silu(proj_ref[:, OFF_RG:OFF_RG + W3])
    out_ref[:, SSD_WIDTH + W2:SSD_WIDTH + W2 + W3] = o_r.astype(out_ref.dtype)


def _mixer(x2, y, mod_prev, mod_l, g, w, expert_w, layer, batch, seq, conv_w, conv_b, dtb, alog, dexp, ssdg,
           lbraw, hgg, cos_t, sin_t, rdec, rdq, rdk, rtot):
    t, d = x2.shape
    d_mix = SSD_WIDTH + HGRN_WIDTH + RET_WIDTH
    nj = seq // CHUNK
    const2 = lambda j: (0, 0)
    has_res = y is not None
    proj_idx = lambda j: (0, jnp.minimum(j, nj - 1), 0)
    mix_idx = lambda j: (0, jnp.maximum(j - 1, 0), 0)
    x_spec = pl.BlockSpec((batch, CHUNK, d), proj_idx)
    mod_spec = pl.BlockSpec((batch, MOD_ROWS, d), lambda j: (0, 0, 0))
    head_specs = [x_spec]
    head_args = [x2.reshape(batch, seq, d)]
    if has_res:
        head_specs += [pl.BlockSpec((batch, CHUNK * ROW_TILES, LANES), proj_idx), mod_spec]
        head_args += [y.reshape(batch, seq * ROW_TILES, LANES), mod_prev]
    head_specs += [mod_spec, pl.BlockSpec((1, d), const2),
                   pl.BlockSpec((None,) + w.shape[1:], lambda j: (layer, 0, 0))]
    head_args += [mod_l, g, w]
    cast_specs, cast_shapes = [], []
    for ew in expert_w:
        depth_e, ne, ra, cb = ew.shape
        rows = ne * ra
        assert rows % nj == 0
        blk = rows // nj
        head_specs.append(pl.BlockSpec((None, blk, cb), lambda j: (layer, jnp.minimum(j, nj - 1), 0)))
        head_args.append(ew.reshape(depth_e, rows, cb))
        cast_specs.append(pl.BlockSpec((blk, cb), lambda j: (jnp.minimum(j, nj - 1), 0)))
        cast_shapes.append(jax.ShapeDtypeStruct((rows, cb), BF16))
    table_spec = pl.BlockSpec((CHUNK, RET_WIDTH), lambda j: (jnp.maximum(j - 1, 0), 0))
    mixed_spec = pl.BlockSpec((batch, CHUNK, d_mix), mix_idx)
    mixed_shape = jax.ShapeDtypeStruct((batch, seq, d_mix), BF16)
    res = pl.pallas_call(
        functools.partial(_mixer_kernel, layer, has_res),
        grid=(nj + 1,),
        in_specs=head_specs + [
            pl.BlockSpec(conv_w.shape, const2),
            pl.BlockSpec(conv_b.shape, const2),
            pl.BlockSpec(dtb.shape, const2),
            pl.BlockSpec(alog.shape, const2),
            pl.BlockSpec(dexp.shape, const2),
            pl.BlockSpec(ssdg.shape, const2),
            pl.BlockSpec(lbraw.shape, const2),
            pl.BlockSpec(hgg.shape, const2),
            table_spec,
            table_spec,
            pl.BlockSpec(rdec.shape, lambda j: (0, 0, 0)),
            pl.BlockSpec(rdq.shape, const2),
            pl.BlockSpec(rdk.shape, const2),
            pl.BlockSpec(rtot.shape, const2),
        ],
        out_specs=([x_spec] if has_res else []) + [mixed_spec] + cast_specs,
        out_shape=([jax.ShapeDtypeStruct((batch, seq, d), F32)] if has_res else []) + [mixed_shape] + cast_shapes,
        scratch_shapes=[
            pltpu.VMEM((batch, CHUNK, N_PROJ), F32),
            pltpu.VMEM((batch, CHUNK, N_PROJ), F32),
            pltpu.VMEM((batch, TAIL + CHUNK, XBC_WIDTH), F32),
            pltpu.VMEM((batch, LANES, SSD_WIDTH), F32),
            pltpu.VMEM((batch, HGRN_WIDTH // LANES, LANES, LANES), F32),
            pltpu.VMEM((batch, RET_WIDTH // LANES, LANES, LANES), F32),
        ],
        compiler_params=_cparams(("arbitrary",)),
        name="norm_inproj_mixers",
    )(*head_args, conv_w, conv_b, dtb, alog, dexp, ssdg, lbraw, hgg, cos_t, sin_t, rdec, rdq, rdk, rtot)
    if has_res:
        x2, res = res[0].reshape(t, d), res[1:]
    casts = [c.reshape(ew.shape[1:]) for c, ew in zip(res[1:], expert_w)]
    return x2, res[0].reshape(t, d_mix), casts


ROW_TILES = 8


def _store_row_tiled(ref, val):
    m = val.shape[0]
    for k in range(ROW_TILES):
        ref[pl.ds(k, m, stride=ROW_TILES), :] = val[:, k * LANES:(k + 1) * LANES]


def _load_row_tiled(ref):
    m = ref.shape[0] // ROW_TILES
    return jnp.concatenate([ref[pl.ds(k, m, stride=ROW_TILES), :] for k in range(ROW_TILES)], axis=1)


R_E1, R_E2, R_W1, R_W2 = range(4)


def _outproj_kernel(x_ref, mixed_ref, mod_ref, g_ref, w_ref, wr_ref, br_ref,
                    xo_ref, hp_ref, route_ref):
    x = x_ref[...] + mod_ref[G_MIX:G_MIX + 1, :] * _dot(mixed_ref[...], w_ref[...])
    xo_ref[...] = x
    h = _rms(x) * g_ref[...] * (1.0 + mod_ref[SC_FFN:SC_FFN + 1, :]) + mod_ref[SH_FFN:SH_FFN + 1, :]
    _store_row_tiled(hp_ref, h)
    h_hi = h.astype(BF16)
    h_lo = (h - h_hi.astype(F32)).astype(BF16)
    logits = (_dot(h_hi, wr_ref[0]) + _dot(h_lo, wr_ref[0]) + _dot(h_hi, wr_ref[1])) + br_ref[...]
    lane = lax.broadcasted_iota(jnp.int32, logits.shape, 1).astype(F32)
    neg = -jnp.inf
    big = float(ROUTER_LANES)
    is_grp = lane < N_GROUPS
    gl = jnp.where(is_grp, logits, neg)
    gmax = jnp.max(gl, axis=-1, keepdims=True)
    g_idx = jnp.min(jnp.where(gl == gmax, lane, big), axis=-1, keepdims=True)
    p_grp = 1.0 / jnp.sum(jnp.where(is_grp, jnp.exp(gl - gmax), 0.0), axis=-1, keepdims=True)
    first = N_GROUPS + g_idx * EXPERTS_PER_GROUP
    valid = (lane >= first) & (lane < first + EXPERTS_PER_GROUP)
    el = jnp.where(valid, logits, neg)
    m1 = jnp.max(el, axis=-1, keepdims=True)
    i1 = jnp.min(jnp.where(el == m1, lane, big), axis=-1, keepdims=True)
    el2 = jnp.where(lane == i1, neg, el)
    m2 = jnp.max(el2, axis=-1, keepdims=True)
    i2 = jnp.min(jnp.where(el2 == m2, lane, big), axis=-1, keepdims=True)
    e = jnp.exp(m2 - m1)
    w1 = p_grp / (1.0 + e)
    w2 = p_grp * e / (1.0 + e)
    route_ref[...] = jnp.where(lane == R_E1, i1 - N_GROUPS,
                               jnp.where(lane == R_E2, i2 - N_GROUPS,
                                         jnp.where(lane == R_W1, w1, jnp.where(lane == R_W2, w2, 0.0))))


def _outproj(x2, mixed, mod_l, g, w, wr, br, seq, layer, tm=512):
    t, d = x2.shape
    per_b = seq // tm
    return pl.pallas_call(
        _outproj_kernel,
        grid=(t // tm,),
        in_specs=[
            pl.BlockSpec((tm, d), lambda i: (i, 0)),
            pl.BlockSpec((tm, mixed.shape[1]), lambda i: (i, 0)),
            pl.BlockSpec((None, MOD_ROWS, d), lambda i: (i // per_b, 0, 0)),
            pl.BlockSpec((1, d), lambda i: (0, 0)),
            pl.BlockSpec((None,) + w.shape[1:], lambda i: (layer, 0, 0)),
            pl.BlockSpec((None,) + wr.shape[1:], lambda i: (layer, 0, 0, 0)),
            pl.BlockSpec(br.shape, lambda i: (0, 0)),
        ],
        out_specs=[
            pl.BlockSpec((tm, d), lambda i: (i, 0)),
            pl.BlockSpec((tm * ROW_TILES, LANES), lambda i: (i, 0)),
            pl.BlockSpec((tm, ROUTER_LANES), lambda i: (i, 0)),
        ],
        out_shape=[
            jax.ShapeDtypeStruct((t, d), F32),
            jax.ShapeDtypeStruct((t * ROW_TILES, LANES), F32),
            jax.ShapeDtypeStruct((t, ROUTER_LANES), F32),
        ],
        compiler_params=_cparams(("arbitrary",)),
        name="outproj_router",
    )(x2, mixed, mod_l, g, w, wr, br)


MOE_CHUNK = 2048
MOE_TILE = 160
SEG_ALIGN = 8
COMBINE_STEPS = 4
EXPERTS_PER_STEP = 4
SLOTS_PAD = 2 * MOE_CHUNK + N_EXPERTS * SEG_ALIGN
RANK_BLOCK = 256
M_OFF, M_CNT = 0, 1
S_POS1, S_POS2, S_W1, S_W2 = 0, 1, R_W1, R_W2


def _rank_kernel(route_ref, slots_ref, meta_ref):
    tb = route_ref.shape[0]
    lane = lax.broadcasted_iota(jnp.int32, (RANK_BLOCK, ROUTER_LANES), 1).astype(F32)
    r_i = lax.broadcasted_iota(jnp.int32, (RANK_BLOCK, RANK_BLOCK), 0)
    c_i = lax.broadcasted_iota(jnp.int32, (RANK_BLOCK, RANK_BLOCK), 1)
    strict_tril = jnp.where(r_i > c_i, 1.0, 0.0).astype(BF16)
    carry = jnp.zeros((1, ROUTER_LANES), F32)
    ranks = []
    for b in range(tb // RANK_BLOCK):
        blk = route_ref[b * RANK_BLOCK:(b + 1) * RANK_BLOCK, :]
        onehot = jnp.where((lane == blk[:, R_E1:R_E1 + 1]) | (lane == blk[:, R_E2:R_E2 + 1]), 1.0, 0.0)
        ranks.append(_dot(strict_tril, onehot.astype(BF16)) + carry)
        carry = carry + jnp.sum(onehot, axis=0, keepdims=True)
    cnt = carry
    seg = jnp.floor((cnt + (SEG_ALIGN - 1)) * (1.0 / SEG_ALIGN)) * SEG_ALIGN
    u_r = lax.broadcasted_iota(jnp.int32, (ROUTER_LANES, ROUTER_LANES), 0)
    u_c = lax.broadcasted_iota(jnp.int32, (ROUTER_LANES, ROUTER_LANES), 1)
    strict_triu = jnp.where(u_r < u_c, 1.0, 0.0).astype(BF16)
    seg_parts = _split3(jnp.broadcast_to(seg, (8, ROUTER_LANES)))
    off = (_dot(seg_parts[0], strict_triu) + _dot(seg_parts[1], strict_triu)
           + _dot(seg_parts[2], strict_triu))[0:1, :]
    for b in range(tb // RANK_BLOCK):
        blk = route_ref[b * RANK_BLOCK:(b + 1) * RANK_BLOCK, :]
        dest = off + ranks[b]
        p1 = jnp.sum(jnp.where(lane == blk[:, R_E1:R_E1 + 1], dest, 0.0), axis=-1, keepdims=True)
        p2 = jnp.sum(jnp.where(lane == blk[:, R_E2:R_E2 + 1], dest, 0.0), axis=-1, keepdims=True)
        rec = jnp.where(lane == S_POS1, p1, jnp.where(lane == S_POS2, p2,
                        jnp.where((lane == S_W1) | (lane == S_W2), blk, 0.0)))
        for h in range(RANK_BLOCK // LANES):
            t0 = b * RANK_BLOCK + h * LANES
            slots_ref[:, t0:t0 + LANES] = rec[h * LANES:(h + 1) * LANES, :].T[0:8, :]
    row8 = lax.broadcasted_iota(jnp.int32, (8, ROUTER_LANES), 0)
    meta_ref[...] = jnp.where(row8 == M_OFF, off, jnp.where(row8 == M_CNT, cnt, 0.0)).astype(jnp.int32)


def _rank(route):
    t = route.shape[0]
    nc = t // MOE_CHUNK
    return pl.pallas_call(
        _rank_kernel,
        grid=(nc,),
        in_specs=[pl.BlockSpec((MOE_CHUNK, ROUTER_LANES), lambda c: (c, 0))],
        out_specs=[pl.BlockSpec((8, MOE_CHUNK), lambda c: (0, c)),
                   pl.BlockSpec((None, 8, ROUTER_LANES), lambda c: (c, 0, 0))],
        out_shape=[jax.ShapeDtypeStruct((8, t), F32),
                   jax.ShapeDtypeStruct((nc, 8, ROUTER_LANES), jnp.int32)],
        compiler_params=_cparams(("arbitrary",)),
        name="slot_rank",
    )(route)


def _moe_kernel(pos1_ref, pos2_ref, w1_ref, w2_ref, off_ref, cnt_ref,
                hp_ref, wg_ref, wu_ref, wd_ref, y_ref, inv_ref, xt_ref, ys_ref):
    c = pl.program_id(0)
    e = pl.program_id(1)
    rt = ROW_TILES
    tb = hp_ref.shape[0] // rt
    tok0 = c * tb
    seg0 = c * N_EXPERTS

    @pl.when((c == 0) & (e == 0))
    def _():
        xt_ref[...] = jnp.zeros_like(xt_ref)

    @pl.when(e == 0)
    def _():
        def pads(ee, carry):
            o = off_ref[seg0 + ee]
            n_e = cnt_ref[seg0 + ee]

            def zero(i, carry2):
                inv_ref[o + i] = 0
                return carry2
            lax.fori_loop(n_e, (n_e + SEG_ALIGN - 1) // SEG_ALIGN * SEG_ALIGN, zero, 0)
            return carry
        lax.fori_loop(0, N_EXPERTS, pads, 0)
        n_last = cnt_ref[seg0 + N_EXPERTS - 1]
        total = off_ref[seg0 + N_EXPERTS - 1] + (n_last + SEG_ALIGN - 1) // SEG_ALIGN * SEG_ALIGN

        def zero_tail(i, carry):
            inv_ref[total + i] = 0
            return carry
        lax.fori_loop(0, MOE_TILE, zero_tail, 0)

        def place(i, carry):
            inv_ref[pos1_ref[tok0 + i]] = i
            inv_ref[pos2_ref[tok0 + i]] = i
            return carry
        lax.fori_loop(0, tb, place, 0, unroll=8)

    expert_steps = N_EXPERTS // EXPERTS_PER_STEP
    is_expert = e < expert_steps
    for k in range(EXPERTS_PER_STEP):
        e_k = jnp.minimum(e, expert_steps - 1) * EXPERTS_PER_STEP + k
        n = jnp.where(is_expert, cnt_ref[seg0 + e_k], 0)
        base = off_ref[seg0 + e_k]

        @pl.when(n > 0)
        def _(k=k, n=n, base=base):
            wg, wu, wd = wg_ref.at[k], wu_ref.at[k], wd_ref.at[k]

            def tile(i, carry):
                start = pl.multiple_of(base + i * MOE_TILE, SEG_ALIGN)

                def gather(r, carry2):
                    src = pl.multiple_of(inv_ref[start + r] * rt, rt)
                    xt_ref[pl.ds(pl.multiple_of(r * rt, rt), rt), :] = hp_ref[pl.ds(src, rt), :]
                    return carry2
                rows = jnp.minimum(MOE_TILE, (n - i * MOE_TILE + SEG_ALIGN - 1) // SEG_ALIGN * SEG_ALIGN)
                def gather_group(g, carry2):
                    for u in range(SEG_ALIGN):
                        gather(g * SEG_ALIGN + u, carry2)
                    return carry2
                lax.fori_loop(0, rows // SEG_ALIGN, gather_group, 0)
                xrow = jnp.concatenate([xt_ref[pl.ds(j, MOE_TILE, stride=rt), :] for j in range(rt)], axis=1)
                xb = xrow.astype(BF16)
                hid = (_silu(_dot(xb, wg[...])) * _dot(xb, wu[...])).astype(BF16)
                out = _dot(hid, wd[...])
                for j in range(rt):
                    ys_ref[pl.ds(start * rt + j, MOE_TILE, stride=rt), :] = out[:, j * LANES:(j + 1) * LANES]
                return carry
            lax.fori_loop(0, (n + MOE_TILE - 1) // MOE_TILE, tile, 0)

    @pl.when(e >= expert_steps)
    def _():
        tq = y_ref.shape[0] // rt
        t0 = tok0 + (e - expert_steps) * tq

        def combine(i, carry):
            r1 = ys_ref[pl.ds(pl.multiple_of(pos1_ref[t0 + i] * rt, rt), rt), :]
            r2 = ys_ref[pl.ds(pl.multiple_of(pos2_ref[t0 + i] * rt, rt), rt), :]
            y_ref[pl.ds(pl.multiple_of(i * rt, rt), rt), :] = w1_ref[t0 + i] * r1 + w2_ref[t0 + i] * r2
            return carry
        lax.fori_loop(0, tq, combine, 0, unroll=8)


def _moe(hp, pos1, pos2, w1, w2, off, cnt, w_gate, w_up, w_down):
    rt = ROW_TILES
    t = hp.shape[0] // rt
    ne, d, f = w_gate.shape
    nc = t // MOE_CHUNK
    tq = MOE_CHUNK // COMBINE_STEPS

    es = ne // EXPERTS_PER_STEP

    def w_idx(c, e, *_):
        return (jnp.where(e < es, e, 0), 0, 0)

    grid_spec = pltpu.PrefetchScalarGridSpec(
        num_scalar_prefetch=6,
        grid=(nc, es + COMBINE_STEPS),
        in_specs=[
            pl.BlockSpec((MOE_CHUNK * rt, LANES), lambda c, e, *_: (c, 0)),
            pl.BlockSpec((EXPERTS_PER_STEP, d, f), w_idx),
            pl.BlockSpec((EXPERTS_PER_STEP, d, f), w_idx),
            pl.BlockSpec((EXPERTS_PER_STEP, f, d), w_idx),
        ],
        out_specs=pl.BlockSpec((tq * rt, LANES),
                               lambda c, e, *_: (c * COMBINE_STEPS + jnp.maximum(e - es, 0), 0)),
        scratch_shapes=[
            pltpu.SMEM((SLOTS_PAD + MOE_TILE,), jnp.int32),
            pltpu.VMEM((MOE_TILE * rt, LANES), F32),
            pltpu.VMEM(((SLOTS_PAD + MOE_TILE) * rt, LANES), F32),
        ],
    )
    return pl.pallas_call(
        _moe_kernel,
        grid_spec=grid_spec,
        out_shape=jax.ShapeDtypeStruct((t * rt, LANES), F32),
        compiler_params=_cparams(("arbitrary", "arbitrary")),
        name="moe_top2",
    )(pos1, pos2, w1, w2, off, cnt, hp, w_gate, w_up, w_down)


def _final_kernel(x_ref, y_ref, mod_ref, g_ref, o_ref):
    x = x_ref[...] + mod_ref[G_FFN:G_FFN + 1, :] * _load_row_tiled(y_ref)
    o_ref[...] = _rms(x) * g_ref[...]


def _final(x2, y, mod_l, g, seq, tm=512):
    t, d = x2.shape
    per_b = seq // tm
    return pl.pallas_call(
        _final_kernel,
        grid=(t // tm,),
        in_specs=[
            pl.BlockSpec((tm, d), lambda i: (i, 0)),
            pl.BlockSpec((tm * ROW_TILES, LANES), lambda i: (i, 0)),
            pl.BlockSpec((None, MOD_ROWS, d), lambda i: (i // per_b, 0, 0)),
            pl.BlockSpec((1, d), lambda i: (0, 0)),
        ],
        out_specs=pl.BlockSpec((tm, d), lambda i: (i, 0)),
        out_shape=jax.ShapeDtypeStruct((t, d), F32),
        compiler_params=_cparams(("arbitrary",)),
        name="final_norm",
    )(x2, y, mod_l, g)


def _retention_tables(seq):
    half = HEAD_DIM // 2
    inv_freq = ROPE_BASE ** (-jnp.arange(half, dtype=F32) / half)
    ang = jnp.arange(seq, dtype=F32)[:, None] * inv_freq[None, :]
    cos, sin = jnp.cos(ang), jnp.sin(ang)
    cos_t = jnp.tile(jnp.concatenate([cos, cos], axis=-1), (1, RET_HEADS))
    sin_t = jnp.tile(jnp.concatenate([-sin, sin], axis=-1), (1, RET_HEADS))
    log_gamma = jnp.log1p(-jnp.exp2(-5.0 - jnp.arange(RET_HEADS, dtype=F32)))
    tt = jnp.arange(CHUNK, dtype=F32)
    diff = tt[:, None] - tt[None, :]
    rdec = jnp.where(diff >= 0, jnp.exp(jnp.maximum(diff, 0.0)[None] * log_gamma[:, None, None]), 0.0)
    lg_lane = jnp.repeat(log_gamma, HEAD_DIM)[None, :]
    rdq = jnp.exp((tt[:, None] + 1.0) * lg_lane)
    rdk = jnp.exp((CHUNK - 1.0 - tt[:, None]) * lg_lane)
    rtot = jnp.exp(CHUNK * lg_lane)
    return cos_t, sin_t, rdec, rdq, rdk, rtot


def _pad_lanes(v, width=LANES):
    return jnp.pad(v, ((0, 0), (0, width - v.shape[-1])))


def kernel(x, c, norm_mix_g, norm_ffn_g, final_norm_g, w_ada, b_ada, w_in, conv_w, conv_b, ssd_dt_bias, ssd_a_log, ssd_d, ssd_norm_g, hgrn_lower_bounds, hgrn_norm_g, w_out, w_grp, b_grp, w_exp, b_exp, w_gate, w_up, w_down):
    batch, seq, d = x.shape
    depth = w_in.shape[0]
    t = batch * seq

    c8 = jnp.pad(c, ((0, MOD_ROWS - batch), (0, 0)))
    mod = _ada(c8, w_ada, b_ada)
    mod = mod[:, :batch].reshape(depth, batch, 6, d)
    mod = jnp.pad(mod, ((0, 0), (0, 0), (0, MOD_ROWS - 6), (0, 0)))

    dt0 = SSD_WIDTH + XBC_WIDTH
    w_in_p = jnp.concatenate(
        [w_in[:, :, :dt0], w_in[:, :, dt0 + SSD_HEADS:], w_in[:, :, dt0:dt0 + SSD_HEADS],
         jnp.zeros((depth, d, N_PROJ - w_in.shape[2]), w_in.dtype)], axis=-1).astype(BF16)
    w_out_b = w_out.astype(BF16)
    w_router = _pad_lanes(jnp.concatenate([w_grp, w_exp], axis=-1).reshape(depth * d, -1)).reshape(depth, d, ROUTER_LANES)
    w_router_hi = w_router.astype(BF16)
    w_router = jnp.stack([w_router_hi, (w_router - w_router_hi.astype(F32)).astype(BF16)], axis=1)
    b_router = _pad_lanes(jnp.concatenate([b_grp, b_exp], axis=-1))[:, None, :]
    dtb = _pad_lanes(ssd_dt_bias)[:, None, :]
    alog = _pad_lanes(ssd_a_log)[:, None, :]
    dexp = jnp.repeat(ssd_d, HEAD_DIM, axis=-1)[:, None, :]
    tables = _retention_tables(seq)

    x2 = x.reshape(t, d)
    y = None
    for layer in range(depth):
        x2, mixed, expert_w = _mixer(x2, y, mod[layer - 1] if layer else None, mod[layer],
                           norm_mix_g[layer][None, :], w_in_p, (w_gate, w_up, w_down), layer, batch, seq,
                           conv_w[layer], conv_b[layer][None, :], dtb[layer],
                           alog[layer], dexp[layer], ssd_norm_g[layer][None, :], hgrn_lower_bounds,
                           hgrn_norm_g[layer][None, :], *tables)
        x2, hp, route = _outproj(x2, mixed, mod[layer], norm_ffn_g[layer][None, :], w_out_b,
                                 w_router, b_router[layer], seq, layer)
        slots, meta = _rank(route)
        y = _moe(hp, slots[S_POS1].astype(jnp.int32), slots[S_POS2].astype(jnp.int32), slots[S_W1], slots[S_W2],
                 meta[:, M_OFF, :N_EXPERTS].reshape(-1), meta[:, M_CNT, :N_EXPERTS].reshape(-1),
                 *expert_w)
    out = _final(x2, y, mod[depth - 1], final_norm_g[None, :], seq)
    return out.reshape(batch, seq, d)
```

```python
import functools
import math

import jax
import jax.numpy as jnp
from jax import lax
from jax.experimental import pallas as pl
from jax.experimental.pallas import tpu as pltpu

F32 = jnp.float32
BF16 = jnp.bfloat16

EPS = 1e-6
GATE_FLOOR = 1e-30
ROPE_BASE = 10000.0

HEAD_DIM = 64
SSD_HEADS = 8
SSD_GROUPS = 2
SSD_CONV = 4
HGRN_HEADS = 4
RET_HEADS = 4
N_GROUPS = 4
EXPERTS_PER_GROUP = 8
N_EXPERTS = N_GROUPS * EXPERTS_PER_GROUP
CHUNK = 128
LANES = 128
ROUTER_ROWS = 48
ROUTE_ROWS = 8

SH_MIX, SC_MIX, G_MIX, SH_FFN, SC_FFN, G_FFN = range(6)
MOD_ROWS = 8

VMEM_LIMIT = 56 * 1024 * 1024


def _cparams(sem):
    return pltpu.CompilerParams(dimension_semantics=sem, vmem_limit_bytes=VMEM_LIMIT)


def _silu(x):
    return x * jax.nn.sigmoid(x)


def _rms(x):
    return x * lax.rsqrt(jnp.mean(x * x, axis=-1, keepdims=True) + EPS)


def _split3(x):
    hi = x.astype(BF16)
    r = x - hi.astype(F32)
    mid = r.astype(BF16)
    lo = (r - mid.astype(F32)).astype(BF16)
    return hi, mid, lo


def _dot(a, b):
    return jnp.dot(a, b, preferred_element_type=F32)


def _dot_nt(a, b):
    return lax.dot_general(a, b, (((1,), (1,)), ((), ())), preferred_element_type=F32)


def _dot_tn(a, b):
    return lax.dot_general(a, b, (((0,), (0,)), ((), ())), preferred_element_type=F32)


def _ada_kernel(c_ref, w_ref, b_ref, o_ref):
    c = _silu(c_ref[...])
    w = w_ref[...]
    c_hi, w_hi = c.astype(BF16), w.astype(BF16)
    c_lo = (c - c_hi.astype(F32)).astype(BF16)
    w_lo = (w - w_hi.astype(F32)).astype(BF16)
    o_ref[...] = (_dot(c_hi, w_hi) + _dot(c_lo, w_hi) + _dot(c_hi, w_lo)) + b_ref[...]


def _ada(c8, w_ada, b_ada):
    depth, d, d6 = w_ada.shape
    nb = d6 // d
    return pl.pallas_call(
        _ada_kernel,
        grid=(depth, nb),
        in_specs=[
            pl.BlockSpec((MOD_ROWS, d), lambda l, n: (0, 0)),
            pl.BlockSpec((None, d, d), lambda l, n: (l, 0, n)),
            pl.BlockSpec((None, 1, d), lambda l, n: (l, 0, n)),
        ],
        out_specs=pl.BlockSpec((None, MOD_ROWS, d), lambda l, n: (l, 0, n)),
        out_shape=jax.ShapeDtypeStruct((depth, MOD_ROWS, d6), F32),
        compiler_params=_cparams(("arbitrary", "arbitrary")),
        name="ada_mod",
    )(c8, w_ada, b_ada.reshape(depth, 1, d6))


OFF_Z, OFF_XBC, OFF_HQ, OFF_HF, OFF_HI, OFF_HG = 0, 512, 1280, 1536, 1792, 2048
OFF_RQ, OFF_RK, OFF_RV, OFF_RG, OFF_DT = 2304, 2560, 2816, 3072, 3328
N_PROJ = 3456
SSD_WIDTH = SSD_HEADS * HEAD_DIM
XBC_WIDTH = SSD_WIDTH + 2 * SSD_GROUPS * HEAD_DIM
HGRN_WIDTH = HGRN_HEADS * HEAD_DIM
RET_WIDTH = RET_HEADS * HEAD_DIM
TAIL = 8
HGRN_DIRECT = 8
HGRN_LEVELS = (8, 16, 32, 64)


def _pair_blockdiag(v_pair, lo_lane):
    return jnp.concatenate([jnp.where(lo_lane, v_pair, 0.0), jnp.where(lo_lane, 0.0, v_pair)], axis=0)


def _head_mean_square(o, bd):
    c, n = o.shape[0], o.shape[1] // LANES
    sq = (o * o).astype(BF16)
    ms = _dot(jnp.concatenate([sq[:, p * LANES:(p + 1) * LANES] for p in range(n)], axis=0), bd)
    return jnp.concatenate([ms[p * c:(p + 1) * c] for p in range(n)], axis=1)


def _pair_scores(q_pair, k_pair_b, lo_lane):
    c = q_pair.shape[0]
    q2 = jnp.concatenate([jnp.where(lo_lane, q_pair, 0.0), jnp.where(lo_lane, 0.0, q_pair)], axis=0)
    s = _dot_nt(q2.astype(BF16), k_pair_b)
    return s[:c], s[c:]


N_MIXER_PARAMS = 14
N_EXPERT_ARRAYS = 3


def _mixer_kernel(layer, has_res, *refs):
    if has_res:
        x_ref, y_ref, modp_ref = refs[:3]
        refs = refs[3:]
    else:
        x_ref = refs[0]
        refs = refs[1:]
    mod_ref, g_ref, w_ref = refs[:3]
    expert_w = refs[3:3 + N_EXPERT_ARRAYS]
    rest = refs[3 + N_EXPERT_ARRAYS:]
    params, outs = rest[:N_MIXER_PARAMS], rest[N_MIXER_PARAMS:]
    if has_res:
        xo_ref = outs[0]
        outs = outs[1:]
    out_ref = outs[0]
    expert_w_bf16 = outs[1:1 + N_EXPERT_ARRAYS]
    proj_next, proj_cur, xbuf, s_ssd, s_hg, s_ret = outs[1 + N_EXPERT_ARRAYS:]
    j = pl.program_id(0)
    nb = x_ref.shape[0]

    @pl.when(j == 0)
    def _():
        xbuf[:, 0:TAIL, :] = jnp.zeros((nb, TAIL, XBC_WIDTH), F32)
        s_ssd[...] = jnp.zeros_like(s_ssd)
        s_hg[...] = jnp.zeros_like(s_hg)
        s_ret[...] = jnp.zeros_like(s_ret)

    for src, dst in zip(expert_w, expert_w_bf16):
        dst[...] = src[...].astype(BF16)

    hs = []
    for b in range(nb):
        x = x_ref[b]
        if has_res:
            x = x + modp_ref[b, G_FFN:G_FFN + 1, :] * _load_row_tiled(y_ref.at[b])
            xo_ref[b] = x
        hs.append(_rms(x) * g_ref[...] * (1.0 + mod_ref[b, SC_MIX:SC_MIX + 1, :]) + mod_ref[b, SH_MIX:SH_MIX + 1, :])
    proj = _dot(jnp.concatenate(hs, axis=0).astype(BF16), w_ref[...])
    for b in range(nb):
        proj_next[b] = proj[b * CHUNK:(b + 1) * CHUNK]

    @pl.when(j > 0)
    def _():
        stages = [_mixer_body(layer, proj_cur.at[b], *params, out_ref.at[b], xbuf.at[b], s_ssd.at[b],
                              s_hg.at[b], s_ret.at[b]) for b in range(nb)]
        while stages:
            for gen in list(stages):
                if next(gen, StopIteration) is StopIteration:
                    stages.remove(gen)

    proj_cur[...] = proj_next[...]


def _mixer_body(layer, proj_ref, convw_ref, convb_ref, dtb_ref, alog_ref, dexp_ref, ssdg_ref,
                lbraw_ref, hgg_ref, cos_ref, sin_ref, rdec_ref, rdq_ref, rdk_ref, rtot_ref,
                out_ref, xbuf, s_ssd, s_hg, s_ret):
    C = CHUNK

    row = lax.broadcasted_iota(jnp.int32, (C, LANES), 0)
    lane = lax.broadcasted_iota(jnp.int32, (C, LANES), 1)
    lo_lane = lane < HEAD_DIM
    causal = row >= lane
    bd_mask = jnp.right_shift(row, 6) == jnp.right_shift(lane, 6)
    bd_mean = jnp.where(bd_mask, 1.0 / HEAD_DIM, 0.0).astype(BF16)
    tril = jnp.where(causal, 1.0, 0.0).astype(BF16)

    def cumsum_rows(x):
        hi, mid, lo = _split3(x)
        return _dot(tril, hi) + _dot(tril, mid) + _dot(tril, lo)

    xbuf[TAIL:TAIL + C, :] = proj_ref[:, OFF_XBC:OFF_XBC + XBC_WIDTH]
    conv = convb_ref[...]
    for jj in range(SSD_CONV):
        off = TAIL - (SSD_CONV - 1) + jj
        conv = conv + convw_ref[jj:jj + 1, :] * xbuf[off:off + C, :]
    xbuf[0:TAIL, :] = xbuf[C:C + TAIL, :]
    xc = _silu(conv)
    xs = xc[:, 0:SSD_WIDTH]
    bm = xc[:, SSD_WIDTH:SSD_WIDTH + LANES]
    cm = xc[:, SSD_WIDTH + LANES:SSD_WIDTH + 2 * LANES]

    dt8 = jax.nn.softplus(proj_ref[:, OFF_DT:OFF_DT + LANES] + dtb_ref[...])
    la8 = dt8 * (-jnp.exp(alog_ref[...]))
    cs8 = cumsum_rows(la8)
    cs8t = cs8.T

    bm_b = bm.astype(BF16)
    cm_b = cm.astype(BF16)
    scores_g = _pair_scores(cm, bm_b, lo_lane)
    yield

    o_intra, e1_cols, e2_cols, dt_cols = [], [], [], []
    for p in range(SSD_HEADS // 2):
        a, b = 2 * p, 2 * p + 1
        g = a // (SSD_HEADS // SSD_GROUPS)
        col_a = jnp.broadcast_to(cs8[:, a:a + 1], (C, LANES))
        col_b = jnp.broadcast_to(cs8[:, b:b + 1], (C, LANES))
        cs_pair = jnp.where(lo_lane, col_a, col_b)
        dt_pair = jnp.where(lo_lane, jnp.broadcast_to(dt8[:, a:a + 1], (C, LANES)),
                            jnp.broadcast_to(dt8[:, b:b + 1], (C, LANES)))
        last = cs_pair[C - 1:C, :]
        e1_cols.append(jnp.exp(cs_pair))
        e2_cols.append(jnp.exp(last - cs_pair))
        dt_cols.append(dt_pair)
        dec_a = jnp.where(causal, jnp.exp(jnp.minimum(col_a - cs8t[a:a + 1, :], 0.0)), 0.0)
        dec_b = jnp.where(causal, jnp.exp(jnp.minimum(col_b - cs8t[b:b + 1, :], 0.0)), 0.0)
        pa = (scores_g[g] * dec_a).astype(BF16)
        pb = (scores_g[g] * dec_b).astype(BF16)
        v_pair = xs[:, p * LANES:(p + 1) * LANES] * dt_pair
        o_intra.append(_dot(jnp.concatenate([pa, pb], axis=1),
                            _pair_blockdiag(v_pair, lo_lane).astype(BF16)))
        yield
    o_intra = jnp.concatenate(o_intra, axis=1)
    e1 = jnp.concatenate(e1_cols, axis=1)
    e2 = jnp.concatenate(e2_cols, axis=1)
    dtx = jnp.concatenate(dt_cols, axis=1)
    v_all = xs * dtx
    s_prev = s_ssd[...]
    o_inter = e1 * _dot(cm_b, s_prev.astype(BF16))
    y = o_intra + o_inter + dexp_ref[...] * xs
    y = y * _silu(proj_ref[:, OFF_Z:OFF_Z + SSD_WIDTH])
    gw = SSD_WIDTH // SSD_GROUPS
    grp_mean = jnp.full((gw, gw), 1.0 / gw, BF16)
    ysq = (y * y).astype(BF16)
    ms = jnp.concatenate([_dot(ysq[:, g * gw:(g + 1) * gw], grp_mean) for g in range(SSD_GROUPS)], axis=1)
    out_ref[:, 0:SSD_WIDTH] = (y * lax.rsqrt(ms + EPS) * ssdg_ref[...]).astype(out_ref.dtype)
    row_s = lax.broadcasted_iota(jnp.int32, (LANES, SSD_WIDTH), 0)
    lane_s = lax.broadcasted_iota(jnp.int32, (LANES, SSD_WIDTH), 1)
    grp_mask = jnp.right_shift(row_s, 6) == jnp.right_shift(lane_s, 8)
    upd = _dot_tn(bm_b, (v_all * e2).astype(BF16))
    s_ssd[...] = e1[C - 1:C, :] * s_prev + jnp.where(grp_mask, upd, 0.0)
    yield

    W2 = HGRN_WIDTH
    row2 = lax.broadcasted_iota(jnp.int32, (C, W2), 0)
    lb_raw = lbraw_ref[...]
    lb_e = jnp.exp(lb_raw - jnp.max(lb_raw, axis=0, keepdims=True))
    lb_soft = lb_e / jnp.sum(lb_e, axis=0, keepdims=True)
    lb = jnp.sum(lb_soft[0:layer + 1, :], axis=0, keepdims=True) - lb_soft[0:1, :]
    forget = lb + (1.0 - lb) * jax.nn.sigmoid(proj_ref[:, OFF_HF:OFF_HF + W2])
    fc = jnp.maximum(forget, GATE_FLOOR)
    kk = 1.0 - forget
    hq = proj_ref[:, OFF_HQ:OFF_HQ + W2]
    hv = proj_ref[:, OFF_HI:OFF_HI + W2]
    cum = cumsum_rows(jnp.log(fc))
    npair = W2 // LANES

    sub = jnp.bitwise_and(row2, HGRN_DIRECT - 1)
    bd2 = jnp.where(jnp.right_shift(lax.broadcasted_iota(jnp.int32, (W2, W2), 0), 6)
                    == jnp.right_shift(lax.broadcasted_iota(jnp.int32, (W2, W2), 1), 6), 1.0, 0.0).astype(BF16)
    def prev_row(a):
        return pltpu.roll(a.reshape(C // HGRN_DIRECT, HGRN_DIRECT, W2), 1, 1).reshape(C, W2)

    not_first = sub != 0
    kdec = kk
    vprev = hv
    terms, vals = [(hq * kk).astype(BF16)], [hv]
    for dlt in range(1, HGRN_DIRECT):
        kdec = jnp.where(not_first, fc * prev_row(kdec), 0.0)
        vprev = prev_row(vprev)
        terms.append((hq * kdec).astype(BF16))
        vals.append(vprev)
    sc_all = _dot(jnp.concatenate(terms, axis=0), bd2)
    o_h = sc_all[0:C] * vals[0]
    for dlt in range(1, HGRN_DIRECT):
        o_h = o_h + sc_all[dlt * C:(dlt + 1) * C] * vals[dlt]
    yield
    p_sum = [[None, None] for _ in range(npair)]
    for m in HGRN_LEVELS:
        nb = C // (2 * m)
        cum3 = cum.reshape(nb, 2 * m, W2)
        ref = jnp.broadcast_to(cum3[:, m - 1:m, :], (nb, 2 * m, W2)).reshape(C, W2)
        right = jnp.bitwise_and(row2, m) != 0
        e = jnp.exp(jnp.where(right, cum - ref, ref - cum))
        qm = hq * e
        km = (kk * e).astype(BF16)
        sh = int(math.log2(2 * m))
        same_block = ((jnp.right_shift(row, sh) == jnp.right_shift(lane, sh))
                      & (jnp.bitwise_and(row, m) != 0) & (jnp.bitwise_and(lane, m) == 0))
        for p in range(npair):
            sl = slice(p * LANES, (p + 1) * LANES)
            for hh, s_h in enumerate(_pair_scores(qm[:, sl], km[:, sl], lo_lane)):
                sc = jnp.where(same_block, s_h, 0.0)
                p_sum[p][hh] = sc if p_sum[p][hh] is None else p_sum[p][hh] + sc
        yield
    cols = []
    for p in range(npair):
        vbd = _pair_blockdiag(hv[:, p * LANES:(p + 1) * LANES], lo_lane).astype(BF16)
        cols.append(_dot(jnp.concatenate([p_sum[p][0].astype(BF16), p_sum[p][1].astype(BF16)], axis=1), vbd))
    o_h = o_h + jnp.concatenate(cols, axis=1)
    q_in = (hq * jnp.exp(cum)).astype(BF16)
    last2 = cum[C - 1:C, :]
    k_end = (kk * jnp.exp(last2 - cum)).astype(BF16)
    tot2 = jnp.exp(last2)
    hv_b = hv.astype(BF16)
    inter = []
    for p in range(npair):
        sl = slice(p * LANES, (p + 1) * LANES)
        st = s_hg[p]
        inter.append(_dot_nt(q_in[:, sl], st.astype(BF16)))
        upd = _dot_tn(hv_b[:, sl], k_end[:, sl])
        s_hg[p] = st * tot2[:, sl] + jnp.where(bd_mask, upd, 0.0)
    o_h = o_h + jnp.concatenate(inter, axis=1)
    ms = _head_mean_square(o_h, bd_mean)
    o_h = o_h * lax.rsqrt(ms + EPS) * hgg_ref[...]
    o_h = o_h * jax.nn.sigmoid(proj_ref[:, OFF_HG:OFF_HG + W2])
    out_ref[:, SSD_WIDTH:SSD_WIDTH + W2] = o_h.astype(out_ref.dtype)
    yield

    W3 = RET_WIDTH
    lane3 = lax.broadcasted_iota(jnp.int32, (C, W3), 1)
    first_half = jnp.bitwise_and(lane3, HEAD_DIM // 2) == 0
    cosv = cos_ref[...]
    sinv = sin_ref[...]

    def rotary(xr):
        swapped = jnp.where(first_half, pltpu.roll(xr, W3 - HEAD_DIM // 2, 1),
                            pltpu.roll(xr, HEAD_DIM // 2, 1))
        return xr * cosv + swapped * sinv

    qr = rotary(proj_ref[:, OFF_RQ:OFF_RQ + W3])
    kr = rotary(proj_ref[:, OFF_RK:OFF_RK + W3]) * (HEAD_DIM ** -0.5)
    rv = proj_ref[:, OFF_RV:OFF_RV + W3]
    kr_b = kr.astype(BF16)
    q_in = (qr * rdq_ref[...]).astype(BF16)
    k_end = (kr * rdk_ref[...]).astype(BF16)
    rv_b = rv.astype(BF16)
    o_cols = []
    for p in range(W3 // LANES):
        sl = slice(p * LANES, (p + 1) * LANES)
        sa, sb = _pair_scores(qr[:, sl], kr_b[:, sl], lo_lane)
        sa = sa * rdec_ref[2 * p]
        sb = sb * rdec_ref[2 * p + 1]
        vbd = _pair_blockdiag(rv[:, sl], lo_lane).astype(BF16)
        o_p = _dot(jnp.concatenate([sa.astype(BF16), sb.astype(BF16)], axis=1), vbd)
        st = s_ret[p]
        o_p = o_p + _dot(q_in[:, sl], st.astype(BF16))
        upd = _dot_tn(k_end[:, sl], rv_b[:, sl])
        s_ret[p] = st * rtot_ref[:, sl] + jnp.where(bd_mask, upd, 0.0)
        o_cols.append(o_p)
        yield
    o_r = jnp.concatenate(o_cols, axis=1)
    ms = _head_mean_square(o_r, bd_mean)
    o_r = o_r * lax.rsqrt(ms + EPS) * _silu(proj_ref[:, OFF_RG:OFF_RG + W3])
    out_ref[:, SSD_WIDTH + W2:SSD_WIDTH + W2 + W3] = o_r.astype(out_ref.dtype)


def _mixer(x2, y, mod_prev, mod_l, g, w, expert_w, layer, batch, seq, conv_w, conv_b, dtb, alog, dexp, ssdg,
           lbraw, hgg, cos_t, sin_t, rdec, rdq, rdk, rtot):
    t, d = x2.shape
    d_mix = SSD_WIDTH + HGRN_WIDTH + RET_WIDTH
    nj = seq // CHUNK
    const2 = lambda j: (0, 0)
    has_res = y is not None
    proj_idx = lambda j: (0, jnp.minimum(j, nj - 1), 0)
    mix_idx = lambda j: (0, jnp.maximum(j - 1, 0), 0)
    x_spec = pl.BlockSpec((batch, CHUNK, d), proj_idx)
    mod_spec = pl.BlockSpec((batch, MOD_ROWS, d), lambda j: (0, 0, 0))
    head_specs = [x_spec]
    head_args = [x2.reshape(batch, seq, d)]
    if has_res:
        head_specs += [pl.BlockSpec((batch, CHUNK * ROW_TILES, LANES), proj_idx), mod_spec]
        head_args += [y.reshape(batch, seq * ROW_TILES, LANES), mod_prev]
    head_specs += [mod_spec, pl.BlockSpec((1, d), const2),
                   pl.BlockSpec((None,) + w.shape[1:], lambda j: (layer, 0, 0))]
    head_args += [mod_l, g, w]
    cast_specs, cast_shapes = [], []
    for ew in expert_w:
        depth_e, ne, ra, cb = ew.shape
        rows = ne * ra
        assert rows % nj == 0
        blk = rows // nj
        head_specs.append(pl.BlockSpec((None, blk, cb), lambda j: (layer, jnp.minimum(j, nj - 1), 0)))
        head_args.append(ew.reshape(depth_e, rows, cb))
        cast_specs.append(pl.BlockSpec((blk, cb), lambda j: (jnp.minimum(j, nj - 1), 0)))
        cast_shapes.append(jax.ShapeDtypeStruct((rows, cb), BF16))
    table_spec = pl.BlockSpec((CHUNK, RET_WIDTH), lambda j: (jnp.maximum(j - 1, 0), 0))
    mixed_spec = pl.BlockSpec((batch, CHUNK, d_mix), mix_idx)
    mixed_shape = jax.ShapeDtypeStruct((batch, seq, d_mix), BF16)
    res = pl.pallas_call(
        functools.partial(_mixer_kernel, layer, has_res),
        grid=(nj + 1,),
        in_specs=head_specs + [
            pl.BlockSpec(conv_w.shape, const2),
            pl.BlockSpec(conv_b.shape, const2),
            pl.BlockSpec(dtb.shape, const2),
            pl.BlockSpec(alog.shape, const2),
            pl.BlockSpec(dexp.shape, const2),
            pl.BlockSpec(ssdg.shape, const2),
            pl.BlockSpec(lbraw.shape, const2),
            pl.BlockSpec(hgg.shape, const2),
            table_spec,
            table_spec,
            pl.BlockSpec(rdec.shape, lambda j: (0, 0, 0)),
            pl.BlockSpec(rdq.shape, const2),
            pl.BlockSpec(rdk.shape, const2),
            pl.BlockSpec(rtot.shape, const2),
        ],
        out_specs=([x_spec] if has_res else []) + [mixed_spec] + cast_specs,
        out_shape=([jax.ShapeDtypeStruct((batch, seq, d), F32)] if has_res else []) + [mixed_shape] + cast_shapes,
        scratch_shapes=[
            pltpu.VMEM((batch, CHUNK, N_PROJ), F32),
            pltpu.VMEM((batch, CHUNK, N_PROJ), F32),
            pltpu.VMEM((batch, TAIL + CHUNK, XBC_WIDTH), F32),
            pltpu.VMEM((batch, LANES, SSD_WIDTH), F32),
            pltpu.VMEM((batch, HGRN_WIDTH // LANES, LANES, LANES), F32),
            pltpu.VMEM((batch, RET_WIDTH // LANES, LANES, LANES), F32),
        ],
        compiler_params=_cparams(("arbitrary",)),
        name="norm_inproj_mixers",
    )(*head_args, conv_w, conv_b, dtb, alog, dexp, ssdg, lbraw, hgg, cos_t, sin_t, rdec, rdq, rdk, rtot)
    if has_res:
        x2, res = res[0].reshape(t, d), res[1:]
    casts = [c.reshape(ew.shape[1:]) for c, ew in zip(res[1:], expert_w)]
    return x2, res[0].reshape(t, d_mix), casts


ROW_TILES = 8


def _store_row_tiled(ref, val):
    m = val.shape[0]
    for k in range(ROW_TILES):
        ref[pl.ds(k, m, stride=ROW_TILES), :] = val[:, k * LANES:(k + 1) * LANES]


def _load_row_tiled(ref):
    m = ref.shape[0] // ROW_TILES
    return jnp.concatenate([ref[pl.ds(k, m, stride=ROW_TILES), :] for k in range(ROW_TILES)], axis=1)


R_E1, R_E2, R_W1, R_W2 = range(4)


def _outproj_kernel(x_ref, mixed_ref, mod_ref, g_ref, w_ref, wr_ref, br_ref,
                    xo_ref, hp_ref, route_ref):
    x = x_ref[...] + mod_ref[G_MIX:G_MIX + 1, :] * _dot(mixed_ref[...], w_ref[...])
    xo_ref[...] = x
    h = _rms(x) * g_ref[...] * (1.0 + mod_ref[SC_FFN:SC_FFN + 1, :]) + mod_ref[SH_FFN:SH_FFN + 1, :]
    _store_row_tiled(hp_ref, h)
    h_hi = h.astype(BF16)
    h_lo = (h - h_hi.astype(F32)).astype(BF16)
    logits = (_dot_nt(wr_ref[0], h_hi) + _dot_nt(wr_ref[0], h_lo) + _dot_nt(wr_ref[1], h_hi)) + br_ref[...]
    rowi = lax.broadcasted_iota(jnp.int32, logits.shape, 0).astype(F32)
    neg = -jnp.inf
    big = float(ROUTER_ROWS)
    is_grp = rowi < N_GROUPS
    gl = jnp.where(is_grp, logits, neg)
    gmax = jnp.max(gl, axis=0, keepdims=True)
    g_idx = jnp.min(jnp.where(gl == gmax, rowi, big), axis=0, keepdims=True)
    p_grp = 1.0 / jnp.sum(jnp.where(is_grp, jnp.exp(gl - gmax), 0.0), axis=0, keepdims=True)
    first = N_GROUPS + g_idx * EXPERTS_PER_GROUP
    valid = (rowi >= first) & (rowi < first + EXPERTS_PER_GROUP)
    el = jnp.where(valid, logits, neg)
    m1 = jnp.max(el, axis=0, keepdims=True)
    i1 = jnp.min(jnp.where(el == m1, rowi, big), axis=0, keepdims=True)
    el2 = jnp.where(rowi == i1, neg, el)
    m2 = jnp.max(el2, axis=0, keepdims=True)
    i2 = jnp.min(jnp.where(el2 == m2, rowi, big), axis=0, keepdims=True)
    e = jnp.exp(m2 - m1)
    w1 = p_grp / (1.0 + e)
    w2 = p_grp * e / (1.0 + e)
    r8 = lax.broadcasted_iota(jnp.int32, route_ref.shape, 0)
    route_ref[...] = jnp.where(r8 == R_E1, i1 - N_GROUPS,
                               jnp.where(r8 == R_E2, i2 - N_GROUPS,
                                         jnp.where(r8 == R_W1, w1, jnp.where(r8 == R_W2, w2, 0.0))))


def _outproj(x2, mixed, mod_l, g, w, wr, br, seq, layer, tm=512):
    t, d = x2.shape
    per_b = seq // tm
    return pl.pallas_call(
        _outproj_kernel,
        grid=(t // tm,),
        in_specs=[
            pl.BlockSpec((tm, d), lambda i: (i, 0)),
            pl.BlockSpec((tm, mixed.shape[1]), lambda i: (i, 0)),
            pl.BlockSpec((None, MOD_ROWS, d), lambda i: (i // per_b, 0, 0)),
            pl.BlockSpec((1, d), lambda i: (0, 0)),
            pl.BlockSpec((None,) + w.shape[1:], lambda i: (layer, 0, 0)),
            pl.BlockSpec((None,) + wr.shape[1:], lambda i: (layer, 0, 0, 0)),
            pl.BlockSpec((None,) + br.shape[1:], lambda i: (layer, 0, 0)),
        ],
        out_specs=[
            pl.BlockSpec((tm, d), lambda i: (i, 0)),
            pl.BlockSpec((tm * ROW_TILES, LANES), lambda i: (i, 0)),
            pl.BlockSpec((ROUTE_ROWS, tm), lambda i: (0, i)),
        ],
        out_shape=[
            jax.ShapeDtypeStruct((t, d), F32),
            jax.ShapeDtypeStruct((t * ROW_TILES, LANES), F32),
            jax.ShapeDtypeStruct((ROUTE_ROWS, t), F32),
        ],
        compiler_params=_cparams(("arbitrary",)),
        name="outproj_router",
    )(x2, mixed, mod_l, g, w, wr, br)


MOE_CHUNK = 2048
MOE_TILE = 160
SEG_ALIGN = 8
COMBINE_STEPS = 4
EXPERTS_PER_STEP = 4
SLOTS_PAD = 2 * MOE_CHUNK + N_EXPERTS * SEG_ALIGN
RANK_BLOCK = 256
M_OFF, M_CNT = 0, 1
S_POS1, S_POS2, S_W1, S_W2 = 0, 1, R_W1, R_W2


def _rank_kernel(route_ref, slots_ref, meta_ref):
    tb = route_ref.shape[1]
    ne = N_EXPERTS
    rowe = lax.broadcasted_iota(jnp.int32, (ne, RANK_BLOCK), 0).astype(F32)
    r_i = lax.broadcasted_iota(jnp.int32, (RANK_BLOCK, RANK_BLOCK), 0)
    c_i = lax.broadcasted_iota(jnp.int32, (RANK_BLOCK, RANK_BLOCK), 1)
    earlier = jnp.where(r_i < c_i, 1.0, 0.0).astype(BF16)
    carry = jnp.zeros((ne, 1), F32)
    ranks = []
    for b in range(tb // RANK_BLOCK):
        blk = route_ref[:, b * RANK_BLOCK:(b + 1) * RANK_BLOCK]
        onehot = jnp.where((rowe == blk[R_E1:R_E1 + 1, :]) | (rowe == blk[R_E2:R_E2 + 1, :]), 1.0, 0.0)
        ranks.append(_dot(onehot.astype(BF16), earlier) + carry)
        carry = carry + jnp.sum(onehot, axis=1, keepdims=True)
    cnt = jnp.broadcast_to(carry, (ne, LANES))
    seg = jnp.floor((cnt + (SEG_ALIGN - 1)) * (1.0 / SEG_ALIGN)) * SEG_ALIGN
    e_r = lax.broadcasted_iota(jnp.int32, (ne, ne), 0)
    e_c = lax.broadcasted_iota(jnp.int32, (ne, ne), 1)
    before = jnp.where(e_c < e_r, 1.0, 0.0).astype(BF16)
    seg_parts = _split3(seg)
    off = _dot(before, seg_parts[0]) + _dot(before, seg_parts[1]) + _dot(before, seg_parts[2])
    off_col = off[:, 0:1]
    r8 = lax.broadcasted_iota(jnp.int32, (ROUTE_ROWS, RANK_BLOCK), 0)
    for b in range(tb // RANK_BLOCK):
        blk = route_ref[:, b * RANK_BLOCK:(b + 1) * RANK_BLOCK]
        dest = off_col + ranks[b]
        p1 = jnp.sum(jnp.where(rowe == blk[R_E1:R_E1 + 1, :], dest, 0.0), axis=0, keepdims=True)
        p2 = jnp.sum(jnp.where(rowe == blk[R_E2:R_E2 + 1, :], dest, 0.0), axis=0, keepdims=True)
        slots_ref[:, b * RANK_BLOCK:(b + 1) * RANK_BLOCK] = jnp.where(
            r8 == S_POS1, p1, jnp.where(r8 == S_POS2, p2, jnp.where((r8 == S_W1) | (r8 == S_W2), blk, 0.0)))
    meta_ref[M_OFF] = off.astype(jnp.int32)
    meta_ref[M_CNT] = cnt.astype(jnp.int32)


def _rank(route):
    t = route.shape[1]
    nc = t // MOE_CHUNK
    return pl.pallas_call(
        _rank_kernel,
        grid=(nc,),
        in_specs=[pl.BlockSpec((ROUTE_ROWS, MOE_CHUNK), lambda c: (0, c))],
        out_specs=[pl.BlockSpec((ROUTE_ROWS, MOE_CHUNK), lambda c: (0, c)),
                   pl.BlockSpec((None, 2, N_EXPERTS, LANES), lambda c: (c, 0, 0, 0))],
        out_shape=[jax.ShapeDtypeStruct((ROUTE_ROWS, t), F32),
                   jax.ShapeDtypeStruct((nc, 2, N_EXPERTS, LANES), jnp.int32)],
        compiler_params=_cparams(("arbitrary",)),
        name="slot_rank",
    )(route)


def _moe_kernel(pos1_ref, pos2_ref, w1_ref, w2_ref, off_ref, cnt_ref,
                hp_ref, wg_ref, wu_ref, wd_ref, y_ref, inv_ref, xt_ref, ys_ref):
    c = pl.program_id(0)
    e = pl.program_id(1)
    rt = ROW_TILES
    tb = hp_ref.shape[0] // rt
    tok0 = c * tb
    seg0 = c * N_EXPERTS

    @pl.when((c == 0) & (e == 0))
    def _():
        xt_ref[...] = jnp.zeros_like(xt_ref)

    @pl.when(e == 0)
    def _():
        def pads(ee, carry):
            o = off_ref[seg0 + ee]
            n_e = cnt_ref[seg0 + ee]

            def zero(i, carry2):
                inv_ref[o + i] = 0
                return carry2
            lax.fori_loop(n_e, (n_e + SEG_ALIGN - 1) // SEG_ALIGN * SEG_ALIGN, zero, 0)
            return carry
        lax.fori_loop(0, N_EXPERTS, pads, 0)
        n_last = cnt_ref[seg0 + N_EXPERTS - 1]
        total = off_ref[seg0 + N_EXPERTS - 1] + (n_last + SEG_ALIGN - 1) // SEG_ALIGN * SEG_ALIGN

        def zero_tail(i, carry):
            inv_ref[total + i] = 0
            return carry
        lax.fori_loop(0, MOE_TILE, zero_tail, 0)

        def place(i, row):
            inv_ref[pos1_ref[tok0 + i]] = row
            inv_ref[pos2_ref[tok0 + i]] = row
            return row + rt
        lax.fori_loop(0, tb, place, 0, unroll=8)

    expert_steps = N_EXPERTS // EXPERTS_PER_STEP
    is_expert = e < expert_steps
    for k in range(EXPERTS_PER_STEP):
        e_k = jnp.minimum(e, expert_steps - 1) * EXPERTS_PER_STEP + k
        n = jnp.where(is_expert, cnt_ref[seg0 + e_k], 0)
        base = off_ref[seg0 + e_k]

        @pl.when(n > 0)
        def _(k=k, n=n, base=base):
            wg, wu, wd = wg_ref.at[k], wu_ref.at[k], wd_ref.at[k]

            def tile(i, carry):
                start = pl.multiple_of(base + i * MOE_TILE, SEG_ALIGN)

                rows = jnp.minimum(MOE_TILE, (n - i * MOE_TILE + SEG_ALIGN - 1) // SEG_ALIGN * SEG_ALIGN)

                def gather_group(g, carry2):
                    slot0 = start + g * SEG_ALIGN
                    dst0 = pl.multiple_of(g * (SEG_ALIGN * rt), SEG_ALIGN * rt)
                    for u in range(SEG_ALIGN):
                        src = pl.multiple_of(inv_ref[slot0 + u], rt)
                        xt_ref[pl.ds(dst0 + u * rt, rt), :] = hp_ref[pl.ds(src, rt), :]
                    return carry2
                lax.fori_loop(0, rows // SEG_ALIGN, gather_group, 0)
                xrow = jnp.concatenate([xt_ref[pl.ds(j, MOE_TILE, stride=rt), :] for j in range(rt)], axis=1)
                xb = xrow.astype(BF16)
                hid = (_silu(_dot(xb, wg[...])) * _dot(xb, wu[...])).astype(BF16)
                out = _dot(hid, wd[...])
                for j in range(rt):
                    ys_ref[pl.ds(start * rt + j, MOE_TILE, stride=rt), :] = out[:, j * LANES:(j + 1) * LANES]
                return carry
            lax.fori_loop(0, (n + MOE_TILE - 1) // MOE_TILE, tile, 0)

    @pl.when(e >= expert_steps)
    def _():
        tq = y_ref.shape[0] // rt
        t0 = tok0 + (e - expert_steps) * tq

        def combine(i, carry):
            r1 = ys_ref[pl.ds(pl.multiple_of(pos1_ref[t0 + i] * rt, rt), rt), :]
            r2 = ys_ref[pl.ds(pl.multiple_of(pos2_ref[t0 + i] * rt, rt), rt), :]
            y_ref[pl.ds(pl.multiple_of(i * rt, rt), rt), :] = w1_ref[t0 + i] * r1 + w2_ref[t0 + i] * r2
            return carry
        lax.fori_loop(0, tq, combine, 0, unroll=8)


def _moe(hp, pos1, pos2, w1, w2, off, cnt, w_gate, w_up, w_down):
    rt = ROW_TILES
    t = hp.shape[0] // rt
    ne, d, f = w_gate.shape
    nc = t // MOE_CHUNK
    tq = MOE_CHUNK // COMBINE_STEPS

    es = ne // EXPERTS_PER_STEP

    def w_idx(c, e, *_):
        return (jnp.where(e < es, e, 0), 0, 0)

    grid_spec = pltpu.PrefetchScalarGridSpec(
        num_scalar_prefetch=6,
        grid=(nc, es + COMBINE_STEPS),
        in_specs=[
            pl.BlockSpec((MOE_CHUNK * rt, LANES), lambda c, e, *_: (c, 0)),
            pl.BlockSpec((EXPERTS_PER_STEP, d, f), w_idx),
            pl.BlockSpec((EXPERTS_PER_STEP, d, f), w_idx),
            pl.BlockSpec((EXPERTS_PER_STEP, f, d), w_idx),
        ],
        out_specs=pl.BlockSpec((tq * rt, LANES),
                               lambda c, e, *_: (c * COMBINE_STEPS + jnp.maximum(e - es, 0), 0)),
        scratch_shapes=[
            pltpu.SMEM((SLOTS_PAD + MOE_TILE,), jnp.int32),
            pltpu.VMEM((MOE_TILE * rt, LANES), F32),
            pltpu.VMEM(((SLOTS_PAD + MOE_TILE) * rt, LANES), F32),
        ],
    )
    return pl.pallas_call(
        _moe_kernel,
        grid_spec=grid_spec,
        out_shape=jax.ShapeDtypeStruct((t * rt, LANES), F32),
        compiler_params=_cparams(("arbitrary", "arbitrary")),
        name="moe_top2",
    )(pos1, pos2, w1, w2, off, cnt, hp, w_gate, w_up, w_down)


def _final_kernel(x_ref, y_ref, mod_ref, g_ref, o_ref):
    x = x_ref[...] + mod_ref[G_FFN:G_FFN + 1, :] * _load_row_tiled(y_ref)
    o_ref[...] = _rms(x) * g_ref[...]


def _final(x2, y, mod_l, g, seq, tm=512):
    t, d = x2.shape
    per_b = seq // tm
    return pl.pallas_call(
        _final_kernel,
        grid=(t // tm,),
        in_specs=[
            pl.BlockSpec((tm, d), lambda i: (i, 0)),
            pl.BlockSpec((tm * ROW_TILES, LANES), lambda i: (i, 0)),
            pl.BlockSpec((None, MOD_ROWS, d), lambda i: (i // per_b, 0, 0)),
            pl.BlockSpec((1, d), lambda i: (0, 0)),
        ],
        out_specs=pl.BlockSpec((tm, d), lambda i: (i, 0)),
        out_shape=jax.ShapeDtypeStruct((t, d), F32),
        compiler_params=_cparams(("arbitrary",)),
        name="final_norm",
    )(x2, y, mod_l, g)


def _retention_tables(seq):
    half = HEAD_DIM // 2
    inv_freq = ROPE_BASE ** (-jnp.arange(half, dtype=F32) / half)
    ang = jnp.arange(seq, dtype=F32)[:, None] * inv_freq[None, :]
    cos, sin = jnp.cos(ang), jnp.sin(ang)
    cos_t = jnp.tile(jnp.concatenate([cos, cos], axis=-1), (1, RET_HEADS))
    sin_t = jnp.tile(jnp.concatenate([-sin, sin], axis=-1), (1, RET_HEADS))
    log_gamma = jnp.log1p(-jnp.exp2(-5.0 - jnp.arange(RET_HEADS, dtype=F32)))
    tt = jnp.arange(CHUNK, dtype=F32)
    diff = tt[:, None] - tt[None, :]
    rdec = jnp.where(diff >= 0, jnp.exp(jnp.maximum(diff, 0.0)[None] * log_gamma[:, None, None]), 0.0)
    lg_lane = jnp.repeat(log_gamma, HEAD_DIM)[None, :]
    rdq = jnp.exp((tt[:, None] + 1.0) * lg_lane)
    rdk = jnp.exp((CHUNK - 1.0 - tt[:, None]) * lg_lane)
    rtot = jnp.exp(CHUNK * lg_lane)
    return cos_t, sin_t, rdec, rdq, rdk, rtot


def _pad_lanes(v, width=LANES):
    return jnp.pad(v, ((0, 0), (0, width - v.shape[-1])))


def kernel(x, c, norm_mix_g, norm_ffn_g, final_norm_g, w_ada, b_ada, w_in, conv_w, conv_b, ssd_dt_bias, ssd_a_log, ssd_d, ssd_norm_g, hgrn_lower_bounds, hgrn_norm_g, w_out, w_grp, b_grp, w_exp, b_exp, w_gate, w_up, w_down):
    batch, seq, d = x.shape
    depth = w_in.shape[0]
    t = batch * seq

    c8 = jnp.pad(c, ((0, MOD_ROWS - batch), (0, 0)))
    mod = _ada(c8, w_ada, b_ada)
    mod = mod[:, :batch].reshape(depth, batch, 6, d)
    mod = jnp.pad(mod, ((0, 0), (0, 0), (0, MOD_ROWS - 6), (0, 0)))

    dt0 = SSD_WIDTH + XBC_WIDTH
    w_in_p = jnp.concatenate(
        [w_in[:, :, :dt0], w_in[:, :, dt0 + SSD_HEADS:], w_in[:, :, dt0:dt0 + SSD_HEADS],
         jnp.zeros((depth, d, N_PROJ - w_in.shape[2]), w_in.dtype)], axis=-1).astype(BF16)
    w_out_b = w_out.astype(BF16)
    n_router = N_GROUPS + N_EXPERTS
    w_router = jnp.pad(jnp.swapaxes(jnp.concatenate([w_grp, w_exp], axis=-1), 1, 2),
                       ((0, 0), (0, ROUTER_ROWS - n_router), (0, 0)))
    w_router_hi = w_router.astype(BF16)
    w_router = jnp.stack([w_router_hi, (w_router - w_router_hi.astype(F32)).astype(BF16)], axis=1)
    b_router = jnp.pad(jnp.concatenate([b_grp, b_exp], axis=-1), ((0, 0), (0, ROUTER_ROWS - n_router)))[:, :, None]
    dtb = _pad_lanes(ssd_dt_bias)[:, None, :]
    alog = _pad_lanes(ssd_a_log)[:, None, :]
    dexp = jnp.repeat(ssd_d, HEAD_DIM, axis=-1)[:, None, :]
    tables = _retention_tables(seq)

    x2 = x.reshape(t, d)
    y = None
    for layer in range(depth):
        x2, mixed, expert_w = _mixer(x2, y, mod[layer - 1] if layer else None, mod[layer],
                           norm_mix_g[layer][None, :], w_in_p, (w_gate, w_up, w_down), layer, batch, seq,
                           conv_w[layer], conv_b[layer][None, :], dtb[layer],
                           alog[layer], dexp[layer], ssd_norm_g[layer][None, :], hgrn_lower_bounds,
                           hgrn_norm_g[layer][None, :], *tables)
        x2, hp, route = _outproj(x2, mixed, mod[layer], norm_ffn_g[layer][None, :], w_out_b,
                                 w_router, b_router, seq, layer)
        slots, meta = _rank(route)
        y = _moe(hp, slots[S_POS1].astype(jnp.int32), slots[S_POS2].astype(jnp.int32), slots[S_W1], slots[S_W2],
                 meta[:, M_OFF, :, 0].reshape(-1), meta[:, M_CNT, :, 0].reshape(-1),
                 *expert_w)
    out = _final(x2, y, mod[depth - 1], final_norm_g[None, :], seq)
    return out.reshape(batch, seq, d)
```

```python
import functools
import math

import jax
import jax.numpy as jnp
from jax import lax
from jax.experimental import pallas as pl
from jax.experimental.pallas import tpu as pltpu

F32 = jnp.float32
BF16 = jnp.bfloat16

EPS = 1e-6
GATE_FLOOR = 1e-30
ROPE_BASE = 10000.0

HEAD_DIM = 64
SSD_HEADS = 8
SSD_GROUPS = 2
SSD_CONV = 4
HGRN_HEADS = 4
RET_HEADS = 4
N_GROUPS = 4
EXPERTS_PER_GROUP = 8
N_EXPERTS = N_GROUPS * EXPERTS_PER_GROUP
CHUNK = 128
LANES = 128
ROUTER_ROWS = 48
ROUTE_ROWS = 8

SH_MIX, SC_MIX, G_MIX, SH_FFN, SC_FFN, G_FFN = range(6)
MOD_ROWS = 8

VMEM_LIMIT = 56 * 1024 * 1024


def _cparams(sem):
    return pltpu.CompilerParams(dimension_semantics=sem, vmem_limit_bytes=VMEM_LIMIT)


def _silu(x):
    return x * jax.nn.sigmoid(x)


def _rms(x):
    return x * lax.rsqrt(jnp.mean(x * x, axis=-1, keepdims=True) + EPS)


def _split3(x):
    hi = x.astype(BF16)
    r = x - hi.astype(F32)
    mid = r.astype(BF16)
    lo = (r - mid.astype(F32)).astype(BF16)
    return hi, mid, lo


def _dot(a, b):
    return jnp.dot(a, b, preferred_element_type=F32)


def _dot_nt(a, b):
    return lax.dot_general(a, b, (((1,), (1,)), ((), ())), preferred_element_type=F32)


def _dot_tn(a, b):
    return lax.dot_general(a, b, (((0,), (0,)), ((), ())), preferred_element_type=F32)


def _ada_kernel(c_ref, w_ref, b_ref, o_ref):
    c = _silu(c_ref[...])
    w = w_ref[...]
    c_hi, w_hi = c.astype(BF16), w.astype(BF16)
    c_lo = (c - c_hi.astype(F32)).astype(BF16)
    w_lo = (w - w_hi.astype(F32)).astype(BF16)
    o_ref[...] = (_dot(c_hi, w_hi) + _dot(c_lo, w_hi) + _dot(c_hi, w_lo)) + b_ref[...]


def _ada(c8, w_ada, b_ada):
    depth, d, d6 = w_ada.shape
    nb = d6 // d
    return pl.pallas_call(
        _ada_kernel,
        grid=(depth, nb),
        in_specs=[
            pl.BlockSpec((MOD_ROWS, d), lambda l, n: (0, 0)),
            pl.BlockSpec((None, d, d), lambda l, n: (l, 0, n)),
            pl.BlockSpec((None, 1, d), lambda l, n: (l, 0, n)),
        ],
        out_specs=pl.BlockSpec((None, MOD_ROWS, d), lambda l, n: (l, 0, n)),
        out_shape=jax.ShapeDtypeStruct((depth, MOD_ROWS, d6), F32),
        compiler_params=_cparams(("arbitrary", "arbitrary")),
        name="ada_mod",
    )(c8, w_ada, b_ada.reshape(depth, 1, d6))


OFF_Z, OFF_XBC, OFF_HQ, OFF_HF, OFF_HI, OFF_HG = 0, 512, 1280, 1536, 1792, 2048
OFF_RQ, OFF_RK, OFF_RV, OFF_RG, OFF_DT = 2304, 2560, 2816, 3072, 3328
N_PROJ = 3456
SSD_WIDTH = SSD_HEADS * HEAD_DIM
XBC_WIDTH = SSD_WIDTH + 2 * SSD_GROUPS * HEAD_DIM
HGRN_WIDTH = HGRN_HEADS * HEAD_DIM
RET_WIDTH = RET_HEADS * HEAD_DIM
TAIL = 8
HGRN_DIRECT = 8
HGRN_LEVELS = (8, 16, 32, 64)


def _pair_blockdiag(v_pair, lo_lane):
    return jnp.concatenate([jnp.where(lo_lane, v_pair, 0.0), jnp.where(lo_lane, 0.0, v_pair)], axis=0)


def _head_mean_square(o, bd):
    c, n = o.shape[0], o.shape[1] // LANES
    sq = (o * o).astype(BF16)
    ms = _dot(jnp.concatenate([sq[:, p * LANES:(p + 1) * LANES] for p in range(n)], axis=0), bd)
    return jnp.concatenate([ms[p * c:(p + 1) * c] for p in range(n)], axis=1)


def _pair_scores(q_pair, k_pair_b, lo_lane):
    c = q_pair.shape[0]
    q2 = jnp.concatenate([jnp.where(lo_lane, q_pair, 0.0), jnp.where(lo_lane, 0.0, q_pair)], axis=0)
    s = _dot_nt(q2.astype(BF16), k_pair_b)
    return s[:c], s[c:]


N_MIXER_PARAMS = 14
N_EXPERT_ARRAYS = 3


def _mixer_kernel(layer, has_res, *refs):
    if has_res:
        x_ref, y_ref, modp_ref = refs[:3]
        refs = refs[3:]
    else:
        x_ref = refs[0]
        refs = refs[1:]
    mod_ref, g_ref, w_ref = refs[:3]
    expert_w = refs[3:3 + N_EXPERT_ARRAYS]
    rest = refs[3 + N_EXPERT_ARRAYS:]
    params, outs = rest[:N_MIXER_PARAMS], rest[N_MIXER_PARAMS:]
    if has_res:
        xo_ref = outs[0]
        outs = outs[1:]
    out_ref = outs[0]
    expert_w_bf16 = outs[1:1 + N_EXPERT_ARRAYS]
    proj_next, proj_cur, xbuf, s_ssd, s_hg, s_ret = outs[1 + N_EXPERT_ARRAYS:]
    j = pl.program_id(0)
    nb = x_ref.shape[0]

    @pl.when(j == 0)
    def _():
        xbuf[:, 0:TAIL, :] = jnp.zeros((nb, TAIL, XBC_WIDTH), F32)
        s_ssd[...] = jnp.zeros_like(s_ssd)
        s_hg[...] = jnp.zeros_like(s_hg)
        s_ret[...] = jnp.zeros_like(s_ret)

    for src, dst in zip(expert_w, expert_w_bf16):
        dst[...] = src[...].astype(BF16)

    hs = []
    for b in range(nb):
        x = x_ref[b]
        if has_res:
            x = x + modp_ref[b, G_FFN:G_FFN + 1, :] * _load_row_tiled(y_ref.at[b])
            xo_ref[b] = x
        hs.append(_rms(x) * g_ref[...] * (1.0 + mod_ref[b, SC_MIX:SC_MIX + 1, :]) + mod_ref[b, SH_MIX:SH_MIX + 1, :])
    proj = _dot(jnp.concatenate(hs, axis=0).astype(BF16), w_ref[...])
    for b in range(nb):
        proj_next[b] = proj[b * CHUNK:(b + 1) * CHUNK]

    @pl.when(j > 0)
    def _():
        stages = [_mixer_body(layer, proj_cur.at[b], *params, out_ref.at[b], xbuf.at[b], s_ssd.at[b],
                              s_hg.at[b], s_ret.at[b]) for b in range(nb)]
        while stages:
            for gen in list(stages):
                if next(gen, StopIteration) is StopIteration:
                    stages.remove(gen)

    proj_cur[...] = proj_next[...]


def _mixer_body(layer, proj_ref, convw_ref, convb_ref, dtb_ref, alog_ref, dexp_ref, ssdg_ref,
                lbraw_ref, hgg_ref, cos_ref, sin_ref, rdec_ref, rdq_ref, rdk_ref, rtot_ref,
                out_ref, xbuf, s_ssd, s_hg, s_ret):
    C = CHUNK

    row = lax.broadcasted_iota(jnp.int32, (C, LANES), 0)
    lane = lax.broadcasted_iota(jnp.int32, (C, LANES), 1)
    lo_lane = lane < HEAD_DIM
    causal = row >= lane
    bd_mask = jnp.right_shift(row, 6) == jnp.right_shift(lane, 6)
    bd_mean = jnp.where(bd_mask, 1.0 / HEAD_DIM, 0.0).astype(BF16)
    tril = jnp.where(causal, 1.0, 0.0).astype(BF16)

    def cumsum_rows(x):
        hi, mid, lo = _split3(x)
        return _dot(tril, hi) + _dot(tril, mid) + _dot(tril, lo)

    xbuf[TAIL:TAIL + C, :] = proj_ref[:, OFF_XBC:OFF_XBC + XBC_WIDTH]
    conv = convb_ref[...]
    for jj in range(SSD_CONV):
        off = TAIL - (SSD_CONV - 1) + jj
        conv = conv + convw_ref[jj:jj + 1, :] * xbuf[off:off + C, :]
    xbuf[0:TAIL, :] = xbuf[C:C + TAIL, :]
    xc = _silu(conv)
    xs = xc[:, 0:SSD_WIDTH]
    bm = xc[:, SSD_WIDTH:SSD_WIDTH + LANES]
    cm = xc[:, SSD_WIDTH + LANES:SSD_WIDTH + 2 * LANES]

    dt8 = jax.nn.softplus(proj_ref[:, OFF_DT:OFF_DT + LANES] + dtb_ref[...])
    la8 = dt8 * (-jnp.exp(alog_ref[...]))
    cs8 = cumsum_rows(la8)
    cs8t = cs8.T

    bm_b = bm.astype(BF16)
    cm_b = cm.astype(BF16)
    scores_g = _pair_scores(cm, bm_b, lo_lane)
    yield

    o_intra, e1_cols, e2_cols, dt_cols = [], [], [], []
    for p in range(SSD_HEADS // 2):
        a, b = 2 * p, 2 * p + 1
        g = a // (SSD_HEADS // SSD_GROUPS)
        col_a = jnp.broadcast_to(cs8[:, a:a + 1], (C, LANES))
        col_b = jnp.broadcast_to(cs8[:, b:b + 1], (C, LANES))
        cs_pair = jnp.where(lo_lane, col_a, col_b)
        dt_pair = jnp.where(lo_lane, jnp.broadcast_to(dt8[:, a:a + 1], (C, LANES)),
                            jnp.broadcast_to(dt8[:, b:b + 1], (C, LANES)))
        last = cs_pair[C - 1:C, :]
        e1_cols.append(jnp.exp(cs_pair))
        e2_cols.append(jnp.exp(last - cs_pair))
        dt_cols.append(dt_pair)
        dec_a = jnp.where(causal, jnp.exp(jnp.minimum(col_a - cs8t[a:a + 1, :], 0.0)), 0.0)
        dec_b = jnp.where(causal, jnp.exp(jnp.minimum(col_b - cs8t[b:b + 1, :], 0.0)), 0.0)
        pa = (scores_g[g] * dec_a).astype(BF16)
        pb = (scores_g[g] * dec_b).astype(BF16)
        v_pair = xs[:, p * LANES:(p + 1) * LANES] * dt_pair
        o_intra.append(_dot(jnp.concatenate([pa, pb], axis=1),
                            _pair_blockdiag(v_pair, lo_lane).astype(BF16)))
        yield
    o_intra = jnp.concatenate(o_intra, axis=1)
    e1 = jnp.concatenate(e1_cols, axis=1)
    e2 = jnp.concatenate(e2_cols, axis=1)
    dtx = jnp.concatenate(dt_cols, axis=1)
    v_all = xs * dtx
    s_prev = s_ssd[...]
    o_inter = e1 * _dot(cm_b, s_prev.astype(BF16))
    y = o_intra + o_inter + dexp_ref[...] * xs
    y = y * _silu(proj_ref[:, OFF_Z:OFF_Z + SSD_WIDTH])
    gw = SSD_WIDTH // SSD_GROUPS
    grp_mean = jnp.full((gw, gw), 1.0 / gw, BF16)
    ysq = (y * y).astype(BF16)
    ms = jnp.concatenate([_dot(ysq[:, g * gw:(g + 1) * gw], grp_mean) for g in range(SSD_GROUPS)], axis=1)
    out_ref[:, 0:SSD_WIDTH] = (y * lax.rsqrt(ms + EPS) * ssdg_ref[...]).astype(out_ref.dtype)
    row_s = lax.broadcasted_iota(jnp.int32, (LANES, SSD_WIDTH), 0)
    lane_s = lax.broadcasted_iota(jnp.int32, (LANES, SSD_WIDTH), 1)
    grp_mask = jnp.right_shift(row_s, 6) == jnp.right_shift(lane_s, 8)
    upd = _dot_tn(bm_b, (v_all * e2).astype(BF16))
    s_ssd[...] = e1[C - 1:C, :] * s_prev + jnp.where(grp_mask, upd, 0.0)
    yield

    W2 = HGRN_WIDTH
    row2 = lax.broadcasted_iota(jnp.int32, (C, W2), 0)
    lb_raw = lbraw_ref[...]
    lb_e = jnp.exp(lb_raw - jnp.max(lb_raw, axis=0, keepdims=True))
    lb_soft = lb_e / jnp.sum(lb_e, axis=0, keepdims=True)
    lb = jnp.sum(lb_soft[0:layer + 1, :], axis=0, keepdims=True) - lb_soft[0:1, :]
    forget = lb + (1.0 - lb) * jax.nn.sigmoid(proj_ref[:, OFF_HF:OFF_HF + W2])
    fc = jnp.maximum(forget, GATE_FLOOR)
    kk = 1.0 - forget
    hq = proj_ref[:, OFF_HQ:OFF_HQ + W2]
    hv = proj_ref[:, OFF_HI:OFF_HI + W2]
    cum = cumsum_rows(jnp.log(fc))
    npair = W2 // LANES

    sub = jnp.bitwise_and(row2, HGRN_DIRECT - 1)
    bd2 = jnp.where(jnp.right_shift(lax.broadcasted_iota(jnp.int32, (W2, W2), 0), 6)
                    == jnp.right_shift(lax.broadcasted_iota(jnp.int32, (W2, W2), 1), 6), 1.0, 0.0).astype(BF16)
    def prev_row(a):
        return pltpu.roll(a.reshape(C // HGRN_DIRECT, HGRN_DIRECT, W2), 1, 1).reshape(C, W2)

    not_first = sub != 0
    kdec = kk
    vprev = hv
    terms, vals = [(hq * kk).astype(BF16)], [hv]
    for dlt in range(1, HGRN_DIRECT):
        kdec = jnp.where(not_first, fc * prev_row(kdec), 0.0)
        vprev = prev_row(vprev)
        terms.append((hq * kdec).astype(BF16))
        vals.append(vprev)
    sc_all = _dot(jnp.concatenate(terms, axis=0), bd2)
    o_h = sc_all[0:C] * vals[0]
    for dlt in range(1, HGRN_DIRECT):
        o_h = o_h + sc_all[dlt * C:(dlt + 1) * C] * vals[dlt]
    yield
    p_sum = [[None, None] for _ in range(npair)]
    for m in HGRN_LEVELS:
        nb = C // (2 * m)
        cum3 = cum.reshape(nb, 2 * m, W2)
        ref = jnp.broadcast_to(cum3[:, m - 1:m, :], (nb, 2 * m, W2)).reshape(C, W2)
        right = jnp.bitwise_and(row2, m) != 0
        e = jnp.exp(jnp.where(right, cum - ref, ref - cum))
        qm = hq * e
        km = (kk * e).astype(BF16)
        sh = int(math.log2(2 * m))
        same_block = ((jnp.right_shift(row, sh) == jnp.right_shift(lane, sh))
                      & (jnp.bitwise_and(row, m) != 0) & (jnp.bitwise_and(lane, m) == 0))
        for p in range(npair):
            sl = slice(p * LANES, (p + 1) * LANES)
            for hh, s_h in enumerate(_pair_scores(qm[:, sl], km[:, sl], lo_lane)):
                sc = jnp.where(same_block, s_h, 0.0)
                p_sum[p][hh] = sc if p_sum[p][hh] is None else p_sum[p][hh] + sc
        yield
    cols = []
    for p in range(npair):
        vbd = _pair_blockdiag(hv[:, p * LANES:(p + 1) * LANES], lo_lane).astype(BF16)
        cols.append(_dot(jnp.concatenate([p_sum[p][0].astype(BF16), p_sum[p][1].astype(BF16)], axis=1), vbd))
    o_h = o_h + jnp.concatenate(cols, axis=1)
    q_in = (hq * jnp.exp(cum)).astype(BF16)
    last2 = cum[C - 1:C, :]
    k_end = (kk * jnp.exp(last2 - cum)).astype(BF16)
    tot2 = jnp.exp(last2)
    hv_b = hv.astype(BF16)
    inter = []
    for p in range(npair):
        sl = slice(p * LANES, (p + 1) * LANES)
        st = s_hg[p]
        inter.append(_dot_nt(q_in[:, sl], st.astype(BF16)))
        upd = _dot_tn(hv_b[:, sl], k_end[:, sl])
        s_hg[p] = st * tot2[:, sl] + jnp.where(bd_mask, upd, 0.0)
    o_h = o_h + jnp.concatenate(inter, axis=1)
    ms = _head_mean_square(o_h, bd_mean)
    o_h = o_h * lax.rsqrt(ms + EPS) * hgg_ref[...]
    o_h = o_h * jax.nn.sigmoid(proj_ref[:, OFF_HG:OFF_HG + W2])
    out_ref[:, SSD_WIDTH:SSD_WIDTH + W2] = o_h.astype(out_ref.dtype)
    yield

    W3 = RET_WIDTH
    lane3 = lax.broadcasted_iota(jnp.int32, (C, W3), 1)
    first_half = jnp.bitwise_and(lane3, HEAD_DIM // 2) == 0
    cosv = cos_ref[...]
    sinv = sin_ref[...]

    def rotary(xr):
        swapped = jnp.where(first_half, pltpu.roll(xr, W3 - HEAD_DIM // 2, 1),
                            pltpu.roll(xr, HEAD_DIM // 2, 1))
        return xr * cosv + swapped * sinv

    qr = rotary(proj_ref[:, OFF_RQ:OFF_RQ + W3])
    kr = rotary(proj_ref[:, OFF_RK:OFF_RK + W3]) * (HEAD_DIM ** -0.5)
    rv = proj_ref[:, OFF_RV:OFF_RV + W3]
    kr_b = kr.astype(BF16)
    q_in = (qr * rdq_ref[...]).astype(BF16)
    k_end = (kr * rdk_ref[...]).astype(BF16)
    rv_b = rv.astype(BF16)
    o_cols = []
    for p in range(W3 // LANES):
        sl = slice(p * LANES, (p + 1) * LANES)
        sa, sb = _pair_scores(qr[:, sl], kr_b[:, sl], lo_lane)
        sa = sa * rdec_ref[2 * p]
        sb = sb * rdec_ref[2 * p + 1]
        vbd = _pair_blockdiag(rv[:, sl], lo_lane).astype(BF16)
        o_p = _dot(jnp.concatenate([sa.astype(BF16), sb.astype(BF16)], axis=1), vbd)
        st = s_ret[p]
        o_p = o_p + _dot(q_in[:, sl], st.astype(BF16))
        upd = _dot_tn(k_end[:, sl], rv_b[:, sl])
        s_ret[p] = st * rtot_ref[:, sl] + jnp.where(bd_mask, upd, 0.0)
        o_cols.append(o_p)
        yield
    o_r = jnp.concatenate(o_cols, axis=1)
    ms = _head_mean_square(o_r, bd_mean)
    o_r = o_r * lax.rsqrt(ms + EPS) * _silu(proj_ref[:, OFF_RG:OFF_RG + W3])
    out_ref[:, SSD_WIDTH + W2:SSD_WIDTH + W2 + W3] = o_r.astype(out_ref.dtype)


def _mixer(x2, y, mod_prev, mod_l, g, w, expert_w, layer, batch, seq, conv_w, conv_b, dtb, alog, dexp, ssdg,
           lbraw, hgg, cos_t, sin_t, rdec, rdq, rdk, rtot):
    t, d = x2.shape
    d_mix = SSD_WIDTH + HGRN_WIDTH + RET_WIDTH
    nj = seq // CHUNK
    const2 = lambda j: (0, 0)
    has_res = y is not None
    proj_idx = lambda j: (0, jnp.minimum(j, nj - 1), 0)
    mix_idx = lambda j: (0, jnp.maximum(j - 1, 0), 0)
    x_spec = pl.BlockSpec((batch, CHUNK, d), proj_idx)
    mod_spec = pl.BlockSpec((batch, MOD_ROWS, d), lambda j: (0, 0, 0))
    head_specs = [x_spec]
    head_args = [x2.reshape(batch, seq, d)]
    if has_res:
        head_specs += [pl.BlockSpec((batch, CHUNK * ROW_TILES, LANES), proj_idx), mod_spec]
        head_args += [y.reshape(batch, seq * ROW_TILES, LANES), mod_prev]
    head_specs += [mod_spec, pl.BlockSpec((1, d), const2),
                   pl.BlockSpec((None,) + w.shape[1:], lambda j: (layer, 0, 0))]
    head_args += [mod_l, g, w]
    cast_specs, cast_shapes = [], []
    for ew in expert_w:
        depth_e, ne, ra, cb = ew.shape
        rows = ne * ra
        assert rows % nj == 0
        blk = rows // nj
        head_specs.append(pl.BlockSpec((None, blk, cb), lambda j: (layer, jnp.minimum(j, nj - 1), 0)))
        head_args.append(ew.reshape(depth_e, rows, cb))
        cast_specs.append(pl.BlockSpec((blk, cb), lambda j: (jnp.minimum(j, nj - 1), 0)))
        cast_shapes.append(jax.ShapeDtypeStruct((rows, cb), BF16))
    table_spec = pl.BlockSpec((CHUNK, RET_WIDTH), lambda j: (jnp.maximum(j - 1, 0), 0))
    mixed_spec = pl.BlockSpec((batch, CHUNK, d_mix), mix_idx)
    mixed_shape = jax.ShapeDtypeStruct((batch, seq, d_mix), BF16)
    res = pl.pallas_call(
        functools.partial(_mixer_kernel, layer, has_res),
        grid=(nj + 1,),
        in_specs=head_specs + [
            pl.BlockSpec(conv_w.shape, const2),
            pl.BlockSpec(conv_b.shape, const2),
            pl.BlockSpec(dtb.shape, const2),
            pl.BlockSpec(alog.shape, const2),
            pl.BlockSpec(dexp.shape, const2),
            pl.BlockSpec(ssdg.shape, const2),
            pl.BlockSpec(lbraw.shape, const2),
            pl.BlockSpec(hgg.shape, const2),
            table_spec,
            table_spec,
            pl.BlockSpec(rdec.shape, lambda j: (0, 0, 0)),
            pl.BlockSpec(rdq.shape, const2),
            pl.BlockSpec(rdk.shape, const2),
            pl.BlockSpec(rtot.shape, const2),
        ],
        out_specs=([x_spec] if has_res else []) + [mixed_spec] + cast_specs,
        out_shape=([jax.ShapeDtypeStruct((batch, seq, d), F32)] if has_res else []) + [mixed_shape] + cast_shapes,
        scratch_shapes=[
            pltpu.VMEM((batch, CHUNK, N_PROJ), F32),
            pltpu.VMEM((batch, CHUNK, N_PROJ), F32),
            pltpu.VMEM((batch, TAIL + CHUNK, XBC_WIDTH), F32),
            pltpu.VMEM((batch, LANES, SSD_WIDTH), F32),
            pltpu.VMEM((batch, HGRN_WIDTH // LANES, LANES, LANES), F32),
            pltpu.VMEM((batch, RET_WIDTH // LANES, LANES, LANES), F32),
        ],
        compiler_params=_cparams(("arbitrary",)),
        name="norm_inproj_mixers",
    )(*head_args, conv_w, conv_b, dtb, alog, dexp, ssdg, lbraw, hgg, cos_t, sin_t, rdec, rdq, rdk, rtot)
    if has_res:
        x2, res = res[0].reshape(t, d), res[1:]
    casts = [c.reshape(ew.shape[1:]) for c, ew in zip(res[1:], expert_w)]
    return x2, res[0].reshape(t, d_mix), casts


ROW_TILES = 8


def _store_row_tiled(ref, val):
    m = val.shape[0]
    for k in range(ROW_TILES):
        ref[pl.ds(k, m, stride=ROW_TILES), :] = val[:, k * LANES:(k + 1) * LANES]


def _load_row_tiled(ref):
    m = ref.shape[0] // ROW_TILES
    return jnp.concatenate([ref[pl.ds(k, m, stride=ROW_TILES), :] for k in range(ROW_TILES)], axis=1)


R_E1, R_E2, R_W1, R_W2 = range(4)


def _outproj_kernel(x_ref, mixed_ref, mod_ref, g_ref, w_ref, wr_ref, br_ref,
                    xo_ref, hp_ref, route_ref):
    x = x_ref[...] + mod_ref[G_MIX:G_MIX + 1, :] * _dot(mixed_ref[...], w_ref[...])
    xo_ref[...] = x
    h = _rms(x) * g_ref[...] * (1.0 + mod_ref[SC_FFN:SC_FFN + 1, :]) + mod_ref[SH_FFN:SH_FFN + 1, :]
    _store_row_tiled(hp_ref, h)
    h_hi = h.astype(BF16)
    h_lo = (h - h_hi.astype(F32)).astype(BF16)
    logits = (_dot_nt(wr_ref[0], h_hi) + _dot_nt(wr_ref[0], h_lo) + _dot_nt(wr_ref[1], h_hi)) + br_ref[...]
    rowi = lax.broadcasted_iota(jnp.int32, logits.shape, 0).astype(F32)
    neg = -jnp.inf
    big = float(ROUTER_ROWS)
    is_grp = rowi < N_GROUPS
    gl = jnp.where(is_grp, logits, neg)
    gmax = jnp.max(gl, axis=0, keepdims=True)
    g_idx = jnp.min(jnp.where(gl == gmax, rowi, big), axis=0, keepdims=True)
    p_grp = 1.0 / jnp.sum(jnp.where(is_grp, jnp.exp(gl - gmax), 0.0), axis=0, keepdims=True)
    first = N_GROUPS + g_idx * EXPERTS_PER_GROUP
    valid = (rowi >= first) & (rowi < first + EXPERTS_PER_GROUP)
    el = jnp.where(valid, logits, neg)
    m1 = jnp.max(el, axis=0, keepdims=True)
    i1 = jnp.min(jnp.where(el == m1, rowi, big), axis=0, keepdims=True)
    el2 = jnp.where(rowi == i1, neg, el)
    m2 = jnp.max(el2, axis=0, keepdims=True)
    i2 = jnp.min(jnp.where(el2 == m2, rowi, big), axis=0, keepdims=True)
    e = jnp.exp(m2 - m1)
    w1 = p_grp / (1.0 + e)
    w2 = p_grp * e / (1.0 + e)
    r8 = lax.broadcasted_iota(jnp.int32, route_ref.shape, 0)
    route_ref[...] = jnp.where(r8 == R_E1, i1 - N_GROUPS,
                               jnp.where(r8 == R_E2, i2 - N_GROUPS,
                                         jnp.where(r8 == R_W1, w1, jnp.where(r8 == R_W2, w2, 0.0))))


def _outproj(x2, mixed, mod_l, g, w, wr, br, seq, layer, tm=512):
    t, d = x2.shape
    per_b = seq // tm
    return pl.pallas_call(
        _outproj_kernel,
        grid=(t // tm,),
        in_specs=[
            pl.BlockSpec((tm, d), lambda i: (i, 0)),
            pl.BlockSpec((tm, mixed.shape[1]), lambda i: (i, 0)),
            pl.BlockSpec((None, MOD_ROWS, d), lambda i: (i // per_b, 0, 0)),
            pl.BlockSpec((1, d), lambda i: (0, 0)),
            pl.BlockSpec((None,) + w.shape[1:], lambda i: (layer, 0, 0)),
            pl.BlockSpec((None,) + wr.shape[1:], lambda i: (layer, 0, 0, 0)),
            pl.BlockSpec((None,) + br.shape[1:], lambda i: (layer, 0, 0)),
        ],
        out_specs=[
            pl.BlockSpec((tm, d), lambda i: (i, 0)),
            pl.BlockSpec((tm * ROW_TILES, LANES), lambda i: (i, 0)),
            pl.BlockSpec((ROUTE_ROWS, tm), lambda i: (0, i)),
        ],
        out_shape=[
            jax.ShapeDtypeStruct((t, d), F32),
            jax.ShapeDtypeStruct((t * ROW_TILES, LANES), F32),
            jax.ShapeDtypeStruct((ROUTE_ROWS, t), F32),
        ],
        compiler_params=_cparams(("arbitrary",)),
        name="outproj_router",
    )(x2, mixed, mod_l, g, w, wr, br)


MOE_CHUNK = 2048
MOE_TILE = 160
SEG_ALIGN = 8
COMBINE_STEPS = 4
EXPERTS_PER_STEP = 4
SLOTS_PAD = 2 * MOE_CHUNK + N_EXPERTS * SEG_ALIGN
RANK_BLOCK = 256
M_OFF, M_CNT = 0, 1
S_POS1, S_POS2, S_W1, S_W2 = 0, 1, R_W1, R_W2


def _rank_kernel(route_ref, slots_ref, meta_ref):
    tb = route_ref.shape[1]
    ne = N_EXPERTS
    rowe = lax.broadcasted_iota(jnp.int32, (ne, RANK_BLOCK), 0).astype(F32)
    r_i = lax.broadcasted_iota(jnp.int32, (RANK_BLOCK, RANK_BLOCK), 0)
    c_i = lax.broadcasted_iota(jnp.int32, (RANK_BLOCK, RANK_BLOCK), 1)
    earlier = jnp.where(r_i < c_i, 1.0, 0.0).astype(BF16)
    carry = jnp.zeros((ne, 1), F32)
    ranks = []
    for b in range(tb // RANK_BLOCK):
        blk = route_ref[:, b * RANK_BLOCK:(b + 1) * RANK_BLOCK]
        onehot = jnp.where((rowe == blk[R_E1:R_E1 + 1, :]) | (rowe == blk[R_E2:R_E2 + 1, :]), 1.0, 0.0)
        ranks.append(_dot(onehot.astype(BF16), earlier) + carry)
        carry = carry + jnp.sum(onehot, axis=1, keepdims=True)
    cnt = jnp.broadcast_to(carry, (ne, LANES))
    seg = jnp.floor((cnt + (SEG_ALIGN - 1)) * (1.0 / SEG_ALIGN)) * SEG_ALIGN
    e_r = lax.broadcasted_iota(jnp.int32, (ne, ne), 0)
    e_c = lax.broadcasted_iota(jnp.int32, (ne, ne), 1)
    before = jnp.where(e_c < e_r, 1.0, 0.0).astype(BF16)
    seg_parts = _split3(seg)
    off = _dot(before, seg_parts[0]) + _dot(before, seg_parts[1]) + _dot(before, seg_parts[2])
    off_col = off[:, 0:1]
    r8 = lax.broadcasted_iota(jnp.int32, (ROUTE_ROWS, RANK_BLOCK), 0)
    for b in range(tb // RANK_BLOCK):
        blk = route_ref[:, b * RANK_BLOCK:(b + 1) * RANK_BLOCK]
        dest = off_col + ranks[b]
        p1 = jnp.sum(jnp.where(rowe == blk[R_E1:R_E1 + 1, :], dest, 0.0), axis=0, keepdims=True)
        p2 = jnp.sum(jnp.where(rowe == blk[R_E2:R_E2 + 1, :], dest, 0.0), axis=0, keepdims=True)
        slots_ref[:, b * RANK_BLOCK:(b + 1) * RANK_BLOCK] = jnp.where(
            r8 == S_POS1, p1, jnp.where(r8 == S_POS2, p2, jnp.where((r8 == S_W1) | (r8 == S_W2), blk, 0.0)))
    meta_ref[M_OFF] = off.astype(jnp.int32)
    meta_ref[M_CNT] = cnt.astype(jnp.int32)


def _rank(route):
    t = route.shape[1]
    nc = t // MOE_CHUNK
    return pl.pallas_call(
        _rank_kernel,
        grid=(nc,),
        in_specs=[pl.BlockSpec((ROUTE_ROWS, MOE_CHUNK), lambda c: (0, c))],
        out_specs=[pl.BlockSpec((ROUTE_ROWS, MOE_CHUNK), lambda c: (0, c)),
                   pl.BlockSpec((None, 2, N_EXPERTS, LANES), lambda c: (c, 0, 0, 0))],
        out_shape=[jax.ShapeDtypeStruct((ROUTE_ROWS, t), F32),
                   jax.ShapeDtypeStruct((nc, 2, N_EXPERTS, LANES), jnp.int32)],
        compiler_params=_cparams(("arbitrary",)),
        name="slot_rank",
    )(route)


def _moe_kernel(pos1_ref, pos2_ref, w1_ref, w2_ref, off_ref, cnt_ref,
                hp_ref, wg_ref, wu_ref, wd_ref, y_ref, inv_ref, ys_ref, xt_ref, xt2_ref, xt3_ref):
    c = pl.program_id(0)
    e = pl.program_id(1)
    rt = ROW_TILES
    tb = hp_ref.shape[0] // rt
    tok0 = c * tb
    seg0 = c * N_EXPERTS

    @pl.when((c == 0) & (e == 0))
    def _():
        xt_ref[...] = jnp.zeros_like(xt_ref)
        xt3_ref[...] = jnp.zeros_like(xt3_ref)

    @pl.when(e == 0)
    def _():
        def pads(ee, carry):
            o = off_ref[seg0 + ee]
            n_e = cnt_ref[seg0 + ee]

            def zero(i, carry2):
                inv_ref[o + i] = 0
                return carry2
            lax.fori_loop(n_e, (n_e + SEG_ALIGN - 1) // SEG_ALIGN * SEG_ALIGN, zero, 0)
            return carry
        lax.fori_loop(0, N_EXPERTS, pads, 0)
        n_last = cnt_ref[seg0 + N_EXPERTS - 1]
        total = off_ref[seg0 + N_EXPERTS - 1] + (n_last + SEG_ALIGN - 1) // SEG_ALIGN * SEG_ALIGN

        def zero_tail(i, carry):
            inv_ref[total + i] = 0
            return carry
        lax.fori_loop(0, MOE_TILE, zero_tail, 0)

        def place(i, row):
            inv_ref[pos1_ref[tok0 + i]] = row
            inv_ref[pos2_ref[tok0 + i]] = row
            return row + rt
        lax.fori_loop(0, tb, place, 0, unroll=8)

    expert_steps = N_EXPERTS // EXPERTS_PER_STEP
    is_expert = e < expert_steps
    e_first = jnp.minimum(e, expert_steps - 1) * EXPERTS_PER_STEP
    counts = [cnt_ref[seg0 + e_first + k] for k in range(EXPERTS_PER_STEP)]
    bases = [off_ref[seg0 + e_first + k] for k in range(EXPERTS_PER_STEP)]

    def gather_valid(buf, start, n_rows):
        def gather_group(g, carry):
            slot0 = start + g * SEG_ALIGN
            dst0 = pl.multiple_of(g * (SEG_ALIGN * rt), SEG_ALIGN * rt)
            for u in range(SEG_ALIGN):
                src = pl.multiple_of(inv_ref[slot0 + u], rt)
                buf[pl.ds(dst0 + u * rt, rt), :] = hp_ref[pl.ds(src, rt), :]
            return carry
        rows = jnp.minimum(MOE_TILE, (n_rows + SEG_ALIGN - 1) // SEG_ALIGN * SEG_ALIGN)
        lax.fori_loop(0, rows // SEG_ALIGN, gather_group, 0)

    def gather_static(buf, start):
        for r in range(MOE_TILE):
            src = pl.multiple_of(inv_ref[start + r], rt)
            buf[r * rt:(r + 1) * rt, :] = hp_ref[pl.ds(src, rt), :]

    def expert_tile(buf, k, start):
        xrow = jnp.concatenate([buf[pl.ds(j, MOE_TILE, stride=rt), :] for j in range(rt)], axis=1)
        xb = xrow.astype(BF16)
        hid = (_silu(_dot(xb, wg_ref[k])) * _dot(xb, wu_ref[k])).astype(BF16)
        out = _dot(hid, wd_ref[k])
        for j in range(rt):
            ys_ref[pl.ds(start * rt + j, MOE_TILE, stride=rt), :] = out[:, j * LANES:(j + 1) * LANES]

    @pl.when(is_expert)
    def _():
        bufs = (xt_ref, xt2_ref)
        gather_valid(bufs[0], pl.multiple_of(bases[0], SEG_ALIGN), counts[0])
        for k in range(EXPERTS_PER_STEP):
            start = pl.multiple_of(bases[k], SEG_ALIGN)
            if k + 1 < EXPERTS_PER_STEP:
                gather_static(bufs[(k + 1) % 2], pl.multiple_of(bases[k + 1], SEG_ALIGN))
            expert_tile(bufs[k % 2], k, start)

            @pl.when(counts[k] > MOE_TILE)
            def _(k=k, start=start):
                def more(i, carry):
                    s_i = pl.multiple_of(start + i * MOE_TILE, SEG_ALIGN)
                    gather_valid(xt3_ref, s_i, counts[k] - i * MOE_TILE)
                    expert_tile(xt3_ref, k, s_i)
                    return carry
                lax.fori_loop(1, (counts[k] + MOE_TILE - 1) // MOE_TILE, more, 0)

    @pl.when(e >= expert_steps)
    def _():
        tq = y_ref.shape[0] // rt
        t0 = tok0 + (e - expert_steps) * tq

        def combine(i, carry):
            r1 = ys_ref[pl.ds(pl.multiple_of(pos1_ref[t0 + i] * rt, rt), rt), :]
            r2 = ys_ref[pl.ds(pl.multiple_of(pos2_ref[t0 + i] * rt, rt), rt), :]
            y_ref[pl.ds(pl.multiple_of(i * rt, rt), rt), :] = w1_ref[t0 + i] * r1 + w2_ref[t0 + i] * r2
            return carry
        lax.fori_loop(0, tq, combine, 0, unroll=8)


def _moe(hp, pos1, pos2, w1, w2, off, cnt, w_gate, w_up, w_down):
    rt = ROW_TILES
    t = hp.shape[0] // rt
    ne, d, f = w_gate.shape
    nc = t // MOE_CHUNK
    tq = MOE_CHUNK // COMBINE_STEPS

    es = ne // EXPERTS_PER_STEP

    def w_idx(c, e, *_):
        return (jnp.where(e < es, e, 0), 0, 0)

    grid_spec = pltpu.PrefetchScalarGridSpec(
        num_scalar_prefetch=6,
        grid=(nc, es + COMBINE_STEPS),
        in_specs=[
            pl.BlockSpec((MOE_CHUNK * rt, LANES), lambda c, e, *_: (c, 0)),
            pl.BlockSpec((EXPERTS_PER_STEP, d, f), w_idx),
            pl.BlockSpec((EXPERTS_PER_STEP, d, f), w_idx),
            pl.BlockSpec((EXPERTS_PER_STEP, f, d), w_idx),
        ],
        out_specs=pl.BlockSpec((tq * rt, LANES),
                               lambda c, e, *_: (c * COMBINE_STEPS + jnp.maximum(e - es, 0), 0)),
        scratch_shapes=[
            pltpu.SMEM((SLOTS_PAD + MOE_TILE,), jnp.int32),
            pltpu.VMEM(((SLOTS_PAD + MOE_TILE) * rt, LANES), F32),
            pltpu.VMEM((MOE_TILE * rt, LANES), F32),
            pltpu.VMEM((MOE_TILE * rt, LANES), F32),
            pltpu.VMEM((MOE_TILE * rt, LANES), F32),
        ],
    )
    return pl.pallas_call(
        _moe_kernel,
        grid_spec=grid_spec,
        out_shape=jax.ShapeDtypeStruct((t * rt, LANES), F32),
        compiler_params=_cparams(("arbitrary", "arbitrary")),
        name="moe_top2",
    )(pos1, pos2, w1, w2, off, cnt, hp, w_gate, w_up, w_down)


def _final_kernel(x_ref, y_ref, mod_ref, g_ref, o_ref):
    x = x_ref[...] + mod_ref[G_FFN:G_FFN + 1, :] * _load_row_tiled(y_ref)
    o_ref[...] = _rms(x) * g_ref[...]


def _final(x2, y, mod_l, g, seq, tm=512):
    t, d = x2.shape
    per_b = seq // tm
    return pl.pallas_call(
        _final_kernel,
        grid=(t // tm,),
        in_specs=[
            pl.BlockSpec((tm, d), lambda i: (i, 0)),
            pl.BlockSpec((tm * ROW_TILES, LANES), lambda i: (i, 0)),
            pl.BlockSpec((None, MOD_ROWS, d), lambda i: (i // per_b, 0, 0)),
            pl.BlockSpec((1, d), lambda i: (0, 0)),
        ],
        out_specs=pl.BlockSpec((tm, d), lambda i: (i, 0)),
        out_shape=jax.ShapeDtypeStruct((t, d), F32),
        compiler_params=_cparams(("arbitrary",)),
        name="final_norm",
    )(x2, y, mod_l, g)


def _retention_tables(seq):
    half = HEAD_DIM // 2
    inv_freq = ROPE_BASE ** (-jnp.arange(half, dtype=F32) / half)
    ang = jnp.arange(seq, dtype=F32)[:, None] * inv_freq[None, :]
    cos, sin = jnp.cos(ang), jnp.sin(ang)
    cos_t = jnp.tile(jnp.concatenate([cos, cos], axis=-1), (1, RET_HEADS))
    sin_t = jnp.tile(jnp.concatenate([-sin, sin], axis=-1), (1, RET_HEADS))
    log_gamma = jnp.log1p(-jnp.exp2(-5.0 - jnp.arange(RET_HEADS, dtype=F32)))
    tt = jnp.arange(CHUNK, dtype=F32)
    diff = tt[:, None] - tt[None, :]
    rdec = jnp.where(diff >= 0, jnp.exp(jnp.maximum(diff, 0.0)[None] * log_gamma[:, None, None]), 0.0)
    lg_lane = jnp.repeat(log_gamma, HEAD_DIM)[None, :]
    rdq = jnp.exp((tt[:, None] + 1.0) * lg_lane)
    rdk = jnp.exp((CHUNK - 1.0 - tt[:, None]) * lg_lane)
    rtot = jnp.exp(CHUNK * lg_lane)
    return cos_t, sin_t, rdec, rdq, rdk, rtot


def _pad_lanes(v, width=LANES):
    return jnp.pad(v, ((0, 0), (0, width - v.shape[-1])))


def kernel(x, c, norm_mix_g, norm_ffn_g, final_norm_g, w_ada, b_ada, w_in, conv_w, conv_b, ssd_dt_bias, ssd_a_log, ssd_d, ssd_norm_g, hgrn_lower_bounds, hgrn_norm_g, w_out, w_grp, b_grp, w_exp, b_exp, w_gate, w_up, w_down):
    batch, seq, d = x.shape
    depth = w_in.shape[0]
    t = batch * seq

    c8 = jnp.pad(c, ((0, MOD_ROWS - batch), (0, 0)))
    mod = _ada(c8, w_ada, b_ada)
    mod = mod[:, :batch].reshape(depth, batch, 6, d)
    mod = jnp.pad(mod, ((0, 0), (0, 0), (0, MOD_ROWS - 6), (0, 0)))

    dt0 = SSD_WIDTH + XBC_WIDTH
    w_in_p = jnp.concatenate(
        [w_in[:, :, :dt0], w_in[:, :, dt0 + SSD_HEADS:], w_in[:, :, dt0:dt0 + SSD_HEADS],
         jnp.zeros((depth, d, N_PROJ - w_in.shape[2]), w_in.dtype)], axis=-1).astype(BF16)
    w_out_b = w_out.astype(BF16)
    n_router = N_GROUPS + N_EXPERTS
    w_router = jnp.pad(jnp.swapaxes(jnp.concatenate([w_grp, w_exp], axis=-1), 1, 2),
                       ((0, 0), (0, ROUTER_ROWS - n_router), (0, 0)))
    w_router_hi = w_router.astype(BF16)
    w_router = jnp.stack([w_router_hi, (w_router - w_router_hi.astype(F32)).astype(BF16)], axis=1)
    b_router = jnp.pad(jnp.concatenate([b_grp, b_exp], axis=-1), ((0, 0), (0, ROUTER_ROWS - n_router)))[:, :, None]
    dtb = _pad_lanes(ssd_dt_bias)[:, None, :]
    alog = _pad_lanes(ssd_a_log)[:, None, :]
    dexp = jnp.repeat(ssd_d, HEAD_DIM, axis=-1)[:, None, :]
    tables = _retention_tables(seq)

    x2 = x.reshape(t, d)
    y = None
    for layer in range(depth):
        x2, mixed, expert_w = _mixer(x2, y, mod[layer - 1] if layer else None, mod[layer],
                           norm_mix_g[layer][None, :], w_in_p, (w_gate, w_up, w_down), layer, batch, seq,
                           conv_w[layer], conv_b[layer][None, :], dtb[layer],
                           alog[layer], dexp[layer], ssd_norm_g[layer][None, :], hgrn_lower_bounds,
                           hgrn_norm_g[layer][None, :], *tables)
        x2, hp, route = _outproj(x2, mixed, mod[layer], norm_ffn_g[layer][None, :], w_out_b,
                                 w_router, b_router, seq, layer)
        slots, meta = _rank(route)
        y = _moe(hp, slots[S_POS1].astype(jnp.int32), slots[S_POS2].astype(jnp.int32), slots[S_W1], slots[S_W2],
                 meta[:, M_OFF, :, 0].reshape(-1), meta[:, M_CNT, :, 0].reshape(-1),
                 *expert_w)
    out = _final(x2, y, mod[depth - 1], final_norm_g[None, :], seq)
    return out.reshape(batch, seq, d)
```

```python
import functools
import math

import jax
import jax.numpy as jnp
from jax import lax
from jax.experimental import pallas as pl
from jax.experimental.pallas import tpu as pltpu

F32 = jnp.float32
BF16 = jnp.bfloat16

EPS = 1e-6
GATE_FLOOR = 1e-30
ROPE_BASE = 10000.0

HEAD_DIM = 64
SSD_HEADS = 8
SSD_GROUPS = 2
SSD_CONV = 4
HGRN_HEADS = 4
RET_HEADS = 4
N_GROUPS = 4
EXPERTS_PER_GROUP = 8
N_EXPERTS = N_GROUPS * EXPERTS_PER_GROUP
CHUNK = 128
LANES = 128
ROUTER_ROWS = 48
ROUTE_ROWS = 8

SH_MIX, SC_MIX, G_MIX, SH_FFN, SC_FFN, G_FFN = range(6)
MOD_ROWS = 8

VMEM_LIMIT = 56 * 1024 * 1024


def _cparams(sem):
    return pltpu.CompilerParams(dimension_semantics=sem, vmem_limit_bytes=VMEM_LIMIT)


def _silu(x):
    return x * jax.nn.sigmoid(x)


def _rms(x):
    return x * lax.rsqrt(jnp.mean(x * x, axis=-1, keepdims=True) + EPS)


def _split3(x):
    hi = x.astype(BF16)
    r = x - hi.astype(F32)
    mid = r.astype(BF16)
    lo = (r - mid.astype(F32)).astype(BF16)
    return hi, mid, lo


def _dot(a, b):
    return jnp.dot(a, b, preferred_element_type=F32)


def _dot_nt(a, b):
    return lax.dot_general(a, b, (((1,), (1,)), ((), ())), preferred_element_type=F32)


def _dot_tn(a, b):
    return lax.dot_general(a, b, (((0,), (0,)), ((), ())), preferred_element_type=F32)


def _ada_kernel(c_ref, w_ref, b_ref, o_ref):
    c = _silu(c_ref[...])
    w = w_ref[...]
    c_hi, w_hi = c.astype(BF16), w.astype(BF16)
    c_lo = (c - c_hi.astype(F32)).astype(BF16)
    w_lo = (w - w_hi.astype(F32)).astype(BF16)
    o_ref[...] = (_dot(c_hi, w_hi) + _dot(c_lo, w_hi) + _dot(c_hi, w_lo)) + b_ref[...]


def _ada(c8, w_ada, b_ada):
    depth, d, d6 = w_ada.shape
    nb = d6 // d
    return pl.pallas_call(
        _ada_kernel,
        grid=(depth, nb),
        in_specs=[
            pl.BlockSpec((MOD_ROWS, d), lambda l, n: (0, 0)),
            pl.BlockSpec((None, d, d), lambda l, n: (l, 0, n)),
            pl.BlockSpec((None, 1, d), lambda l, n: (l, 0, n)),
        ],
        out_specs=pl.BlockSpec((None, MOD_ROWS, d), lambda l, n: (l, 0, n)),
        out_shape=jax.ShapeDtypeStruct((depth, MOD_ROWS, d6), F32),
        compiler_params=_cparams(("arbitrary", "arbitrary")),
        name="ada_mod",
    )(c8, w_ada, b_ada.reshape(depth, 1, d6))


OFF_Z, OFF_XBC, OFF_HQ, OFF_HF, OFF_HI, OFF_HG = 0, 512, 1280, 1536, 1792, 2048
OFF_RQ, OFF_RK, OFF_RV, OFF_RG, OFF_DT = 2304, 2560, 2816, 3072, 3328
N_PROJ = 3456
SSD_WIDTH = SSD_HEADS * HEAD_DIM
XBC_WIDTH = SSD_WIDTH + 2 * SSD_GROUPS * HEAD_DIM
HGRN_WIDTH = HGRN_HEADS * HEAD_DIM
RET_WIDTH = RET_HEADS * HEAD_DIM
TAIL = 8
HGRN_DIRECT = 8
HGRN_LEVELS = (8, 16, 32, 64)


def _pair_blockdiag(v_pair, lo_lane):
    return jnp.concatenate([jnp.where(lo_lane, v_pair, 0.0), jnp.where(lo_lane, 0.0, v_pair)], axis=0)


def _head_mean_square(o, bd):
    c, n = o.shape[0], o.shape[1] // LANES
    sq = (o * o).astype(BF16)
    ms = _dot(jnp.concatenate([sq[:, p * LANES:(p + 1) * LANES] for p in range(n)], axis=0), bd)
    return jnp.concatenate([ms[p * c:(p + 1) * c] for p in range(n)], axis=1)


def _pair_scores(q_pair, k_pair_b, lo_lane):
    c = q_pair.shape[0]
    q2 = jnp.concatenate([jnp.where(lo_lane, q_pair, 0.0), jnp.where(lo_lane, 0.0, q_pair)], axis=0)
    s = _dot_nt(q2.astype(BF16), k_pair_b)
    return s[:c], s[c:]


N_MIXER_PARAMS = 14
N_EXPERT_ARRAYS = 3


def _mixer_kernel(layer, has_res, *refs):
    if has_res:
        x_ref, y_ref, modp_ref = refs[:3]
        refs = refs[3:]
    else:
        x_ref = refs[0]
        refs = refs[1:]
    mod_ref, g_ref, w_ref = refs[:3]
    expert_w = refs[3:3 + N_EXPERT_ARRAYS]
    rest = refs[3 + N_EXPERT_ARRAYS:]
    params, outs = rest[:N_MIXER_PARAMS], rest[N_MIXER_PARAMS:]
    if has_res:
        xo_ref = outs[0]
        outs = outs[1:]
    out_ref = outs[0]
    expert_w_bf16 = outs[1:1 + N_EXPERT_ARRAYS]
    proj_next, proj_cur, xbuf, s_ssd, s_hg, s_ret = outs[1 + N_EXPERT_ARRAYS:]
    j = pl.program_id(0)
    nb = x_ref.shape[0]

    @pl.when(j == 0)
    def _():
        xbuf[:, 0:TAIL, :] = jnp.zeros((nb, TAIL, XBC_WIDTH), F32)
        s_ssd[...] = jnp.zeros_like(s_ssd)
        s_hg[...] = jnp.zeros_like(s_hg)
        s_ret[...] = jnp.zeros_like(s_ret)

    for src, dst in zip(expert_w, expert_w_bf16):
        dst[...] = src[...].astype(BF16)

    hs = []
    for b in range(nb):
        x = x_ref[b]
        if has_res:
            x = x + modp_ref[b, G_FFN:G_FFN + 1, :] * _load_row_tiled(y_ref.at[b])
            xo_ref[b] = x
        hs.append(_rms(x) * g_ref[...] * (1.0 + mod_ref[b, SC_MIX:SC_MIX + 1, :]) + mod_ref[b, SH_MIX:SH_MIX + 1, :])
    proj = _dot(jnp.concatenate(hs, axis=0).astype(BF16), w_ref[...])
    for b in range(nb):
        proj_next[b] = proj[b * CHUNK:(b + 1) * CHUNK]

    @pl.when(j > 0)
    def _():
        stages = [_mixer_body(layer, proj_cur.at[b], *params, out_ref.at[b], xbuf.at[b], s_ssd.at[b],
                              s_hg.at[b], s_ret.at[b]) for b in range(nb)]
        while stages:
            for gen in list(stages):
                if next(gen, StopIteration) is StopIteration:
                    stages.remove(gen)

    proj_cur[...] = proj_next[...]


def _mixer_body(layer, proj_ref, convw_ref, convb_ref, dtb_ref, alog_ref, dexp_ref, ssdg_ref,
                lbraw_ref, hgg_ref, cos_ref, sin_ref, rdec_ref, rdq_ref, rdk_ref, rtot_ref,
                out_ref, xbuf, s_ssd, s_hg, s_ret):
    C = CHUNK

    row = lax.broadcasted_iota(jnp.int32, (C, LANES), 0)
    lane = lax.broadcasted_iota(jnp.int32, (C, LANES), 1)
    lo_lane = lane < HEAD_DIM
    causal = row >= lane
    bd_mask = jnp.right_shift(row, 6) == jnp.right_shift(lane, 6)
    bd_mean = jnp.where(bd_mask, 1.0 / HEAD_DIM, 0.0).astype(BF16)
    tril = jnp.where(causal, 1.0, 0.0).astype(BF16)

    def cumsum_rows(x):
        hi, mid, lo = _split3(x)
        return _dot(tril, hi) + _dot(tril, mid) + _dot(tril, lo)

    xbuf[TAIL:TAIL + C, :] = proj_ref[:, OFF_XBC:OFF_XBC + XBC_WIDTH]
    conv = convb_ref[...]
    for jj in range(SSD_CONV):
        off = TAIL - (SSD_CONV - 1) + jj
        conv = conv + convw_ref[jj:jj + 1, :] * xbuf[off:off + C, :]
    xbuf[0:TAIL, :] = xbuf[C:C + TAIL, :]
    xc = _silu(conv)
    xs = xc[:, 0:SSD_WIDTH]
    bm = xc[:, SSD_WIDTH:SSD_WIDTH + LANES]
    cm = xc[:, SSD_WIDTH + LANES:SSD_WIDTH + 2 * LANES]

    dt8 = jax.nn.softplus(proj_ref[:, OFF_DT:OFF_DT + LANES] + dtb_ref[...])
    la8 = dt8 * (-jnp.exp(alog_ref[...]))
    cs8 = cumsum_rows(la8)
    cs8t = cs8.T

    bm_b = bm.astype(BF16)
    cm_b = cm.astype(BF16)
    scores_g = _pair_scores(cm, bm_b, lo_lane)
    yield

    o_intra, e1_cols, e2_cols, dt_cols = [], [], [], []
    for p in range(SSD_HEADS // 2):
        a, b = 2 * p, 2 * p + 1
        g = a // (SSD_HEADS // SSD_GROUPS)
        col_a = jnp.broadcast_to(cs8[:, a:a + 1], (C, LANES))
        col_b = jnp.broadcast_to(cs8[:, b:b + 1], (C, LANES))
        cs_pair = jnp.where(lo_lane, col_a, col_b)
        dt_pair = jnp.where(lo_lane, jnp.broadcast_to(dt8[:, a:a + 1], (C, LANES)),
                            jnp.broadcast_to(dt8[:, b:b + 1], (C, LANES)))
        last = cs_pair[C - 1:C, :]
        e1_cols.append(jnp.exp(cs_pair))
        e2_cols.append(jnp.exp(last - cs_pair))
        dt_cols.append(dt_pair)
        dec_a = jnp.where(causal, jnp.exp(jnp.minimum(col_a - cs8t[a:a + 1, :], 0.0)), 0.0)
        dec_b = jnp.where(causal, jnp.exp(jnp.minimum(col_b - cs8t[b:b + 1, :], 0.0)), 0.0)
        pa = (scores_g[g] * dec_a).astype(BF16)
        pb = (scores_g[g] * dec_b).astype(BF16)
        v_pair = xs[:, p * LANES:(p + 1) * LANES] * dt_pair
        o_intra.append(_dot(jnp.concatenate([pa, pb], axis=1),
                            _pair_blockdiag(v_pair, lo_lane).astype(BF16)))
        yield
    o_intra = jnp.concatenate(o_intra, axis=1)
    e1 = jnp.concatenate(e1_cols, axis=1)
    e2 = jnp.concatenate(e2_cols, axis=1)
    dtx = jnp.concatenate(dt_cols, axis=1)
    v_all = xs * dtx
    s_prev = s_ssd[...]
    o_inter = e1 * _dot(cm_b, s_prev.astype(BF16))
    y = o_intra + o_inter + dexp_ref[...] * xs
    y = y * _silu(proj_ref[:, OFF_Z:OFF_Z + SSD_WIDTH])
    gw = SSD_WIDTH // SSD_GROUPS
    grp_mean = jnp.full((gw, gw), 1.0 / gw, BF16)
    ysq = (y * y).astype(BF16)
    ms = jnp.concatenate([_dot(ysq[:, g * gw:(g + 1) * gw], grp_mean) for g in range(SSD_GROUPS)], axis=1)
    out_ref[:, 0:SSD_WIDTH] = (y * lax.rsqrt(ms + EPS) * ssdg_ref[...]).astype(out_ref.dtype)
    row_s = lax.broadcasted_iota(jnp.int32, (LANES, SSD_WIDTH), 0)
    lane_s = lax.broadcasted_iota(jnp.int32, (LANES, SSD_WIDTH), 1)
    grp_mask = jnp.right_shift(row_s, 6) == jnp.right_shift(lane_s, 8)
    upd = _dot_tn(bm_b, (v_all * e2).astype(BF16))
    s_ssd[...] = e1[C - 1:C, :] * s_prev + jnp.where(grp_mask, upd, 0.0)
    yield

    W2 = HGRN_WIDTH
    row2 = lax.broadcasted_iota(jnp.int32, (C, W2), 0)
    lb_raw = lbraw_ref[...]
    lb_e = jnp.exp(lb_raw - jnp.max(lb_raw, axis=0, keepdims=True))
    lb_soft = lb_e / jnp.sum(lb_e, axis=0, keepdims=True)
    lb = jnp.sum(lb_soft[0:layer + 1, :], axis=0, keepdims=True) - lb_soft[0:1, :]
    forget = lb + (1.0 - lb) * jax.nn.sigmoid(proj_ref[:, OFF_HF:OFF_HF + W2])
    fc = jnp.maximum(forget, GATE_FLOOR)
    kk = 1.0 - forget
    hq = proj_ref[:, OFF_HQ:OFF_HQ + W2]
    hv = proj_ref[:, OFF_HI:OFF_HI + W2]
    cum = cumsum_rows(jnp.log(fc))
    npair = W2 // LANES

    sub = jnp.bitwise_and(row2, HGRN_DIRECT - 1)
    bd2 = jnp.where(jnp.right_shift(lax.broadcasted_iota(jnp.int32, (W2, W2), 0), 6)
                    == jnp.right_shift(lax.broadcasted_iota(jnp.int32, (W2, W2), 1), 6), 1.0, 0.0).astype(BF16)
    def prev_row(a):
        return pltpu.roll(a.reshape(C // HGRN_DIRECT, HGRN_DIRECT, W2), 1, 1).reshape(C, W2)

    not_first = sub != 0
    kdec = kk
    vprev = hv
    terms, vals = [(hq * kk).astype(BF16)], [hv]
    for dlt in range(1, HGRN_DIRECT):
        kdec = jnp.where(not_first, fc * prev_row(kdec), 0.0)
        vprev = prev_row(vprev)
        terms.append((hq * kdec).astype(BF16))
        vals.append(vprev)
    sc_all = _dot(jnp.concatenate(terms, axis=0), bd2)
    o_h = sc_all[0:C] * vals[0]
    for dlt in range(1, HGRN_DIRECT):
        o_h = o_h + sc_all[dlt * C:(dlt + 1) * C] * vals[dlt]
    yield
    p_sum = [[None, None] for _ in range(npair)]
    for m in HGRN_LEVELS:
        nb = C // (2 * m)
        cum3 = cum.reshape(nb, 2 * m, W2)
        ref = jnp.broadcast_to(cum3[:, m - 1:m, :], (nb, 2 * m, W2)).reshape(C, W2)
        right = jnp.bitwise_and(row2, m) != 0
        e = jnp.exp(jnp.where(right, cum - ref, ref - cum))
        qm = hq * e
        km = (kk * e).astype(BF16)
        sh = int(math.log2(2 * m))
        same_block = ((jnp.right_shift(row, sh) == jnp.right_shift(lane, sh))
                      & (jnp.bitwise_and(row, m) != 0) & (jnp.bitwise_and(lane, m) == 0))
        for p in range(npair):
            sl = slice(p * LANES, (p + 1) * LANES)
            for hh, s_h in enumerate(_pair_scores(qm[:, sl], km[:, sl], lo_lane)):
                sc = jnp.where(same_block, s_h, 0.0)
                p_sum[p][hh] = sc if p_sum[p][hh] is None else p_sum[p][hh] + sc
        yield
    cols = []
    for p in range(npair):
        vbd = _pair_blockdiag(hv[:, p * LANES:(p + 1) * LANES], lo_lane).astype(BF16)
        cols.append(_dot(jnp.concatenate([p_sum[p][0].astype(BF16), p_sum[p][1].astype(BF16)], axis=1), vbd))
    o_h = o_h + jnp.concatenate(cols, axis=1)
    q_in = (hq * jnp.exp(cum)).astype(BF16)
    last2 = cum[C - 1:C, :]
    k_end = (kk * jnp.exp(last2 - cum)).astype(BF16)
    tot2 = jnp.exp(last2)
    hv_b = hv.astype(BF16)
    inter = []
    for p in range(npair):
        sl = slice(p * LANES, (p + 1) * LANES)
        st = s_hg[p]
        inter.append(_dot_nt(q_in[:, sl], st.astype(BF16)))
        upd = _dot_tn(hv_b[:, sl], k_end[:, sl])
        s_hg[p] = st * tot2[:, sl] + jnp.where(bd_mask, upd, 0.0)
    o_h = o_h + jnp.concatenate(inter, axis=1)
    ms = _head_mean_square(o_h, bd_mean)
    o_h = o_h * lax.rsqrt(ms + EPS) * hgg_ref[...]
    o_h = o_h * jax.nn.sigmoid(proj_ref[:, OFF_HG:OFF_HG + W2])
    out_ref[:, SSD_WIDTH:SSD_WIDTH + W2] = o_h.astype(out_ref.dtype)
    yield

    W3 = RET_WIDTH
    lane3 = lax.broadcasted_iota(jnp.int32, (C, W3), 1)
    first_half = jnp.bitwise_and(lane3, HEAD_DIM // 2) == 0
    cosv = cos_ref[...]
    sinv = sin_ref[...]

    def rotary(xr):
        swapped = jnp.where(first_half, pltpu.roll(xr, W3 - HEAD_DIM // 2, 1),
                            pltpu.roll(xr, HEAD_DIM // 2, 1))
        return xr * cosv + swapped * sinv

    qr = rotary(proj_ref[:, OFF_RQ:OFF_RQ + W3])
    kr = rotary(proj_ref[:, OFF_RK:OFF_RK + W3]) * (HEAD_DIM ** -0.5)
    rv = proj_ref[:, OFF_RV:OFF_RV + W3]
    kr_b = kr.astype(BF16)
    q_in = (qr * rdq_ref[...]).astype(BF16)
    k_end = (kr * rdk_ref[...]).astype(BF16)
    rv_b = rv.astype(BF16)
    o_cols = []
    for p in range(W3 // LANES):
        sl = slice(p * LANES, (p + 1) * LANES)
        sa, sb = _pair_scores(qr[:, sl], kr_b[:, sl], lo_lane)
        sa = sa * rdec_ref[2 * p]
        sb = sb * rdec_ref[2 * p + 1]
        vbd = _pair_blockdiag(rv[:, sl], lo_lane).astype(BF16)
        o_p = _dot(jnp.concatenate([sa.astype(BF16), sb.astype(BF16)], axis=1), vbd)
        st = s_ret[p]
        o_p = o_p + _dot(q_in[:, sl], st.astype(BF16))
        upd = _dot_tn(k_end[:, sl], rv_b[:, sl])
        s_ret[p] = st * rtot_ref[:, sl] + jnp.where(bd_mask, upd, 0.0)
        o_cols.append(o_p)
        yield
    o_r = jnp.concatenate(o_cols, axis=1)
    ms = _head_mean_square(o_r, bd_mean)
    o_r = o_r * lax.rsqrt(ms + EPS) * _silu(proj_ref[:, OFF_RG:OFF_RG + W3])
    out_ref[:, SSD_WIDTH + W2:SSD_WIDTH + W2 + W3] = o_r.astype(out_ref.dtype)


def _mixer(x2, y, mod_prev, mod_l, g, w, expert_w, layer, batch, seq, conv_w, conv_b, dtb, alog, dexp, ssdg,
           lbraw, hgg, cos_t, sin_t, rdec, rdq, rdk, rtot):
    t, d = x2.shape
    d_mix = SSD_WIDTH + HGRN_WIDTH + RET_WIDTH
    nj = seq // CHUNK
    const2 = lambda j: (0, 0)
    has_res = y is not None
    proj_idx = lambda j: (0, jnp.minimum(j, nj - 1), 0)
    mix_idx = lambda j: (0, jnp.maximum(j - 1, 0), 0)
    x_spec = pl.BlockSpec((batch, CHUNK, d), proj_idx)
    mod_spec = pl.BlockSpec((batch, MOD_ROWS, d), lambda j: (0, 0, 0))
    head_specs = [x_spec]
    head_args = [x2.reshape(batch, seq, d)]
    if has_res:
        head_specs += [pl.BlockSpec((batch, CHUNK * ROW_TILES, LANES), proj_idx), mod_spec]
        head_args += [y.reshape(batch, seq * ROW_TILES, LANES), mod_prev]
    head_specs += [mod_spec, pl.BlockSpec((1, d), const2),
                   pl.BlockSpec((None,) + w.shape[1:], lambda j: (layer, 0, 0))]
    head_args += [mod_l, g, w]
    cast_specs, cast_shapes = [], []
    for ew in expert_w:
        depth_e, ne, ra, cb = ew.shape
        rows = ne * ra
        assert rows % nj == 0
        blk = rows // nj
        head_specs.append(pl.BlockSpec((None, blk, cb), lambda j: (layer, jnp.minimum(j, nj - 1), 0)))
        head_args.append(ew.reshape(depth_e, rows, cb))
        cast_specs.append(pl.BlockSpec((blk, cb), lambda j: (jnp.minimum(j, nj - 1), 0)))
        cast_shapes.append(jax.ShapeDtypeStruct((rows, cb), BF16))
    table_spec = pl.BlockSpec((CHUNK, RET_WIDTH), lambda j: (jnp.maximum(j - 1, 0), 0))
    mixed_spec = pl.BlockSpec((batch, CHUNK, d_mix), mix_idx)
    mixed_shape = jax.ShapeDtypeStruct((batch, seq, d_mix), BF16)
    res = pl.pallas_call(
        functools.partial(_mixer_kernel, layer, has_res),
        grid=(nj + 1,),
        in_specs=head_specs + [
            pl.BlockSpec(conv_w.shape, const2),
            pl.BlockSpec(conv_b.shape, const2),
            pl.BlockSpec(dtb.shape, const2),
            pl.BlockSpec(alog.shape, const2),
            pl.BlockSpec(dexp.shape, const2),
            pl.BlockSpec(ssdg.shape, const2),
            pl.BlockSpec(lbraw.shape, const2),
            pl.BlockSpec(hgg.shape, const2),
            table_spec,
            table_spec,
            pl.BlockSpec(rdec.shape, lambda j: (0, 0, 0)),
            pl.BlockSpec(rdq.shape, const2),
            pl.BlockSpec(rdk.shape, const2),
            pl.BlockSpec(rtot.shape, const2),
        ],
        out_specs=([x_spec] if has_res else []) + [mixed_spec] + cast_specs,
        out_shape=([jax.ShapeDtypeStruct((batch, seq, d), F32)] if has_res else []) + [mixed_shape] + cast_shapes,
        scratch_shapes=[
            pltpu.VMEM((batch, CHUNK, N_PROJ), F32),
            pltpu.VMEM((batch, CHUNK, N_PROJ), F32),
            pltpu.VMEM((batch, TAIL + CHUNK, XBC_WIDTH), F32),
            pltpu.VMEM((batch, LANES, SSD_WIDTH), F32),
            pltpu.VMEM((batch, HGRN_WIDTH // LANES, LANES, LANES), F32),
            pltpu.VMEM((batch, RET_WIDTH // LANES, LANES, LANES), F32),
        ],
        compiler_params=_cparams(("arbitrary",)),
        name="norm_inproj_mixers",
    )(*head_args, conv_w, conv_b, dtb, alog, dexp, ssdg, lbraw, hgg, cos_t, sin_t, rdec, rdq, rdk, rtot)
    if has_res:
        x2, res = res[0].reshape(t, d), res[1:]
    casts = [c.reshape(ew.shape[1:]) for c, ew in zip(res[1:], expert_w)]
    return x2, res[0].reshape(t, d_mix), casts


ROW_TILES = 8


def _store_row_tiled(ref, val):
    m = val.shape[0]
    for k in range(ROW_TILES):
        ref[pl.ds(k, m, stride=ROW_TILES), :] = val[:, k * LANES:(k + 1) * LANES]


def _load_row_tiled(ref):
    m = ref.shape[0] // ROW_TILES
    return jnp.concatenate([ref[pl.ds(k, m, stride=ROW_TILES), :] for k in range(ROW_TILES)], axis=1)


R_E1, R_E2, R_W1, R_W2 = range(4)


def _outproj_kernel(x_ref, mixed_ref, mod_ref, g_ref, w_ref, wr_ref, br_ref,
                    xo_ref, hp_ref, route_ref):
    x = x_ref[...] + mod_ref[G_MIX:G_MIX + 1, :] * _dot(mixed_ref[...], w_ref[...])
    xo_ref[...] = x
    h = _rms(x) * g_ref[...] * (1.0 + mod_ref[SC_FFN:SC_FFN + 1, :]) + mod_ref[SH_FFN:SH_FFN + 1, :]
    _store_row_tiled(hp_ref, h)
    h_hi = h.astype(BF16)
    h_lo = (h - h_hi.astype(F32)).astype(BF16)
    logits = (_dot_nt(wr_ref[0], h_hi) + _dot_nt(wr_ref[0], h_lo) + _dot_nt(wr_ref[1], h_hi)) + br_ref[...]
    rowi = lax.broadcasted_iota(jnp.int32, logits.shape, 0).astype(F32)
    neg = -jnp.inf
    big = float(ROUTER_ROWS)
    is_grp = rowi < N_GROUPS
    gl = jnp.where(is_grp, logits, neg)
    gmax = jnp.max(gl, axis=0, keepdims=True)
    g_idx = jnp.min(jnp.where(gl == gmax, rowi, big), axis=0, keepdims=True)
    p_grp = 1.0 / jnp.sum(jnp.where(is_grp, jnp.exp(gl - gmax), 0.0), axis=0, keepdims=True)
    first = N_GROUPS + g_idx * EXPERTS_PER_GROUP
    valid = (rowi >= first) & (rowi < first + EXPERTS_PER_GROUP)
    el = jnp.where(valid, logits, neg)
    m1 = jnp.max(el, axis=0, keepdims=True)
    i1 = jnp.min(jnp.where(el == m1, rowi, big), axis=0, keepdims=True)
    el2 = jnp.where(rowi == i1, neg, el)
    m2 = jnp.max(el2, axis=0, keepdims=True)
    i2 = jnp.min(jnp.where(el2 == m2, rowi, big), axis=0, keepdims=True)
    e = jnp.exp(m2 - m1)
    w1 = p_grp / (1.0 + e)
    w2 = p_grp * e / (1.0 + e)
    r8 = lax.broadcasted_iota(jnp.int32, route_ref.shape, 0)
    route_ref[...] = jnp.where(r8 == R_E1, i1 - N_GROUPS,
                               jnp.where(r8 == R_E2, i2 - N_GROUPS,
                                         jnp.where(r8 == R_W1, w1, jnp.where(r8 == R_W2, w2, 0.0))))


def _outproj(x2, mixed, mod_l, g, w, wr, br, seq, layer, tm=512):
    t, d = x2.shape
    per_b = seq // tm
    return pl.pallas_call(
        _outproj_kernel,
        grid=(t // tm,),
        in_specs=[
            pl.BlockSpec((tm, d), lambda i: (i, 0)),
            pl.BlockSpec((tm, mixed.shape[1]), lambda i: (i, 0)),
            pl.BlockSpec((None, MOD_ROWS, d), lambda i: (i // per_b, 0, 0)),
            pl.BlockSpec((1, d), lambda i: (0, 0)),
            pl.BlockSpec((None,) + w.shape[1:], lambda i: (layer, 0, 0)),
            pl.BlockSpec((None,) + wr.shape[1:], lambda i: (layer, 0, 0, 0)),
            pl.BlockSpec((None,) + br.shape[1:], lambda i: (layer, 0, 0)),
        ],
        out_specs=[
            pl.BlockSpec((tm, d), lambda i: (i, 0)),
            pl.BlockSpec((tm * ROW_TILES, LANES), lambda i: (i, 0)),
            pl.BlockSpec((ROUTE_ROWS, tm), lambda i: (0, i)),
        ],
        out_shape=[
            jax.ShapeDtypeStruct((t, d), F32),
            jax.ShapeDtypeStruct((t * ROW_TILES, LANES), F32),
            jax.ShapeDtypeStruct((ROUTE_ROWS, t), F32),
        ],
        compiler_params=_cparams(("arbitrary",)),
        name="outproj_router",
    )(x2, mixed, mod_l, g, w, wr, br)


MOE_CHUNK = 2048
MOE_TILE = 160
SEG_ALIGN = 8
COMBINE_STEPS = 4
EXPERTS_PER_STEP = 4
SLOTS_PAD = 2 * MOE_CHUNK + N_EXPERTS * SEG_ALIGN
RANK_BLOCK = 256
M_OFF, M_CNT = 0, 1
S_POS1, S_POS2, S_W1, S_W2 = 0, 1, R_W1, R_W2


def _rank_kernel(route_ref, slots_ref, meta_ref):
    tb = route_ref.shape[1]
    ne = N_EXPERTS
    rowe = lax.broadcasted_iota(jnp.int32, (ne, RANK_BLOCK), 0).astype(F32)
    r_i = lax.broadcasted_iota(jnp.int32, (RANK_BLOCK, RANK_BLOCK), 0)
    c_i = lax.broadcasted_iota(jnp.int32, (RANK_BLOCK, RANK_BLOCK), 1)
    earlier = jnp.where(r_i < c_i, 1.0, 0.0).astype(BF16)
    carry = jnp.zeros((ne, 1), F32)
    ranks = []
    for b in range(tb // RANK_BLOCK):
        blk = route_ref[:, b * RANK_BLOCK:(b + 1) * RANK_BLOCK]
        onehot = jnp.where((rowe == blk[R_E1:R_E1 + 1, :]) | (rowe == blk[R_E2:R_E2 + 1, :]), 1.0, 0.0)
        ranks.append(_dot(onehot.astype(BF16), earlier) + carry)
        carry = carry + jnp.sum(onehot, axis=1, keepdims=True)
    cnt = jnp.broadcast_to(carry, (ne, LANES))
    seg = jnp.floor((cnt + (SEG_ALIGN - 1)) * (1.0 / SEG_ALIGN)) * SEG_ALIGN
    e_r = lax.broadcasted_iota(jnp.int32, (ne, ne), 0)
    e_c = lax.broadcasted_iota(jnp.int32, (ne, ne), 1)
    before = jnp.where(e_c < e_r, 1.0, 0.0).astype(BF16)
    seg_parts = _split3(seg)
    off = _dot(before, seg_parts[0]) + _dot(before, seg_parts[1]) + _dot(before, seg_parts[2])
    off_col = off[:, 0:1]
    r8 = lax.broadcasted_iota(jnp.int32, (ROUTE_ROWS, RANK_BLOCK), 0)
    for b in range(tb // RANK_BLOCK):
        blk = route_ref[:, b * RANK_BLOCK:(b + 1) * RANK_BLOCK]
        dest = off_col + ranks[b]
        p1 = jnp.sum(jnp.where(rowe == blk[R_E1:R_E1 + 1, :], dest, 0.0), axis=0, keepdims=True)
        p2 = jnp.sum(jnp.where(rowe == blk[R_E2:R_E2 + 1, :], dest, 0.0), axis=0, keepdims=True)
        slots_ref[:, b * RANK_BLOCK:(b + 1) * RANK_BLOCK] = jnp.where(
            r8 == S_POS1, p1, jnp.where(r8 == S_POS2, p2, jnp.where((r8 == S_W1) | (r8 == S_W2), blk, 0.0)))
    meta_ref[M_OFF] = off.astype(jnp.int32)
    meta_ref[M_CNT] = cnt.astype(jnp.int32)


def _rank(route):
    t = route.shape[1]
    nc = t // MOE_CHUNK
    return pl.pallas_call(
        _rank_kernel,
        grid=(nc,),
        in_specs=[pl.BlockSpec((ROUTE_ROWS, MOE_CHUNK), lambda c: (0, c))],
        out_specs=[pl.BlockSpec((ROUTE_ROWS, MOE_CHUNK), lambda c: (0, c)),
                   pl.BlockSpec((None, 2, N_EXPERTS, LANES), lambda c: (c, 0, 0, 0))],
        out_shape=[jax.ShapeDtypeStruct((ROUTE_ROWS, t), F32),
                   jax.ShapeDtypeStruct((nc, 2, N_EXPERTS, LANES), jnp.int32)],
        compiler_params=_cparams(("arbitrary",)),
        name="slot_rank",
    )(route)


def _moe_kernel(pos1_ref, pos2_ref, row1_ref, row2_ref, w1_ref, w2_ref, off_ref, cnt_ref,
                hp_ref, wg_ref, wu_ref, wd_ref, y_ref, inv_ref, ys_ref, xt_ref, xt2_ref, xt3_ref):
    c = pl.program_id(0)
    e = pl.program_id(1)
    rt = ROW_TILES
    tb = hp_ref.shape[0] // rt
    tok0 = c * tb
    seg0 = c * N_EXPERTS

    @pl.when((c == 0) & (e == 0))
    def _():
        xt_ref[...] = jnp.zeros_like(xt_ref)
        xt3_ref[...] = jnp.zeros_like(xt3_ref)

    @pl.when(e == 0)
    def _():
        def pads(ee, carry):
            o = off_ref[seg0 + ee]
            n_e = cnt_ref[seg0 + ee]

            def zero(i, carry2):
                inv_ref[o + i] = 0
                return carry2
            lax.fori_loop(n_e, (n_e + SEG_ALIGN - 1) // SEG_ALIGN * SEG_ALIGN, zero, 0)
            return carry
        lax.fori_loop(0, N_EXPERTS, pads, 0)
        n_last = cnt_ref[seg0 + N_EXPERTS - 1]
        total = off_ref[seg0 + N_EXPERTS - 1] + (n_last + SEG_ALIGN - 1) // SEG_ALIGN * SEG_ALIGN

        def zero_tail(i, carry):
            inv_ref[total + i] = 0
            return carry
        lax.fori_loop(0, MOE_TILE, zero_tail, 0)

        def place(i, row):
            inv_ref[pos1_ref[tok0 + i]] = row
            inv_ref[pos2_ref[tok0 + i]] = row
            return row + rt
        lax.fori_loop(0, tb, place, 0, unroll=8)

    expert_steps = N_EXPERTS // EXPERTS_PER_STEP
    is_expert = e < expert_steps
    e_first = jnp.minimum(e, expert_steps - 1) * EXPERTS_PER_STEP
    counts = [cnt_ref[seg0 + e_first + k] for k in range(EXPERTS_PER_STEP)]
    bases = [off_ref[seg0 + e_first + k] for k in range(EXPERTS_PER_STEP)]

    def gather_valid(buf, start, n_rows):
        def gather_group(g, carry):
            slot0 = start + g * SEG_ALIGN
            dst0 = pl.multiple_of(g * (SEG_ALIGN * rt), SEG_ALIGN * rt)
            for u in range(SEG_ALIGN):
                src = pl.multiple_of(inv_ref[slot0 + u], rt)
                buf[pl.ds(dst0 + u * rt, rt), :] = hp_ref[pl.ds(src, rt), :]
            return carry
        rows = jnp.minimum(MOE_TILE, (n_rows + SEG_ALIGN - 1) // SEG_ALIGN * SEG_ALIGN)
        lax.fori_loop(0, rows // SEG_ALIGN, gather_group, 0)

    def gather_static(buf, start):
        for r in range(MOE_TILE):
            src = pl.multiple_of(inv_ref[start + r], rt)
            buf[r * rt:(r + 1) * rt, :] = hp_ref[pl.ds(src, rt), :]

    def expert_tile(buf, k, start):
        xrow = jnp.concatenate([buf[pl.ds(j, MOE_TILE, stride=rt), :] for j in range(rt)], axis=1)
        xb = xrow.astype(BF16)
        hid = (_silu(_dot(xb, wg_ref[k])) * _dot(xb, wu_ref[k])).astype(BF16)
        out = _dot(hid, wd_ref[k])
        for j in range(rt):
            ys_ref[pl.ds(start * rt + j, MOE_TILE, stride=rt), :] = out[:, j * LANES:(j + 1) * LANES]

    @pl.when(is_expert)
    def _():
        bufs = (xt_ref, xt2_ref)

        @pl.when(e == 0)
        def _():
            gather_valid(bufs[0], pl.multiple_of(bases[0], SEG_ALIGN), counts[0])

        next_first = off_ref[seg0 + jnp.minimum(e_first + EXPERTS_PER_STEP, N_EXPERTS - 1)]
        for k in range(EXPERTS_PER_STEP):
            start = pl.multiple_of(bases[k], SEG_ALIGN)
            nxt = bases[k + 1] if k + 1 < EXPERTS_PER_STEP else next_first
            gather_static(bufs[(k + 1) % 2], pl.multiple_of(nxt, SEG_ALIGN))
            expert_tile(bufs[k % 2], k, start)

            @pl.when(counts[k] > MOE_TILE)
            def _(k=k, start=start):
                def more(i, carry):
                    s_i = pl.multiple_of(start + i * MOE_TILE, SEG_ALIGN)
                    gather_valid(xt3_ref, s_i, counts[k] - i * MOE_TILE)
                    expert_tile(xt3_ref, k, s_i)
                    return carry
                lax.fori_loop(1, (counts[k] + MOE_TILE - 1) // MOE_TILE, more, 0)

    @pl.when(e >= expert_steps)
    def _():
        tq = y_ref.shape[0] // rt
        t0 = tok0 + (e - expert_steps) * tq

        def combine(i, carry):
            r1 = ys_ref[pl.ds(pl.multiple_of(row1_ref[t0 + i], rt), rt), :]
            r2 = ys_ref[pl.ds(pl.multiple_of(row2_ref[t0 + i], rt), rt), :]
            y_ref[pl.ds(pl.multiple_of(i * rt, rt), rt), :] = w1_ref[t0 + i] * r1 + w2_ref[t0 + i] * r2
            return carry
        lax.fori_loop(0, tq, combine, 0, unroll=8)


def _moe(hp, pos1, pos2, w1, w2, off, cnt, w_gate, w_up, w_down):
    rt = ROW_TILES
    t = hp.shape[0] // rt
    ne, d, f = w_gate.shape
    nc = t // MOE_CHUNK
    tq = MOE_CHUNK // COMBINE_STEPS

    es = ne // EXPERTS_PER_STEP

    def w_idx(c, e, *_):
        return (jnp.where(e < es, e, 0), 0, 0)

    grid_spec = pltpu.PrefetchScalarGridSpec(
        num_scalar_prefetch=8,
        grid=(nc, es + COMBINE_STEPS),
        in_specs=[
            pl.BlockSpec((MOE_CHUNK * rt, LANES), lambda c, e, *_: (c, 0)),
            pl.BlockSpec((EXPERTS_PER_STEP, d, f), w_idx),
            pl.BlockSpec((EXPERTS_PER_STEP, d, f), w_idx),
            pl.BlockSpec((EXPERTS_PER_STEP, f, d), w_idx),
        ],
        out_specs=pl.BlockSpec((tq * rt, LANES),
                               lambda c, e, *_: (c * COMBINE_STEPS + jnp.maximum(e - es, 0), 0)),
        scratch_shapes=[
            pltpu.SMEM((SLOTS_PAD + MOE_TILE,), jnp.int32),
            pltpu.VMEM(((SLOTS_PAD + MOE_TILE) * rt, LANES), F32),
            pltpu.VMEM((MOE_TILE * rt, LANES), F32),
            pltpu.VMEM((MOE_TILE * rt, LANES), F32),
            pltpu.VMEM((MOE_TILE * rt, LANES), F32),
        ],
    )
    return pl.pallas_call(
        _moe_kernel,
        grid_spec=grid_spec,
        out_shape=jax.ShapeDtypeStruct((t * rt, LANES), F32),
        compiler_params=_cparams(("arbitrary", "arbitrary")),
        name="moe_top2",
    )(pos1, pos2, pos1 * rt, pos2 * rt, w1, w2, off, cnt, hp, w_gate, w_up, w_down)


def _final_kernel(x_ref, y_ref, mod_ref, g_ref, o_ref):
    x = x_ref[...] + mod_ref[G_FFN:G_FFN + 1, :] * _load_row_tiled(y_ref)
    o_ref[...] = _rms(x) * g_ref[...]


def _final(x2, y, mod_l, g, seq, tm=512):
    t, d = x2.shape
    per_b = seq // tm
    return pl.pallas_call(
        _final_kernel,
        grid=(t // tm,),
        in_specs=[
            pl.BlockSpec((tm, d), lambda i: (i, 0)),
            pl.BlockSpec((tm * ROW_TILES, LANES), lambda i: (i, 0)),
            pl.BlockSpec((None, MOD_ROWS, d), lambda i: (i // per_b, 0, 0)),
            pl.BlockSpec((1, d), lambda i: (0, 0)),
        ],
        out_specs=pl.BlockSpec((tm, d), lambda i: (i, 0)),
        out_shape=jax.ShapeDtypeStruct((t, d), F32),
        compiler_params=_cparams(("arbitrary",)),
        name="final_norm",
    )(x2, y, mod_l, g)


def _reorder_w_in_kernel(w_ref, o_ref):
    dt0 = OFF_HQ
    rest = OFF_DT - OFF_HQ
    o_ref[:, 0:dt0] = w_ref[:, 0:dt0].astype(BF16)
    o_ref[:, dt0:OFF_DT] = w_ref[:, dt0 + SSD_HEADS:dt0 + SSD_HEADS + rest].astype(BF16)
    lane = lax.broadcasted_iota(jnp.int32, (w_ref.shape[0], LANES), 1)
    o_ref[:, OFF_DT:N_PROJ] = jnp.where(lane < SSD_HEADS, w_ref[:, dt0:dt0 + LANES], 0.0).astype(BF16)


def _reorder_w_in(w_in, tm=256):
    depth, d, n_in = w_in.shape
    assert n_in == OFF_DT + SSD_HEADS
    return pl.pallas_call(
        _reorder_w_in_kernel,
        grid=(depth, d // tm),
        in_specs=[pl.BlockSpec((None, tm, n_in), lambda l, i: (l, i, 0))],
        out_specs=pl.BlockSpec((None, tm, N_PROJ), lambda l, i: (l, i, 0)),
        out_shape=jax.ShapeDtypeStruct((depth, d, N_PROJ), BF16),
        compiler_params=_cparams(("arbitrary", "arbitrary")),
        name="reorder_w_in",
    )(w_in)


def _retention_tables(seq):
    half = HEAD_DIM // 2
    inv_freq = ROPE_BASE ** (-jnp.arange(half, dtype=F32) / half)
    ang = jnp.arange(seq, dtype=F32)[:, None] * inv_freq[None, :]
    cos, sin = jnp.cos(ang), jnp.sin(ang)
    cos_t = jnp.tile(jnp.concatenate([cos, cos], axis=-1), (1, RET_HEADS))
    sin_t = jnp.tile(jnp.concatenate([-sin, sin], axis=-1), (1, RET_HEADS))
    log_gamma = jnp.log1p(-jnp.exp2(-5.0 - jnp.arange(RET_HEADS, dtype=F32)))
    tt = jnp.arange(CHUNK, dtype=F32)
    diff = tt[:, None] - tt[None, :]
    rdec = jnp.where(diff >= 0, jnp.exp(jnp.maximum(diff, 0.0)[None] * log_gamma[:, None, None]), 0.0)
    lg_lane = jnp.repeat(log_gamma, HEAD_DIM)[None, :]
    rdq = jnp.exp((tt[:, None] + 1.0) * lg_lane)
    rdk = jnp.exp((CHUNK - 1.0 - tt[:, None]) * lg_lane)
    rtot = jnp.exp(CHUNK * lg_lane)
    return cos_t, sin_t, rdec, rdq, rdk, rtot


def _pad_lanes(v, width=LANES):
    return jnp.pad(v, ((0, 0), (0, width - v.shape[-1])))


def kernel(x, c, norm_mix_g, norm_ffn_g, final_norm_g, w_ada, b_ada, w_in, conv_w, conv_b, ssd_dt_bias, ssd_a_log, ssd_d, ssd_norm_g, hgrn_lower_bounds, hgrn_norm_g, w_out, w_grp, b_grp, w_exp, b_exp, w_gate, w_up, w_down):
    batch, seq, d = x.shape
    depth = w_in.shape[0]
    t = batch * seq

    c8 = jnp.pad(c, ((0, MOD_ROWS - batch), (0, 0)))
    mod = _ada(c8, w_ada, b_ada)
    mod = mod[:, :batch].reshape(depth, batch, 6, d)
    mod = jnp.pad(mod, ((0, 0), (0, 0), (0, MOD_ROWS - 6), (0, 0)))

    w_in_p = _reorder_w_in(w_in)
    w_out_b = w_out.astype(BF16)
    n_router = N_GROUPS + N_EXPERTS
    w_router = jnp.pad(jnp.swapaxes(jnp.concatenate([w_grp, w_exp], axis=-1), 1, 2),
                       ((0, 0), (0, ROUTER_ROWS - n_router), (0, 0)))
    w_router_hi = w_router.astype(BF16)
    w_router = jnp.stack([w_router_hi, (w_router - w_router_hi.astype(F32)).astype(BF16)], axis=1)
    b_router = jnp.pad(jnp.concatenate([b_grp, b_exp], axis=-1), ((0, 0), (0, ROUTER_ROWS - n_router)))[:, :, None]
    dtb = _pad_lanes(ssd_dt_bias)[:, None, :]
    alog = _pad_lanes(ssd_a_log)[:, None, :]
    dexp = jnp.repeat(ssd_d, HEAD_DIM, axis=-1)[:, None, :]
    tables = _retention_tables(seq)

    x2 = x.reshape(t, d)
    y = None
    for layer in range(depth):
        x2, mixed, expert_w = _mixer(x2, y, mod[layer - 1] if layer else None, mod[layer],
                           norm_mix_g[layer][None, :], w_in_p, (w_gate, w_up, w_down), layer, batch, seq,
                           conv_w[layer], conv_b[layer][None, :], dtb[layer],
                           alog[layer], dexp[layer], ssd_norm_g[layer][None, :], hgrn_lower_bounds,
                           hgrn_norm_g[layer][None, :], *tables)
        x2, hp, route = _outproj(x2, mixed, mod[layer], norm_ffn_g[layer][None, :], w_out_b,
                                 w_router, b_router, seq, layer)
        slots, meta = _rank(route)
        y = _moe(hp, slots[S_POS1].astype(jnp.int32), slots[S_POS2].astype(jnp.int32), slots[S_W1], slots[S_W2],
                 meta[:, M_OFF, :, 0].reshape(-1), meta[:, M_CNT, :, 0].reshape(-1),
                 *expert_w)
    out = _final(x2, y, mod[depth - 1], final_norm_g[None, :], seq)
    return out.reshape(batch, seq, d)
```

```python
import functools
import math

import jax
import jax.numpy as jnp
from jax import lax
from jax.experimental import pallas as pl
from jax.experimental.pallas import tpu as pltpu

F32 = jnp.float32
BF16 = jnp.bfloat16

EPS = 1e-6
GATE_FLOOR = 1e-30
ROPE_BASE = 10000.0

HEAD_DIM = 64
SSD_HEADS = 8
SSD_GROUPS = 2
SSD_CONV = 4
HGRN_HEADS = 4
RET_HEADS = 4
N_GROUPS = 4
EXPERTS_PER_GROUP = 8
N_EXPERTS = N_GROUPS * EXPERTS_PER_GROUP
CHUNK = 128
LANES = 128
ROUTER_ROWS = 48
ROUTE_ROWS = 8

SH_MIX, SC_MIX, G_MIX, SH_FFN, SC_FFN, G_FFN = range(6)
MOD_ROWS = 8

VMEM_LIMIT = 56 * 1024 * 1024


def _cparams(sem):
    return pltpu.CompilerParams(dimension_semantics=sem, vmem_limit_bytes=VMEM_LIMIT)


def _silu(x):
    return x * jax.nn.sigmoid(x)


def _rms(x):
    return x * lax.rsqrt(jnp.mean(x * x, axis=-1, keepdims=True) + EPS)


def _split3(x):
    hi = x.astype(BF16)
    r = x - hi.astype(F32)
    mid = r.astype(BF16)
    lo = (r - mid.astype(F32)).astype(BF16)
    return hi, mid, lo


def _dot(a, b):
    return jnp.dot(a, b, preferred_element_type=F32)


def _dot_nt(a, b):
    return lax.dot_general(a, b, (((1,), (1,)), ((), ())), preferred_element_type=F32)


def _dot_tn(a, b):
    return lax.dot_general(a, b, (((0,), (0,)), ((), ())), preferred_element_type=F32)


def _ada_kernel(c_ref, w_ref, b_ref, o_ref):
    c = _silu(c_ref[...])
    w = w_ref[...]
    c_hi, w_hi = c.astype(BF16), w.astype(BF16)
    c_lo = (c - c_hi.astype(F32)).astype(BF16)
    w_lo = (w - w_hi.astype(F32)).astype(BF16)
    o_ref[...] = (_dot(c_hi, w_hi) + _dot(c_lo, w_hi) + _dot(c_hi, w_lo)) + b_ref[...]


def _ada(c8, w_ada, b_ada):
    depth, d, d6 = w_ada.shape
    nb = d6 // d
    return pl.pallas_call(
        _ada_kernel,
        grid=(depth, nb),
        in_specs=[
            pl.BlockSpec((MOD_ROWS, d), lambda l, n: (0, 0)),
            pl.BlockSpec((None, d, d), lambda l, n: (l, 0, n)),
            pl.BlockSpec((None, 1, d), lambda l, n: (l, 0, n)),
        ],
        out_specs=pl.BlockSpec((None, MOD_ROWS, d), lambda l, n: (l, 0, n)),
        out_shape=jax.ShapeDtypeStruct((depth, MOD_ROWS, d6), F32),
        compiler_params=_cparams(("arbitrary", "arbitrary")),
        name="ada_mod",
    )(c8, w_ada, b_ada.reshape(depth, 1, d6))


OFF_Z, OFF_XBC, OFF_HQ, OFF_HF, OFF_HI, OFF_HG = 0, 512, 1280, 1536, 1792, 2048
OFF_RQ, OFF_RK, OFF_RV, OFF_RG, OFF_DT = 2304, 2560, 2816, 3072, 3328
N_PROJ = 3456
SSD_WIDTH = SSD_HEADS * HEAD_DIM
XBC_WIDTH = SSD_WIDTH + 2 * SSD_GROUPS * HEAD_DIM
HGRN_WIDTH = HGRN_HEADS * HEAD_DIM
RET_WIDTH = RET_HEADS * HEAD_DIM
TAIL = 8
HGRN_DIRECT = 8
HGRN_LEVELS = (8, 16, 32, 64)


def _pair_blockdiag(v_pair, lo_lane):
    return jnp.concatenate([jnp.where(lo_lane, v_pair, 0.0), jnp.where(lo_lane, 0.0, v_pair)], axis=0)


def _head_mean_square(o, bd):
    c, n = o.shape[0], o.shape[1] // LANES
    sq = (o * o).astype(BF16)
    ms = _dot(jnp.concatenate([sq[:, p * LANES:(p + 1) * LANES] for p in range(n)], axis=0), bd)
    return jnp.concatenate([ms[p * c:(p + 1) * c] for p in range(n)], axis=1)


def _pair_scores(q_pair, k_pair_b, lo_lane):
    c = q_pair.shape[0]
    q2 = jnp.concatenate([jnp.where(lo_lane, q_pair, 0.0), jnp.where(lo_lane, 0.0, q_pair)], axis=0)
    s = _dot_nt(q2.astype(BF16), k_pair_b)
    return s[:c], s[c:]


N_MIXER_PARAMS = 14
N_EXPERT_ARRAYS = 3


def _mixer_kernel(layer, has_res, *refs):
    if has_res:
        x_ref, y_ref, modp_ref = refs[:3]
        refs = refs[3:]
    else:
        x_ref = refs[0]
        refs = refs[1:]
    mod_ref, g_ref, w_ref = refs[:3]
    expert_w = refs[3:3 + N_EXPERT_ARRAYS]
    rest = refs[3 + N_EXPERT_ARRAYS:]
    params, outs = rest[:N_MIXER_PARAMS], rest[N_MIXER_PARAMS:]
    if has_res:
        xo_ref = outs[0]
        outs = outs[1:]
    out_ref = outs[0]
    expert_w_bf16 = outs[1:1 + N_EXPERT_ARRAYS]
    proj_next, proj_cur, xbuf, s_ssd, s_hg, s_ret = outs[1 + N_EXPERT_ARRAYS:]
    j = pl.program_id(0)
    nb = x_ref.shape[0]

    @pl.when(j == 0)
    def _():
        xbuf[:, 0:TAIL, :] = jnp.zeros((nb, TAIL, XBC_WIDTH), F32)
        s_ssd[...] = jnp.zeros_like(s_ssd)
        s_hg[...] = jnp.zeros_like(s_hg)
        s_ret[...] = jnp.zeros_like(s_ret)

    for src, dst in zip(expert_w, expert_w_bf16):
        dst[...] = src[...].astype(BF16)

    hs = []
    for b in range(nb):
        x = x_ref[b]
        if has_res:
            x = x + modp_ref[b, G_FFN:G_FFN + 1, :] * _load_row_tiled(y_ref.at[b])
            xo_ref[b] = x
        hs.append(_rms(x) * g_ref[...] * (1.0 + mod_ref[b, SC_MIX:SC_MIX + 1, :]) + mod_ref[b, SH_MIX:SH_MIX + 1, :])
    proj = _dot(jnp.concatenate(hs, axis=0).astype(BF16), w_ref[...])
    for b in range(nb):
        proj_next[b] = proj[b * CHUNK:(b + 1) * CHUNK]

    @pl.when(j > 0)
    def _():
        stages = [_mixer_body(layer, proj_cur.at[b], *params, out_ref.at[b], xbuf.at[b], s_ssd.at[b],
                              s_hg.at[b], s_ret.at[b]) for b in range(nb)]
        while stages:
            for gen in list(stages):
                if next(gen, StopIteration) is StopIteration:
                    stages.remove(gen)

    proj_cur[...] = proj_next[...]


def _mixer_body(layer, proj_ref, convw_ref, convb_ref, dtb_ref, alog_ref, dexp_ref, ssdg_ref,
                lbraw_ref, hgg_ref, cos_ref, sin_ref, rdec_ref, rdq_ref, rdk_ref, rtot_ref,
                out_ref, xbuf, s_ssd, s_hg, s_ret):
    C = CHUNK

    row = lax.broadcasted_iota(jnp.int32, (C, LANES), 0)
    lane = lax.broadcasted_iota(jnp.int32, (C, LANES), 1)
    lo_lane = lane < HEAD_DIM
    causal = row >= lane
    bd_mask = jnp.right_shift(row, 6) == jnp.right_shift(lane, 6)
    bd_mean = jnp.where(bd_mask, 1.0 / HEAD_DIM, 0.0).astype(BF16)
    tril = jnp.where(causal, 1.0, 0.0).astype(BF16)

    def cumsum_rows(x):
        hi, mid, lo = _split3(x)
        return _dot(tril, hi) + _dot(tril, mid) + _dot(tril, lo)

    xbuf[TAIL:TAIL + C, :] = proj_ref[:, OFF_XBC:OFF_XBC + XBC_WIDTH]
    conv = convb_ref[...]
    for jj in range(SSD_CONV):
        off = TAIL - (SSD_CONV - 1) + jj
        conv = conv + convw_ref[jj:jj + 1, :] * xbuf[off:off + C, :]
    xbuf[0:TAIL, :] = xbuf[C:C + TAIL, :]
    xc = _silu(conv)
    xs = xc[:, 0:SSD_WIDTH]
    bm = xc[:, SSD_WIDTH:SSD_WIDTH + LANES]
    cm = xc[:, SSD_WIDTH + LANES:SSD_WIDTH + 2 * LANES]

    dt8 = jax.nn.softplus(proj_ref[:, OFF_DT:OFF_DT + LANES] + dtb_ref[...])
    la8 = dt8 * (-jnp.exp(alog_ref[...]))
    cs8 = cumsum_rows(la8)
    cs8t = cs8.T

    bm_b = bm.astype(BF16)
    cm_b = cm.astype(BF16)
    scores_g = _pair_scores(cm, bm_b, lo_lane)
    yield

    o_intra, e1_cols, e2_cols, dt_cols = [], [], [], []
    for p in range(SSD_HEADS // 2):
        a, b = 2 * p, 2 * p + 1
        g = a // (SSD_HEADS // SSD_GROUPS)
        col_a = jnp.broadcast_to(cs8[:, a:a + 1], (C, LANES))
        col_b = jnp.broadcast_to(cs8[:, b:b + 1], (C, LANES))
        cs_pair = jnp.where(lo_lane, col_a, col_b)
        dt_pair = jnp.where(lo_lane, jnp.broadcast_to(dt8[:, a:a + 1], (C, LANES)),
                            jnp.broadcast_to(dt8[:, b:b + 1], (C, LANES)))
        last = cs_pair[C - 1:C, :]
        e1_cols.append(jnp.exp(cs_pair))
        e2_cols.append(jnp.exp(last - cs_pair))
        dt_cols.append(dt_pair)
        dec_a = jnp.where(causal, jnp.exp(jnp.minimum(col_a - cs8t[a:a + 1, :], 0.0)), 0.0)
        dec_b = jnp.where(causal, jnp.exp(jnp.minimum(col_b - cs8t[b:b + 1, :], 0.0)), 0.0)
        pa = (scores_g[g] * dec_a).astype(BF16)
        pb = (scores_g[g] * dec_b).astype(BF16)
        v_pair = xs[:, p * LANES:(p + 1) * LANES] * dt_pair
        o_intra.append(_dot(jnp.concatenate([pa, pb], axis=1),
                            _pair_blockdiag(v_pair, lo_lane).astype(BF16)))
        yield
    o_intra = jnp.concatenate(o_intra, axis=1)
    e1 = jnp.concatenate(e1_cols, axis=1)
    e2 = jnp.concatenate(e2_cols, axis=1)
    dtx = jnp.concatenate(dt_cols, axis=1)
    v_all = xs * dtx
    s_prev = s_ssd[...]
    o_inter = e1 * _dot(cm_b, s_prev.astype(BF16))
    y = o_intra + o_inter + dexp_ref[...] * xs
    y = y * _silu(proj_ref[:, OFF_Z:OFF_Z + SSD_WIDTH])
    gw = SSD_WIDTH // SSD_GROUPS
    grp_mean = jnp.full((gw, gw), 1.0 / gw, BF16)
    ysq = (y * y).astype(BF16)
    ms = jnp.concatenate([_dot(ysq[:, g * gw:(g + 1) * gw], grp_mean) for g in range(SSD_GROUPS)], axis=1)
    out_ref[:, 0:SSD_WIDTH] = (y * lax.rsqrt(ms + EPS) * ssdg_ref[...]).astype(out_ref.dtype)
    row_s = lax.broadcasted_iota(jnp.int32, (LANES, SSD_WIDTH), 0)
    lane_s = lax.broadcasted_iota(jnp.int32, (LANES, SSD_WIDTH), 1)
    grp_mask = jnp.right_shift(row_s, 6) == jnp.right_shift(lane_s, 8)
    upd = _dot_tn(bm_b, (v_all * e2).astype(BF16))
    s_ssd[...] = e1[C - 1:C, :] * s_prev + jnp.where(grp_mask, upd, 0.0)
    yield

    W2 = HGRN_WIDTH
    row2 = lax.broadcasted_iota(jnp.int32, (C, W2), 0)
    lb_raw = lbraw_ref[...]
    lb_e = jnp.exp(lb_raw - jnp.max(lb_raw, axis=0, keepdims=True))
    lb_soft = lb_e / jnp.sum(lb_e, axis=0, keepdims=True)
    lb = jnp.sum(lb_soft[0:layer + 1, :], axis=0, keepdims=True) - lb_soft[0:1, :]
    forget = lb + (1.0 - lb) * jax.nn.sigmoid(proj_ref[:, OFF_HF:OFF_HF + W2])
    fc = jnp.maximum(forget, GATE_FLOOR)
    kk = 1.0 - forget
    hq = proj_ref[:, OFF_HQ:OFF_HQ + W2]
    hv = proj_ref[:, OFF_HI:OFF_HI + W2]
    cum = cumsum_rows(jnp.log(fc))
    npair = W2 // LANES

    sub = jnp.bitwise_and(row2, HGRN_DIRECT - 1)
    bd2 = jnp.where(jnp.right_shift(lax.broadcasted_iota(jnp.int32, (W2, W2), 0), 6)
                    == jnp.right_shift(lax.broadcasted_iota(jnp.int32, (W2, W2), 1), 6), 1.0, 0.0).astype(BF16)
    def prev_row(a):
        return pltpu.roll(a.reshape(C // HGRN_DIRECT, HGRN_DIRECT, W2), 1, 1).reshape(C, W2)

    not_first = sub != 0
    kdec = kk
    vprev = hv
    terms, vals = [(hq * kk).astype(BF16)], [hv]
    for dlt in range(1, HGRN_DIRECT):
        kdec = jnp.where(not_first, fc * prev_row(kdec), 0.0)
        vprev = prev_row(vprev)
        terms.append((hq * kdec).astype(BF16))
        vals.append(vprev)
    sc_all = _dot(jnp.concatenate(terms, axis=0), bd2)
    o_h = sc_all[0:C] * vals[0]
    for dlt in range(1, HGRN_DIRECT):
        o_h = o_h + sc_all[dlt * C:(dlt + 1) * C] * vals[dlt]
    yield
    p_sum = [[None, None] for _ in range(npair)]
    for m in HGRN_LEVELS:
        nb = C // (2 * m)
        cum3 = cum.reshape(nb, 2 * m, W2)
        ref = jnp.broadcast_to(cum3[:, m - 1:m, :], (nb, 2 * m, W2)).reshape(C, W2)
        right = jnp.bitwise_and(row2, m) != 0
        e = jnp.exp(jnp.where(right, cum - ref, ref - cum))
        qm = hq * e
        km = (kk * e).astype(BF16)
        sh = int(math.log2(2 * m))
        same_block = ((jnp.right_shift(row, sh) == jnp.right_shift(lane, sh))
                      & (jnp.bitwise_and(row, m) != 0) & (jnp.bitwise_and(lane, m) == 0))
        for p in range(npair):
            sl = slice(p * LANES, (p + 1) * LANES)
            for hh, s_h in enumerate(_pair_scores(qm[:, sl], km[:, sl], lo_lane)):
                sc = jnp.where(same_block, s_h, 0.0)
                p_sum[p][hh] = sc if p_sum[p][hh] is None else p_sum[p][hh] + sc
        yield
    cols = []
    for p in range(npair):
        vbd = _pair_blockdiag(hv[:, p * LANES:(p + 1) * LANES], lo_lane).astype(BF16)
        cols.append(_dot(jnp.concatenate([p_sum[p][0].astype(BF16), p_sum[p][1].astype(BF16)], axis=1), vbd))
    o_h = o_h + jnp.concatenate(cols, axis=1)
    q_in = (hq * jnp.exp(cum)).astype(BF16)
    last2 = cum[C - 1:C, :]
    k_end = (kk * jnp.exp(last2 - cum)).astype(BF16)
    tot2 = jnp.exp(last2)
    hv_b = hv.astype(BF16)
    inter = []
    for p in range(npair):
        sl = slice(p * LANES, (p + 1) * LANES)
        st = s_hg[p]
        inter.append(_dot_nt(q_in[:, sl], st.astype(BF16)))
        upd = _dot_tn(hv_b[:, sl], k_end[:, sl])
        s_hg[p] = st * tot2[:, sl] + jnp.where(bd_mask, upd, 0.0)
    o_h = o_h + jnp.concatenate(inter, axis=1)
    ms = _head_mean_square(o_h, bd_mean)
    o_h = o_h * lax.rsqrt(ms + EPS) * hgg_ref[...]
    o_h = o_h * jax.nn.sigmoid(proj_ref[:, OFF_HG:OFF_HG + W2])
    out_ref[:, SSD_WIDTH:SSD_WIDTH + W2] = o_h.astype(out_ref.dtype)
    yield

    W3 = RET_WIDTH
    lane3 = lax.broadcasted_iota(jnp.int32, (C, W3), 1)
    first_half = jnp.bitwise_and(lane3, HEAD_DIM // 2) == 0
    cosv = cos_ref[...]
    sinv = sin_ref[...]

    def rotary(xr):
        swapped = jnp.where(first_half, pltpu.roll(xr, W3 - HEAD_DIM // 2, 1),
                            pltpu.roll(xr, HEAD_DIM // 2, 1))
        return xr * cosv + swapped * sinv

    qr = rotary(proj_ref[:, OFF_RQ:OFF_RQ + W3])
    kr = rotary(proj_ref[:, OFF_RK:OFF_RK + W3]) * (HEAD_DIM ** -0.5)
    rv = proj_ref[:, OFF_RV:OFF_RV + W3]
    kr_b = kr.astype(BF16)
    q_in = (qr * rdq_ref[...]).astype(BF16)
    k_end = (kr * rdk_ref[...]).astype(BF16)
    rv_b = rv.astype(BF16)
    o_cols = []
    for p in range(W3 // LANES):
        sl = slice(p * LANES, (p + 1) * LANES)
        sa, sb = _pair_scores(qr[:, sl], kr_b[:, sl], lo_lane)
        sa = sa * rdec_ref[2 * p]
        sb = sb * rdec_ref[2 * p + 1]
        vbd = _pair_blockdiag(rv[:, sl], lo_lane).astype(BF16)
        o_p = _dot(jnp.concatenate([sa.astype(BF16), sb.astype(BF16)], axis=1), vbd)
        st = s_ret[p]
        o_p = o_p + _dot(q_in[:, sl], st.astype(BF16))
        upd = _dot_tn(k_end[:, sl], rv_b[:, sl])
        s_ret[p] = st * rtot_ref[:, sl] + jnp.where(bd_mask, upd, 0.0)
        o_cols.append(o_p)
        yield
    o_r = jnp.concatenate(o_cols, axis=1)
    ms = _head_mean_square(o_r, bd_mean)
    o_r = o_r * lax.rsqrt(ms + EPS) * _silu(proj_ref[:, OFF_RG:OFF_RG + W3])
    out_ref[:, SSD_WIDTH + W2:SSD_WIDTH + W2 + W3] = o_r.astype(out_ref.dtype)


def _mixer(x2, y, mod_prev, mod_l, g, w, expert_w, layer, batch, seq, conv_w, conv_b, dtb, alog, dexp, ssdg,
           lbraw, hgg, cos_t, sin_t, rdec, rdq, rdk, rtot):
    t, d = x2.shape
    d_mix = SSD_WIDTH + HGRN_WIDTH + RET_WIDTH
    nj = seq // CHUNK
    const2 = lambda j: (0, 0)
    has_res = y is not None
    proj_idx = lambda j: (0, jnp.minimum(j, nj - 1), 0)
    mix_idx = lambda j: (0, jnp.maximum(j - 1, 0), 0)
    x_spec = pl.BlockSpec((batch, CHUNK, d), proj_idx)
    mod_spec = pl.BlockSpec((batch, MOD_ROWS, d), lambda j: (0, 0, 0))
    head_specs = [x_spec]
    head_args = [x2.reshape(batch, seq, d)]
    if has_res:
        head_specs += [pl.BlockSpec((batch, CHUNK * ROW_TILES, LANES), proj_idx), mod_spec]
        head_args += [y.reshape(batch, seq * ROW_TILES, LANES), mod_prev]
    head_specs += [mod_spec, pl.BlockSpec((1, d), const2),
                   pl.BlockSpec((None,) + w.shape[1:], lambda j: (layer, 0, 0))]
    head_args += [mod_l, g, w]
    cast_specs, cast_shapes = [], []
    for ew in expert_w:
        depth_e, ne, ra, cb = ew.shape
        rows = ne * ra
        assert rows % nj == 0
        blk = rows // nj
        head_specs.append(pl.BlockSpec((None, blk, cb), lambda j: (layer, jnp.minimum(j, nj - 1), 0)))
        head_args.append(ew.reshape(depth_e, rows, cb))
        cast_specs.append(pl.BlockSpec((blk, cb), lambda j: (jnp.minimum(j, nj - 1), 0)))
        cast_shapes.append(jax.ShapeDtypeStruct((rows, cb), BF16))
    table_spec = pl.BlockSpec((CHUNK, RET_WIDTH), lambda j: (jnp.maximum(j - 1, 0), 0))
    mixed_spec = pl.BlockSpec((batch, CHUNK, d_mix), mix_idx)
    mixed_shape = jax.ShapeDtypeStruct((batch, seq, d_mix), BF16)
    res = pl.pallas_call(
        functools.partial(_mixer_kernel, layer, has_res),
        grid=(nj + 1,),
        in_specs=head_specs + [
            pl.BlockSpec(conv_w.shape, const2),
            pl.BlockSpec(conv_b.shape, const2),
            pl.BlockSpec(dtb.shape, const2),
            pl.BlockSpec(alog.shape, const2),
            pl.BlockSpec(dexp.shape, const2),
            pl.BlockSpec(ssdg.shape, const2),
            pl.BlockSpec(lbraw.shape, const2),
            pl.BlockSpec(hgg.shape, const2),
            table_spec,
            table_spec,
            pl.BlockSpec(rdec.shape, lambda j: (0, 0, 0)),
            pl.BlockSpec(rdq.shape, const2),
            pl.BlockSpec(rdk.shape, const2),
            pl.BlockSpec(rtot.shape, const2),
        ],
        out_specs=([x_spec] if has_res else []) + [mixed_spec] + cast_specs,
        out_shape=([jax.ShapeDtypeStruct((batch, seq, d), F32)] if has_res else []) + [mixed_shape] + cast_shapes,
        scratch_shapes=[
            pltpu.VMEM((batch, CHUNK, N_PROJ), F32),
            pltpu.VMEM((batch, CHUNK, N_PROJ), F32),
            pltpu.VMEM((batch, TAIL + CHUNK, XBC_WIDTH), F32),
            pltpu.VMEM((batch, LANES, SSD_WIDTH), F32),
            pltpu.VMEM((batch, HGRN_WIDTH // LANES, LANES, LANES), F32),
            pltpu.VMEM((batch, RET_WIDTH // LANES, LANES, LANES), F32),
        ],
        compiler_params=_cparams(("arbitrary",)),
        name="norm_inproj_mixers",
    )(*head_args, conv_w, conv_b, dtb, alog, dexp, ssdg, lbraw, hgg, cos_t, sin_t, rdec, rdq, rdk, rtot)
    if has_res:
        x2, res = res[0].reshape(t, d), res[1:]
    casts = [c.reshape(ew.shape[1:]) for c, ew in zip(res[1:], expert_w)]
    return x2, res[0].reshape(t, d_mix), casts


ROW_TILES = 8


def _store_row_tiled(ref, val):
    m = val.shape[0]
    for k in range(ROW_TILES):
        ref[pl.ds(k, m, stride=ROW_TILES), :] = val[:, k * LANES:(k + 1) * LANES]


def _load_row_tiled(ref):
    m = ref.shape[0] // ROW_TILES
    return jnp.concatenate([ref[pl.ds(k, m, stride=ROW_TILES), :] for k in range(ROW_TILES)], axis=1)


R_E1, R_E2, R_W1, R_W2 = range(4)


def _outproj_kernel(x_ref, mixed_ref, mod_ref, g_ref, w_ref, wr_ref, br_ref,
                    xo_ref, hp_ref, route_ref):
    x = x_ref[...] + mod_ref[G_MIX:G_MIX + 1, :] * _dot(mixed_ref[...], w_ref[...])
    xo_ref[...] = x
    h = _rms(x) * g_ref[...] * (1.0 + mod_ref[SC_FFN:SC_FFN + 1, :]) + mod_ref[SH_FFN:SH_FFN + 1, :]
    _store_row_tiled(hp_ref, h)
    h_hi = h.astype(BF16)
    h_lo = (h - h_hi.astype(F32)).astype(BF16)
    logits = (_dot_nt(wr_ref[0], h_hi) + _dot_nt(wr_ref[0], h_lo) + _dot_nt(wr_ref[1], h_hi)) + br_ref[...]
    rowi = lax.broadcasted_iota(jnp.int32, logits.shape, 0).astype(F32)
    neg = -jnp.inf
    big = float(ROUTER_ROWS)
    is_grp = rowi < N_GROUPS
    gl = jnp.where(is_grp, logits, neg)
    gmax = jnp.max(gl, axis=0, keepdims=True)
    g_idx = jnp.min(jnp.where(gl == gmax, rowi, big), axis=0, keepdims=True)
    p_grp = 1.0 / jnp.sum(jnp.where(is_grp, jnp.exp(gl - gmax), 0.0), axis=0, keepdims=True)
    first = N_GROUPS + g_idx * EXPERTS_PER_GROUP
    valid = (rowi >= first) & (rowi < first + EXPERTS_PER_GROUP)
    el = jnp.where(valid, logits, neg)
    m1 = jnp.max(el, axis=0, keepdims=True)
    i1 = jnp.min(jnp.where(el == m1, rowi, big), axis=0, keepdims=True)
    el2 = jnp.where(rowi == i1, neg, el)
    m2 = jnp.max(el2, axis=0, keepdims=True)
    i2 = jnp.min(jnp.where(el2 == m2, rowi, big), axis=0, keepdims=True)
    e = jnp.exp(m2 - m1)
    w1 = p_grp / (1.0 + e)
    w2 = p_grp * e / (1.0 + e)
    r8 = lax.broadcasted_iota(jnp.int32, route_ref.shape, 0)
    route_ref[...] = jnp.where(r8 == R_E1, i1 - N_GROUPS,
                               jnp.where(r8 == R_E2, i2 - N_GROUPS,
                                         jnp.where(r8 == R_W1, w1, jnp.where(r8 == R_W2, w2, 0.0))))


def _outproj(x2, mixed, mod_l, g, w, wr, br, seq, layer, tm=512):
    t, d = x2.shape
    per_b = seq // tm
    return pl.pallas_call(
        _outproj_kernel,
        grid=(t // tm,),
        in_specs=[
            pl.BlockSpec((tm, d), lambda i: (i, 0)),
            pl.BlockSpec((tm, mixed.shape[1]), lambda i: (i, 0)),
            pl.BlockSpec((None, MOD_ROWS, d), lambda i: (i // per_b, 0, 0)),
            pl.BlockSpec((1, d), lambda i: (0, 0)),
            pl.BlockSpec((None,) + w.shape[1:], lambda i: (layer, 0, 0)),
            pl.BlockSpec((None,) + wr.shape[1:], lambda i: (layer, 0, 0, 0)),
            pl.BlockSpec((None,) + br.shape[1:], lambda i: (layer, 0, 0)),
        ],
        out_specs=[
            pl.BlockSpec((tm, d), lambda i: (i, 0)),
            pl.BlockSpec((tm * ROW_TILES, LANES), lambda i: (i, 0)),
            pl.BlockSpec((ROUTE_ROWS, tm), lambda i: (0, i)),
        ],
        out_shape=[
            jax.ShapeDtypeStruct((t, d), F32),
            jax.ShapeDtypeStruct((t * ROW_TILES, LANES), F32),
            jax.ShapeDtypeStruct((ROUTE_ROWS, t), F32),
        ],
        compiler_params=_cparams(("arbitrary",)),
        name="outproj_router",
    )(x2, mixed, mod_l, g, w, wr, br)


MOE_CHUNK = 2048
MOE_TILE = 160
SEG_ALIGN = 8
COMBINE_STEPS = 4
EXPERTS_PER_STEP = 4
SLOTS_PAD = 2 * MOE_CHUNK + N_EXPERTS * SEG_ALIGN
RANK_BLOCK = 256
M_OFF, M_CNT = 0, 1
S_POS1, S_POS2, S_W1, S_W2 = 0, 1, R_W1, R_W2


def _rank_kernel(route_ref, slots_ref, meta_ref):
    tb = route_ref.shape[1]
    ne = N_EXPERTS
    rowe = lax.broadcasted_iota(jnp.int32, (ne, RANK_BLOCK), 0).astype(F32)
    r_i = lax.broadcasted_iota(jnp.int32, (RANK_BLOCK, RANK_BLOCK), 0)
    c_i = lax.broadcasted_iota(jnp.int32, (RANK_BLOCK, RANK_BLOCK), 1)
    earlier = jnp.where(r_i < c_i, 1.0, 0.0).astype(BF16)
    carry = jnp.zeros((ne, 1), F32)
    ranks = []
    for b in range(tb // RANK_BLOCK):
        blk = route_ref[:, b * RANK_BLOCK:(b + 1) * RANK_BLOCK]
        onehot = jnp.where((rowe == blk[R_E1:R_E1 + 1, :]) | (rowe == blk[R_E2:R_E2 + 1, :]), 1.0, 0.0)
        ranks.append(_dot(onehot.astype(BF16), earlier) + carry)
        carry = carry + jnp.sum(onehot, axis=1, keepdims=True)
    cnt = jnp.broadcast_to(carry, (ne, LANES))
    seg = jnp.floor((cnt + (SEG_ALIGN - 1)) * (1.0 / SEG_ALIGN)) * SEG_ALIGN
    e_r = lax.broadcasted_iota(jnp.int32, (ne, ne), 0)
    e_c = lax.broadcasted_iota(jnp.int32, (ne, ne), 1)
    before = jnp.where(e_c < e_r, 1.0, 0.0).astype(BF16)
    seg_parts = _split3(seg)
    off = _dot(before, seg_parts[0]) + _dot(before, seg_parts[1]) + _dot(before, seg_parts[2])
    off_col = off[:, 0:1]
    r8 = lax.broadcasted_iota(jnp.int32, (ROUTE_ROWS, RANK_BLOCK), 0)
    for b in range(tb // RANK_BLOCK):
        blk = route_ref[:, b * RANK_BLOCK:(b + 1) * RANK_BLOCK]
        dest = off_col + ranks[b]
        p1 = jnp.sum(jnp.where(rowe == blk[R_E1:R_E1 + 1, :], dest, 0.0), axis=0, keepdims=True)
        p2 = jnp.sum(jnp.where(rowe == blk[R_E2:R_E2 + 1, :], dest, 0.0), axis=0, keepdims=True)
        slots_ref[:, b * RANK_BLOCK:(b + 1) * RANK_BLOCK] = jnp.where(
            r8 == S_POS1, p1, jnp.where(r8 == S_POS2, p2, jnp.where((r8 == S_W1) | (r8 == S_W2), blk, 0.0)))
    meta_ref[M_OFF] = off.astype(jnp.int32)
    meta_ref[M_CNT] = cnt.astype(jnp.int32)


def _rank(route):
    t = route.shape[1]
    nc = t // MOE_CHUNK
    return pl.pallas_call(
        _rank_kernel,
        grid=(nc,),
        in_specs=[pl.BlockSpec((ROUTE_ROWS, MOE_CHUNK), lambda c: (0, c))],
        out_specs=[pl.BlockSpec((ROUTE_ROWS, MOE_CHUNK), lambda c: (0, c)),
                   pl.BlockSpec((None, 2, N_EXPERTS, LANES), lambda c: (c, 0, 0, 0))],
        out_shape=[jax.ShapeDtypeStruct((ROUTE_ROWS, t), F32),
                   jax.ShapeDtypeStruct((nc, 2, N_EXPERTS, LANES), jnp.int32)],
        compiler_params=_cparams(("arbitrary",)),
        name="slot_rank",
    )(route)


def _moe_kernel(pos1_ref, pos2_ref, row1_ref, row2_ref, w1_ref, w2_ref, off_ref, cnt_ref,
                hp_ref, wg_ref, wu_ref, wd_ref, y_ref, inv_ref, ys_ref, xt_ref, xt2_ref, xt3_ref):
    c = pl.program_id(0)
    e = pl.program_id(1)
    rt = ROW_TILES
    tb = hp_ref.shape[0] // rt
    tok0 = c * tb
    seg0 = c * N_EXPERTS

    @pl.when((c == 0) & (e == 0))
    def _():
        xt_ref[...] = jnp.zeros_like(xt_ref)
        xt3_ref[...] = jnp.zeros_like(xt3_ref)

    inv_half = inv_ref.shape[0] // 2
    ib = (c % 2) * inv_half
    ib_next = inv_half - ib
    tok_next = jnp.minimum(c + 1, pl.num_programs(0) - 1) * tb

    @pl.when(e == 0)
    def _():
        def pads(ee, carry):
            o = off_ref[seg0 + ee]
            n_e = cnt_ref[seg0 + ee]

            def zero(i, carry2):
                inv_ref[ib + o + i] = 0
                return carry2
            lax.fori_loop(n_e, (n_e + SEG_ALIGN - 1) // SEG_ALIGN * SEG_ALIGN, zero, 0)
            return carry
        lax.fori_loop(0, N_EXPERTS, pads, 0)
        n_last = cnt_ref[seg0 + N_EXPERTS - 1]
        total = off_ref[seg0 + N_EXPERTS - 1] + (n_last + SEG_ALIGN - 1) // SEG_ALIGN * SEG_ALIGN

        def zero_tail(i, carry):
            inv_ref[ib + total + i] = 0
            return carry
        lax.fori_loop(0, MOE_TILE, zero_tail, 0)

    @pl.when((c == 0) & (e == 0))
    def _():
        def place(i, row):
            inv_ref[ib + pos1_ref[tok0 + i]] = row
            inv_ref[ib + pos2_ref[tok0 + i]] = row
            return row + rt
        lax.fori_loop(0, tb, place, 0, unroll=8)

    def place_slice(stage):
        n_tok = tb // N_EXPERTS
        first = stage * n_tok
        for i in range(n_tok):
            t_loc = first + i
            inv_ref[ib_next + pos1_ref[tok_next + t_loc]] = t_loc * rt
            inv_ref[ib_next + pos2_ref[tok_next + t_loc]] = t_loc * rt

    expert_steps = N_EXPERTS // EXPERTS_PER_STEP
    is_expert = e < expert_steps
    e_first = jnp.minimum(e, expert_steps - 1) * EXPERTS_PER_STEP
    counts = [cnt_ref[seg0 + e_first + k] for k in range(EXPERTS_PER_STEP)]
    bases = [off_ref[seg0 + e_first + k] for k in range(EXPERTS_PER_STEP)]

    def gather_valid(buf, start, n_rows):
        def gather_group(g, carry):
            slot0 = start + g * SEG_ALIGN
            dst0 = pl.multiple_of(g * (SEG_ALIGN * rt), SEG_ALIGN * rt)
            for u in range(SEG_ALIGN):
                src = pl.multiple_of(inv_ref[ib + slot0 + u], rt)
                buf[pl.ds(dst0 + u * rt, rt), :] = hp_ref[pl.ds(src, rt), :]
            return carry
        rows = jnp.minimum(MOE_TILE, (n_rows + SEG_ALIGN - 1) // SEG_ALIGN * SEG_ALIGN)
        lax.fori_loop(0, rows // SEG_ALIGN, gather_group, 0)

    def gather_static(buf, start):
        first = ib + start
        for r in range(MOE_TILE):
            src = pl.multiple_of(inv_ref[first + r], rt)
            buf[r * rt:(r + 1) * rt, :] = hp_ref[pl.ds(src, rt), :]

    def expert_tile(buf, k, start):
        xrow = jnp.concatenate([buf[pl.ds(j, MOE_TILE, stride=rt), :] for j in range(rt)], axis=1)
        xb = xrow.astype(BF16)
        hid = (_silu(_dot(xb, wg_ref[k])) * _dot(xb, wu_ref[k])).astype(BF16)
        out = _dot(hid, wd_ref[k])
        for j in range(rt):
            ys_ref[pl.ds(start * rt + j, MOE_TILE, stride=rt), :] = out[:, j * LANES:(j + 1) * LANES]

    @pl.when(is_expert)
    def _():
        bufs = (xt_ref, xt2_ref)

        @pl.when(e == 0)
        def _():
            gather_valid(bufs[0], pl.multiple_of(bases[0], SEG_ALIGN), counts[0])

        next_first = off_ref[seg0 + jnp.minimum(e_first + EXPERTS_PER_STEP, N_EXPERTS - 1)]
        for k in range(EXPERTS_PER_STEP):
            start = pl.multiple_of(bases[k], SEG_ALIGN)
            nxt = bases[k + 1] if k + 1 < EXPERTS_PER_STEP else next_first
            gather_static(bufs[(k + 1) % 2], pl.multiple_of(nxt, SEG_ALIGN))
            place_slice(e_first + k)
            expert_tile(bufs[k % 2], k, start)

            @pl.when(counts[k] > MOE_TILE)
            def _(k=k, start=start):
                def more(i, carry):
                    s_i = pl.multiple_of(start + i * MOE_TILE, SEG_ALIGN)
                    gather_valid(xt3_ref, s_i, counts[k] - i * MOE_TILE)
                    expert_tile(xt3_ref, k, s_i)
                    return carry
                lax.fori_loop(1, (counts[k] + MOE_TILE - 1) // MOE_TILE, more, 0)

    @pl.when(e >= expert_steps)
    def _():
        tq = y_ref.shape[0] // rt
        t0 = tok0 + (e - expert_steps) * tq

        def combine(i, carry):
            r1 = ys_ref[pl.ds(pl.multiple_of(row1_ref[t0 + i], rt), rt), :]
            r2 = ys_ref[pl.ds(pl.multiple_of(row2_ref[t0 + i], rt), rt), :]
            y_ref[pl.ds(pl.multiple_of(i * rt, rt), rt), :] = w1_ref[t0 + i] * r1 + w2_ref[t0 + i] * r2
            return carry
        lax.fori_loop(0, tq, combine, 0, unroll=8)


def _moe(hp, pos1, pos2, w1, w2, off, cnt, w_gate, w_up, w_down):
    rt = ROW_TILES
    t = hp.shape[0] // rt
    ne, d, f = w_gate.shape
    nc = t // MOE_CHUNK
    tq = MOE_CHUNK // COMBINE_STEPS

    es = ne // EXPERTS_PER_STEP

    def w_idx(c, e, *_):
        return (jnp.where(e < es, e, 0), 0, 0)

    grid_spec = pltpu.PrefetchScalarGridSpec(
        num_scalar_prefetch=8,
        grid=(nc, es + COMBINE_STEPS),
        in_specs=[
            pl.BlockSpec((MOE_CHUNK * rt, LANES), lambda c, e, *_: (c, 0)),
            pl.BlockSpec((EXPERTS_PER_STEP, d, f), w_idx),
            pl.BlockSpec((EXPERTS_PER_STEP, d, f), w_idx),
            pl.BlockSpec((EXPERTS_PER_STEP, f, d), w_idx),
        ],
        out_specs=pl.BlockSpec((tq * rt, LANES),
                               lambda c, e, *_: (c * COMBINE_STEPS + jnp.maximum(e - es, 0), 0)),
        scratch_shapes=[
            pltpu.SMEM((2 * (SLOTS_PAD + MOE_TILE),), jnp.int32),
            pltpu.VMEM(((SLOTS_PAD + MOE_TILE) * rt, LANES), F32),
            pltpu.VMEM((MOE_TILE * rt, LANES), F32),
            pltpu.VMEM((MOE_TILE * rt, LANES), F32),
            pltpu.VMEM((MOE_TILE * rt, LANES), F32),
        ],
    )
    return pl.pallas_call(
        _moe_kernel,
        grid_spec=grid_spec,
        out_shape=jax.ShapeDtypeStruct((t * rt, LANES), F32),
        compiler_params=_cparams(("arbitrary", "arbitrary")),
        name="moe_top2",
    )(pos1, pos2, pos1 * rt, pos2 * rt, w1, w2, off, cnt, hp, w_gate, w_up, w_down)


def _final_kernel(x_ref, y_ref, mod_ref, g_ref, o_ref):
    x = x_ref[...] + mod_ref[G_FFN:G_FFN + 1, :] * _load_row_tiled(y_ref)
    o_ref[...] = _rms(x) * g_ref[...]


def _final(x2, y, mod_l, g, seq, tm=512):
    t, d = x2.shape
    per_b = seq // tm
    return pl.pallas_call(
        _final_kernel,
        grid=(t // tm,),
        in_specs=[
            pl.BlockSpec((tm, d), lambda i: (i, 0)),
            pl.BlockSpec((tm * ROW_TILES, LANES), lambda i: (i, 0)),
            pl.BlockSpec((None, MOD_ROWS, d), lambda i: (i // per_b, 0, 0)),
            pl.BlockSpec((1, d), lambda i: (0, 0)),
        ],
        out_specs=pl.BlockSpec((tm, d), lambda i: (i, 0)),
        out_shape=jax.ShapeDtypeStruct((t, d), F32),
        compiler_params=_cparams(("arbitrary",)),
        name="final_norm",
    )(x2, y, mod_l, g)


def _reorder_w_in_kernel(w_ref, o_ref):
    dt0 = OFF_HQ
    rest = OFF_DT - OFF_HQ
    o_ref[:, 0:dt0] = w_ref[:, 0:dt0].astype(BF16)
    o_ref[:, dt0:OFF_DT] = w_ref[:, dt0 + SSD_HEADS:dt0 + SSD_HEADS + rest].astype(BF16)
    lane = lax.broadcasted_iota(jnp.int32, (w_ref.shape[0], LANES), 1)
    o_ref[:, OFF_DT:N_PROJ] = jnp.where(lane < SSD_HEADS, w_ref[:, dt0:dt0 + LANES], 0.0).astype(BF16)


def _reorder_w_in(w_in, tm=256):
    depth, d, n_in = w_in.shape
    assert n_in == OFF_DT + SSD_HEADS
    return pl.pallas_call(
        _reorder_w_in_kernel,
        grid=(depth, d // tm),
        in_specs=[pl.BlockSpec((None, tm, n_in), lambda l, i: (l, i, 0))],
        out_specs=pl.BlockSpec((None, tm, N_PROJ), lambda l, i: (l, i, 0)),
        out_shape=jax.ShapeDtypeStruct((depth, d, N_PROJ), BF16),
        compiler_params=_cparams(("arbitrary", "arbitrary")),
        name="reorder_w_in",
    )(w_in)


def _retention_tables(seq):
    half = HEAD_DIM // 2
    inv_freq = ROPE_BASE ** (-jnp.arange(half, dtype=F32) / half)
    ang = jnp.arange(seq, dtype=F32)[:, None] * inv_freq[None, :]
    cos, sin = jnp.cos(ang), jnp.sin(ang)
    cos_t = jnp.tile(jnp.concatenate([cos, cos], axis=-1), (1, RET_HEADS))
    sin_t = jnp.tile(jnp.concatenate([-sin, sin], axis=-1), (1, RET_HEADS))
    log_gamma = jnp.log1p(-jnp.exp2(-5.0 - jnp.arange(RET_HEADS, dtype=F32)))
    tt = jnp.arange(CHUNK, dtype=F32)
    diff = tt[:, None] - tt[None, :]
    rdec = jnp.where(diff >= 0, jnp.exp(jnp.maximum(diff, 0.0)[None] * log_gamma[:, None, None]), 0.0)
    lg_lane = jnp.repeat(log_gamma, HEAD_DIM)[None, :]
    rdq = jnp.exp((tt[:, None] + 1.0) * lg_lane)
    rdk = jnp.exp((CHUNK - 1.0 - tt[:, None]) * lg_lane)
    rtot = jnp.exp(CHUNK * lg_lane)
    return cos_t, sin_t, rdec, rdq, rdk, rtot


def _pad_lanes(v, width=LANES):
    return jnp.pad(v, ((0, 0), (0, width - v.shape[-1])))


def kernel(x, c, norm_mix_g, norm_ffn_g, final_norm_g, w_ada, b_ada, w_in, conv_w, conv_b, ssd_dt_bias, ssd_a_log, ssd_d, ssd_norm_g, hgrn_lower_bounds, hgrn_norm_g, w_out, w_grp, b_grp, w_exp, b_exp, w_gate, w_up, w_down):
    batch, seq, d = x.shape
    depth = w_in.shape[0]
    t = batch * seq

    c8 = jnp.pad(c, ((0, MOD_ROWS - batch), (0, 0)))
    mod = _ada(c8, w_ada, b_ada)
    mod = mod[:, :batch].reshape(depth, batch, 6, d)
    mod = jnp.pad(mod, ((0, 0), (0, 0), (0, MOD_ROWS - 6), (0, 0)))

    w_in_p = _reorder_w_in(w_in)
    w_out_b = w_out.astype(BF16)
    n_router = N_GROUPS + N_EXPERTS
    w_router = jnp.pad(jnp.swapaxes(jnp.concatenate([w_grp, w_exp], axis=-1), 1, 2),
                       ((0, 0), (0, ROUTER_ROWS - n_router), (0, 0)))
    w_router_hi = w_router.astype(BF16)
    w_router = jnp.stack([w_router_hi, (w_router - w_router_hi.astype(F32)).astype(BF16)], axis=1)
    b_router = jnp.pad(jnp.concatenate([b_grp, b_exp], axis=-1), ((0, 0), (0, ROUTER_ROWS - n_router)))[:, :, None]
    dtb = _pad_lanes(ssd_dt_bias)[:, None, :]
    alog = _pad_lanes(ssd_a_log)[:, None, :]
    dexp = jnp.repeat(ssd_d, HEAD_DIM, axis=-1)[:, None, :]
    tables = _retention_tables(seq)

    x2 = x.reshape(t, d)
    y = None
    for layer in range(depth):
        x2, mixed, expert_w = _mixer(x2, y, mod[layer - 1] if layer else None, mod[layer],
                           norm_mix_g[layer][None, :], w_in_p, (w_gate, w_up, w_down), layer, batch, seq,
                           conv_w[layer], conv_b[layer][None, :], dtb[layer],
                           alog[layer], dexp[layer], ssd_norm_g[layer][None, :], hgrn_lower_bounds,
                           hgrn_norm_g[layer][None, :], *tables)
        x2, hp, route = _outproj(x2, mixed, mod[layer], norm_ffn_g[layer][None, :], w_out_b,
                                 w_router, b_router, seq, layer)
        slots, meta = _rank(route)
        y = _moe(hp, slots[S_POS1].astype(jnp.int32), slots[S_POS2].astype(jnp.int32), slots[S_W1], slots[S_W2],
                 meta[:, M_OFF, :, 0].reshape(-1), meta[:, M_CNT, :, 0].reshape(-1),
                 *expert_w)
    out = _final(x2, y, mod[depth - 1], final_norm_g[None, :], seq)
    return out.reshape(batch, seq, d)
```

```python
import functools
import math

import jax
import jax.numpy as jnp
from jax import lax
from jax.experimental import pallas as pl
from jax.experimental.pallas import tpu as pltpu

F32 = jnp.float32
BF16 = jnp.bfloat16

EPS = 1e-6
GATE_FLOOR = 1e-30
ROPE_BASE = 10000.0

HEAD_DIM = 64
SSD_HEADS = 8
SSD_GROUPS = 2
SSD_CONV = 4
HGRN_HEADS = 4
RET_HEADS = 4
N_GROUPS = 4
EXPERTS_PER_GROUP = 8
N_EXPERTS = N_GROUPS * EXPERTS_PER_GROUP
CHUNK = 128
LANES = 128
ROUTER_ROWS = 48
ROUTE_ROWS = 8

SH_MIX, SC_MIX, G_MIX, SH_FFN, SC_FFN, G_FFN = range(6)
MOD_ROWS = 8

VMEM_LIMIT = 56 * 1024 * 1024


def _cparams(sem):
    return pltpu.CompilerParams(dimension_semantics=sem, vmem_limit_bytes=VMEM_LIMIT)


def _silu(x):
    return x * jax.nn.sigmoid(x)


def _rms(x):
    return x * lax.rsqrt(jnp.mean(x * x, axis=-1, keepdims=True) + EPS)


def _split3(x):
    hi = x.astype(BF16)
    r = x - hi.astype(F32)
    mid = r.astype(BF16)
    lo = (r - mid.astype(F32)).astype(BF16)
    return hi, mid, lo


def _dot(a, b):
    return jnp.dot(a, b, preferred_element_type=F32)


def _dot_nt(a, b):
    return lax.dot_general(a, b, (((1,), (1,)), ((), ())), preferred_element_type=F32)


def _dot_tn(a, b):
    return lax.dot_general(a, b, (((0,), (0,)), ((), ())), preferred_element_type=F32)


def _ada_kernel(c_ref, w_ref, b_ref, o_ref):
    c = _silu(c_ref[...])
    w = w_ref[...]
    c_hi, w_hi = c.astype(BF16), w.astype(BF16)
    c_lo = (c - c_hi.astype(F32)).astype(BF16)
    w_lo = (w - w_hi.astype(F32)).astype(BF16)
    o_ref[...] = (_dot(c_hi, w_hi) + _dot(c_lo, w_hi) + _dot(c_hi, w_lo)) + b_ref[...]


def _ada(c8, w_ada, b_ada):
    depth, d, d6 = w_ada.shape
    nb = d6 // d
    return pl.pallas_call(
        _ada_kernel,
        grid=(depth, nb),
        in_specs=[
            pl.BlockSpec((MOD_ROWS, d), lambda l, n: (0, 0)),
            pl.BlockSpec((None, d, d), lambda l, n: (l, 0, n)),
            pl.BlockSpec((None, 1, d), lambda l, n: (l, 0, n)),
        ],
        out_specs=pl.BlockSpec((None, MOD_ROWS, d), lambda l, n: (l, 0, n)),
        out_shape=jax.ShapeDtypeStruct((depth, MOD_ROWS, d6), F32),
        compiler_params=_cparams(("arbitrary", "arbitrary")),
        name="ada_mod",
    )(c8, w_ada, b_ada.reshape(depth, 1, d6))


OFF_Z, OFF_XBC, OFF_HQ, OFF_HF, OFF_HI, OFF_HG = 0, 512, 1280, 1536, 1792, 2048
OFF_RQ, OFF_RK, OFF_RV, OFF_RG, OFF_DT = 2304, 2560, 2816, 3072, 3328
N_PROJ = 3456
SSD_WIDTH = SSD_HEADS * HEAD_DIM
XBC_WIDTH = SSD_WIDTH + 2 * SSD_GROUPS * HEAD_DIM
HGRN_WIDTH = HGRN_HEADS * HEAD_DIM
RET_WIDTH = RET_HEADS * HEAD_DIM
TAIL = 8
HGRN_DIRECT = 8
HGRN_LEVELS = (8, 16, 32, 64)


def _pair_blockdiag(v_pair, lo_lane):
    return jnp.concatenate([jnp.where(lo_lane, v_pair, 0.0), jnp.where(lo_lane, 0.0, v_pair)], axis=0)


def _head_mean_square(o, bd):
    c, n = o.shape[0], o.shape[1] // LANES
    sq = (o * o).astype(BF16)
    ms = _dot(jnp.concatenate([sq[:, p * LANES:(p + 1) * LANES] for p in range(n)], axis=0), bd)
    return jnp.concatenate([ms[p * c:(p + 1) * c] for p in range(n)], axis=1)


def _pair_scores(q_pair, k_pair_b, lo_lane):
    c = q_pair.shape[0]
    q2 = jnp.concatenate([jnp.where(lo_lane, q_pair, 0.0), jnp.where(lo_lane, 0.0, q_pair)], axis=0)
    s = _dot_nt(q2.astype(BF16), k_pair_b)
    return s[:c], s[c:]


N_MIXER_PARAMS = 14
N_EXPERT_ARRAYS = 3


def _mixer_kernel(layer, has_res, *refs):
    if has_res:
        x_ref, y_ref, modp_ref = refs[:3]
        refs = refs[3:]
    else:
        x_ref = refs[0]
        refs = refs[1:]
    mod_ref, g_ref, w_ref = refs[:3]
    expert_w = refs[3:3 + N_EXPERT_ARRAYS]
    rest = refs[3 + N_EXPERT_ARRAYS:]
    params, outs = rest[:N_MIXER_PARAMS], rest[N_MIXER_PARAMS:]
    if has_res:
        xo_ref = outs[0]
        outs = outs[1:]
    out_ref = outs[0]
    expert_w_bf16 = outs[1:1 + N_EXPERT_ARRAYS]
    proj_next, proj_cur, xbuf, s_ssd, s_hg, s_ret = outs[1 + N_EXPERT_ARRAYS:]
    j = pl.program_id(0)
    nb = x_ref.shape[0]

    @pl.when(j == 0)
    def _():
        xbuf[:, 0:TAIL, :] = jnp.zeros((nb, TAIL, XBC_WIDTH), F32)
        s_ssd[...] = jnp.zeros_like(s_ssd)
        s_hg[...] = jnp.zeros_like(s_hg)
        s_ret[...] = jnp.zeros_like(s_ret)

    for src, dst in zip(expert_w, expert_w_bf16):
        dst[...] = src[...].astype(BF16)

    hs = []
    for b in range(nb):
        x = x_ref[b]
        if has_res:
            x = x + modp_ref[b, G_FFN:G_FFN + 1, :] * _load_row_tiled(y_ref.at[b])
            xo_ref[b] = x
        hs.append(_rms(x) * g_ref[...] * (1.0 + mod_ref[b, SC_MIX:SC_MIX + 1, :]) + mod_ref[b, SH_MIX:SH_MIX + 1, :])
    proj = _dot(jnp.concatenate(hs, axis=0).astype(BF16), w_ref[...])
    for b in range(nb):
        proj_next[b] = proj[b * CHUNK:(b + 1) * CHUNK]

    @pl.when(j > 0)
    def _():
        stages = [_mixer_body(layer, proj_cur.at[b], *params, out_ref.at[b], xbuf.at[b], s_ssd.at[b],
                              s_hg.at[b], s_ret.at[b]) for b in range(nb)]
        while stages:
            for gen in list(stages):
                if next(gen, StopIteration) is StopIteration:
                    stages.remove(gen)

    proj_cur[...] = proj_next[...]


def _mixer_body(layer, proj_ref, convw_ref, convb_ref, dtb_ref, alog_ref, dexp_ref, ssdg_ref,
                lbraw_ref, hgg_ref, cos_ref, sin_ref, rdec_ref, rdq_ref, rdk_ref, rtot_ref,
                out_ref, xbuf, s_ssd, s_hg, s_ret):
    C = CHUNK

    row = lax.broadcasted_iota(jnp.int32, (C, LANES), 0)
    lane = lax.broadcasted_iota(jnp.int32, (C, LANES), 1)
    lo_lane = lane < HEAD_DIM
    causal = row >= lane
    bd_mask = jnp.right_shift(row, 6) == jnp.right_shift(lane, 6)
    bd_mean = jnp.where(bd_mask, 1.0 / HEAD_DIM, 0.0).astype(BF16)
    tril = jnp.where(causal, 1.0, 0.0).astype(BF16)

    def cumsum_rows(x):
        hi, mid, lo = _split3(x)
        return _dot(tril, hi) + _dot(tril, mid) + _dot(tril, lo)

    xbuf[TAIL:TAIL + C, :] = proj_ref[:, OFF_XBC:OFF_XBC + XBC_WIDTH]
    conv = convb_ref[...]
    for jj in range(SSD_CONV):
        off = TAIL - (SSD_CONV - 1) + jj
        conv = conv + convw_ref[jj:jj + 1, :] * xbuf[off:off + C, :]
    xbuf[0:TAIL, :] = xbuf[C:C + TAIL, :]
    xc = _silu(conv)
    xs = xc[:, 0:SSD_WIDTH]
    bm = xc[:, SSD_WIDTH:SSD_WIDTH + LANES]
    cm = xc[:, SSD_WIDTH + LANES:SSD_WIDTH + 2 * LANES]

    dt8 = jax.nn.softplus(proj_ref[:, OFF_DT:OFF_DT + LANES] + dtb_ref[...])
    la8 = dt8 * (-jnp.exp(alog_ref[...]))
    cs8 = cumsum_rows(la8)
    cs8t = cs8.T

    bm_b = bm.astype(BF16)
    cm_b = cm.astype(BF16)
    scores_g = _pair_scores(cm, bm_b, lo_lane)
    yield

    o_intra, e1_cols, e2_cols, dt_cols = [], [], [], []
    for p in range(SSD_HEADS // 2):
        a, b = 2 * p, 2 * p + 1
        g = a // (SSD_HEADS // SSD_GROUPS)
        col_a = jnp.broadcast_to(cs8[:, a:a + 1], (C, LANES))
        col_b = jnp.broadcast_to(cs8[:, b:b + 1], (C, LANES))
        cs_pair = jnp.where(lo_lane, col_a, col_b)
        dt_pair = jnp.where(lo_lane, jnp.broadcast_to(dt8[:, a:a + 1], (C, LANES)),
                            jnp.broadcast_to(dt8[:, b:b + 1], (C, LANES)))
        last = cs_pair[C - 1:C, :]
        e1_cols.append(jnp.exp(cs_pair))
        e2_cols.append(jnp.exp(last - cs_pair))
        dt_cols.append(dt_pair)
        dec_a = jnp.where(causal, jnp.exp(jnp.minimum(col_a - cs8t[a:a + 1, :], 0.0)), 0.0)
        dec_b = jnp.where(causal, jnp.exp(jnp.minimum(col_b - cs8t[b:b + 1, :], 0.0)), 0.0)
        pa = (scores_g[g] * dec_a).astype(BF16)
        pb = (scores_g[g] * dec_b).astype(BF16)
        v_pair = xs[:, p * LANES:(p + 1) * LANES] * dt_pair
        o_intra.append(_dot(jnp.concatenate([pa, pb], axis=1),
                            _pair_blockdiag(v_pair, lo_lane).astype(BF16)))
        yield
    o_intra = jnp.concatenate(o_intra, axis=1)
    e1 = jnp.concatenate(e1_cols, axis=1)
    e2 = jnp.concatenate(e2_cols, axis=1)
    dtx = jnp.concatenate(dt_cols, axis=1)
    v_all = xs * dtx
    s_prev = s_ssd[...]
    o_inter = e1 * _dot(cm_b, s_prev.astype(BF16))
    y = o_intra + o_inter + dexp_ref[...] * xs
    y = y * _silu(proj_ref[:, OFF_Z:OFF_Z + SSD_WIDTH])
    gw = SSD_WIDTH // SSD_GROUPS
    grp_mean = jnp.full((gw, gw), 1.0 / gw, BF16)
    ysq = (y * y).astype(BF16)
    ms = jnp.concatenate([_dot(ysq[:, g * gw:(g + 1) * gw], grp_mean) for g in range(SSD_GROUPS)], axis=1)
    out_ref[:, 0:SSD_WIDTH] = (y * lax.rsqrt(ms + EPS) * ssdg_ref[...]).astype(out_ref.dtype)
    row_s = lax.broadcasted_iota(jnp.int32, (LANES, SSD_WIDTH), 0)
    lane_s = lax.broadcasted_iota(jnp.int32, (LANES, SSD_WIDTH), 1)
    grp_mask = jnp.right_shift(row_s, 6) == jnp.right_shift(lane_s, 8)
    upd = _dot_tn(bm_b, (v_all * e2).astype(BF16))
    s_ssd[...] = e1[C - 1:C, :] * s_prev + jnp.where(grp_mask, upd, 0.0)
    yield

    W2 = HGRN_WIDTH
    row2 = lax.broadcasted_iota(jnp.int32, (C, W2), 0)
    lb_raw = lbraw_ref[...]
    lb_e = jnp.exp(lb_raw - jnp.max(lb_raw, axis=0, keepdims=True))
    lb_soft = lb_e / jnp.sum(lb_e, axis=0, keepdims=True)
    lb = jnp.sum(lb_soft[0:layer + 1, :], axis=0, keepdims=True) - lb_soft[0:1, :]
    forget = lb + (1.0 - lb) * jax.nn.sigmoid(proj_ref[:, OFF_HF:OFF_HF + W2])
    fc = jnp.maximum(forget, GATE_FLOOR)
    kk = 1.0 - forget
    hq = proj_ref[:, OFF_HQ:OFF_HQ + W2]
    hv = proj_ref[:, OFF_HI:OFF_HI + W2]
    cum = cumsum_rows(jnp.log(fc))
    npair = W2 // LANES

    sub = jnp.bitwise_and(row2, HGRN_DIRECT - 1)
    bd2 = jnp.where(jnp.right_shift(lax.broadcasted_iota(jnp.int32, (W2, W2), 0), 6)
                    == jnp.right_shift(lax.broadcasted_iota(jnp.int32, (W2, W2), 1), 6), 1.0, 0.0).astype(BF16)
    def prev_row(a):
        return pltpu.roll(a.reshape(C // HGRN_DIRECT, HGRN_DIRECT, W2), 1, 1).reshape(C, W2)

    not_first = sub != 0
    kdec = kk
    vprev = hv
    terms, vals = [(hq * kk).astype(BF16)], [hv]
    for dlt in range(1, HGRN_DIRECT):
        kdec = jnp.where(not_first, fc * prev_row(kdec), 0.0)
        vprev = prev_row(vprev)
        terms.append((hq * kdec).astype(BF16))
        vals.append(vprev)
    sc_all = _dot(jnp.concatenate(terms, axis=0), bd2)
    o_h = sc_all[0:C] * vals[0]
    for dlt in range(1, HGRN_DIRECT):
        o_h = o_h + sc_all[dlt * C:(dlt + 1) * C] * vals[dlt]
    yield
    p_sum = [[None, None] for _ in range(npair)]
    for m in HGRN_LEVELS:
        nb = C // (2 * m)
        cum3 = cum.reshape(nb, 2 * m, W2)
        ref = jnp.broadcast_to(cum3[:, m - 1:m, :], (nb, 2 * m, W2)).reshape(C, W2)
        right = jnp.bitwise_and(row2, m) != 0
        e = jnp.exp(jnp.where(right, cum - ref, ref - cum))
        qm = hq * e
        km = (kk * e).astype(BF16)
        sh = int(math.log2(2 * m))
        same_block = ((jnp.right_shift(row, sh) == jnp.right_shift(lane, sh))
                      & (jnp.bitwise_and(row, m) != 0) & (jnp.bitwise_and(lane, m) == 0))
        for p in range(npair):
            sl = slice(p * LANES, (p + 1) * LANES)
            for hh, s_h in enumerate(_pair_scores(qm[:, sl], km[:, sl], lo_lane)):
                sc = jnp.where(same_block, s_h, 0.0)
                p_sum[p][hh] = sc if p_sum[p][hh] is None else p_sum[p][hh] + sc
        yield
    cols = []
    for p in range(npair):
        vbd = _pair_blockdiag(hv[:, p * LANES:(p + 1) * LANES], lo_lane).astype(BF16)
        cols.append(_dot(jnp.concatenate([p_sum[p][0].astype(BF16), p_sum[p][1].astype(BF16)], axis=1), vbd))
    o_h = o_h + jnp.concatenate(cols, axis=1)
    q_in = (hq * jnp.exp(cum)).astype(BF16)
    last2 = cum[C - 1:C, :]
    k_end = (kk * jnp.exp(last2 - cum)).astype(BF16)
    tot2 = jnp.exp(last2)
    hv_b = hv.astype(BF16)
    inter = []
    for p in range(npair):
        sl = slice(p * LANES, (p + 1) * LANES)
        st = s_hg[p]
        inter.append(_dot_nt(q_in[:, sl], st.astype(BF16)))
        upd = _dot_tn(hv_b[:, sl], k_end[:, sl])
        s_hg[p] = st * tot2[:, sl] + jnp.where(bd_mask, upd, 0.0)
    o_h = o_h + jnp.concatenate(inter, axis=1)
    ms = _head_mean_square(o_h, bd_mean)
    o_h = o_h * lax.rsqrt(ms + EPS) * hgg_ref[...]
    o_h = o_h * jax.nn.sigmoid(proj_ref[:, OFF_HG:OFF_HG + W2])
    out_ref[:, SSD_WIDTH:SSD_WIDTH + W2] = o_h.astype(out_ref.dtype)
    yield

    W3 = RET_WIDTH
    lane3 = lax.broadcasted_iota(jnp.int32, (C, W3), 1)
    first_half = jnp.bitwise_and(lane3, HEAD_DIM // 2) == 0
    cosv = cos_ref[...]
    sinv = sin_ref[...]

    def rotary(xr):
        swapped = jnp.where(first_half, pltpu.roll(xr, W3 - HEAD_DIM // 2, 1),
                            pltpu.roll(xr, HEAD_DIM // 2, 1))
        return xr * cosv + swapped * sinv

    qr = rotary(proj_ref[:, OFF_RQ:OFF_RQ + W3])
    kr = rotary(proj_ref[:, OFF_RK:OFF_RK + W3]) * (HEAD_DIM ** -0.5)
    rv = proj_ref[:, OFF_RV:OFF_RV + W3]
    kr_b = kr.astype(BF16)
    q_in = (qr * rdq_ref[...]).astype(BF16)
    k_end = (kr * rdk_ref[...]).astype(BF16)
    rv_b = rv.astype(BF16)
    o_cols = []
    for p in range(W3 // LANES):
        sl = slice(p * LANES, (p + 1) * LANES)
        sa, sb = _pair_scores(qr[:, sl], kr_b[:, sl], lo_lane)
        sa = sa * rdec_ref[2 * p]
        sb = sb * rdec_ref[2 * p + 1]
        vbd = _pair_blockdiag(rv[:, sl], lo_lane).astype(BF16)
        o_p = _dot(jnp.concatenate([sa.astype(BF16), sb.astype(BF16)], axis=1), vbd)
        st = s_ret[p]
        o_p = o_p + _dot(q_in[:, sl], st.astype(BF16))
        upd = _dot_tn(k_end[:, sl], rv_b[:, sl])
        s_ret[p] = st * rtot_ref[:, sl] + jnp.where(bd_mask, upd, 0.0)
        o_cols.append(o_p)
        yield
    o_r = jnp.concatenate(o_cols, axis=1)
    ms = _head_mean_square(o_r, bd_mean)
    o_r = o_r * lax.rsqrt(ms + EPS) * _silu(proj_ref[:, OFF_RG:OFF_RG + W3])
    out_ref[:, SSD_WIDTH + W2:SSD_WIDTH + W2 + W3] = o_r.astype(out_ref.dtype)


def _mixer(x2, y, mod_prev, mod_l, g, w, expert_w, layer, batch, seq, conv_w, conv_b, dtb, alog, dexp, ssdg,
           lbraw, hgg, cos_t, sin_t, rdec, rdq, rdk, rtot):
    t, d = x2.shape
    d_mix = SSD_WIDTH + HGRN_WIDTH + RET_WIDTH
    nj = seq // CHUNK
    const2 = lambda j: (0, 0)
    has_res = y is not None
    proj_idx = lambda j: (0, jnp.minimum(j, nj - 1), 0)
    mix_idx = lambda j: (0, jnp.maximum(j - 1, 0), 0)
    x_spec = pl.BlockSpec((batch, CHUNK, d), proj_idx)
    mod_spec = pl.BlockSpec((batch, MOD_ROWS, d), lambda j: (0, 0, 0))
    head_specs = [x_spec]
    head_args = [x2.reshape(batch, seq, d)]
    if has_res:
        head_specs += [pl.BlockSpec((batch, CHUNK * ROW_TILES, LANES), proj_idx), mod_spec]
        head_args += [y.reshape(batch, seq * ROW_TILES, LANES), mod_prev]
    head_specs += [mod_spec, pl.BlockSpec((1, d), const2),
                   pl.BlockSpec((None,) + w.shape[1:], lambda j: (layer, 0, 0))]
    head_args += [mod_l, g, w]
    cast_specs, cast_shapes = [], []
    for ew in expert_w:
        depth_e, ne, ra, cb = ew.shape
        rows = ne * ra
        assert rows % nj == 0
        blk = rows // nj
        head_specs.append(pl.BlockSpec((None, blk, cb), lambda j: (layer, jnp.minimum(j, nj - 1), 0)))
        head_args.append(ew.reshape(depth_e, rows, cb))
        cast_specs.append(pl.BlockSpec((blk, cb), lambda j: (jnp.minimum(j, nj - 1), 0)))
        cast_shapes.append(jax.ShapeDtypeStruct((rows, cb), BF16))
    table_spec = pl.BlockSpec((CHUNK, RET_WIDTH), lambda j: (jnp.maximum(j - 1, 0), 0))
    mixed_spec = pl.BlockSpec((batch, CHUNK, d_mix), mix_idx)
    mixed_shape = jax.ShapeDtypeStruct((batch, seq, d_mix), BF16)
    res = pl.pallas_call(
        functools.partial(_mixer_kernel, layer, has_res),
        grid=(nj + 1,),
        in_specs=head_specs + [
            pl.BlockSpec(conv_w.shape, const2),
            pl.BlockSpec(conv_b.shape, const2),
            pl.BlockSpec(dtb.shape, const2),
            pl.BlockSpec(alog.shape, const2),
            pl.BlockSpec(dexp.shape, const2),
            pl.BlockSpec(ssdg.shape, const2),
            pl.BlockSpec(lbraw.shape, const2),
            pl.BlockSpec(hgg.shape, const2),
            table_spec,
            table_spec,
            pl.BlockSpec(rdec.shape, lambda j: (0, 0, 0)),
            pl.BlockSpec(rdq.shape, const2),
            pl.BlockSpec(rdk.shape, const2),
            pl.BlockSpec(rtot.shape, const2),
        ],
        out_specs=([x_spec] if has_res else []) + [mixed_spec] + cast_specs,
        out_shape=([jax.ShapeDtypeStruct((batch, seq, d), F32)] if has_res else []) + [mixed_shape] + cast_shapes,
        scratch_shapes=[
            pltpu.VMEM((batch, CHUNK, N_PROJ), F32),
            pltpu.VMEM((batch, CHUNK, N_PROJ), F32),
            pltpu.VMEM((batch, TAIL + CHUNK, XBC_WIDTH), F32),
            pltpu.VMEM((batch, LANES, SSD_WIDTH), F32),
            pltpu.VMEM((batch, HGRN_WIDTH // LANES, LANES, LANES), F32),
            pltpu.VMEM((batch, RET_WIDTH // LANES, LANES, LANES), F32),
        ],
        compiler_params=_cparams(("arbitrary",)),
        name="norm_inproj_mixers",
    )(*head_args, conv_w, conv_b, dtb, alog, dexp, ssdg, lbraw, hgg, cos_t, sin_t, rdec, rdq, rdk, rtot)
    if has_res:
        x2, res = res[0].reshape(t, d), res[1:]
    casts = [c.reshape(ew.shape[1:]) for c, ew in zip(res[1:], expert_w)]
    return x2, res[0].reshape(t, d_mix), casts


ROW_TILES = 8


def _store_row_tiled(ref, val):
    m = val.shape[0]
    for k in range(ROW_TILES):
        ref[pl.ds(k, m, stride=ROW_TILES), :] = val[:, k * LANES:(k + 1) * LANES]


def _load_row_tiled(ref):
    m = ref.shape[0] // ROW_TILES
    return jnp.concatenate([ref[pl.ds(k, m, stride=ROW_TILES), :] for k in range(ROW_TILES)], axis=1)


R_E1, R_E2, R_W1, R_W2 = range(4)


def _outproj_kernel(x_ref, mixed_ref, mod_ref, g_ref, w_ref, wr_ref, br_ref,
                    xo_ref, hp_ref, route_ref):
    x = x_ref[...] + mod_ref[G_MIX:G_MIX + 1, :] * _dot(mixed_ref[...], w_ref[...])
    xo_ref[...] = x
    h = _rms(x) * g_ref[...] * (1.0 + mod_ref[SC_FFN:SC_FFN + 1, :]) + mod_ref[SH_FFN:SH_FFN + 1, :]
    _store_row_tiled(hp_ref, h)
    h_hi = h.astype(BF16)
    h_lo = (h - h_hi.astype(F32)).astype(BF16)
    logits = (_dot_nt(wr_ref[0], h_hi) + _dot_nt(wr_ref[0], h_lo) + _dot_nt(wr_ref[1], h_hi)) + br_ref[...]
    rowi = lax.broadcasted_iota(jnp.int32, logits.shape, 0).astype(F32)
    neg = -jnp.inf
    big = float(ROUTER_ROWS)
    is_grp = rowi < N_GROUPS
    gl = jnp.where(is_grp, logits, neg)
    gmax = jnp.max(gl, axis=0, keepdims=True)
    g_idx = jnp.min(jnp.where(gl == gmax, rowi, big), axis=0, keepdims=True)
    p_grp = 1.0 / jnp.sum(jnp.where(is_grp, jnp.exp(gl - gmax), 0.0), axis=0, keepdims=True)
    first = N_GROUPS + g_idx * EXPERTS_PER_GROUP
    valid = (rowi >= first) & (rowi < first + EXPERTS_PER_GROUP)
    el = jnp.where(valid, logits, neg)
    m1 = jnp.max(el, axis=0, keepdims=True)
    i1 = jnp.min(jnp.where(el == m1, rowi, big), axis=0, keepdims=True)
    el2 = jnp.where(rowi == i1, neg, el)
    m2 = jnp.max(el2, axis=0, keepdims=True)
    i2 = jnp.min(jnp.where(el2 == m2, rowi, big), axis=0, keepdims=True)
    e = jnp.exp(m2 - m1)
    w1 = p_grp / (1.0 + e)
    w2 = p_grp * e / (1.0 + e)
    r8 = lax.broadcasted_iota(jnp.int32, route_ref.shape, 0)
    route_ref[...] = jnp.where(r8 == R_E1, i1 - N_GROUPS,
                               jnp.where(r8 == R_E2, i2 - N_GROUPS,
                                         jnp.where(r8 == R_W1, w1, jnp.where(r8 == R_W2, w2, 0.0))))


def _outproj(x2, mixed, mod_l, g, w, wr, br, seq, layer, tm=512):
    t, d = x2.shape
    per_b = seq // tm
    return pl.pallas_call(
        _outproj_kernel,
        grid=(t // tm,),
        in_specs=[
            pl.BlockSpec((tm, d), lambda i: (i, 0)),
            pl.BlockSpec((tm, mixed.shape[1]), lambda i: (i, 0)),
            pl.BlockSpec((None, MOD_ROWS, d), lambda i: (i // per_b, 0, 0)),
            pl.BlockSpec((1, d), lambda i: (0, 0)),
            pl.BlockSpec((None,) + w.shape[1:], lambda i: (layer, 0, 0)),
            pl.BlockSpec((None,) + wr.shape[1:], lambda i: (layer, 0, 0, 0)),
            pl.BlockSpec((None,) + br.shape[1:], lambda i: (layer, 0, 0)),
        ],
        out_specs=[
            pl.BlockSpec((tm, d), lambda i: (i, 0)),
            pl.BlockSpec((tm * ROW_TILES, LANES), lambda i: (i, 0)),
            pl.BlockSpec((ROUTE_ROWS, tm), lambda i: (0, i)),
        ],
        out_shape=[
            jax.ShapeDtypeStruct((t, d), F32),
            jax.ShapeDtypeStruct((t * ROW_TILES, LANES), F32),
            jax.ShapeDtypeStruct((ROUTE_ROWS, t), F32),
        ],
        compiler_params=_cparams(("arbitrary",)),
        name="outproj_router",
    )(x2, mixed, mod_l, g, w, wr, br)


MOE_CHUNK = 2048
MOE_TILE = 160
SEG_ALIGN = 8
COMBINE_STEPS = 4
EXPERTS_PER_STEP = 4
SLOTS_PAD = 2 * MOE_CHUNK + N_EXPERTS * SEG_ALIGN
RANK_BLOCK = 256
M_OFF, M_CNT = 0, 1
S_POS1, S_POS2, S_W1, S_W2 = 0, 1, R_W1, R_W2


def _rank_kernel(route_ref, slots_ref, meta_ref):
    tb = route_ref.shape[1]
    ne = N_EXPERTS
    rowe = lax.broadcasted_iota(jnp.int32, (ne, RANK_BLOCK), 0).astype(F32)
    r_i = lax.broadcasted_iota(jnp.int32, (RANK_BLOCK, RANK_BLOCK), 0)
    c_i = lax.broadcasted_iota(jnp.int32, (RANK_BLOCK, RANK_BLOCK), 1)
    earlier = jnp.where(r_i < c_i, 1.0, 0.0).astype(BF16)
    carry = jnp.zeros((ne, 1), F32)
    ranks = []
    for b in range(tb // RANK_BLOCK):
        blk = route_ref[:, b * RANK_BLOCK:(b + 1) * RANK_BLOCK]
        onehot = jnp.where((rowe == blk[R_E1:R_E1 + 1, :]) | (rowe == blk[R_E2:R_E2 + 1, :]), 1.0, 0.0)
        ranks.append(_dot(onehot.astype(BF16), earlier) + carry)
        carry = carry + jnp.sum(onehot, axis=1, keepdims=True)
    cnt = jnp.broadcast_to(carry, (ne, LANES))
    seg = jnp.floor((cnt + (SEG_ALIGN - 1)) * (1.0 / SEG_ALIGN)) * SEG_ALIGN
    e_r = lax.broadcasted_iota(jnp.int32, (ne, ne), 0)
    e_c = lax.broadcasted_iota(jnp.int32, (ne, ne), 1)
    before = jnp.where(e_c < e_r, 1.0, 0.0).astype(BF16)
    seg_parts = _split3(seg)
    off = _dot(before, seg_parts[0]) + _dot(before, seg_parts[1]) + _dot(before, seg_parts[2])
    off_col = off[:, 0:1]
    r8 = lax.broadcasted_iota(jnp.int32, (ROUTE_ROWS, RANK_BLOCK), 0)
    for b in range(tb // RANK_BLOCK):
        blk = route_ref[:, b * RANK_BLOCK:(b + 1) * RANK_BLOCK]
        dest = off_col + ranks[b]
        p1 = jnp.sum(jnp.where(rowe == blk[R_E1:R_E1 + 1, :], dest, 0.0), axis=0, keepdims=True)
        p2 = jnp.sum(jnp.where(rowe == blk[R_E2:R_E2 + 1, :], dest, 0.0), axis=0, keepdims=True)
        slots_ref[:, b * RANK_BLOCK:(b + 1) * RANK_BLOCK] = jnp.where(
            r8 == S_POS1, p1, jnp.where(r8 == S_POS2, p2, jnp.where((r8 == S_W1) | (r8 == S_W2), blk, 0.0)))
    meta_ref[M_OFF] = off.astype(jnp.int32)
    meta_ref[M_CNT] = cnt.astype(jnp.int32)


def _rank(route):
    t = route.shape[1]
    nc = t // MOE_CHUNK
    return pl.pallas_call(
        _rank_kernel,
        grid=(nc,),
        in_specs=[pl.BlockSpec((ROUTE_ROWS, MOE_CHUNK), lambda c: (0, c))],
        out_specs=[pl.BlockSpec((ROUTE_ROWS, MOE_CHUNK), lambda c: (0, c)),
                   pl.BlockSpec((None, 2, N_EXPERTS, LANES), lambda c: (c, 0, 0, 0))],
        out_shape=[jax.ShapeDtypeStruct((ROUTE_ROWS, t), F32),
                   jax.ShapeDtypeStruct((nc, 2, N_EXPERTS, LANES), jnp.int32)],
        compiler_params=_cparams(("arbitrary",)),
        name="slot_rank",
    )(route)


def _moe_kernel(pos1_ref, pos2_ref, row1_ref, row2_ref, w1_ref, w2_ref, off_ref, cnt_ref,
                hp_ref, wg_ref, wu_ref, wd_ref, y_ref, inv_ref, ys_ref, xt_ref, xt2_ref, xt3_ref):
    c = pl.program_id(0)
    e = pl.program_id(1)
    rt = ROW_TILES
    tb = hp_ref.shape[0] // rt
    tok0 = c * tb
    seg0 = c * N_EXPERTS

    @pl.when((c == 0) & (e == 0))
    def _():
        xt_ref[...] = jnp.zeros_like(xt_ref)
        xt3_ref[...] = jnp.zeros_like(xt3_ref)

    inv_half = inv_ref.shape[0] // 2
    ib = (c % 2) * inv_half
    ib_next = inv_half - ib
    tok_next = jnp.minimum(c + 1, pl.num_programs(0) - 1) * tb

    @pl.when((c == 0) & (e == 0))
    def _():
        def pads(ee, carry):
            o = off_ref[seg0 + ee]
            n_e = cnt_ref[seg0 + ee]

            def zero(i, carry2):
                inv_ref[ib + o + i] = 0
                return carry2
            lax.fori_loop(n_e, (n_e + SEG_ALIGN - 1) // SEG_ALIGN * SEG_ALIGN, zero, 0)
            return carry
        lax.fori_loop(0, N_EXPERTS, pads, 0)
        n_last = cnt_ref[seg0 + N_EXPERTS - 1]
        total = off_ref[seg0 + N_EXPERTS - 1] + (n_last + SEG_ALIGN - 1) // SEG_ALIGN * SEG_ALIGN

        def zero_tail(i, carry):
            inv_ref[ib + total + i] = 0
            return carry
        lax.fori_loop(0, MOE_TILE, zero_tail, 0)

    @pl.when((c == 0) & (e == 0))
    def _():
        def place(i, row):
            inv_ref[ib + pos1_ref[tok0 + i]] = row
            inv_ref[ib + pos2_ref[tok0 + i]] = row
            return row + rt
        lax.fori_loop(0, tb, place, 0, unroll=8)

    def place_slice(stage):
        n_tok = tb // N_EXPERTS
        first = stage * n_tok
        for i in range(n_tok):
            t_loc = first + i
            inv_ref[ib_next + pos1_ref[tok_next + t_loc]] = t_loc * rt
            inv_ref[ib_next + pos2_ref[tok_next + t_loc]] = t_loc * rt
        seg_next = (tok_next // tb) * N_EXPERTS
        o = off_ref[seg_next + stage]
        n_e = cnt_ref[seg_next + stage]
        seg_end = (n_e + SEG_ALIGN - 1) // SEG_ALIGN * SEG_ALIGN
        spare = inv_half - 1
        for i in range(SEG_ALIGN - 1):
            inv_ref[ib_next + jnp.where(n_e + i < seg_end, o + n_e + i, spare)] = 0
        n_last = cnt_ref[seg_next + N_EXPERTS - 1]
        total = off_ref[seg_next + N_EXPERTS - 1] + (n_last + SEG_ALIGN - 1) // SEG_ALIGN * SEG_ALIGN
        share = MOE_TILE // N_EXPERTS
        for i in range(share):
            inv_ref[ib_next + total + stage * share + i] = 0

    expert_steps = N_EXPERTS // EXPERTS_PER_STEP
    is_expert = e < expert_steps
    e_first = jnp.minimum(e, expert_steps - 1) * EXPERTS_PER_STEP
    counts = [cnt_ref[seg0 + e_first + k] for k in range(EXPERTS_PER_STEP)]
    bases = [off_ref[seg0 + e_first + k] for k in range(EXPERTS_PER_STEP)]

    def gather_valid(buf, start, n_rows):
        def gather_group(g, carry):
            slot0 = start + g * SEG_ALIGN
            dst0 = pl.multiple_of(g * (SEG_ALIGN * rt), SEG_ALIGN * rt)
            for u in range(SEG_ALIGN):
                src = pl.multiple_of(inv_ref[ib + slot0 + u], rt)
                buf[pl.ds(dst0 + u * rt, rt), :] = hp_ref[pl.ds(src, rt), :]
            return carry
        rows = jnp.minimum(MOE_TILE, (n_rows + SEG_ALIGN - 1) // SEG_ALIGN * SEG_ALIGN)
        lax.fori_loop(0, rows // SEG_ALIGN, gather_group, 0)

    def gather_static(buf, start):
        first = ib + start
        for r in range(MOE_TILE):
            src = pl.multiple_of(inv_ref[first + r], rt)
            buf[r * rt:(r + 1) * rt, :] = hp_ref[pl.ds(src, rt), :]

    def expert_tile(buf, k, start):
        xrow = jnp.concatenate([buf[pl.ds(j, MOE_TILE, stride=rt), :] for j in range(rt)], axis=1)
        xb = xrow.astype(BF16)
        hid = (_silu(_dot(xb, wg_ref[k])) * _dot(xb, wu_ref[k])).astype(BF16)
        out = _dot(hid, wd_ref[k])
        for j in range(rt):
            ys_ref[pl.ds(start * rt + j, MOE_TILE, stride=rt), :] = out[:, j * LANES:(j + 1) * LANES]

    @pl.when(is_expert)
    def _():
        bufs = (xt_ref, xt2_ref)

        @pl.when(e == 0)
        def _():
            gather_valid(bufs[0], pl.multiple_of(bases[0], SEG_ALIGN), counts[0])

        next_first = off_ref[seg0 + jnp.minimum(e_first + EXPERTS_PER_STEP, N_EXPERTS - 1)]
        for k in range(EXPERTS_PER_STEP):
            start = pl.multiple_of(bases[k], SEG_ALIGN)
            nxt = bases[k + 1] if k + 1 < EXPERTS_PER_STEP else next_first
            gather_static(bufs[(k + 1) % 2], pl.multiple_of(nxt, SEG_ALIGN))
            place_slice(e_first + k)
            expert_tile(bufs[k % 2], k, start)

            @pl.when(counts[k] > MOE_TILE)
            def _(k=k, start=start):
                def more(i, carry):
                    s_i = pl.multiple_of(start + i * MOE_TILE, SEG_ALIGN)
                    gather_valid(xt3_ref, s_i, counts[k] - i * MOE_TILE)
                    expert_tile(xt3_ref, k, s_i)
                    return carry
                lax.fori_loop(1, (counts[k] + MOE_TILE - 1) // MOE_TILE, more, 0)

    @pl.when(e >= expert_steps)
    def _():
        tq = y_ref.shape[0] // rt
        t0 = tok0 + (e - expert_steps) * tq

        def combine(i, carry):
            r1 = ys_ref[pl.ds(pl.multiple_of(row1_ref[t0 + i], rt), rt), :]
            r2 = ys_ref[pl.ds(pl.multiple_of(row2_ref[t0 + i], rt), rt), :]
            y_ref[pl.ds(pl.multiple_of(i * rt, rt), rt), :] = w1_ref[t0 + i] * r1 + w2_ref[t0 + i] * r2
            return carry
        lax.fori_loop(0, tq, combine, 0, unroll=8)


def _moe(hp, pos1, pos2, w1, w2, off, cnt, w_gate, w_up, w_down):
    rt = ROW_TILES
    t = hp.shape[0] // rt
    ne, d, f = w_gate.shape
    nc = t // MOE_CHUNK
    tq = MOE_CHUNK // COMBINE_STEPS

    es = ne // EXPERTS_PER_STEP
    assert MOE_CHUNK % ne == 0 and MOE_TILE % ne == 0 and EXPERTS_PER_STEP % 2 == 0

    def w_idx(c, e, *_):
        return (jnp.where(e < es, e, 0), 0, 0)

    grid_spec = pltpu.PrefetchScalarGridSpec(
        num_scalar_prefetch=8,
        grid=(nc, es + COMBINE_STEPS),
        in_specs=[
            pl.BlockSpec((MOE_CHUNK * rt, LANES), lambda c, e, *_: (c, 0)),
            pl.BlockSpec((EXPERTS_PER_STEP, d, f), w_idx),
            pl.BlockSpec((EXPERTS_PER_STEP, d, f), w_idx),
            pl.BlockSpec((EXPERTS_PER_STEP, f, d), w_idx),
        ],
        out_specs=pl.BlockSpec((tq * rt, LANES),
                               lambda c, e, *_: (c * COMBINE_STEPS + jnp.maximum(e - es, 0), 0)),
        scratch_shapes=[
            pltpu.SMEM((2 * (SLOTS_PAD + MOE_TILE),), jnp.int32),
            pltpu.VMEM(((SLOTS_PAD + MOE_TILE) * rt, LANES), F32),
            pltpu.VMEM((MOE_TILE * rt, LANES), F32),
            pltpu.VMEM((MOE_TILE * rt, LANES), F32),
            pltpu.VMEM((MOE_TILE * rt, LANES), F32),
        ],
    )
    return pl.pallas_call(
        _moe_kernel,
        grid_spec=grid_spec,
        out_shape=jax.ShapeDtypeStruct((t * rt, LANES), F32),
        compiler_params=_cparams(("arbitrary", "arbitrary")),
        name="moe_top2",
    )(pos1, pos2, pos1 * rt, pos2 * rt, w1, w2, off, cnt, hp, w_gate, w_up, w_down)


def _final_kernel(x_ref, y_ref, mod_ref, g_ref, o_ref):
    x = x_ref[...] + mod_ref[G_FFN:G_FFN + 1, :] * _load_row_tiled(y_ref)
    o_ref[...] = _rms(x) * g_ref[...]


def _final(x2, y, mod_l, g, seq, tm=512):
    t, d = x2.shape
    per_b = seq // tm
    return pl.pallas_call(
        _final_kernel,
        grid=(t // tm,),
        in_specs=[
            pl.BlockSpec((tm, d), lambda i: (i, 0)),
            pl.BlockSpec((tm * ROW_TILES, LANES), lambda i: (i, 0)),
            pl.BlockSpec((None, MOD_ROWS, d), lambda i: (i // per_b, 0, 0)),
            pl.BlockSpec((1, d), lambda i: (0, 0)),
        ],
        out_specs=pl.BlockSpec((tm, d), lambda i: (i, 0)),
        out_shape=jax.ShapeDtypeStruct((t, d), F32),
        compiler_params=_cparams(("arbitrary",)),
        name="final_norm",
    )(x2, y, mod_l, g)


def _reorder_w_in_kernel(w_ref, o_ref):
    dt0 = OFF_HQ
    rest = OFF_DT - OFF_HQ
    o_ref[:, 0:dt0] = w_ref[:, 0:dt0].astype(BF16)
    o_ref[:, dt0:OFF_DT] = w_ref[:, dt0 + SSD_HEADS:dt0 + SSD_HEADS + rest].astype(BF16)
    lane = lax.broadcasted_iota(jnp.int32, (w_ref.shape[0], LANES), 1)
    o_ref[:, OFF_DT:N_PROJ] = jnp.where(lane < SSD_HEADS, w_ref[:, dt0:dt0 + LANES], 0.0).astype(BF16)


def _reorder_w_in(w_in, tm=256):
    depth, d, n_in = w_in.shape
    assert n_in == OFF_DT + SSD_HEADS
    return pl.pallas_call(
        _reorder_w_in_kernel,
        grid=(depth, d // tm),
        in_specs=[pl.BlockSpec((None, tm, n_in), lambda l, i: (l, i, 0))],
        out_specs=pl.BlockSpec((None, tm, N_PROJ), lambda l, i: (l, i, 0)),
        out_shape=jax.ShapeDtypeStruct((depth, d, N_PROJ), BF16),
        compiler_params=_cparams(("arbitrary", "arbitrary")),
        name="reorder_w_in",
    )(w_in)


def _retention_tables(seq):
    half = HEAD_DIM // 2
    inv_freq = ROPE_BASE ** (-jnp.arange(half, dtype=F32) / half)
    ang = jnp.arange(seq, dtype=F32)[:, None] * inv_freq[None, :]
    cos, sin = jnp.cos(ang), jnp.sin(ang)
    cos_t = jnp.tile(jnp.concatenate([cos, cos], axis=-1), (1, RET_HEADS))
    sin_t = jnp.tile(jnp.concatenate([-sin, sin], axis=-1), (1, RET_HEADS))
    log_gamma = jnp.log1p(-jnp.exp2(-5.0 - jnp.arange(RET_HEADS, dtype=F32)))
    tt = jnp.arange(CHUNK, dtype=F32)
    diff = tt[:, None] - tt[None, :]
    rdec = jnp.where(diff >= 0, jnp.exp(jnp.maximum(diff, 0.0)[None] * log_gamma[:, None, None]), 0.0)
    lg_lane = jnp.repeat(log_gamma, HEAD_DIM)[None, :]
    rdq = jnp.exp((tt[:, None] + 1.0) * lg_lane)
    rdk = jnp.exp((CHUNK - 1.0 - tt[:, None]) * lg_lane)
    rtot = jnp.exp(CHUNK * lg_lane)
    return cos_t, sin_t, rdec, rdq, rdk, rtot


def _pad_lanes(v, width=LANES):
    return jnp.pad(v, ((0, 0), (0, width - v.shape[-1])))


def kernel(x, c, norm_mix_g, norm_ffn_g, final_norm_g, w_ada, b_ada, w_in, conv_w, conv_b, ssd_dt_bias, ssd_a_log, ssd_d, ssd_norm_g, hgrn_lower_bounds, hgrn_norm_g, w_out, w_grp, b_grp, w_exp, b_exp, w_gate, w_up, w_down):
    batch, seq, d = x.shape
    depth = w_in.shape[0]
    t = batch * seq

    c8 = jnp.pad(c, ((0, MOD_ROWS - batch), (0, 0)))
    mod = _ada(c8, w_ada, b_ada)
    mod = mod[:, :batch].reshape(depth, batch, 6, d)
    mod = jnp.pad(mod, ((0, 0), (0, 0), (0, MOD_ROWS - 6), (0, 0)))

    w_in_p = _reorder_w_in(w_in)
    w_out_b = w_out.astype(BF16)
    n_router = N_GROUPS + N_EXPERTS
    w_router = jnp.pad(jnp.swapaxes(jnp.concatenate([w_grp, w_exp], axis=-1), 1, 2),
                       ((0, 0), (0, ROUTER_ROWS - n_router), (0, 0)))
    w_router_hi = w_router.astype(BF16)
    w_router = jnp.stack([w_router_hi, (w_router - w_router_hi.astype(F32)).astype(BF16)], axis=1)
    b_router = jnp.pad(jnp.concatenate([b_grp, b_exp], axis=-1), ((0, 0), (0, ROUTER_ROWS - n_router)))[:, :, None]
    dtb = _pad_lanes(ssd_dt_bias)[:, None, :]
    alog = _pad_lanes(ssd_a_log)[:, None, :]
    dexp = jnp.repeat(ssd_d, HEAD_DIM, axis=-1)[:, None, :]
    tables = _retention_tables(seq)

    x2 = x.reshape(t, d)
    y = None
    for layer in range(depth):
        x2, mixed, expert_w = _mixer(x2, y, mod[layer - 1] if layer else None, mod[layer],
                           norm_mix_g[layer][None, :], w_in_p, (w_gate, w_up, w_down), layer, batch, seq,
                           conv_w[layer], conv_b[layer][None, :], dtb[layer],
                           alog[layer], dexp[layer], ssd_norm_g[layer][None, :], hgrn_lower_bounds,
                           hgrn_norm_g[layer][None, :], *tables)
        x2, hp, route = _outproj(x2, mixed, mod[layer], norm_ffn_g[layer][None, :], w_out_b,
                                 w_router, b_router, seq, layer)
        slots, meta = _rank(route)
        y = _moe(hp, slots[S_POS1].astype(jnp.int32), slots[S_POS2].astype(jnp.int32), slots[S_W1], slots[S_W2],
                 meta[:, M_OFF, :, 0].reshape(-1), meta[:, M_CNT, :, 0].reshape(-1),
                 *expert_w)
    out = _final(x2, y, mod[depth - 1], final_norm_g[None, :], seq)
    return out.reshape(batch, seq, d)
```
